```python
import math
import jax, jax.numpy as jnp
from jax import lax
import numpy as np

D_MODEL = 2048
BATCH = 8
SEQ = 4096
DEPTH = 2

CTX_LEN = 256
GRID_W = 64

HEAD_DIM = 128
D_GDN = 3 * D_MODEL // 8
D_HYENA = D_MODEL // 4
D_MLSTM = 3 * D_MODEL // 8
D_MIX = D_GDN + D_HYENA + D_MLSTM
GDN_HEADS = D_GDN // HEAD_DIM
MLSTM_HEADS = D_MLSTM // HEAD_DIM
GDN_CONV = 5
HYENA_ORDER = 2
HYENA_CONV = 3
FILTER_EMB = 33
FILTER_BANDS = (FILTER_EMB - 1) // 2
FILTER_HIDDEN = 64
DECAY_TARGET = 1e-2
FAST_DECAY_PCT = 0.3
SLOW_DECAY_PCT = 1.5
MIN_DECAY = math.log(DECAY_TARGET) / SLOW_DECAY_PCT
MAX_DECAY = math.log(DECAY_TARGET) / FAST_DECAY_PCT
CHUNK = 64
NORM_EPS = 1e-6
IN_SPLITS = (3 * D_GDN, D_GDN, 4 * GDN_HEADS, 3 * D_HYENA, D_HYENA, 3 * D_MLSTM, D_MLSTM, D_MLSTM, 4 * MLSTM_HEADS)
D_IN = sum(IN_SPLITS)

kernel_name = "hymba_gdn_hyena_mlstm_flow_block"


def rms_norm(x, w):
    xf = x.astype(jnp.float32)
    y = xf * lax.rsqrt(jnp.mean(xf * xf, axis=-1, keepdims=True) + NORM_EPS)
    return (y * w.astype(jnp.float32)).astype(x.dtype)


def adaln(cond, w, b):
    m = jax.nn.silu(cond) @ w + b
    return jnp.split(m, 3, axis=-1)


def split_cols(u):
    idx = np.cumsum(IN_SPLITS)[:-1].tolist()
    return jnp.split(u, idx, axis=-1)


def to_heads(t, n_heads):
    b, l, _ = t.shape
    return t.reshape(b, l, n_heads, HEAD_DIM).transpose(0, 2, 1, 3)


def from_heads(t):
    b, h, l, d = t.shape
    return t.transpose(0, 2, 1, 3).reshape(b, l, h * d)


def flip_seq(t):
    return jnp.flip(t, axis=2)


def l2norm(t):
    return t * lax.rsqrt(jnp.sum(t * t, axis=-1, keepdims=True) + NORM_EPS)


def head_out(o, w, z):
    o = o * lax.rsqrt(jnp.mean(o * o, axis=-1, keepdims=True) + NORM_EPS) * w.astype(jnp.float32)
    return from_heads(o).astype(z.dtype) * jax.nn.silu(z)


def dwconv_centred(x, w, grid):
    k_w = w.shape[0]
    pad = k_w // 2
    length = x.shape[1]
    xp = jnp.pad(x, ((0, 0), (pad, pad), (0, 0)))
    y = None
    for j in range(k_w):
        off = j - pad
        term = xp[:, j:j + length, :] * w[j]
        if grid and off != 0:
            col = jnp.arange(length) % GRID_W
            ok = (col + off >= 0) & (col + off < GRID_W)
            term = jnp.where(ok[None, :, None], term, jnp.zeros_like(term))
        y = term if y is None else y + term
    return y


def gdn_chunk_scan(q, k, v, g, beta, s0):
    b_, h_, length, dk = q.shape
    dv = v.shape[-1]
    n_ch = length // CHUNK

    def ch(t):
        return t.reshape((b_, h_, n_ch, CHUNK) + t.shape[3:])

    qc, kc, vc, bc = ch(q), ch(k), ch(v), ch(beta)
    gcum = jnp.cumsum(ch(g), axis=-1)
    idx = jnp.arange(CHUNK)
    incl = idx[:, None] >= idx[None, :]
    strict = idx[:, None] > idx[None, :]
    dec = jnp.exp(jnp.where(incl, gcum[..., :, None] - gcum[..., None, :], -jnp.inf))
    kk = jnp.einsum('bhnid,bhnjd->bhnij', kc, kc)
    m_low = jnp.where(strict, bc[..., :, None] * kk * dec, 0.0)
    a_mat = m_low + jnp.eye(CHUNK, dtype=m_low.dtype)
    rhs = jnp.concatenate([bc[..., None] * vc, (bc * jnp.exp(gcum))[..., None] * kc], axis=-1)
    sol = lax.linalg.triangular_solve(a_mat, rhs, left_side=True, lower=True, unit_diagonal=True)
    u_tilde, w_mat = sol[..., :dv], sol[..., dv:]
    a_qk = jnp.einsum('bhnid,bhnjd->bhnij', qc, kc) * dec
    q_dec = qc * jnp.exp(gcum)[..., None]
    k_end = kc * jnp.exp(gcum[..., -1:] - gcum)[..., None]
    g_end = jnp.exp(gcum[..., -1])

    def step(s, xs):
        u_t, w_c, a_c, qd, ke, ge = xs
        u = u_t - jnp.einsum('bhck,bhkv->bhcv', w_c, s)
        o = jnp.einsum('bhck,bhkv->bhcv', qd, s) + jnp.einsum('bhij,bhjv->bhiv', a_c, u)
        s = ge[..., None, None] * s + jnp.einsum('bhck,bhcv->bhkv', ke, u)
        return s, o

    xs = tuple(jnp.moveaxis(t, 2, 0) for t in (u_tilde, w_mat, a_qk, q_dec, k_end, g_end))
    s_fin, o = lax.scan(step, s0, xs)
    o = jnp.moveaxis(o, 0, 2).reshape(b_, h_, length, dv)
    return o, s_fin


def mlstm_chunk_scan(q, k, v, log_i, log_f, state):
    b_, h_, length, dk = q.shape
    dv = v.shape[-1]
    n_ch = length // CHUNK

    def ch(t):
        return t.reshape((b_, h_, n_ch, CHUNK) + t.shape[3:])

    qc, kc, vc, li = ch(q), ch(k), ch(v), ch(log_i)
    bcum = jnp.cumsum(ch(log_f), axis=-1)
    idx = jnp.arange(CHUNK)
    incl = idx[:, None] >= idx[None, :]
    d_log = jnp.where(incl, bcum[..., :, None] - bcum[..., None, :] + li[..., None, :], -jnp.inf)
    qk = jnp.einsum('bhnid,bhnjd->bhnij', qc, kc)
    end_log = bcum[..., -1:] - bcum + li

    def step(carry, xs):
        c_m, n_v, m_s = carry
        q_c, k_c, v_c, b_c, dl, qk_c, el = xs
        inter = b_c + m_s[..., None]
        m_i = jnp.maximum(inter, jnp.max(dl, axis=-1))
        w_inter = jnp.exp(inter - m_i)
        p = jnp.exp(dl - m_i[..., None]) * qk_c
        num = w_inter[..., None] * jnp.einsum('bhck,bhkv->bhcv', q_c, c_m) + jnp.einsum('bhij,bhjv->bhiv', p, v_c)
        den = w_inter * jnp.einsum('bhck,bhk->bhc', q_c, n_v) + jnp.sum(p, axis=-1)
        h = num / jnp.maximum(jnp.abs(den), jnp.exp(-m_i))[..., None]
        carry_log = b_c[..., -1] + m_s
        m_new = jnp.maximum(carry_log, jnp.max(el, axis=-1))
        w_c = jnp.exp(el - m_new[..., None])
        dec = jnp.exp(carry_log - m_new)
        c_m = dec[..., None, None] * c_m + jnp.einsum('bhc,bhck,bhcv->bhkv', w_c, k_c, v_c)
        n_v = dec[..., None] * n_v + jnp.einsum('bhc,bhck->bhk', w_c, k_c)
        return (c_m, n_v, m_new), h

    xs = tuple(jnp.moveaxis(t, 2, 0) for t in (qc, kc, vc, bcum, d_log, qk, end_log))
    state_fin, h = lax.scan(step, state, xs)
    h = jnp.moveaxis(h, 0, 2).reshape(b_, h_, length, dv)
    return h, state_fin


def gdn_prep(qkv, ab, conv_w, a_log, dt_bias, grid):
    b_, length, _ = qkv.shape
    qkv = jax.nn.silu(dwconv_centred(qkv, conv_w, grid)).astype(jnp.float32)
    q, k, v = jnp.split(qkv, 3, axis=-1)
    q = l2norm(to_heads(q, GDN_HEADS)) * HEAD_DIM ** -0.5
    k = l2norm(to_heads(k, GDN_HEADS))
    v = to_heads(v, GDN_HEADS)
    ab = ab.astype(jnp.float32).reshape(b_, length, 4, GDN_HEADS)
    g = -jnp.exp(a_log.astype(jnp.float32)) * jax.nn.softplus(ab[:, :, :2] + dt_bias.astype(jnp.float32))
    beta = jax.nn.sigmoid(ab[:, :, 2:])
    return q, k, v, g.transpose(2, 0, 3, 1), beta.transpose(2, 0, 3, 1)


def gdn_branch(qkv_l, ab_l, z_l, qkv_c, ab_c, z_c, conv_w, a_log, dt_bias, norm_w, ctx_out):
    ql, kl, vl, gl, bl = gdn_prep(qkv_l, ab_l, conv_w, a_log, dt_bias, True)
    qc, kc, vc, gc, bc = gdn_prep(qkv_c, ab_c, conv_w, a_log, dt_bias, False)
    s0 = jnp.zeros(qc.shape[:2] + (HEAD_DIM, HEAD_DIM), jnp.float32)
    o_cf, s_f = gdn_chunk_scan(qc, kc, vc, gc[0], bc[0], s0)
    o_cb, s_b = gdn_chunk_scan(flip_seq(qc), flip_seq(kc), flip_seq(vc), flip_seq(gc[1]), flip_seq(bc[1]), s0)
    o_lf, _ = gdn_chunk_scan(ql, kl, vl, gl[0], bl[0], s_f)
    o_lb, _ = gdn_chunk_scan(flip_seq(ql), flip_seq(kl), flip_seq(vl), flip_seq(gl[1]), flip_seq(bl[1]), s_b)
    y_l = head_out(o_lf + flip_seq(o_lb), norm_w, z_l)
    y_c = head_out(o_cf + flip_seq(o_cb), norm_w, z_c) if ctx_out else None
    return y_l, y_c


def mlstm_prep(qkv, o_pre, gates, gate_bias):
    b_, length, _ = qkv.shape
    q, k, v = jnp.split(qkv.astype(jnp.float32), 3, axis=-1)
    q = to_heads(q, MLSTM_HEADS)
    k = to_heads(k, MLSTM_HEADS) * HEAD_DIM ** -0.5
    v = to_heads(v, MLSTM_HEADS)
    gt = (gates.astype(jnp.float32).reshape(b_, length, 4, MLSTM_HEADS) + gate_bias.astype(jnp.float32)).transpose(2, 0, 3, 1)
    log_i = gt[0::2]
    log_f = jax.nn.log_sigmoid(gt[1::2])
    o_gate = jax.nn.sigmoid(to_heads(o_pre.astype(jnp.float32), MLSTM_HEADS))
    return q, k, v, log_i, log_f, o_gate


def mlstm_branch(qkv_l, o_l, g_l, z_l, qkv_c, o_c, g_c, z_c, gate_bias, norm_w, ctx_out):
    ql, kl, vl, lil, lfl, ogl = mlstm_prep(qkv_l, o_l, g_l, gate_bias)
    qc, kc, vc, lic, lfc, ogc = mlstm_prep(qkv_c, o_c, g_c, gate_bias)
    b_, h_ = qc.shape[:2]
    s0 = (jnp.zeros((b_, h_, HEAD_DIM, HEAD_DIM), jnp.float32), jnp.zeros((b_, h_, HEAD_DIM), jnp.float32), jnp.zeros((b_, h_), jnp.float32))
    h_cf, s_f = mlstm_chunk_scan(qc, kc, vc, lic[0], lfc[0], s0)
    h_cb, s_b = mlstm_chunk_scan(flip_seq(qc), flip_seq(kc), flip_seq(vc), flip_seq(lic[1]), flip_seq(lfc[1]), s0)
    h_lf, _ = mlstm_chunk_scan(ql, kl, vl, lil[0], lfl[0], s_f)
    h_lb, _ = mlstm_chunk_scan(flip_seq(ql), flip_seq(kl), flip_seq(vl), flip_seq(lil[1]), flip_seq(lfl[1]), s_b)
    y_l = head_out(ogl * (h_lf + flip_seq(h_lb)), norm_w, z_l)
    y_c = head_out(ogc * (h_cf + flip_seq(h_cb)), norm_w, z_c) if ctx_out else None
    return y_l, y_c


def hyena_filter_spectra(length, w1, b1, freq, w2, b2, w3):
    f32 = jnp.float32
    pos = jnp.arange(length, dtype=f32)
    t = pos / max(length - 1, 1)
    ang = 2.0 * math.pi * pos / length
    bands = jnp.linspace(1e-4, FILTER_BANDS - 1, FILTER_BANDS, dtype=f32)
    feats = jnp.concatenate([t[:, None], jnp.cos(ang[:, None] * bands), -jnp.sin(ang[:, None] * bands)], axis=-1)
    fr = freq.astype(f32)
    h = jnp.sin(fr * (feats @ w1.astype(f32) + b1.astype(f32)))
    h = jnp.sin(fr * (h @ w2.astype(f32) + b2.astype(f32)))
    h = (h @ w3.astype(f32)).reshape(length, HYENA_ORDER, 2, D_HYENA)
    deltas = jnp.abs(jnp.linspace(MIN_DECAY, MAX_DECAY, D_HYENA, dtype=f32))
    h = h * jnp.exp(-t[:, None, None, None] * deltas)
    h = h / (jnp.sum(jnp.abs(h), axis=0, keepdims=True) + NORM_EPS)
    spec = jnp.fft.rfft(h, n=2 * length, axis=0)
    return spec[:, :, 0] + jnp.conj(spec[:, :, 1])


def long_conv(y, spec):
    length = y.shape[1]
    yf = jnp.fft.rfft(y, n=2 * length, axis=1)
    return jnp.fft.irfft(yf * spec[None], n=2 * length, axis=1)[:, :length]


def hyena_branch(p, z, conv_w, w1, b1, freq, w2, b2, w3, skip, grid):
    length = p.shape[1]
    p = dwconv_centred(p, conv_w, grid)
    x1, x2, v = jnp.split(p, 3, axis=-1)
    spec = hyena_filter_spectra(length, w1, b1, freq, w2, b2, w3)
    sk = skip.astype(jnp.float32)
    y = v.astype(jnp.float32)
    y = x1.astype(jnp.float32) * (long_conv(y, spec[:, 0]) + sk[0] * y)
    y = x2.astype(jnp.float32) * (long_conv(y, spec[:, 1]) + sk[1] * y)
    return y.astype(z.dtype) * jax.nn.silu(z)


def setup_inputs(seed: int = 0) -> dict:
    key = jax.random.key(seed)
    ks = jax.random.split(key, 26)
    f32 = jnp.float32

    def nrm(k, shape, s):
        return jax.random.normal(k, shape, f32) * s

    x = nrm(ks[0], (BATCH, SEQ, D_MODEL), 1.0)
    c = nrm(ks[1], (BATCH, D_MODEL), 1.0)
    ctx = nrm(ks[2], (BATCH, CTX_LEN, D_MODEL), 1.0)
    c_ctx = nrm(ks[3], (D_MODEL,), 1.0)
    norm_w = 1.0 + nrm(ks[4], (DEPTH, D_MODEL), 0.02)
    mod_w = nrm(ks[5], (DEPTH, D_MODEL, 3 * D_MODEL), 0.5 * D_MODEL ** -0.5)
    mod_b = nrm(ks[6], (DEPTH, 3 * D_MODEL), 0.02)
    w_in = nrm(ks[7], (DEPTH, D_MODEL, D_IN), D_MODEL ** -0.5)
    gdn_conv = nrm(ks[8], (DEPTH, GDN_CONV, 3 * D_GDN), GDN_CONV ** -0.5)
    gdn_a_log = jnp.log(jax.random.uniform(ks[9], (DEPTH, 2, GDN_HEADS), f32, 1.0, 16.0))
    dt = jnp.exp(jax.random.uniform(ks[10], (DEPTH, 2, GDN_HEADS), f32, math.log(1e-3), math.log(1e-1)))
    gdn_dt_bias = dt + jnp.log(-jnp.expm1(-dt))
    gdn_norm = 1.0 + nrm(ks[11], (DEPTH, HEAD_DIM), 0.02)
    hy_conv = nrm(ks[12], (DEPTH, HYENA_CONV, 3 * D_HYENA), HYENA_CONV ** -0.5)
    hy_w1 = nrm(ks[13], (DEPTH, FILTER_EMB, FILTER_HIDDEN), FILTER_EMB ** -0.5)
    hy_b1 = nrm(ks[14], (DEPTH, FILTER_HIDDEN), 0.02)
    hy_freq = 1.0 + nrm(ks[15], (DEPTH, FILTER_HIDDEN), 0.02)
    hy_w2 = nrm(ks[16], (DEPTH, FILTER_HIDDEN, FILTER_HIDDEN), FILTER_HIDDEN ** -0.5)
    hy_b2 = nrm(ks[17], (DEPTH, FILTER_HIDDEN), 0.02)
    hy_w3 = nrm(ks[18], (DEPTH, FILTER_HIDDEN, HYENA_ORDER * 2 * D_HYENA), FILTER_HIDDEN ** -0.5)
    hy_skip = nrm(ks[19], (DEPTH, HYENA_ORDER, D_HYENA), 0.5)
    f_bias = jnp.linspace(3.0, 6.0, MLSTM_HEADS, dtype=f32)
    zero_b = jnp.zeros((MLSTM_HEADS,), f32)
    ml_gate_bias = jnp.stack([zero_b, f_bias, zero_b, f_bias])[None] + nrm(ks[20], (DEPTH, 4, MLSTM_HEADS), 0.1)
    ml_norm = 1.0 + nrm(ks[21], (DEPTH, HEAD_DIM), 0.02)
    w_out = nrm(ks[22], (DEPTH, D_MIX, D_MODEL), D_MIX ** -0.5)
    final_norm = 1.0 + nrm(ks[23], (D_MODEL,), 0.02)
    return {"x": x, "c": c, "ctx": ctx, "c_ctx": c_ctx, "norm_w": norm_w, "mod_w": mod_w, "mod_b": mod_b,
            "w_in": w_in, "gdn_conv": gdn_conv, "gdn_a_log": gdn_a_log, "gdn_dt_bias": gdn_dt_bias,
            "gdn_norm": gdn_norm, "hy_conv": hy_conv, "hy_w1": hy_w1, "hy_b1": hy_b1, "hy_freq": hy_freq,
            "hy_w2": hy_w2, "hy_b2": hy_b2, "hy_w3": hy_w3, "hy_skip": hy_skip, "ml_gate_bias": ml_gate_bias,
            "ml_norm": ml_norm, "w_out": w_out, "final_norm": final_norm}


def reference(x, c, ctx, c_ctx, norm_w, mod_w, mod_b, w_in, gdn_conv, gdn_a_log, gdn_dt_bias, gdn_norm,
              hy_conv, hy_w1, hy_b1, hy_freq, hy_w2, hy_b2, hy_w3, hy_skip, ml_gate_bias, ml_norm, w_out,
              final_norm):
    for layer in range(DEPTH):
        last = layer == DEPTH - 1
        sh, sc, gt = adaln(c[:, None, :], mod_w[layer], mod_b[layer])
        sh_c, sc_c, gt_c = adaln(c_ctx[None, None, :], mod_w[layer], mod_b[layer])
        u = (rms_norm(x, norm_w[layer]) * (1 + sc) + sh) @ w_in[layer]
        uc = (rms_norm(ctx, norm_w[layer]) * (1 + sc_c) + sh_c) @ w_in[layer]
        g_qkv, g_z, g_ab, h_p, h_z, m_qkv, m_o, m_z, m_g = split_cols(u)
        gc_qkv, gc_z, gc_ab, hc_p, hc_z, mc_qkv, mc_o, mc_z, mc_g = split_cols(uc)
        hy_params = (hy_conv[layer], hy_w1[layer], hy_b1[layer], hy_freq[layer], hy_w2[layer], hy_b2[layer],
                     hy_w3[layer], hy_skip[layer])
        y_gdn, y_gdn_c = gdn_branch(g_qkv, g_ab, g_z, gc_qkv, gc_ab, gc_z, gdn_conv[layer], gdn_a_log[layer],
                                    gdn_dt_bias[layer], gdn_norm[layer], not last)
        y_hy = hyena_branch(h_p, h_z, *hy_params, True)
        y_ml, y_ml_c = mlstm_branch(m_qkv, m_o, m_g, m_z, mc_qkv, mc_o, mc_g, mc_z, ml_gate_bias[layer],
                                    ml_norm[layer], not last)
        x = x + gt * (jnp.concatenate([y_gdn, y_hy, y_ml], axis=-1) @ w_out[layer])
        if not last:
            y_hy_c = hyena_branch(hc_p, hc_z, *hy_params, False)
            ctx = ctx + gt_c * (jnp.concatenate([y_gdn_c, y_hy_c, y_ml_c], axis=-1) @ w_out[layer])
    return rms_norm(x, final_norm)
```

```python
import functools
import math

import jax
import jax.numpy as jnp
from jax import lax
from jax.experimental import pallas as pl
from jax.experimental.pallas import tpu as pltpu

HEAD_DIM = 128
CHUNK = 64
LANES = 128
NORM_EPS = 1e-6
HYENA_ORDER = 2
FILTER_BANDS = 16
DECAY_TARGET = 1e-2
MIN_DECAY = math.log(DECAY_TARGET) / 1.5
MAX_DECAY = math.log(DECAY_TARGET) / 0.3
COND_ROWS = 16

F32 = jnp.float32
BF16 = jnp.bfloat16
HI = lax.Precision.HIGHEST


def _cparams(sem, vmem_mb):
    return pltpu.CompilerParams(dimension_semantics=sem, vmem_limit_bytes=vmem_mb << 20)


def _dot(a, b):
    return jnp.dot(a.astype(BF16), b.astype(BF16), preferred_element_type=F32)


def _dot_hi(a, b):
    return jnp.dot(a, b, precision=HI, preferred_element_type=F32)


def _dot_nt(a, b):
    return lax.dot_general(a.astype(BF16), b.astype(BF16), (((1,), (1,)), ((), ())),
                           preferred_element_type=F32)


def _dot_tn(a, b):
    return lax.dot_general(a.astype(BF16), b.astype(BF16), (((0,), (0,)), ((), ())),
                           preferred_element_type=F32)


def _silu(x):
    return x * jax.nn.sigmoid(x)


def _softplus(x):
    return jnp.maximum(x, 0.0) + jnp.log(1.0 + jnp.exp(-jnp.abs(x)))


def _pick_tile(n, cap, unit):
    t = (min(n, cap) // unit) * unit
    while n % t:
        t -= unit
    return t


def _adaln_kernel(c_ref, w_ref, b_ref, o_ref):
    o_ref[...] = _dot_hi(_silu(c_ref[...]), w_ref[...]) + b_ref[...]


def _adaln(cond, w, b):
    d, n = w.shape
    tn = _pick_tile(n, 768, LANES)
    return pl.pallas_call(
        _adaln_kernel,
        grid=(n // tn,),
        in_specs=[pl.BlockSpec((COND_ROWS, d), lambda j: (0, 0)),
                  pl.BlockSpec((d, tn), lambda j: (0, j)),
                  pl.BlockSpec((1, tn), lambda j: (0, j))],
        out_specs=pl.BlockSpec((COND_ROWS, tn), lambda j: (0, j)),
        out_shape=jax.ShapeDtypeStruct((COND_ROWS, n), F32),
        compiler_params=_cparams(("parallel",), 40),
        name="adaln",
    )(cond, w, b)


def _inproj_kernel(x_ref, nw_ref, sc_ref, sh_ref, w_ref, o_ref, xn_ref):
    @pl.when(pl.program_id(2) == 0)
    def _():
        x = x_ref[0]
        r = lax.rsqrt(jnp.mean(x * x, axis=-1, keepdims=True) + NORM_EPS)
        y = (x * r * nw_ref[...]) * (1.0 + sc_ref[0]) + sh_ref[0]
        xn_ref[...] = y.astype(BF16)

    o_ref[0] = jnp.dot(xn_ref[...], w_ref[...], preferred_element_type=F32)


def _inproj(x, nw, sc, sh, wp):
    b, l, d = x.shape
    n = wp.shape[1]
    tm = _pick_tile(l, 1024, 8)
    tn = _pick_tile(n, 1280, LANES)
    return pl.pallas_call(
        _inproj_kernel,
        grid=(b, l // tm, n // tn),
        in_specs=[pl.BlockSpec((1, tm, d), lambda i, m, j: (i, m, 0)),
                  pl.BlockSpec((1, d), lambda i, m, j: (0, 0)),
                  pl.BlockSpec((1, 1, d), lambda i, m, j: (i, 0, 0)),
                  pl.BlockSpec((1, 1, d), lambda i, m, j: (i, 0, 0)),
                  pl.BlockSpec((d, tn), lambda i, m, j: (0, j))],
        out_specs=pl.BlockSpec((1, tm, tn), lambda i, m, j: (i, m, j)),
        out_shape=jax.ShapeDtypeStruct((b, l, n), F32),
        scratch_shapes=[pltpu.VMEM((tm, d), BF16)],
        compiler_params=_cparams(("parallel", "parallel", "arbitrary"), 56),
        name="inproj",
    )(x, nw, sc, sh, wp)


def _outproj_kernel(x_ref, yg_ref, yh_ref, ym_ref, wg_ref, wh_ref, wm_ref, gt_ref, fw_ref, o_ref, *, final):
    acc = jnp.dot(yg_ref[0], wg_ref[...], preferred_element_type=F32)
    acc = acc + jnp.dot(yh_ref[0], wh_ref[...], preferred_element_type=F32)
    acc = acc + jnp.dot(ym_ref[0], wm_ref[...], preferred_element_type=F32)
    xn = x_ref[0] + gt_ref[0] * acc
    if final:
        r = lax.rsqrt(jnp.mean(xn * xn, axis=-1, keepdims=True) + NORM_EPS)
        xn = xn * r * fw_ref[...]
    o_ref[0] = xn


def _outproj(x, yg, yh, ym, wg, wh, wm, gt, fw, final):
    b, l, d = x.shape
    tm = _pick_tile(l, 512, 8)
    row = lambda w: pl.BlockSpec((1, tm, w), lambda i, m: (i, m, 0))
    full = lambda a: pl.BlockSpec(a.shape, lambda i, m: (0, 0))
    return pl.pallas_call(
        functools.partial(_outproj_kernel, final=final),
        grid=(b, l // tm),
        in_specs=[row(d), row(yg.shape[2]), row(yh.shape[2]), row(ym.shape[2]),
                  full(wg), full(wh), full(wm),
                  pl.BlockSpec((1, 1, d), lambda i, m: (i, 0, 0)),
                  pl.BlockSpec((1, d), lambda i, m: (0, 0))],
        out_specs=row(d),
        out_shape=jax.ShapeDtypeStruct((b, l, d), F32),
        compiler_params=_cparams(("parallel", "parallel"), 48),
        name="outproj",
    )(x, yg, yh, ym, wg, wh, wm, gt, fw)


def _conv_rows(x, w, period):
    rows = x.shape[0]
    taps = w.shape[0]
    pad = taps // 2
    pos = lax.broadcasted_iota(jnp.int32, x.shape, 0) % period
    y = None
    for j in range(taps):
        off = j - pad
        if off == 0:
            term = x * w[j:j + 1]
        else:
            shifted = pltpu.roll(x, (-off) % rows, axis=0)
            ok = (pos >= -off) if off < 0 else (pos < period - off)
            term = jnp.where(ok, shifted, 0.0) * w[j:j + 1]
        y = term if y is None else y + term
    return y


def _seg_cumsum(x, reverse):
    lane = lax.broadcasted_iota(jnp.int32, x.shape, 1) % CHUNK
    s = 1
    while s < CHUNK:
        if reverse:
            shifted = pltpu.roll(x, LANES - s, axis=1)
            ok = lane < CHUNK - s
        else:
            shifted = pltpu.roll(x, s, axis=1)
            ok = lane >= s
        x = x + jnp.where(ok, shifted, 0.0)
        s *= 2
    return x


def _chunk_masks(d):
    ii = lax.broadcasted_iota(jnp.int32, (CHUNK, CHUNK), 0)
    jj = lax.broadcasted_iota(jnp.int32, (CHUNK, CHUNK), 1)
    eye = ii == jj
    if d == 0:
        return eye, jj <= ii, jj < ii
    return eye, jj >= ii, jj > ii


def _to_col(row, eye):
    return jnp.sum(jnp.where(eye, jnp.broadcast_to(row, (CHUNK, CHUNK)), 0.0), axis=1, keepdims=True)


def _row_slice(rows_ref, idx, p, half):
    return rows_ref[idx, pl.ds(p, 1), half * CHUNK:(half + 1) * CHUNK]


def _head_out_tiles(acc_ref, nw_ref, z_ref, y_ref, n_rows, og_ref=None):
    tile = _pick_tile(n_rows, 256, 8)

    def body(i, carry):
        r = pl.multiple_of(i * tile, tile)
        o = acc_ref[pl.ds(r, tile), :]
        if og_ref is not None:
            o = jax.nn.sigmoid(og_ref[0, pl.ds(r, tile), :]) * o
        o = o * lax.rsqrt(jnp.mean(o * o, axis=-1, keepdims=True) + NORM_EPS) * nw_ref[...]
        y_ref[0, pl.ds(r, tile), :] = (o * _silu(z_ref[0, pl.ds(r, tile), :])).astype(y_ref.dtype)
        return carry

    lax.fori_loop(0, n_rows // tile, body, 0)


def _ml_rows(g_refs, bias_ref, h, n_heads, rows_ref):
    li_f = g_refs[0][0, 0] + bias_ref[h]
    lf_f = -_softplus(-(g_refs[1][0, 0] + bias_ref[n_heads + h]))
    li_b = g_refs[2][0, 0] + bias_ref[2 * n_heads + h]
    lf_b = -_softplus(-(g_refs[3][0, 0] + bias_ref[3 * n_heads + h]))
    rows_ref[0] = _seg_cumsum(lf_f, False)
    rows_ref[1] = li_f
    rows_ref[2] = _seg_cumsum(lf_b, True)
    rows_ref[3] = li_b


def _ml_chunk(q_ref, k_ref, v_ref, rows_ref, acc_ref, cst, nst, mst, d, p, half):
    eye, incl, _ = _chunk_masks(d)
    r0 = pl.multiple_of(p * LANES + half * CHUNK, CHUNK)
    q = q_ref[0, pl.ds(r0, CHUNK), :]
    k = k_ref[0, pl.ds(r0, CHUNK), :] * (HEAD_DIM ** -0.5)
    v = v_ref[0, pl.ds(r0, CHUNK), :]
    b_row = _row_slice(rows_ref, 2 * d, p, half)
    li_row = _row_slice(rows_ref, 2 * d + 1, p, half)
    b_tot = b_row[:, CHUNK - 1:CHUNK] if d == 0 else b_row[:, 0:1]
    b_col = _to_col(b_row, eye)
    end_row = b_tot - b_row + li_row
    e_max = jnp.max(end_row, axis=1, keepdims=True)
    end_col = _to_col(end_row, eye)
    dlog = jnp.where(incl, b_col - b_row + li_row, -jnp.inf)
    rowmax = jnp.max(dlog, axis=1, keepdims=True)
    pp = jnp.exp(dlog - rowmax) * _dot_nt(q, k)
    pv = _dot(pp, v)
    psum = jnp.sum(pp, axis=1, keepdims=True)
    kw = k * jnp.exp(end_col - e_max)
    d_c = _dot_tn(kw, v)
    d_n = jnp.sum(kw, axis=0, keepdims=True)

    c_m = cst[d]
    n_v = nst[d, 0:1, :]
    m_s = mst[d, 0:1, 0:1]
    inter = b_col + m_s
    m_i = jnp.maximum(inter, rowmax)
    w_inter = jnp.exp(inter - m_i)
    s_intra = jnp.exp(rowmax - m_i)
    num = w_inter * _dot(q, c_m) + s_intra * pv
    den = w_inter * jnp.sum(q * n_v, axis=1, keepdims=True) + s_intra * psum
    hh = num / jnp.maximum(jnp.abs(den), jnp.exp(-m_i))
    acc_ref[pl.ds(r0, CHUNK), :] += hh

    carry_log = b_tot + m_s
    m_new = jnp.maximum(carry_log, e_max)
    dec = jnp.exp(carry_log - m_new)
    scl = jnp.exp(e_max - m_new)
    cst[d] = dec * c_m + scl * d_c
    nst[d, 0:1, :] = dec * n_v + scl * d_n
    mst[d] = jnp.broadcast_to(m_new, mst.shape[1:])


def _ml_scan(q_ref, k_ref, v_ref, rows_ref, acc_ref, cst, nst, mst, n_pairs):
    acc_ref[...] = jnp.zeros(acc_ref.shape, F32)

    def body(p, carry):
        pb = n_pairs - 1 - p
        _ml_chunk(q_ref, k_ref, v_ref, rows_ref, acc_ref, cst, nst, mst, 0, p, 0)
        _ml_chunk(q_ref, k_ref, v_ref, rows_ref, acc_ref, cst, nst, mst, 1, pb, 1)
        _ml_chunk(q_ref, k_ref, v_ref, rows_ref, acc_ref, cst, nst, mst, 0, p, 1)
        _ml_chunk(q_ref, k_ref, v_ref, rows_ref, acc_ref, cst, nst, mst, 1, pb, 0)
        return carry

    lax.fori_loop(0, n_pairs, body, 0)


def _mlstm_kernel(bias_ref, ql, kl, vl, ol, zl, qc, kc, vc, oc, zc, gl0, gl1, gl2, gl3, gc0, gc1, gc2, gc3,
                  nw_ref, *rest, n_heads, ctx_out):
    if ctx_out:
        yl_ref, yc_ref, cst, nst, mst, rows_l, rows_c, acc_l, acc_c = rest
    else:
        yl_ref, cst, nst, mst, rows_l, rows_c, acc_l, acc_c = rest
    h = pl.program_id(1)
    cst[...] = jnp.zeros(cst.shape, F32)
    nst[...] = jnp.zeros(nst.shape, F32)
    mst[...] = jnp.zeros(mst.shape, F32)
    _ml_rows((gc0, gc1, gc2, gc3), bias_ref, h, n_heads, rows_c)
    _ml_scan(qc, kc, vc, rows_c, acc_c, cst, nst, mst, rows_c.shape[1])
    _ml_rows((gl0, gl1, gl2, gl3), bias_ref, h, n_heads, rows_l)
    _ml_scan(ql, kl, vl, rows_l, acc_l, cst, nst, mst, rows_l.shape[1])
    _head_out_tiles(acc_l, nw_ref, zl, yl_ref, acc_l.shape[0], og_ref=ol)
    if ctx_out:
        _head_out_tiles(acc_c, nw_ref, zc, yc_ref, acc_c.shape[0], og_ref=oc)


def _head_block(l, blk):
    return pl.BlockSpec((1, l, HEAD_DIM), lambda b, h, blk=blk: (b, 0, blk + h))


def _gate_block(n_pairs, j, n_heads):
    return pl.BlockSpec((1, 1, n_pairs, LANES), lambda b, h, j=j: (b, j * n_heads + h, 0, 0))


def _mlstm(u_l, u_c, gt_l, gt_c, bias, nw, lay, ctx_out):
    b, l, _ = u_l.shape
    lc = u_c.shape[1]
    nh = lay["mh"]
    blk = lambda off: off // HEAD_DIM
    offs = [lay["MQ"], lay["MQ"] + lay["DM"], lay["MQ"] + 2 * lay["DM"], lay["MO"], lay["MZ"]]
    in_specs = [pl.BlockSpec(memory_space=pltpu.SMEM)]
    in_specs += [_head_block(l, blk(o)) for o in offs]
    in_specs += [_head_block(lc, blk(o)) for o in offs]
    in_specs += [_gate_block(l // LANES, j, nh) for j in range(4)]
    in_specs += [_gate_block(lc // LANES, j, nh) for j in range(4)]
    in_specs += [pl.BlockSpec((1, HEAD_DIM), lambda b_, h: (0, 0))]
    out_specs = [pl.BlockSpec((1, l, HEAD_DIM), lambda b_, h: (b_, 0, h))]
    out_shape = [jax.ShapeDtypeStruct((b, l, nh * HEAD_DIM), BF16)]
    if ctx_out:
        out_specs.append(pl.BlockSpec((1, lc, HEAD_DIM), lambda b_, h: (b_, 0, h)))
        out_shape.append(jax.ShapeDtypeStruct((b, lc, nh * HEAD_DIM), BF16))
    scratch = [pltpu.VMEM((2, HEAD_DIM, HEAD_DIM), F32), pltpu.VMEM((2, 8, LANES), F32),
               pltpu.VMEM((2, 8, LANES), F32),
               pltpu.VMEM((4, l // LANES, LANES), F32), pltpu.VMEM((4, lc // LANES, LANES), F32),
               pltpu.VMEM((l, HEAD_DIM), F32), pltpu.VMEM((lc, HEAD_DIM), F32)]
    res = pl.pallas_call(
        functools.partial(_mlstm_kernel, n_heads=nh, ctx_out=ctx_out),
        grid=(b, nh),
        in_specs=in_specs, out_specs=out_specs, out_shape=out_shape,
        scratch_shapes=scratch,
        compiler_params=_cparams(("parallel", "parallel"), 48),
        name="mlstm",
    )(bias.reshape(-1), *([u_l] * 5), *([u_c] * 5), *([gt_l] * 4), *([gt_c] * 4), nw)
    return (res[0], res[1]) if ctx_out else (res[0], None)


def _gdn_rows(g_refs, alog_ref, dtb_ref, h, n_heads, rows_ref):
    g_f = -jnp.exp(alog_ref[h]) * _softplus(g_refs[0][0, 0] + dtb_ref[h])
    g_b = -jnp.exp(alog_ref[n_heads + h]) * _softplus(g_refs[1][0, 0] + dtb_ref[n_heads + h])
    rows_ref[0] = _seg_cumsum(g_f, False)
    rows_ref[1] = jax.nn.sigmoid(g_refs[2][0, 0])
    rows_ref[2] = _seg_cumsum(g_b, True)
    rows_ref[3] = jax.nn.sigmoid(g_refs[3][0, 0])


def _gdn_prep(src_refs, w_refs, dst_refs, n_units, unit_rows, period):
    def body(i, carry):
        r = pl.multiple_of(i * unit_rows, unit_rows)
        for idx in range(3):
            t = _silu(_conv_rows(src_refs[idx][0, pl.ds(r, unit_rows), :], w_refs[idx][...], period))
            if idx < 2:
                t = t * lax.rsqrt(jnp.sum(t * t, axis=-1, keepdims=True) + NORM_EPS)
            if idx == 0:
                t = t * (HEAD_DIM ** -0.5)
            dst_refs[idx][pl.ds(r, unit_rows), :] = t
        return carry

    lax.fori_loop(0, n_units, body, 0)


def _gdn_chunk(qs, ks, vs, rows_ref, acc_ref, sst, d, p, half):
    eye, incl, strict = _chunk_masks(d)
    r0 = pl.multiple_of(p * LANES + half * CHUNK, CHUNK)
    q = qs[pl.ds(r0, CHUNK), :]
    k = ks[pl.ds(r0, CHUNK), :]
    v = vs[pl.ds(r0, CHUNK), :]
    g_row = _row_slice(rows_ref, 2 * d, p, half)
    beta_col = _to_col(_row_slice(rows_ref, 2 * d + 1, p, half), eye)
    g_tot = g_row[:, CHUNK - 1:CHUNK] if d == 0 else g_row[:, 0:1]
    g_col = _to_col(g_row, eye)
    dec = jnp.exp(jnp.where(incl, g_col - g_row, -jnp.inf))
    m_low = jnp.where(strict, beta_col * _dot_nt(k, k) * dec, 0.0)
    eye_f = jnp.where(eye, 1.0, 0.0)
    t_inv = eye_f - m_low
    pw = m_low
    n_sq = 1
    while 2 * n_sq < CHUNK:
        pw = _dot_hi(pw, pw)
        t_inv = _dot_hi(t_inv, eye_f + pw)
        n_sq *= 2
    e_g = jnp.exp(g_col)
    rhs = jnp.concatenate([beta_col * v, (beta_col * e_g) * k], axis=1)
    sol = _dot_hi(t_inv, rhs)
    u_t = sol[:, :HEAD_DIM]
    w_m = sol[:, HEAD_DIM:]
    a_qk = _dot_nt(q, k) * dec
    q_dec = q * e_g
    k_end = k * jnp.exp(g_tot - g_col)

    s = sst[d]
    u = u_t - _dot(w_m, s)
    o = _dot(q_dec, s) + _dot(a_qk, u)
    sst[d] = jnp.exp(g_tot) * s + _dot_tn(k_end, u)
    acc_ref[pl.ds(r0, CHUNK), :] += o


def _gdn_scan(qs, ks, vs, rows_ref, acc_ref, sst, n_pairs):
    acc_ref[...] = jnp.zeros(acc_ref.shape, F32)

    def body(p, carry):
        pb = n_pairs - 1 - p
        _gdn_chunk(qs, ks, vs, rows_ref, acc_ref, sst, 0, p, 0)
        _gdn_chunk(qs, ks, vs, rows_ref, acc_ref, sst, 1, pb, 1)
        _gdn_chunk(qs, ks, vs, rows_ref, acc_ref, sst, 0, p, 1)
        _gdn_chunk(qs, ks, vs, rows_ref, acc_ref, sst, 1, pb, 0)
        return carry

    lax.fori_loop(0, n_pairs, body, 0)


def _gdn_kernel(alog_ref, dtb_ref, ql, kl, vl, zl, qc, kc, vc, zc, gl0, gl1, gl2, gl3, gc0, gc1, gc2, gc3,
                wq, wk, wv, nw_ref, *rest, n_heads, ctx_out):
    if ctx_out:
        yl_ref, yc_ref, sst, rows_l, rows_c, acc_l, acc_c, qs, ks, vs, qsc, ksc, vsc = rest
    else:
        yl_ref, sst, rows_l, rows_c, acc_l, acc_c, qs, ks, vs, qsc, ksc, vsc = rest
    h = pl.program_id(1)
    n_l = acc_l.shape[0]
    n_c = acc_c.shape[0]
    sst[...] = jnp.zeros(sst.shape, F32)
    _gdn_rows((gc0, gc1, gc2, gc3), alog_ref, dtb_ref, h, n_heads, rows_c)
    _gdn_prep((qc, kc, vc), (wq, wk, wv), (qsc, ksc, vsc), 1, n_c, n_c)
    _gdn_scan(qsc, ksc, vsc, rows_c, acc_c, sst, rows_c.shape[1])
    _gdn_rows((gl0, gl1, gl2, gl3), alog_ref, dtb_ref, h, n_heads, rows_l)
    _gdn_prep((ql, kl, vl), (wq, wk, wv), (qs, ks, vs), n_l // CHUNK, CHUNK, CHUNK)
    _gdn_scan(qs, ks, vs, rows_l, acc_l, sst, rows_l.shape[1])
    _head_out_tiles(acc_l, nw_ref, zl, yl_ref, n_l)
    if ctx_out:
        _head_out_tiles(acc_c, nw_ref, zc, yc_ref, n_c)


def _gdn(u_l, u_c, gt_l, gt_c, conv_w, a_log, dt_bias, nw, lay, ctx_out):
    b, l, _ = u_l.shape
    lc = u_c.shape[1]
    nh = lay["gh"]
    taps = conv_w.shape[0]
    blk = lambda off: off // HEAD_DIM
    offs = [lay["GQ"], lay["GQ"] + lay["DG"], lay["GQ"] + 2 * lay["DG"], lay["GZ"]]
    smem = pl.BlockSpec(memory_space=pltpu.SMEM)
    in_specs = [smem, smem]
    in_specs += [_head_block(l, blk(o)) for o in offs]
    in_specs += [_head_block(lc, blk(o)) for o in offs]
    in_specs += [_gate_block(l // LANES, j, nh) for j in range(4)]
    in_specs += [_gate_block(lc // LANES, j, nh) for j in range(4)]
    in_specs += [pl.BlockSpec((taps, HEAD_DIM), lambda b_, h, j=j: (0, j * nh + h)) for j in range(3)]
    in_specs += [pl.BlockSpec((1, HEAD_DIM), lambda b_, h: (0, 0))]
    out_specs = [pl.BlockSpec((1, l, HEAD_DIM), lambda b_, h: (b_, 0, h))]
    out_shape = [jax.ShapeDtypeStruct((b, l, nh * HEAD_DIM), BF16)]
    if ctx_out:
        out_specs.append(pl.BlockSpec((1, lc, HEAD_DIM), lambda b_, h: (b_, 0, h)))
        out_shape.append(jax.ShapeDtypeStruct((b, lc, nh * HEAD_DIM), BF16))
    scratch = [pltpu.VMEM((2, HEAD_DIM, HEAD_DIM), F32),
               pltpu.VMEM((4, l // LANES, LANES), F32), pltpu.VMEM((4, lc // LANES, LANES), F32),
               pltpu.VMEM((l, HEAD_DIM), F32), pltpu.VMEM((lc, HEAD_DIM), F32)]
    scratch += [pltpu.VMEM((l, HEAD_DIM), F32)] * 3 + [pltpu.VMEM((lc, HEAD_DIM), F32)] * 3
    res = pl.pallas_call(
        functools.partial(_gdn_kernel, n_heads=nh, ctx_out=ctx_out),
        grid=(b, nh),
        in_specs=in_specs, out_specs=out_specs, out_shape=out_shape,
        scratch_shapes=scratch,
        compiler_params=_cparams(("parallel", "parallel"), 48),
        name="gdn",
    )(a_log.reshape(-1), dt_bias.reshape(-1), *([u_l] * 4), *([u_c] * 4), *([gt_l] * 4), *([gt_c] * 4),
      *([conv_w] * 3), nw)
    return (res[0], res[1]) if ctx_out else (res[0], None)


def _dft_tables(l):
    k = jnp.arange(l, dtype=jnp.int32)
    ang = ((k[:, None] * k[None, :]) % (2 * l)).astype(F32) * (math.pi / l)
    alt = jnp.where(k % 2 == 0, 1.0, -1.0).astype(F32)
    sin_f = jnp.sin(ang).at[0, :].set(alt)
    return jnp.cos(ang).astype(BF16), sin_f.astype(BF16), sin_f.T.astype(BF16)


def _hy_prep_kernel(p_ref, w_ref, x1_ref, x2_ref, v_ref, vb_ref, *, period, dh):
    y = _conv_rows(p_ref[0], w_ref[...], period)
    x1_ref[0] = y[:, :dh]
    x2_ref[0] = y[:, dh:2 * dh]
    v = y[:, 2 * dh:]
    v_ref[0] = v
    vb_ref[0] = v.astype(BF16)


def _hy_prep(u, conv_w, dh, grid_mask):
    b, l, _ = u.shape
    taps = conv_w.shape[0]
    tr = _pick_tile(l, 256, CHUNK)
    period = CHUNK if grid_mask else l
    if not grid_mask:
        tr = l
    o_spec = pl.BlockSpec((1, tr, dh), lambda i, m: (i, m, 0))
    return pl.pallas_call(
        functools.partial(_hy_prep_kernel, period=period, dh=dh),
        grid=(b, l // tr),
        in_specs=[pl.BlockSpec((1, tr, 3 * dh), lambda i, m: (i, m, 0)),
                  pl.BlockSpec((taps, 3 * dh), lambda i, m: (0, 0))],
        out_specs=[o_spec] * 4,
        out_shape=[jax.ShapeDtypeStruct((b, l, dh), F32)] * 3 + [jax.ShapeDtypeStruct((b, l, dh), BF16)],
        compiler_params=_cparams(("parallel", "parallel"), 40),
        name="hy_prep",
    )(u, conv_w)


def _filter_kernel(feats_ref, w1_ref, b1_ref, fr_ref, w2_ref, b2_ref, w3c_ref, w3a_ref, dl_ref,
                   hs_ref, hd_ref, nyq_ref, hid_ref):
    @pl.when((pl.program_id(0) == 0) & (pl.program_id(1) == 0))
    def _():
        hid = jnp.sin(fr_ref[...] * (_dot_hi(feats_ref[...], w1_ref[...]) + b1_ref[...]))
        hid_ref[...] = jnp.sin(fr_ref[...] * (_dot_hi(hid, w2_ref[...]) + b2_ref[...]))

    hid = hid_ref[...]
    win = jnp.exp(-feats_ref[:, 0:1] * dl_ref[...])

    def one(w3_ref):
        f = _dot_hi(hid, w3_ref[...]) * win
        return f / (jnp.sum(jnp.abs(f), axis=0, keepdims=True) + NORM_EPS)

    h_c = one(w3c_ref)
    h_a = one(w3a_ref)
    h_sum = h_c + h_a
    sign = jnp.where(lax.broadcasted_iota(jnp.int32, h_sum.shape, 0) % 2 == 0, 1.0, -1.0)
    nyq_ref[0] = jnp.broadcast_to(jnp.sum(h_sum * sign, axis=0, keepdims=True), nyq_ref.shape[1:])
    hs_ref[0] = h_sum.astype(BF16)
    hd_ref[0] = (h_c - h_a).astype(BF16)


def _hy_filters(l, w1, b1, freq, w2, b2, w3, dh):
    pos = jnp.arange(l, dtype=F32)
    t = pos / max(l - 1, 1)
    ang = 2.0 * math.pi * pos / l
    bands = jnp.linspace(1e-4, FILTER_BANDS - 1, FILTER_BANDS, dtype=F32)
    feats = jnp.concatenate([t[:, None], jnp.cos(ang[:, None] * bands), -jnp.sin(ang[:, None] * bands)], axis=-1)
    n_emb, n_hid = w1.shape
    feats = jnp.pad(feats, ((0, 0), (0, LANES - n_emb)))
    pc = LANES - n_hid
    w1p = jnp.pad(w1, ((0, LANES - n_emb), (0, pc)))
    w2p = jnp.pad(w2, ((0, pc), (0, pc)))
    w3p = jnp.pad(w3, ((0, pc), (0, 0)))
    row = lambda a: jnp.pad(a, (0, pc))[None]
    deltas = jnp.abs(jnp.linspace(MIN_DECAY, MAX_DECAY, dh, dtype=F32))[None]
    nct = dh // LANES
    const = lambda shape: pl.BlockSpec(shape, lambda o, c: (0, 0))
    o_spec = pl.BlockSpec((1, l, LANES), lambda o, c: (o, 0, c))
    return pl.pallas_call(
        _filter_kernel,
        grid=(HYENA_ORDER, nct),
        in_specs=[const((l, LANES)), const((LANES, LANES)), const((1, LANES)), const((1, LANES)),
                  const((LANES, LANES)), const((1, LANES)),
                  pl.BlockSpec((LANES, LANES), lambda o, c: (0, o * 2 * nct + c)),
                  pl.BlockSpec((LANES, LANES), lambda o, c: (0, o * 2 * nct + nct + c)),
                  pl.BlockSpec((1, LANES), lambda o, c: (0, c))],
        out_specs=[o_spec, o_spec, pl.BlockSpec((1, 8, LANES), lambda o, c: (o, 0, c))],
        out_shape=[jax.ShapeDtypeStruct((HYENA_ORDER, l, dh), BF16)] * 2
        + [jax.ShapeDtypeStruct((HYENA_ORDER, 8, dh), F32)],
        scratch_shapes=[pltpu.VMEM((l, LANES), F32)],
        compiler_params=_cparams(("arbitrary", "arbitrary"), 48),
        name="hy_filter",
    )(feats, w1p, row(b1), row(freq), w2p, row(b2), w3p, w3p, deltas)


def _spectrum_kernel(c_ref, s_ref, hs_ref, hd_ref, nyq_ref, a_ref, a2_ref, hsin_ref, *, l, tk):
    a = jnp.dot(c_ref[...], hs_ref[0], preferred_element_type=F32)
    s = jnp.dot(s_ref[...], hd_ref[0], preferred_element_type=F32)
    is0 = (lax.broadcasted_iota(jnp.int32, a.shape, 0) + pl.program_id(1) * tk) == 0
    wk = jnp.where(is0, 0.5 / l, 1.0 / l)
    a_ref[0] = a * wk
    a2_ref[0] = jnp.where(is0, nyq_ref[0, 0:1, :], a) * wk
    hsin_ref[0] = jnp.where(is0, 0.0, s) * wk


def _hy_spectrum(tabs, hs, hd, nyq):
    cos_t, sin_f, _ = tabs
    _, l, dh = hs.shape
    tk = _pick_tile(l, 512, 8)
    tab = pl.BlockSpec((tk, l), lambda o, k: (k, 0))
    filt = pl.BlockSpec((1, l, dh), lambda o, k: (o, 0, 0))
    o_spec = pl.BlockSpec((1, tk, dh), lambda o, k: (o, k, 0))
    return pl.pallas_call(
        functools.partial(_spectrum_kernel, l=l, tk=tk),
        grid=(HYENA_ORDER, l // tk),
        in_specs=[tab, tab, filt, filt, pl.BlockSpec((1, 8, dh), lambda o, k: (o, 0, 0))],
        out_specs=[o_spec] * 3,
        out_shape=[jax.ShapeDtypeStruct((HYENA_ORDER, l, dh), F32)] * 3,
        compiler_params=_cparams(("parallel", "parallel"), 48),
        name="hy_spectrum",
    )(cos_t, sin_f, hs, hd, nyq)


def _hy_fwd_kernel(c_ref, s_ref, v_ref, a_ref, a2_ref, hsin_ref, yc_ref, ys_ref):
    v = v_ref[0]
    xc = jnp.dot(c_ref[...], v, preferred_element_type=F32)
    xs = jnp.dot(s_ref[...], v, preferred_element_type=F32)
    hsin = hsin_ref[0]
    yc_ref[0] = (xc * a_ref[0] - xs * hsin).astype(BF16)
    ys_ref[0] = (xc * hsin + xs * a2_ref[0]).astype(BF16)


def _hy_fwd(tabs, vb, spec, order):
    cos_t, sin_f, _ = tabs
    b, l, dh = vb.shape
    tk = _pick_tile(l, 512, 8)
    tab = pl.BlockSpec((tk, l), lambda k, i: (k, 0))
    filt = pl.BlockSpec((1, tk, dh), lambda k, i: (order, k, 0))
    o_spec = pl.BlockSpec((1, tk, dh), lambda k, i: (i, k, 0))
    return pl.pallas_call(
        _hy_fwd_kernel,
        grid=(l // tk, b),
        in_specs=[tab, tab, pl.BlockSpec((1, l, dh), lambda k, i: (i, 0, 0)), filt, filt, filt],
        out_specs=[o_spec] * 2,
        out_shape=[jax.ShapeDtypeStruct((b, l, dh), BF16)] * 2,
        compiler_params=_cparams(("parallel", "parallel"), 48),
        name="hy_fwd",
    )(cos_t, sin_f, vb, *spec)


def _hy_inv_kernel(c_ref, s_ref, yc_ref, ys_ref, x_ref, v_ref, sk_ref, *rest, last):
    conv = jnp.dot(c_ref[...], yc_ref[0], preferred_element_type=F32)
    conv = conv + jnp.dot(s_ref[...], ys_ref[0], preferred_element_type=F32)
    y = x_ref[0] * (conv + sk_ref[0] * v_ref[0])
    if last:
        z_ref, o_ref = rest
        o_ref[0] = (y * _silu(z_ref[0])).astype(BF16)
    else:
        of_ref, ob_ref = rest
        of_ref[0] = y
        ob_ref[0] = y.astype(BF16)


def _hy_inv(tabs, yc, ys, x, v, skip, order, u=None, z_blk=None):
    cos_t, _, sin_i = tabs
    b, l, dh = yc.shape
    last = u is not None
    tt = _pick_tile(l, 512, 8)
    tab = pl.BlockSpec((tt, l), lambda t, i: (t, 0))
    full = pl.BlockSpec((1, l, dh), lambda t, i: (i, 0, 0))
    tile = pl.BlockSpec((1, tt, dh), lambda t, i: (i, t, 0))
    in_specs = [tab, tab, full, full, tile, tile, pl.BlockSpec((1, 1, dh), lambda t, i: (order, 0, 0))]
    args = [cos_t, sin_i, yc, ys, x, v, skip[:, None, :]]
    if last:
        in_specs.append(pl.BlockSpec((1, tt, dh), lambda t, i: (i, t, z_blk)))
        args.append(u)
        out_specs = tile
        out_shape = jax.ShapeDtypeStruct((b, l, dh), BF16)
    else:
        out_specs = [tile, tile]
        out_shape = [jax.ShapeDtypeStruct((b, l, dh), F32), jax.ShapeDtypeStruct((b, l, dh), BF16)]
    return pl.pallas_call(
        functools.partial(_hy_inv_kernel, last=last),
        grid=(l // tt, b),
        in_specs=in_specs, out_specs=out_specs, out_shape=out_shape,
        compiler_params=_cparams(("parallel", "parallel"), 48),
        name="hy_inv",
    )(*args)


def _hyena(u, tabs, conv_w, w1, b1, freq, w2, b2, w3, skip, lay, grid_mask):
    dh = lay["DH"]
    l = u.shape[1]
    x1, x2, v, vb = _hy_prep(u, conv_w, dh, grid_mask)
    spec = _hy_spectrum(tabs, *_hy_filters(l, w1, b1, freq, w2, b2, w3, dh))
    yc, ys = _hy_fwd(tabs, vb, spec, 0)
    y1, y1b = _hy_inv(tabs, yc, ys, x1, v, skip, 0)
    yc, ys = _hy_fwd(tabs, y1b, spec, 1)
    return _hy_inv(tabs, yc, ys, x2, y1, skip, 1, u=u, z_blk=lay["HZ"] // dh)


def _layout(d):
    dg, dh, dm = 3 * d // 8, d // 4, 3 * d // 8
    lay = {"DG": dg, "DH": dh, "DM": dm, "gh": dg // HEAD_DIM, "mh": dm // HEAD_DIM}
    lay["HZ"] = 3 * dh
    lay["GQ"] = 4 * dh
    lay["GZ"] = lay["GQ"] + 3 * dg
    lay["GAB"] = lay["GZ"] + dg
    lay["MQ"] = lay["GAB"] + LANES
    lay["MO"] = lay["MQ"] + 3 * dm
    lay["MZ"] = lay["MO"] + dm
    lay["MG"] = lay["MZ"] + dm
    lay["NP"] = lay["MG"] + LANES
    return lay


def _pack_w_in(w, lay):
    dg, dh, dm, gh, mh = lay["DG"], lay["DH"], lay["DM"], lay["gh"], lay["mh"]
    sizes = (3 * dg, dg, 4 * gh, 3 * dh, dh, 3 * dm, dm, dm, 4 * mh)
    offs = [0]
    for s in sizes:
        offs.append(offs[-1] + s)
    seg = [w[:, offs[i]:offs[i + 1]] for i in range(len(sizes))]
    g_qkv, g_z, g_ab, h_p, h_z, m_qkv, m_o, m_z, m_g = seg
    padl = lambda a: jnp.pad(a, ((0, 0), (0, LANES - a.shape[1])))
    return jnp.concatenate([h_p, h_z, g_qkv, g_z, padl(g_ab), m_qkv, m_o, m_z, padl(m_g)], axis=1).astype(BF16)


def _gate_rows(u, off, n):
    b, l, _ = u.shape
    return jnp.transpose(u[:, :, off:off + n], (0, 2, 1)).reshape(b, n, l // LANES, LANES)


def kernel(x, c, ctx, c_ctx, norm_w, mod_w, mod_b, w_in, gdn_conv, gdn_a_log, gdn_dt_bias, gdn_norm, hy_conv,
           hy_w1, hy_b1, hy_freq, hy_w2, hy_b2, hy_w3, hy_skip, ml_gate_bias, ml_norm, w_out, final_norm):
    b, l, d = x.shape
    lc = ctx.shape[1]
    depth = norm_w.shape[0]
    lay = _layout(d)
    dg, dh = lay["DG"], lay["DH"]
    assert b < COND_ROWS and l % LANES == 0 and lc % LANES == 0 and d % 1024 == 0
    cond = jnp.zeros((COND_ROWS, d), F32).at[:b].set(c).at[b].set(c_ctx)
    tabs_l = _dft_tables(l)
    tabs_c = _dft_tables(lc)
    for layer in range(depth):
        last = layer == depth - 1
        mods = _adaln(cond, mod_w[layer], mod_b[layer][None])
        sh, sc, gt = mods[:, :d], mods[:, d:2 * d], mods[:, 2 * d:]
        lat = lambda m: m[:b, None, :]
        cx = lambda m: jnp.broadcast_to(m[b][None, None, :], (b, 1, d))
        wp = _pack_w_in(w_in[layer], lay)
        nw = norm_w[layer][None]
        u_l = _inproj(x, nw, lat(sc), lat(sh), wp)
        u_c = _inproj(ctx, nw, cx(sc), cx(sh), wp)
        g_rows = lambda u, off, n: _gate_rows(u, off, n)
        yg_l, yg_c = _gdn(u_l, u_c, g_rows(u_l, lay["GAB"], 4 * lay["gh"]), g_rows(u_c, lay["GAB"], 4 * lay["gh"]),
                          gdn_conv[layer], gdn_a_log[layer], gdn_dt_bias[layer], gdn_norm[layer][None], lay,
                          not last)
        ym_l, ym_c = _mlstm(u_l, u_c, g_rows(u_l, lay["MG"], 4 * lay["mh"]), g_rows(u_c, lay["MG"], 4 * lay["mh"]),
                            ml_gate_bias[layer], ml_norm[layer][None], lay, not last)
        hy = (hy_conv[layer], hy_w1[layer], hy_b1[layer], hy_freq[layer], hy_w2[layer], hy_b2[layer],
              hy_w3[layer], hy_skip[layer])
        yh_l = _hyena(u_l, tabs_l, *hy, lay, True)
        wo = w_out[layer].astype(BF16)
        wg, wh, wm = wo[:dg], wo[dg:dg + dh], wo[dg + dh:]
        fw = final_norm[None]
        x = _outproj(x, yg_l, yh_l, ym_l, wg, wh, wm, lat(gt), fw, last)
        if not last:
            yh_c = _hyena(u_c, tabs_c, *hy, lay, False)
            ctx = _outproj(ctx, yg_c, yh_c, ym_c, wg, wh, wm, cx(gt), fw, False)
    return x
```

```python
import functools
import math

import jax
import jax.numpy as jnp
from jax import lax
from jax.experimental import pallas as pl
from jax.experimental.pallas import tpu as pltpu

HEAD_DIM = 128
CHUNK = 64
LANES = 128
NORM_EPS = 1e-6
HYENA_ORDER = 2
FILTER_BANDS = 16
DECAY_TARGET = 1e-2
MIN_DECAY = math.log(DECAY_TARGET) / 1.5
MAX_DECAY = math.log(DECAY_TARGET) / 0.3
COND_ROWS = 16

F32 = jnp.float32
BF16 = jnp.bfloat16
HI = lax.Precision.HIGHEST


def _cparams(sem, vmem_mb):
    return pltpu.CompilerParams(dimension_semantics=sem, vmem_limit_bytes=vmem_mb << 20)


def _dot(a, b):
    return jnp.dot(a.astype(BF16), b.astype(BF16), preferred_element_type=F32)


def _dot_hi(a, b):
    return jnp.dot(a, b, precision=HI, preferred_element_type=F32)


def _dot_nt(a, b):
    return lax.dot_general(a.astype(BF16), b.astype(BF16), (((1,), (1,)), ((), ())),
                           preferred_element_type=F32)


def _dot_tn(a, b):
    return lax.dot_general(a.astype(BF16), b.astype(BF16), (((0,), (0,)), ((), ())),
                           preferred_element_type=F32)


def _silu(x):
    return x * jax.nn.sigmoid(x)


def _softplus(x):
    return jnp.maximum(x, 0.0) + jnp.log(1.0 + jnp.exp(-jnp.abs(x)))


def _pick_tile(n, cap, unit):
    t = (min(n, cap) // unit) * unit
    while n % t:
        t -= unit
    return t


def _adaln_kernel(c_ref, w_ref, b_ref, o_ref):
    o_ref[...] = _dot_hi(_silu(c_ref[...]), w_ref[...]) + b_ref[...]


def _adaln(cond, w, b):
    d, n = w.shape
    tn = _pick_tile(n, 768, LANES)
    return pl.pallas_call(
        _adaln_kernel,
        grid=(n // tn,),
        in_specs=[pl.BlockSpec((COND_ROWS, d), lambda j: (0, 0)),
                  pl.BlockSpec((d, tn), lambda j: (0, j)),
                  pl.BlockSpec((1, tn), lambda j: (0, j))],
        out_specs=pl.BlockSpec((COND_ROWS, tn), lambda j: (0, j)),
        out_shape=jax.ShapeDtypeStruct((COND_ROWS, n), F32),
        compiler_params=_cparams(("parallel",), 40),
        name="adaln",
    )(cond, w, b)


def _inproj_kernel(x_ref, nw_ref, sc_ref, sh_ref, w_ref, o_ref, xn_ref):
    @pl.when(pl.program_id(2) == 0)
    def _():
        x = x_ref[0]
        r = lax.rsqrt(jnp.mean(x * x, axis=-1, keepdims=True) + NORM_EPS)
        y = (x * r * nw_ref[...]) * (1.0 + sc_ref[0]) + sh_ref[0]
        xn_ref[...] = y.astype(BF16)

    o_ref[0] = jnp.dot(xn_ref[...], w_ref[...], preferred_element_type=F32)


def _inproj(x, nw, sc, sh, wp):
    b, l, d = x.shape
    n = wp.shape[1]
    tm = _pick_tile(l, 1024, 8)
    tn = _pick_tile(n, 1280, LANES)
    return pl.pallas_call(
        _inproj_kernel,
        grid=(b, l // tm, n // tn),
        in_specs=[pl.BlockSpec((1, tm, d), lambda i, m, j: (i, m, 0)),
                  pl.BlockSpec((1, d), lambda i, m, j: (0, 0)),
                  pl.BlockSpec((1, 1, d), lambda i, m, j: (i, 0, 0)),
                  pl.BlockSpec((1, 1, d), lambda i, m, j: (i, 0, 0)),
                  pl.BlockSpec((d, tn), lambda i, m, j: (0, j))],
        out_specs=pl.BlockSpec((1, tm, tn), lambda i, m, j: (i, m, j)),
        out_shape=jax.ShapeDtypeStruct((b, l, n), F32),
        scratch_shapes=[pltpu.VMEM((tm, d), BF16)],
        compiler_params=_cparams(("parallel", "parallel", "arbitrary"), 56),
        name="inproj",
    )(x, nw, sc, sh, wp)


def _outproj_kernel(x_ref, yg_ref, yh_ref, ym_ref, wg_ref, wh_ref, wm_ref, gt_ref, fw_ref, o_ref, *, final):
    acc = jnp.dot(yg_ref[0], wg_ref[...], preferred_element_type=F32)
    acc = acc + jnp.dot(yh_ref[0], wh_ref[...], preferred_element_type=F32)
    acc = acc + jnp.dot(ym_ref[0], wm_ref[...], preferred_element_type=F32)
    xn = x_ref[0] + gt_ref[0] * acc
    if final:
        r = lax.rsqrt(jnp.mean(xn * xn, axis=-1, keepdims=True) + NORM_EPS)
        xn = xn * r * fw_ref[...]
    o_ref[0] = xn


def _outproj(x, yg, yh, ym, wg, wh, wm, gt, fw, final):
    b, l, d = x.shape
    tm = _pick_tile(l, 512, 8)
    row = lambda w: pl.BlockSpec((1, tm, w), lambda i, m: (i, m, 0))
    full = lambda a: pl.BlockSpec(a.shape, lambda i, m: (0, 0))
    return pl.pallas_call(
        functools.partial(_outproj_kernel, final=final),
        grid=(b, l // tm),
        in_specs=[row(d), row(yg.shape[2]), row(yh.shape[2]), row(ym.shape[2]),
                  full(wg), full(wh), full(wm),
                  pl.BlockSpec((1, 1, d), lambda i, m: (i, 0, 0)),
                  pl.BlockSpec((1, d), lambda i, m: (0, 0))],
        out_specs=row(d),
        out_shape=jax.ShapeDtypeStruct((b, l, d), F32),
        compiler_params=_cparams(("parallel", "parallel"), 48),
        name="outproj",
    )(x, yg, yh, ym, wg, wh, wm, gt, fw)


def _conv_rows(x, w, period):
    rows = x.shape[0]
    taps = w.shape[0]
    pad = taps // 2
    pos = lax.broadcasted_iota(jnp.int32, x.shape, 0) % period
    y = None
    for j in range(taps):
        off = j - pad
        if off == 0:
            term = x * w[j:j + 1]
        else:
            shifted = pltpu.roll(x, (-off) % rows, axis=0)
            ok = (pos >= -off) if off < 0 else (pos < period - off)
            term = jnp.where(ok, shifted, 0.0) * w[j:j + 1]
        y = term if y is None else y + term
    return y


def _seg_cumsum(x, reverse):
    lane = lax.broadcasted_iota(jnp.int32, x.shape, 1) % CHUNK
    s = 1
    while s < CHUNK:
        if reverse:
            shifted = pltpu.roll(x, LANES - s, axis=1)
            ok = lane < CHUNK - s
        else:
            shifted = pltpu.roll(x, s, axis=1)
            ok = lane >= s
        x = x + jnp.where(ok, shifted, 0.0)
        s *= 2
    return x


def _chunk_masks(d):
    ii = lax.broadcasted_iota(jnp.int32, (CHUNK, CHUNK), 0)
    jj = lax.broadcasted_iota(jnp.int32, (CHUNK, CHUNK), 1)
    eye = ii == jj
    if d == 0:
        return eye, jj <= ii, jj < ii
    return eye, jj >= ii, jj > ii


def _pair_masks(d):
    ii = lax.broadcasted_iota(jnp.int32, (LANES, LANES), 0)
    jj = lax.broadcasted_iota(jnp.int32, (LANES, LANES), 1)
    lo = (ii // CHUNK) * CHUNK
    eye = ii == jj
    if d == 0:
        return eye, (jj >= lo) & (jj <= ii), (jj >= lo) & (jj < ii)
    return eye, (jj < lo + CHUNK) & (jj >= ii), (jj < lo + CHUNK) & (jj > ii)


def _to_col(row, eye):
    return jnp.sum(jnp.where(eye, jnp.broadcast_to(row, eye.shape), 0.0), axis=1, keepdims=True)


def _split_bf16(a):
    hi = a.astype(BF16)
    return hi, (a - hi.astype(F32)).astype(BF16)


def _dot3(a, b):
    a_hi, a_lo = _split_bf16(a)
    b_hi, b_lo = _split_bf16(b)
    return jnp.dot(jnp.concatenate([a_hi, a_hi, a_lo], axis=1), jnp.concatenate([b_hi, b_lo, b_hi], axis=0),
                   preferred_element_type=F32)


def _row_slice(rows_ref, idx, p, half):
    return rows_ref[idx, pl.ds(p, 1), half * CHUNK:(half + 1) * CHUNK]


def _head_out_tiles(acc_refs, base, nw_ref, z_ref, y_ref, n_rows, og_ref=None):
    tile = _pick_tile(n_rows, 256, 8)

    def body(i, carry):
        r = pl.multiple_of(i * tile, tile)
        o = acc_refs[0][pl.ds(base + r, tile), :]
        for acc_ref in acc_refs[1:]:
            o = o + acc_ref[pl.ds(base + r, tile), :]
        if og_ref is not None:
            o = jax.nn.sigmoid(og_ref[0, pl.ds(r, tile), :]) * o
        o = o * lax.rsqrt(jnp.mean(o * o, axis=-1, keepdims=True) + NORM_EPS) * nw_ref[...]
        y_ref[0, pl.ds(r, tile), :] = (o * _silu(z_ref[0, pl.ds(r, tile), :])).astype(y_ref.dtype)
        return carry

    lax.fori_loop(0, n_rows // tile, body, 0)


def _ml_rows(g_refs, bias_ref, h, n_heads, rows_ref):
    li_f = g_refs[0][0, 0] + bias_ref[h]
    lf_f = -_softplus(-(g_refs[1][0, 0] + bias_ref[n_heads + h]))
    li_b = g_refs[2][0, 0] + bias_ref[2 * n_heads + h]
    lf_b = -_softplus(-(g_refs[3][0, 0] + bias_ref[3 * n_heads + h]))
    rows_ref[0] = _seg_cumsum(lf_f, False)
    rows_ref[1] = li_f
    rows_ref[2] = _seg_cumsum(lf_b, True)
    rows_ref[3] = li_b


def _ml_chunk(q_ref, k_ref, v_ref, rows_ref, acc_ref, cst, nst, mst, d, p, half):
    eye, incl, _ = _chunk_masks(d)
    r0 = pl.multiple_of(p * LANES + half * CHUNK, CHUNK)
    q = q_ref[0, pl.ds(r0, CHUNK), :]
    k = k_ref[0, pl.ds(r0, CHUNK), :] * (HEAD_DIM ** -0.5)
    v = v_ref[0, pl.ds(r0, CHUNK), :]
    b_row = _row_slice(rows_ref, 2 * d, p, half)
    li_row = _row_slice(rows_ref, 2 * d + 1, p, half)
    b_tot = b_row[:, CHUNK - 1:CHUNK] if d == 0 else b_row[:, 0:1]
    b_col = _to_col(b_row, eye)
    end_row = b_tot - b_row + li_row
    e_max = jnp.max(end_row, axis=1, keepdims=True)
    end_col = _to_col(end_row, eye)
    dlog = jnp.where(incl, b_col - b_row + li_row, -jnp.inf)
    rowmax = jnp.max(dlog, axis=1, keepdims=True)
    pp = jnp.exp(dlog - rowmax) * _dot_nt(q, k)
    pv = _dot(pp, v)
    psum = jnp.sum(pp, axis=1, keepdims=True)
    kw = k * jnp.exp(end_col - e_max)
    d_c = _dot_tn(kw, v)
    d_n = jnp.sum(kw, axis=0, keepdims=True)

    c_m = cst[d]
    n_v = nst[d, 0:1, :]
    m_s = mst[d, 0:1, 0:1]
    inter = b_col + m_s
    m_i = jnp.maximum(inter, rowmax)
    w_inter = jnp.exp(inter - m_i)
    s_intra = jnp.exp(rowmax - m_i)
    num = w_inter * _dot(q, c_m) + s_intra * pv
    den = w_inter * jnp.sum(q * n_v, axis=1, keepdims=True) + s_intra * psum
    hh = num / jnp.maximum(jnp.abs(den), jnp.exp(-m_i))
    acc_ref[pl.ds(r0, CHUNK), :] += hh

    carry_log = b_tot + m_s
    m_new = jnp.maximum(carry_log, e_max)
    dec = jnp.exp(carry_log - m_new)
    scl = jnp.exp(e_max - m_new)
    cst[d] = dec * c_m + scl * d_c
    nst[d, 0:1, :] = dec * n_v + scl * d_n
    mst[d] = jnp.broadcast_to(m_new, mst.shape[1:])


def _ml_scan(q_ref, k_ref, v_ref, rows_ref, acc_ref, cst, nst, mst, n_pairs):
    acc_ref[...] = jnp.zeros(acc_ref.shape, F32)

    def body(p, carry):
        pb = n_pairs - 1 - p
        _ml_chunk(q_ref, k_ref, v_ref, rows_ref, acc_ref, cst, nst, mst, 0, p, 0)
        _ml_chunk(q_ref, k_ref, v_ref, rows_ref, acc_ref, cst, nst, mst, 1, pb, 1)
        _ml_chunk(q_ref, k_ref, v_ref, rows_ref, acc_ref, cst, nst, mst, 0, p, 1)
        _ml_chunk(q_ref, k_ref, v_ref, rows_ref, acc_ref, cst, nst, mst, 1, pb, 0)
        return carry

    lax.fori_loop(0, n_pairs, body, 0)


def _mlstm_kernel(bias_ref, ql, kl, vl, ol, zl, qc, kc, vc, oc, zc, gl0, gl1, gl2, gl3, gc0, gc1, gc2, gc3,
                  nw_ref, *rest, n_heads, ctx_out):
    if ctx_out:
        yl_ref, yc_ref, cst, nst, mst, rows_l, rows_c, acc_l, acc_c = rest
    else:
        yl_ref, cst, nst, mst, rows_l, rows_c, acc_l, acc_c = rest
    h = pl.program_id(1)
    cst[...] = jnp.zeros(cst.shape, F32)
    nst[...] = jnp.zeros(nst.shape, F32)
    mst[...] = jnp.zeros(mst.shape, F32)
    _ml_rows((gc0, gc1, gc2, gc3), bias_ref, h, n_heads, rows_c)
    _ml_scan(qc, kc, vc, rows_c, acc_c, cst, nst, mst, rows_c.shape[1])
    _ml_rows((gl0, gl1, gl2, gl3), bias_ref, h, n_heads, rows_l)
    _ml_scan(ql, kl, vl, rows_l, acc_l, cst, nst, mst, rows_l.shape[1])
    _head_out_tiles((acc_l,), 0, nw_ref, zl, yl_ref, acc_l.shape[0], og_ref=ol)
    if ctx_out:
        _head_out_tiles((acc_c,), 0, nw_ref, zc, yc_ref, acc_c.shape[0], og_ref=oc)


def _head_block(l, blk):
    return pl.BlockSpec((1, l, HEAD_DIM), lambda b, h, blk=blk: (b, 0, blk + h))


def _gate_block(n_pairs, j, n_heads):
    return pl.BlockSpec((1, 1, n_pairs, LANES), lambda b, h, j=j: (b, j * n_heads + h, 0, 0))


def _mlstm(u_l, u_c, gt_l, gt_c, bias, nw, lay, ctx_out):
    b, l, _ = u_l.shape
    lc = u_c.shape[1]
    nh = lay["mh"]
    blk = lambda off: off // HEAD_DIM
    offs = [lay["MQ"], lay["MQ"] + lay["DM"], lay["MQ"] + 2 * lay["DM"], lay["MO"], lay["MZ"]]
    in_specs = [pl.BlockSpec(memory_space=pltpu.SMEM)]
    in_specs += [_head_block(l, blk(o)) for o in offs]
    in_specs += [_head_block(lc, blk(o)) for o in offs]
    in_specs += [_gate_block(l // LANES, j, nh) for j in range(4)]
    in_specs += [_gate_block(lc // LANES, j, nh) for j in range(4)]
    in_specs += [pl.BlockSpec((1, HEAD_DIM), lambda b_, h: (0, 0))]
    out_specs = [pl.BlockSpec((1, l, HEAD_DIM), lambda b_, h: (b_, 0, h))]
    out_shape = [jax.ShapeDtypeStruct((b, l, nh * HEAD_DIM), BF16)]
    if ctx_out:
        out_specs.append(pl.BlockSpec((1, lc, HEAD_DIM), lambda b_, h: (b_, 0, h)))
        out_shape.append(jax.ShapeDtypeStruct((b, lc, nh * HEAD_DIM), BF16))
    scratch = [pltpu.VMEM((2, HEAD_DIM, HEAD_DIM), F32), pltpu.VMEM((2, 8, LANES), F32),
               pltpu.VMEM((2, 8, LANES), F32),
               pltpu.VMEM((4, l // LANES, LANES), F32), pltpu.VMEM((4, lc // LANES, LANES), F32),
               pltpu.VMEM((l, HEAD_DIM), F32), pltpu.VMEM((lc, HEAD_DIM), F32)]
    res = pl.pallas_call(
        functools.partial(_mlstm_kernel, n_heads=nh, ctx_out=ctx_out),
        grid=(b, nh),
        in_specs=in_specs, out_specs=out_specs, out_shape=out_shape,
        scratch_shapes=scratch,
        compiler_params=_cparams(("parallel", "parallel"), 48),
        name="mlstm",
    )(bias.reshape(-1), *([u_l] * 5), *([u_c] * 5), *([gt_l] * 4), *([gt_c] * 4), nw)
    return (res[0], res[1]) if ctx_out else (res[0], None)


def _gdn_rows(g_refs, alog_ref, dtb_ref, h, n_heads, rows_ref, base):
    n = g_refs[0].shape[2]
    g_f = -jnp.exp(alog_ref[h]) * _softplus(g_refs[0][0, 0] + dtb_ref[h])
    g_b = -jnp.exp(alog_ref[n_heads + h]) * _softplus(g_refs[1][0, 0] + dtb_ref[n_heads + h])
    rows_ref[0, base:base + n] = _seg_cumsum(g_f, False)
    rows_ref[1, base:base + n] = jax.nn.sigmoid(g_refs[2][0, 0])
    rows_ref[2, base:base + n] = _seg_cumsum(g_b, True)
    rows_ref[3, base:base + n] = jax.nn.sigmoid(g_refs[3][0, 0])


def _gdn_conv(src_refs, w_refs, dst_refs, base, n_units, unit_rows, period):
    def body(i, carry):
        r = pl.multiple_of(i * unit_rows, unit_rows)
        for idx in range(3):
            t = _silu(_conv_rows(src_refs[idx][0, pl.ds(r, unit_rows), :], w_refs[idx][...], period))
            if idx < 2:
                t = t * lax.rsqrt(jnp.sum(t * t, axis=-1, keepdims=True) + NORM_EPS)
            if idx == 0:
                t = t * (HEAD_DIM ** -0.5)
            dst_refs[idx][pl.ds(base + r, unit_rows), :] = t
        return carry

    lax.fori_loop(0, n_units, body, 0)


def _gdn_pairs_prep(qs, ks, vs, rows_ref, ut_s, w_s, qd_s, ket_s, aqk_s, pairs):
    eye = _pair_masks(0)[0]
    eye_f = jnp.where(eye, 1.0, 0.0)
    row_i = lax.broadcasted_iota(jnp.int32, (LANES, 1), 0)
    chains = []
    for p in pairs:
        rows = pl.ds(pl.multiple_of(p * LANES, LANES), LANES)
        q = qs[rows, :]
        k = ks[rows, :]
        kk = _dot_nt(k, k)
        qk = _dot_nt(q, k)
        for d in (0, 1):
            _, incl, strict = _pair_masks(d)
            g_row = rows_ref[2 * d, pl.ds(p, 1), :]
            beta_col = _to_col(rows_ref[2 * d + 1, pl.ds(p, 1), :], eye)
            g_col = _to_col(g_row, eye)
            dec = jnp.exp(jnp.where(incl, g_col - g_row, -jnp.inf))
            aqk_s[d, rows, :] = (qk * dec).astype(BF16)
            m_low = jnp.where(strict, beta_col * kk * dec, 0.0)
            chains.append(dict(d=d, rows=rows, g_row=g_row, g_col=g_col, beta_col=beta_col,
                               pw=m_low, t_inv=eye_f - m_low))
    n_sq = 1
    while 2 * n_sq < CHUNK:
        for c in chains:
            c["pw"] = _dot3(c["pw"], c["pw"])
        for c in chains:
            c["t_inv"] = _dot3(c["t_inv"], eye_f + c["pw"])
        n_sq *= 2
    for c in chains:
        d, rows, g_row, g_col, beta_col = c["d"], c["rows"], c["g_row"], c["g_col"], c["beta_col"]
        q = qs[rows, :]
        k = ks[rows, :]
        e_g = jnp.exp(g_col)
        sol = _dot3(c["t_inv"], jnp.concatenate([beta_col * vs[rows, :], (beta_col * e_g) * k], axis=1))
        ut_s[d, rows, :] = sol[:, :HEAD_DIM]
        w_s[d, rows, :] = sol[:, HEAD_DIM:].astype(BF16)
        qd_s[d, rows, :] = (q * e_g).astype(BF16)
        last = CHUNK - 1 if d == 0 else 0
        g_tot_col = jnp.where(row_i < CHUNK, g_row[:, last:last + 1], g_row[:, CHUNK + last:CHUNK + last + 1])
        ket_s[d, rows, :] = (k * jnp.exp(g_tot_col - g_col)).T.astype(BF16)


def _gdn_steps(rows_ref, ut_s, w_s, qd_s, ket_s, aqk_s, sst, acc_refs, chains):
    st = []
    for d, p, half in chains:
        r0 = pl.ds(pl.multiple_of(p * LANES + half * CHUNK, CHUNK), CHUNK)
        s = sst[d]
        lhs = jnp.concatenate([w_s[d, r0, :], qd_s[d, r0, :]], axis=0)
        st.append((r0, s, jnp.dot(lhs, s.astype(BF16), preferred_element_type=F32)))
    for (d, p, half), (r0, s, ws_qs) in zip(chains, st):
        rp = pl.ds(pl.multiple_of(p * LANES, LANES), LANES)
        lane = half * CHUNK + (CHUNK - 1 if d == 0 else 0)
        g_tot = rows_ref[2 * d, pl.ds(p, 1), lane:lane + 1]
        u = (ut_s[d, r0, :] - ws_qs[:CHUNK]).astype(BF16)
        zero = jnp.zeros_like(u)
        u_pad = jnp.concatenate([u, zero] if half == 0 else [zero, u], axis=0)
        lhs2 = jnp.concatenate([aqk_s[d, r0, :], ket_s[d, rp, :]], axis=0)
        au_ku = jnp.dot(lhs2, u_pad, preferred_element_type=F32)
        acc_refs[d][r0, :] = ws_qs[CHUNK:] + au_ku[:CHUNK]
        sst[d] = jnp.exp(g_tot) * s + au_ku[CHUNK:]


def _gdn_kernel(alog_ref, dtb_ref, ql, kl, vl, zl, qc, kc, vc, zc, gl0, gl1, gl2, gl3, gc0, gc1, gc2, gc3,
                wq, wk, wv, nw_ref, *rest, n_heads, ctx_out):
    if ctx_out:
        yl_ref, yc_ref = rest[:2]
        rest = rest[2:]
    else:
        yl_ref, yc_ref = rest[0], None
        rest = rest[1:]
    sst, rows, qs, ks, vs, ut_s, w_s, qd_s, ket_s, aqk_s, acc_f, acc_b = rest
    h = pl.program_id(1)
    n_l = ql.shape[1]
    n_c = qc.shape[1]
    nl = n_l // LANES
    nc = n_c // LANES
    sst[...] = jnp.zeros(sst.shape, F32)
    _gdn_rows((gl0, gl1, gl2, gl3), alog_ref, dtb_ref, h, n_heads, rows, 0)
    _gdn_rows((gc0, gc1, gc2, gc3), alog_ref, dtb_ref, h, n_heads, rows, nl)
    _gdn_conv((ql, kl, vl), (wq, wk, wv), (qs, ks, vs), 0, n_l // CHUNK, CHUNK, CHUNK)
    _gdn_conv((qc, kc, vc), (wq, wk, wv), (qs, ks, vs), n_l, 1, n_c, n_c)
    prepped = (ut_s, w_s, qd_s, ket_s, aqk_s)

    def prep(i, carry):
        _gdn_pairs_prep(qs, ks, vs, rows, *prepped, [2 * i, 2 * i + 1])
        return carry

    lax.fori_loop(0, (nl + nc) // 2, prep, 0)

    def step(t, carry):
        pf = jnp.where(t < nc, nl + t, t - nc)
        pb = nl + nc - 1 - t
        _gdn_steps(rows, *prepped, sst, (acc_f, acc_b), [(0, pf, 0), (1, pb, 1)])
        _gdn_steps(rows, *prepped, sst, (acc_f, acc_b), [(0, pf, 1), (1, pb, 0)])
        return carry

    lax.fori_loop(0, nl + nc, step, 0)
    _head_out_tiles((acc_f, acc_b), 0, nw_ref, zl, yl_ref, n_l)
    if ctx_out:
        _head_out_tiles((acc_f, acc_b), n_l, nw_ref, zc, yc_ref, n_c)


def _gdn(u_l, u_c, gt_l, gt_c, conv_w, a_log, dt_bias, nw, lay, ctx_out):
    b, l, _ = u_l.shape
    lc = u_c.shape[1]
    nh = lay["gh"]
    taps = conv_w.shape[0]
    blk = lambda off: off // HEAD_DIM
    offs = [lay["GQ"], lay["GQ"] + lay["DG"], lay["GQ"] + 2 * lay["DG"], lay["GZ"]]
    smem = pl.BlockSpec(memory_space=pltpu.SMEM)
    in_specs = [smem, smem]
    in_specs += [_head_block(l, blk(o)) for o in offs]
    in_specs += [_head_block(lc, blk(o)) for o in offs]
    in_specs += [_gate_block(l // LANES, j, nh) for j in range(4)]
    in_specs += [_gate_block(lc // LANES, j, nh) for j in range(4)]
    in_specs += [pl.BlockSpec((taps, HEAD_DIM), lambda b_, h, j=j: (0, j * nh + h)) for j in range(3)]
    in_specs += [pl.BlockSpec((1, HEAD_DIM), lambda b_, h: (0, 0))]
    out_specs = [pl.BlockSpec((1, l, HEAD_DIM), lambda b_, h: (b_, 0, h))]
    out_shape = [jax.ShapeDtypeStruct((b, l, nh * HEAD_DIM), BF16)]
    if ctx_out:
        out_specs.append(pl.BlockSpec((1, lc, HEAD_DIM), lambda b_, h: (b_, 0, h)))
        out_shape.append(jax.ShapeDtypeStruct((b, lc, nh * HEAD_DIM), BF16))
    lt = l + lc
    seq_f32 = pltpu.VMEM((lt, HEAD_DIM), F32)
    dir_bf16 = pltpu.VMEM((2, lt, HEAD_DIM), BF16)
    scratch = [pltpu.VMEM((2, HEAD_DIM, HEAD_DIM), F32), pltpu.VMEM((4, lt // LANES, LANES), F32),
               seq_f32, seq_f32, seq_f32,
               pltpu.VMEM((2, lt, HEAD_DIM), F32), dir_bf16, dir_bf16, dir_bf16, dir_bf16,
               seq_f32, seq_f32]
    res = pl.pallas_call(
        functools.partial(_gdn_kernel, n_heads=nh, ctx_out=ctx_out),
        grid=(b, nh),
        in_specs=in_specs, out_specs=out_specs, out_shape=out_shape,
        scratch_shapes=scratch,
        compiler_params=_cparams(("parallel", "parallel"), 56),
        name="gdn",
    )(a_log.reshape(-1), dt_bias.reshape(-1), *([u_l] * 4), *([u_c] * 4), *([gt_l] * 4), *([gt_c] * 4),
      *([conv_w] * 3), nw)
    return (res[0], res[1]) if ctx_out else (res[0], None)


def _dft_tables(l):
    k = jnp.arange(l, dtype=jnp.int32)
    ang = ((k[:, None] * k[None, :]) % (2 * l)).astype(F32) * (math.pi / l)
    alt = jnp.where(k % 2 == 0, 1.0, -1.0).astype(F32)
    sin_f = jnp.sin(ang).at[0, :].set(alt)
    return jnp.cos(ang).astype(BF16), sin_f.astype(BF16), sin_f.T.astype(BF16)


def _hy_prep_kernel(p_ref, w_ref, x1_ref, x2_ref, v_ref, vb_ref, *, period, dh):
    y = _conv_rows(p_ref[0], w_ref[...], period)
    x1_ref[0] = y[:, :dh]
    x2_ref[0] = y[:, dh:2 * dh]
    v = y[:, 2 * dh:]
    v_ref[0] = v
    vb_ref[0] = v.astype(BF16)


def _hy_prep(u, conv_w, dh, grid_mask):
    b, l, _ = u.shape
    taps = conv_w.shape[0]
    tr = _pick_tile(l, 256, CHUNK)
    period = CHUNK if grid_mask else l
    if not grid_mask:
        tr = l
    o_spec = pl.BlockSpec((1, tr, dh), lambda i, m: (i, m, 0))
    return pl.pallas_call(
        functools.partial(_hy_prep_kernel, period=period, dh=dh),
        grid=(b, l // tr),
        in_specs=[pl.BlockSpec((1, tr, 3 * dh), lambda i, m: (i, m, 0)),
                  pl.BlockSpec((taps, 3 * dh), lambda i, m: (0, 0))],
        out_specs=[o_spec] * 4,
        out_shape=[jax.ShapeDtypeStruct((b, l, dh), F32)] * 3 + [jax.ShapeDtypeStruct((b, l, dh), BF16)],
        compiler_params=_cparams(("parallel", "parallel"), 40),
        name="hy_prep",
    )(u, conv_w)


def _filter_kernel(feats_ref, w1_ref, b1_ref, fr_ref, w2_ref, b2_ref, w3c_ref, w3a_ref, dl_ref,
                   hs_ref, hd_ref, nyq_ref, hid_ref):
    @pl.when((pl.program_id(0) == 0) & (pl.program_id(1) == 0))
    def _():
        hid = jnp.sin(fr_ref[...] * (_dot_hi(feats_ref[...], w1_ref[...]) + b1_ref[...]))
        hid_ref[...] = jnp.sin(fr_ref[...] * (_dot_hi(hid, w2_ref[...]) + b2_ref[...]))

    hid = hid_ref[...]
    win = jnp.exp(-feats_ref[:, 0:1] * dl_ref[...])

    def one(w3_ref):
        f = _dot_hi(hid, w3_ref[...]) * win
        return f / (jnp.sum(jnp.abs(f), axis=0, keepdims=True) + NORM_EPS)

    h_c = one(w3c_ref)
    h_a = one(w3a_ref)
    h_sum = h_c + h_a
    sign = jnp.where(lax.broadcasted_iota(jnp.int32, h_sum.shape, 0) % 2 == 0, 1.0, -1.0)
    nyq_ref[0] = jnp.broadcast_to(jnp.sum(h_sum * sign, axis=0, keepdims=True), nyq_ref.shape[1:])
    hs_ref[0] = h_sum.astype(BF16)
    hd_ref[0] = (h_c - h_a).astype(BF16)


def _hy_filters(l, w1, b1, freq, w2, b2, w3, dh):
    pos = jnp.arange(l, dtype=F32)
    t = pos / max(l - 1, 1)
    ang = 2.0 * math.pi * pos / l
    bands = jnp.linspace(1e-4, FILTER_BANDS - 1, FILTER_BANDS, dtype=F32)
    feats = jnp.concatenate([t[:, None], jnp.cos(ang[:, None] * bands), -jnp.sin(ang[:, None] * bands)], axis=-1)
    n_emb, n_hid = w1.shape
    feats = jnp.pad(feats, ((0, 0), (0, LANES - n_emb)))
    pc = LANES - n_hid
    w1p = jnp.pad(w1, ((0, LANES - n_emb), (0, pc)))
    w2p = jnp.pad(w2, ((0, pc), (0, pc)))
    w3p = jnp.pad(w3, ((0, pc), (0, 0)))
    row = lambda a: jnp.pad(a, (0, pc))[None]
    deltas = jnp.abs(jnp.linspace(MIN_DECAY, MAX_DECAY, dh, dtype=F32))[None]
    nct = dh // LANES
    const = lambda shape: pl.BlockSpec(shape, lambda o, c: (0, 0))
    o_spec = pl.BlockSpec((1, l, LANES), lambda o, c: (o, 0, c))
    return pl.pallas_call(
        _filter_kernel,
        grid=(HYENA_ORDER, nct),
        in_specs=[const((l, LANES)), const((LANES, LANES)), const((1, LANES)), const((1, LANES)),
                  const((LANES, LANES)), const((1, LANES)),
                  pl.BlockSpec((LANES, LANES), lambda o, c: (0, o * 2 * nct + c)),
                  pl.BlockSpec((LANES, LANES), lambda o, c: (0, o * 2 * nct + nct + c)),
                  pl.BlockSpec((1, LANES), lambda o, c: (0, c))],
        out_specs=[o_spec, o_spec, pl.BlockSpec((1, 8, LANES), lambda o, c: (o, 0, c))],
        out_shape=[jax.ShapeDtypeStruct((HYENA_ORDER, l, dh), BF16)] * 2
        + [jax.ShapeDtypeStruct((HYENA_ORDER, 8, dh), F32)],
        scratch_shapes=[pltpu.VMEM((l, LANES), F32)],
        compiler_params=_cparams(("arbitrary", "arbitrary"), 48),
        name="hy_filter",
    )(feats, w1p, row(b1), row(freq), w2p, row(b2), w3p, w3p, deltas)


def _spectrum_kernel(c_ref, s_ref, hs_ref, hd_ref, nyq_ref, a_ref, a2_ref, hsin_ref, *, l, tk):
    a = jnp.dot(c_ref[...], hs_ref[0], preferred_element_type=F32)
    s = jnp.dot(s_ref[...], hd_ref[0], preferred_element_type=F32)
    is0 = (lax.broadcasted_iota(jnp.int32, a.shape, 0) + pl.program_id(1) * tk) == 0
    wk = jnp.where(is0, 0.5 / l, 1.0 / l)
    a_ref[0] = a * wk
    a2_ref[0] = jnp.where(is0, nyq_ref[0, 0:1, :], a) * wk
    hsin_ref[0] = jnp.where(is0, 0.0, s) * wk


def _hy_spectrum(tabs, hs, hd, nyq):
    cos_t, sin_f, _ = tabs
    _, l, dh = hs.shape
    tk = _pick_tile(l, 512, 8)
    tab = pl.BlockSpec((tk, l), lambda o, k: (k, 0))
    filt = pl.BlockSpec((1, l, dh), lambda o, k: (o, 0, 0))
    o_spec = pl.BlockSpec((1, tk, dh), lambda o, k: (o, k, 0))
    return pl.pallas_call(
        functools.partial(_spectrum_kernel, l=l, tk=tk),
        grid=(HYENA_ORDER, l // tk),
        in_specs=[tab, tab, filt, filt, pl.BlockSpec((1, 8, dh), lambda o, k: (o, 0, 0))],
        out_specs=[o_spec] * 3,
        out_shape=[jax.ShapeDtypeStruct((HYENA_ORDER, l, dh), F32)] * 3,
        compiler_params=_cparams(("parallel", "parallel"), 48),
        name="hy_spectrum",
    )(cos_t, sin_f, hs, hd, nyq)


def _hy_fwd_kernel(c_ref, s_ref, v_ref, a_ref, a2_ref, hsin_ref, yc_ref, ys_ref):
    v = v_ref[0]
    xc = jnp.dot(c_ref[...], v, preferred_element_type=F32)
    xs = jnp.dot(s_ref[...], v, preferred_element_type=F32)
    hsin = hsin_ref[0]
    yc_ref[0] = (xc * a_ref[0] - xs * hsin).astype(BF16)
    ys_ref[0] = (xc * hsin + xs * a2_ref[0]).astype(BF16)


def _hy_fwd(tabs, vb, spec, order):
    cos_t, sin_f, _ = tabs
    b, l, dh = vb.shape
    tk = _pick_tile(l, 512, 8)
    tab = pl.BlockSpec((tk, l), lambda k, i: (k, 0))
    filt = pl.BlockSpec((1, tk, dh), lambda k, i: (order, k, 0))
    o_spec = pl.BlockSpec((1, tk, dh), lambda k, i: (i, k, 0))
    return pl.pallas_call(
        _hy_fwd_kernel,
        grid=(l // tk, b),
        in_specs=[tab, tab, pl.BlockSpec((1, l, dh), lambda k, i: (i, 0, 0)), filt, filt, filt],
        out_specs=[o_spec] * 2,
        out_shape=[jax.ShapeDtypeStruct((b, l, dh), BF16)] * 2,
        compiler_params=_cparams(("parallel", "parallel"), 48),
        name="hy_fwd",
    )(cos_t, sin_f, vb, *spec)


def _hy_inv_kernel(c_ref, s_ref, yc_ref, ys_ref, x_ref, v_ref, sk_ref, *rest, last):
    conv = jnp.dot(c_ref[...], yc_ref[0], preferred_element_type=F32)
    conv = conv + jnp.dot(s_ref[...], ys_ref[0], preferred_element_type=F32)
    y = x_ref[0] * (conv + sk_ref[0] * v_ref[0])
    if last:
        z_ref, o_ref = rest
        o_ref[0] = (y * _silu(z_ref[0])).astype(BF16)
    else:
        of_ref, ob_ref = rest
        of_ref[0] = y
        ob_ref[0] = y.astype(BF16)


def _hy_inv(tabs, yc, ys, x, v, skip, order, u=None, z_blk=None):
    cos_t, _, sin_i = tabs
    b, l, dh = yc.shape
    last = u is not None
    tt = _pick_tile(l, 512, 8)
    tab = pl.BlockSpec((tt, l), lambda t, i: (t, 0))
    full = pl.BlockSpec((1, l, dh), lambda t, i: (i, 0, 0))
    tile = pl.BlockSpec((1, tt, dh), lambda t, i: (i, t, 0))
    in_specs = [tab, tab, full, full, tile, tile, pl.BlockSpec((1, 1, dh), lambda t, i: (order, 0, 0))]
    args = [cos_t, sin_i, yc, ys, x, v, skip[:, None, :]]
    if last:
        in_specs.append(pl.BlockSpec((1, tt, dh), lambda t, i: (i, t, z_blk)))
        args.append(u)
        out_specs = tile
        out_shape = jax.ShapeDtypeStruct((b, l, dh), BF16)
    else:
        out_specs = [tile, tile]
        out_shape = [jax.ShapeDtypeStruct((b, l, dh), F32), jax.ShapeDtypeStruct((b, l, dh), BF16)]
    return pl.pallas_call(
        functools.partial(_hy_inv_kernel, last=last),
        grid=(l // tt, b),
        in_specs=in_specs, out_specs=out_specs, out_shape=out_shape,
        compiler_params=_cparams(("parallel", "parallel"), 48),
        name="hy_inv",
    )(*args)


def _hyena(u, tabs, conv_w, w1, b1, freq, w2, b2, w3, skip, lay, grid_mask):
    dh = lay["DH"]
    l = u.shape[1]
    x1, x2, v, vb = _hy_prep(u, conv_w, dh, grid_mask)
    spec = _hy_spectrum(tabs, *_hy_filters(l, w1, b1, freq, w2, b2, w3, dh))
    yc, ys = _hy_fwd(tabs, vb, spec, 0)
    y1, y1b = _hy_inv(tabs, yc, ys, x1, v, skip, 0)
    yc, ys = _hy_fwd(tabs, y1b, spec, 1)
    return _hy_inv(tabs, yc, ys, x2, y1, skip, 1, u=u, z_blk=lay["HZ"] // dh)


def _layout(d):
    dg, dh, dm = 3 * d // 8, d // 4, 3 * d // 8
    lay = {"DG": dg, "DH": dh, "DM": dm, "gh": dg // HEAD_DIM, "mh": dm // HEAD_DIM}
    lay["HZ"] = 3 * dh
    lay["GQ"] = 4 * dh
    lay["GZ"] = lay["GQ"] + 3 * dg
    lay["GAB"] = lay["GZ"] + dg
    lay["MQ"] = lay["GAB"] + LANES
    lay["MO"] = lay["MQ"] + 3 * dm
    lay["MZ"] = lay["MO"] + dm
    lay["MG"] = lay["MZ"] + dm
    lay["NP"] = lay["MG"] + LANES
    return lay


def _pack_w_in(w, lay):
    dg, dh, dm, gh, mh = lay["DG"], lay["DH"], lay["DM"], lay["gh"], lay["mh"]
    sizes = (3 * dg, dg, 4 * gh, 3 * dh, dh, 3 * dm, dm, dm, 4 * mh)
    offs = [0]
    for s in sizes:
        offs.append(offs[-1] + s)
    seg = [w[:, offs[i]:offs[i + 1]] for i in range(len(sizes))]
    g_qkv, g_z, g_ab, h_p, h_z, m_qkv, m_o, m_z, m_g = seg
    padl = lambda a: jnp.pad(a, ((0, 0), (0, LANES - a.shape[1])))
    return jnp.concatenate([h_p, h_z, g_qkv, g_z, padl(g_ab), m_qkv, m_o, m_z, padl(m_g)], axis=1).astype(BF16)


def _gate_rows(u, off, n):
    b, l, _ = u.shape
    return jnp.transpose(u[:, :, off:off + n], (0, 2, 1)).reshape(b, n, l // LANES, LANES)


def kernel(x, c, ctx, c_ctx, norm_w, mod_w, mod_b, w_in, gdn_conv, gdn_a_log, gdn_dt_bias, gdn_norm, hy_conv,
           hy_w1, hy_b1, hy_freq, hy_w2, hy_b2, hy_w3, hy_skip, ml_gate_bias, ml_norm, w_out, final_norm):
    b, l, d = x.shape
    lc = ctx.shape[1]
    depth = norm_w.shape[0]
    lay = _layout(d)
    dg, dh = lay["DG"], lay["DH"]
    assert b < COND_ROWS and l % (2 * LANES) == 0 and lc % (2 * LANES) == 0 and d % 1024 == 0
    cond = jnp.zeros((COND_ROWS, d), F32).at[:b].set(c).at[b].set(c_ctx)
    tabs_l = _dft_tables(l)
    tabs_c = _dft_tables(lc)
    for layer in range(depth):
        last = layer == depth - 1
        mods = _adaln(cond, mod_w[layer], mod_b[layer][None])
        sh, sc, gt = mods[:, :d], mods[:, d:2 * d], mods[:, 2 * d:]
        lat = lambda m: m[:b, None, :]
        cx = lambda m: jnp.broadcast_to(m[b][None, None, :], (b, 1, d))
        wp = _pack_w_in(w_in[layer], lay)
        nw = norm_w[layer][None]
        u_l = _inproj(x, nw, lat(sc), lat(sh), wp)
        u_c = _inproj(ctx, nw, cx(sc), cx(sh), wp)
        g_rows = lambda u, off, n: _gate_rows(u, off, n)
        yg_l, yg_c = _gdn(u_l, u_c, g_rows(u_l, lay["GAB"], 4 * lay["gh"]), g_rows(u_c, lay["GAB"], 4 * lay["gh"]),
                          gdn_conv[layer], gdn_a_log[layer], gdn_dt_bias[layer], gdn_norm[layer][None], lay,
                          not last)
        ym_l, ym_c = _mlstm(u_l, u_c, g_rows(u_l, lay["MG"], 4 * lay["mh"]), g_rows(u_c, lay["MG"], 4 * lay["mh"]),
                            ml_gate_bias[layer], ml_norm[layer][None], lay, not last)
        hy = (hy_conv[layer], hy_w1[layer], hy_b1[layer], hy_freq[layer], hy_w2[layer], hy_b2[layer],
              hy_w3[layer], hy_skip[layer])
        yh_l = _hyena(u_l, tabs_l, *hy, lay, True)
        wo = w_out[layer].astype(BF16)
        wg, wh, wm = wo[:dg], wo[dg:dg + dh], wo[dg + dh:]
        fw = final_norm[None]
        x = _outproj(x, yg_l, yh_l, ym_l, wg, wh, wm, lat(gt), fw, last)
        if not last:
            yh_c = _hyena(u_c, tabs_c, *hy, lay, False)
            ctx = _outproj(ctx, yg_c, yh_c, ym_c, wg, wh, wm, cx(gt), fw, False)
    return x
```

```python
import functools
import math

import jax
import jax.numpy as jnp
from jax import lax
from jax.experimental import pallas as pl
from jax.experimental.pallas import tpu as pltpu

HEAD_DIM = 128
CHUNK = 64
LANES = 128
NORM_EPS = 1e-6
HYENA_ORDER = 2
FILTER_BANDS = 16
DECAY_TARGET = 1e-2
MIN_DECAY = math.log(DECAY_TARGET) / 1.5
MAX_DECAY = math.log(DECAY_TARGET) / 0.3
COND_ROWS = 16

F32 = jnp.float32
BF16 = jnp.bfloat16
HI = lax.Precision.HIGHEST


def _cparams(sem, vmem_mb):
    return pltpu.CompilerParams(dimension_semantics=sem, vmem_limit_bytes=vmem_mb << 20)


def _dot(a, b):
    return jnp.dot(a.astype(BF16), b.astype(BF16), preferred_element_type=F32)


def _dot_hi(a, b):
    return jnp.dot(a, b, precision=HI, preferred_element_type=F32)


def _dot_nt(a, b):
    return lax.dot_general(a.astype(BF16), b.astype(BF16), (((1,), (1,)), ((), ())),
                           preferred_element_type=F32)


def _dot_tn(a, b):
    return lax.dot_general(a.astype(BF16), b.astype(BF16), (((0,), (0,)), ((), ())),
                           preferred_element_type=F32)


def _silu(x):
    return x * jax.nn.sigmoid(x)


def _softplus(x):
    return jnp.maximum(x, 0.0) + jnp.log(1.0 + jnp.exp(-jnp.abs(x)))


def _pick_tile(n, cap, unit):
    t = (min(n, cap) // unit) * unit
    while n % t:
        t -= unit
    return t


def _adaln_kernel(c_ref, w_ref, b_ref, o_ref):
    o_ref[...] = _dot_hi(_silu(c_ref[...]), w_ref[...]) + b_ref[...]


def _adaln(cond, w, b):
    d, n = w.shape
    tn = _pick_tile(n, 768, LANES)
    return pl.pallas_call(
        _adaln_kernel,
        grid=(n // tn,),
        in_specs=[pl.BlockSpec((COND_ROWS, d), lambda j: (0, 0)),
                  pl.BlockSpec((d, tn), lambda j: (0, j)),
                  pl.BlockSpec((1, tn), lambda j: (0, j))],
        out_specs=pl.BlockSpec((COND_ROWS, tn), lambda j: (0, j)),
        out_shape=jax.ShapeDtypeStruct((COND_ROWS, n), F32),
        compiler_params=_cparams(("parallel",), 40),
        name="adaln",
    )(cond, w, b)


def _inproj_kernel(x_ref, nw_ref, sc_ref, sh_ref, w_ref, o_ref, xn_ref):
    @pl.when(pl.program_id(2) == 0)
    def _():
        x = x_ref[0]
        r = lax.rsqrt(jnp.mean(x * x, axis=-1, keepdims=True) + NORM_EPS)
        y = (x * r * nw_ref[...]) * (1.0 + sc_ref[0]) + sh_ref[0]
        xn_ref[...] = y.astype(BF16)

    o_ref[0] = jnp.dot(xn_ref[...], w_ref[...], preferred_element_type=F32)


def _inproj(x, nw, sc, sh, wp):
    b, l, d = x.shape
    n = wp.shape[1]
    tm = _pick_tile(l, 1024, 8)
    tn = _pick_tile(n, 1280, LANES)
    return pl.pallas_call(
        _inproj_kernel,
        grid=(b, l // tm, n // tn),
        in_specs=[pl.BlockSpec((1, tm, d), lambda i, m, j: (i, m, 0)),
                  pl.BlockSpec((1, d), lambda i, m, j: (0, 0)),
                  pl.BlockSpec((1, 1, d), lambda i, m, j: (i, 0, 0)),
                  pl.BlockSpec((1, 1, d), lambda i, m, j: (i, 0, 0)),
                  pl.BlockSpec((d, tn), lambda i, m, j: (0, j))],
        out_specs=pl.BlockSpec((1, tm, tn), lambda i, m, j: (i, m, j)),
        out_shape=jax.ShapeDtypeStruct((b, l, n), F32),
        scratch_shapes=[pltpu.VMEM((tm, d), BF16)],
        compiler_params=_cparams(("parallel", "parallel", "arbitrary"), 56),
        name="inproj",
    )(x, nw, sc, sh, wp)


def _outproj_kernel(x_ref, yg_ref, yh_ref, ym_ref, wg_ref, wh_ref, wm_ref, gt_ref, fw_ref, o_ref, *, final):
    acc = jnp.dot(yg_ref[0], wg_ref[...], preferred_element_type=F32)
    acc = acc + jnp.dot(yh_ref[0], wh_ref[...], preferred_element_type=F32)
    acc = acc + jnp.dot(ym_ref[0], wm_ref[...], preferred_element_type=F32)
    xn = x_ref[0] + gt_ref[0] * acc
    if final:
        r = lax.rsqrt(jnp.mean(xn * xn, axis=-1, keepdims=True) + NORM_EPS)
        xn = xn * r * fw_ref[...]
    o_ref[0] = xn


def _outproj(x, yg, yh, ym, wg, wh, wm, gt, fw, final):
    b, l, d = x.shape
    tm = _pick_tile(l, 512, 8)
    row = lambda w: pl.BlockSpec((1, tm, w), lambda i, m: (i, m, 0))
    full = lambda a: pl.BlockSpec(a.shape, lambda i, m: (0, 0))
    return pl.pallas_call(
        functools.partial(_outproj_kernel, final=final),
        grid=(b, l // tm),
        in_specs=[row(d), row(yg.shape[2]), row(yh.shape[2]), row(ym.shape[2]),
                  full(wg), full(wh), full(wm),
                  pl.BlockSpec((1, 1, d), lambda i, m: (i, 0, 0)),
                  pl.BlockSpec((1, d), lambda i, m: (0, 0))],
        out_specs=row(d),
        out_shape=jax.ShapeDtypeStruct((b, l, d), F32),
        compiler_params=_cparams(("parallel", "parallel"), 48),
        name="outproj",
    )(x, yg, yh, ym, wg, wh, wm, gt, fw)


def _conv_rows(x, w, period):
    rows = x.shape[0]
    taps = w.shape[0]
    pad = taps // 2
    pos = lax.broadcasted_iota(jnp.int32, x.shape, 0) % period
    y = None
    for j in range(taps):
        off = j - pad
        if off == 0:
            term = x * w[j:j + 1]
        else:
            shifted = pltpu.roll(x, (-off) % rows, axis=0)
            ok = (pos >= -off) if off < 0 else (pos < period - off)
            term = jnp.where(ok, shifted, 0.0) * w[j:j + 1]
        y = term if y is None else y + term
    return y


def _seg_cumsum(x, reverse):
    lane = lax.broadcasted_iota(jnp.int32, x.shape, 1) % CHUNK
    s = 1
    while s < CHUNK:
        if reverse:
            shifted = pltpu.roll(x, LANES - s, axis=1)
            ok = lane < CHUNK - s
        else:
            shifted = pltpu.roll(x, s, axis=1)
            ok = lane >= s
        x = x + jnp.where(ok, shifted, 0.0)
        s *= 2
    return x


def _chunk_masks(d):
    ii = lax.broadcasted_iota(jnp.int32, (CHUNK, CHUNK), 0)
    jj = lax.broadcasted_iota(jnp.int32, (CHUNK, CHUNK), 1)
    eye = ii == jj
    if d == 0:
        return eye, jj <= ii, jj < ii
    return eye, jj >= ii, jj > ii


def _pair_masks(d):
    ii = lax.broadcasted_iota(jnp.int32, (LANES, LANES), 0)
    jj = lax.broadcasted_iota(jnp.int32, (LANES, LANES), 1)
    lo = (ii // CHUNK) * CHUNK
    eye = ii == jj
    if d == 0:
        return eye, (jj >= lo) & (jj <= ii), (jj >= lo) & (jj < ii)
    return eye, (jj < lo + CHUNK) & (jj >= ii), (jj < lo + CHUNK) & (jj > ii)


def _to_col(row, eye):
    return jnp.sum(jnp.where(eye, jnp.broadcast_to(row, eye.shape), 0.0), axis=1, keepdims=True)


def _split_bf16(a):
    hi = a.astype(BF16)
    return hi, (a - hi.astype(F32)).astype(BF16)


def _col_bcast(row, eye):
    x = jnp.where(eye, jnp.broadcast_to(row, eye.shape), 0.0)
    hi = x.astype(BF16)
    mid, lo = _split_bf16(x - hi.astype(F32))
    ones = jnp.ones((3 * eye.shape[1], LANES), BF16)
    return jnp.dot(jnp.concatenate([hi, mid, lo], axis=1), ones, preferred_element_type=F32)


def _dot3(a, b):
    a_hi, a_lo = _split_bf16(a)
    b_hi, b_lo = _split_bf16(b)
    return jnp.dot(jnp.concatenate([a_hi, a_hi, a_lo], axis=1), jnp.concatenate([b_hi, b_lo, b_hi], axis=0),
                   preferred_element_type=F32)


def _row_slice(rows_ref, idx, p, half):
    return rows_ref[idx, pl.ds(p, 1), half * CHUNK:(half + 1) * CHUNK]


def _head_out_tiles(acc_refs, base, nw_ref, z_ref, y_ref, n_rows, og_ref=None):
    tile = _pick_tile(n_rows, 256, 8)

    def body(i, carry):
        r = pl.multiple_of(i * tile, tile)
        o = acc_refs[0][pl.ds(base + r, tile), :]
        for acc_ref in acc_refs[1:]:
            o = o + acc_ref[pl.ds(base + r, tile), :]
        if og_ref is not None:
            o = jax.nn.sigmoid(og_ref[0, pl.ds(r, tile), :]) * o
        o = o * lax.rsqrt(jnp.mean(o * o, axis=-1, keepdims=True) + NORM_EPS) * nw_ref[...]
        y_ref[0, pl.ds(r, tile), :] = (o * _silu(z_ref[0, pl.ds(r, tile), :])).astype(y_ref.dtype)
        return carry

    lax.fori_loop(0, n_rows // tile, body, 0)


def _ml_rows(g_refs, bias_ref, h, n_heads, rows_ref):
    li_f = g_refs[0][0, 0] + bias_ref[h]
    lf_f = -_softplus(-(g_refs[1][0, 0] + bias_ref[n_heads + h]))
    li_b = g_refs[2][0, 0] + bias_ref[2 * n_heads + h]
    lf_b = -_softplus(-(g_refs[3][0, 0] + bias_ref[3 * n_heads + h]))
    rows_ref[0] = _seg_cumsum(lf_f, False)
    rows_ref[1] = li_f
    rows_ref[2] = _seg_cumsum(lf_b, True)
    rows_ref[3] = li_b


def _ml_pairs_prep(q_ref, k_ref, v_ref, rows_ref, chains):
    eye = _pair_masks(0)[0]
    lane_i = lax.broadcasted_iota(jnp.int32, (1, LANES), 1)
    row_i = lax.broadcasted_iota(jnp.int32, (LANES, 1), 0)
    lane_sq = lax.broadcasted_iota(jnp.int32, (LANES, LANES), 1)
    ones = jnp.ones((LANES, LANES), BF16)
    st = []
    for d, p in chains:
        rows = pl.ds(pl.multiple_of(p * LANES, LANES), LANES)
        q = q_ref[0, rows, :]
        k = k_ref[0, rows, :] * (HEAD_DIM ** -0.5)
        b_row = rows_ref[2 * d, pl.ds(p, 1), :]
        li_row = rows_ref[2 * d + 1, pl.ds(p, 1), :]
        last = CHUNK - 1 if d == 0 else 0
        b_tot = (b_row[:, last:last + 1], b_row[:, CHUNK + last:CHUNK + last + 1])
        end_row = jnp.where(lane_i < CHUNK, b_tot[0], b_tot[1]) - b_row + li_row
        e_max = (jnp.max(end_row[:, :CHUNK], axis=1, keepdims=True),
                 jnp.max(end_row[:, CHUNK:], axis=1, keepdims=True))
        st.append(dict(d=d, rows=rows, q=q, k=k, b_row=b_row, li_row=li_row, b_tot=b_tot, e_max=e_max,
                       b_cb=_col_bcast(b_row, eye), end_cb=_col_bcast(end_row, eye), qk=_dot_nt(q, k)))
    for c in st:
        _, incl, _ = _pair_masks(c["d"])
        v_ones = jnp.concatenate([v_ref[0, c["rows"], :].astype(BF16), ones], axis=1)
        dlog = jnp.where(incl, c["b_cb"] - c["b_row"] + c["li_row"], -jnp.inf)
        rowmax = jnp.max(dlog, axis=1, keepdims=True)
        p_hi, p_lo = _split_bf16(jnp.exp(dlog - rowmax) * c["qk"])
        rhs = jnp.concatenate([v_ones, jnp.concatenate([jnp.zeros_like(ones), ones], axis=1)], axis=0)
        c["rowmax"] = rowmax
        c["pv_ps"] = jnp.dot(jnp.concatenate([p_hi, p_lo], axis=1), rhs, preferred_element_type=F32)
        kw_t = (c["k"] * jnp.exp(c["end_cb"] - jnp.where(row_i < CHUNK, c["e_max"][0], c["e_max"][1]))).T
        kw_t2 = jnp.concatenate([jnp.where(lane_sq < CHUNK, kw_t, 0.0), jnp.where(lane_sq < CHUNK, 0.0, kw_t)],
                                axis=0)
        c["d_cn"] = jnp.dot(kw_t2.astype(BF16), v_ones, preferred_element_type=F32)
    return st


def _ml_steps(pre, acc_refs, cnst, mst, chains):
    st = []
    for d, p, half in chains:
        sl = slice(half * CHUNK, (half + 1) * CHUNK)
        cn = cnst[d]
        st.append((sl, cn, _dot(pre[d]["q"][sl], cn)))
    for (d, p, half), (sl, cn, q_cn) in zip(chains, st):
        pr = pre[d]
        m_s = mst[d, 0:1, 0:1]
        rowmax = pr["rowmax"][sl]
        inter = pr["b_cb"][sl] + m_s
        m_i = jnp.maximum(inter, rowmax)
        w_inter = jnp.exp(inter - m_i)
        s_intra = jnp.exp(rowmax - m_i)
        num = w_inter * q_cn[:, :HEAD_DIM] + s_intra * pr["pv_ps"][sl, :HEAD_DIM]
        den = w_inter * q_cn[:, HEAD_DIM:] + s_intra * pr["pv_ps"][sl, HEAD_DIM:]
        r0 = pl.ds(pl.multiple_of(p * LANES + half * CHUNK, CHUNK), CHUNK)
        acc_refs[d][r0, :] = num / jnp.maximum(jnp.abs(den), jnp.exp(-m_i))
        carry_log = pr["b_tot"][half] + m_s
        e_max = pr["e_max"][half]
        m_new = jnp.maximum(carry_log, e_max)
        cnst[d] = jnp.exp(carry_log - m_new) * cn + jnp.exp(e_max - m_new) * pr["d_cn"][half * LANES:(half + 1) * LANES]
        mst[d] = jnp.broadcast_to(m_new, mst.shape[1:])


def _ml_scan(q_ref, k_ref, v_ref, rows_ref, acc_refs, cnst, mst, n_pairs):
    def body(p, carry):
        pb = n_pairs - 1 - p
        pre = _ml_pairs_prep(q_ref, k_ref, v_ref, rows_ref, [(0, p), (1, pb)])
        _ml_steps(pre, acc_refs, cnst, mst, [(0, p, 0), (1, pb, 1)])
        _ml_steps(pre, acc_refs, cnst, mst, [(0, p, 1), (1, pb, 0)])
        return carry

    lax.fori_loop(0, n_pairs, body, 0)


def _mlstm_kernel(bias_ref, ql, kl, vl, ol, zl, qc, kc, vc, oc, zc, gl0, gl1, gl2, gl3, gc0, gc1, gc2, gc3,
                  nw_ref, *rest, n_heads, ctx_out):
    if ctx_out:
        yl_ref, yc_ref = rest[:2]
        rest = rest[2:]
    else:
        yl_ref, yc_ref = rest[0], None
        rest = rest[1:]
    cnst, mst, rows_l, rows_c, acc_lf, acc_lb, acc_cf, acc_cb = rest
    h = pl.program_id(1)
    cnst[...] = jnp.zeros(cnst.shape, F32)
    mst[...] = jnp.zeros(mst.shape, F32)
    _ml_rows((gc0, gc1, gc2, gc3), bias_ref, h, n_heads, rows_c)
    _ml_scan(qc, kc, vc, rows_c, (acc_cf, acc_cb), cnst, mst, rows_c.shape[1])
    _ml_rows((gl0, gl1, gl2, gl3), bias_ref, h, n_heads, rows_l)
    _ml_scan(ql, kl, vl, rows_l, (acc_lf, acc_lb), cnst, mst, rows_l.shape[1])
    _head_out_tiles((acc_lf, acc_lb), 0, nw_ref, zl, yl_ref, acc_lf.shape[0], og_ref=ol)
    if ctx_out:
        _head_out_tiles((acc_cf, acc_cb), 0, nw_ref, zc, yc_ref, acc_cf.shape[0], og_ref=oc)


def _head_block(l, blk):
    return pl.BlockSpec((1, l, HEAD_DIM), lambda b, h, blk=blk: (b, 0, blk + h))


def _gate_block(n_pairs, j, n_heads):
    return pl.BlockSpec((1, 1, n_pairs, LANES), lambda b, h, j=j: (b, j * n_heads + h, 0, 0))


def _mlstm(u_l, u_c, gt_l, gt_c, bias, nw, lay, ctx_out):
    b, l, _ = u_l.shape
    lc = u_c.shape[1]
    nh = lay["mh"]
    blk = lambda off: off // HEAD_DIM
    offs = [lay["MQ"], lay["MQ"] + lay["DM"], lay["MQ"] + 2 * lay["DM"], lay["MO"], lay["MZ"]]
    in_specs = [pl.BlockSpec(memory_space=pltpu.SMEM)]
    in_specs += [_head_block(l, blk(o)) for o in offs]
    in_specs += [_head_block(lc, blk(o)) for o in offs]
    in_specs += [_gate_block(l // LANES, j, nh) for j in range(4)]
    in_specs += [_gate_block(lc // LANES, j, nh) for j in range(4)]
    in_specs += [pl.BlockSpec((1, HEAD_DIM), lambda b_, h: (0, 0))]
    out_specs = [pl.BlockSpec((1, l, HEAD_DIM), lambda b_, h: (b_, 0, h))]
    out_shape = [jax.ShapeDtypeStruct((b, l, nh * HEAD_DIM), BF16)]
    if ctx_out:
        out_specs.append(pl.BlockSpec((1, lc, HEAD_DIM), lambda b_, h: (b_, 0, h)))
        out_shape.append(jax.ShapeDtypeStruct((b, lc, nh * HEAD_DIM), BF16))
    scratch = [pltpu.VMEM((2, HEAD_DIM, 2 * HEAD_DIM), F32), pltpu.VMEM((2, 8, LANES), F32),
               pltpu.VMEM((4, l // LANES, LANES), F32), pltpu.VMEM((4, lc // LANES, LANES), F32),
               pltpu.VMEM((l, HEAD_DIM), F32), pltpu.VMEM((l, HEAD_DIM), F32),
               pltpu.VMEM((lc, HEAD_DIM), F32), pltpu.VMEM((lc, HEAD_DIM), F32)]
    res = pl.pallas_call(
        functools.partial(_mlstm_kernel, n_heads=nh, ctx_out=ctx_out),
        grid=(b, nh),
        in_specs=in_specs, out_specs=out_specs, out_shape=out_shape,
        scratch_shapes=scratch,
        compiler_params=_cparams(("parallel", "parallel"), 48),
        name="mlstm",
    )(bias.reshape(-1), *([u_l] * 5), *([u_c] * 5), *([gt_l] * 4), *([gt_c] * 4), nw)
    return (res[0], res[1]) if ctx_out else (res[0], None)


def _gdn_rows(g_refs, alog_ref, dtb_ref, h, n_heads, rows_ref, base):
    n = g_refs[0].shape[2]
    g_f = -jnp.exp(alog_ref[h]) * _softplus(g_refs[0][0, 0] + dtb_ref[h])
    g_b = -jnp.exp(alog_ref[n_heads + h]) * _softplus(g_refs[1][0, 0] + dtb_ref[n_heads + h])
    rows_ref[0, base:base + n] = _seg_cumsum(g_f, False)
    rows_ref[1, base:base + n] = jax.nn.sigmoid(g_refs[2][0, 0])
    rows_ref[2, base:base + n] = _seg_cumsum(g_b, True)
    rows_ref[3, base:base + n] = jax.nn.sigmoid(g_refs[3][0, 0])


def _gdn_conv(src_refs, w_refs, dst_refs, base, n_units, unit_rows, period):
    def body(i, carry):
        r = pl.multiple_of(i * unit_rows, unit_rows)
        for idx in range(3):
            t = _silu(_conv_rows(src_refs[idx][0, pl.ds(r, unit_rows), :], w_refs[idx][...], period))
            if idx < 2:
                t = t * lax.rsqrt(jnp.sum(t * t, axis=-1, keepdims=True) + NORM_EPS)
            if idx == 0:
                t = t * (HEAD_DIM ** -0.5)
            dst_refs[idx][pl.ds(base + r, unit_rows), :] = t
        return carry

    lax.fori_loop(0, n_units, body, 0)


N_NEUMANN = CHUNK.bit_length() - 2


def _gdn_prep_start(qs, ks, rows_ref, aqk_s, chains):
    eye = _pair_masks(0)[0]
    eye_f = jnp.where(eye, 1.0, 0.0)
    st = []
    for d, p in chains:
        rows = pl.ds(pl.multiple_of(p * LANES, LANES), LANES)
        k = ks[rows, :]
        _, incl, strict = _pair_masks(d)
        g_row = rows_ref[2 * d, pl.ds(p, 1), :]
        beta_col = _to_col(rows_ref[2 * d + 1, pl.ds(p, 1), :], eye)
        g_col = _to_col(g_row, eye)
        dec = jnp.exp(jnp.where(incl, g_col - g_row, -jnp.inf))
        aqk_s[d, rows, :] = (_dot_nt(qs[rows, :], k) * dec).astype(BF16)
        m_low = jnp.where(strict, beta_col * _dot_nt(k, k) * dec, 0.0)
        st.append(dict(d=d, rows=rows, g_row=g_row, g_col=g_col, beta_col=beta_col, pw=m_low, t_inv=eye_f - m_low))
    return st


def _gdn_prep_neumann(st, n_iter):
    eye_f = jnp.where(_pair_masks(0)[0], 1.0, 0.0)
    for _ in range(n_iter):
        for c in st:
            c["pw"] = _dot3(c["pw"], c["pw"])
        for c in st:
            c["t_inv"] = _dot3(c["t_inv"], eye_f + c["pw"])


def _gdn_prep_finish(qs, ks, vs, ut_s, w_s, qd_s, ket_s, st):
    row_i = lax.broadcasted_iota(jnp.int32, (LANES, 1), 0)
    for c in st:
        d, rows, g_row, g_col, beta_col = c["d"], c["rows"], c["g_row"], c["g_col"], c["beta_col"]
        k = ks[rows, :]
        e_g = jnp.exp(g_col)
        sol = _dot3(c["t_inv"], jnp.concatenate([beta_col * vs[rows, :], (beta_col * e_g) * k], axis=1))
        ut_s[d, rows, :] = sol[:, :HEAD_DIM]
        w_s[d, rows, :] = sol[:, HEAD_DIM:].astype(BF16)
        qd_s[d, rows, :] = (qs[rows, :] * e_g).astype(BF16)
        last = CHUNK - 1 if d == 0 else 0
        g_tot_col = jnp.where(row_i < CHUNK, g_row[:, last:last + 1], g_row[:, CHUNK + last:CHUNK + last + 1])
        ket_s[d, rows, :] = (k * jnp.exp(g_tot_col - g_col)).T.astype(BF16)


def _gdn_steps(rows_ref, ut_s, w_s, qd_s, ket_s, aqk_s, sst, acc_refs, chains):
    st = []
    for d, p, half in chains:
        r0 = pl.ds(pl.multiple_of(p * LANES + half * CHUNK, CHUNK), CHUNK)
        s = sst[d]
        lhs = jnp.concatenate([w_s[d, r0, :], qd_s[d, r0, :]], axis=0)
        st.append((r0, s, jnp.dot(lhs, s.astype(BF16), preferred_element_type=F32)))
    for (d, p, half), (r0, s, ws_qs) in zip(chains, st):
        rp = pl.ds(pl.multiple_of(p * LANES, LANES), LANES)
        lane = half * CHUNK + (CHUNK - 1 if d == 0 else 0)
        g_tot = rows_ref[2 * d, pl.ds(p, 1), lane:lane + 1]
        u = (ut_s[d, r0, :] - ws_qs[:CHUNK]).astype(BF16)
        zero = jnp.zeros_like(u)
        u_pad = jnp.concatenate([u, zero] if half == 0 else [zero, u], axis=0)
        lhs2 = jnp.concatenate([aqk_s[d, r0, :], ket_s[d, rp, :]], axis=0)
        au_ku = jnp.dot(lhs2, u_pad, preferred_element_type=F32)
        acc_refs[d][r0, :] = ws_qs[CHUNK:] + au_ku[:CHUNK]
        sst[d] = jnp.exp(g_tot) * s + au_ku[CHUNK:]


def _gdn_kernel(alog_ref, dtb_ref, ql, kl, vl, zl, qc, kc, vc, zc, gl0, gl1, gl2, gl3, gc0, gc1, gc2, gc3,
                wq, wk, wv, nw_ref, *rest, n_heads, ctx_out):
    if ctx_out:
        yl_ref, yc_ref = rest[:2]
        rest = rest[2:]
    else:
        yl_ref, yc_ref = rest[0], None
        rest = rest[1:]
    sst, rows, qs, ks, vs, ut_s, w_s, qd_s, ket_s, aqk_s, acc_f, acc_b = rest
    h = pl.program_id(1)
    n_l = ql.shape[1]
    n_c = qc.shape[1]
    nl = n_l // LANES
    nc = n_c // LANES
    sst[...] = jnp.zeros(sst.shape, F32)
    _gdn_rows((gl0, gl1, gl2, gl3), alog_ref, dtb_ref, h, n_heads, rows, 0)
    _gdn_rows((gc0, gc1, gc2, gc3), alog_ref, dtb_ref, h, n_heads, rows, nl)
    conv_rows = _pick_tile(n_l, 4 * CHUNK, CHUNK)
    _gdn_conv((ql, kl, vl), (wq, wk, wv), (qs, ks, vs), 0, n_l // conv_rows, conv_rows, CHUNK)
    _gdn_conv((qc, kc, vc), (wq, wk, wv), (qs, ks, vs), n_l, 1, n_c, n_c)
    prepped = (ut_s, w_s, qd_s, ket_s, aqk_s)
    n_steps = nl + nc

    def fwd_pair(t):
        return jnp.where(t < nc, nl + t, t - nc)

    def bwd_pair(t):
        return n_steps - 1 - t

    def prep(chains, between=()):
        st = _gdn_prep_start(qs, ks, rows, aqk_s, chains)
        for i in range(N_NEUMANN):
            _gdn_prep_neumann(st, 1)
            if i < len(between) and between[i] is not None:
                between[i]()
        _gdn_prep_finish(qs, ks, vs, ut_s, w_s, qd_s, ket_s, st)

    def chains_of(t):
        return [(0, fwd_pair(t)), (1, bwd_pair(t)), (0, fwd_pair(t + 1)), (1, bwd_pair(t + 1))]

    prep(chains_of(0))

    def step(i, carry):
        t = 2 * i
        halves = []
        for tt in (t, t + 1):
            pf, pb = fwd_pair(tt), bwd_pair(tt)
            halves.append(functools.partial(_gdn_steps, rows, *prepped, sst, (acc_f, acc_b), [(0, pf, 0), (1, pb, 1)]))
            halves.append(functools.partial(_gdn_steps, rows, *prepped, sst, (acc_f, acc_b), [(0, pf, 1), (1, pb, 0)]))
        prep(chains_of(jnp.minimum(t + 2, n_steps - 2)), between=halves)
        return carry

    lax.fori_loop(0, n_steps // 2, step, 0)
    _head_out_tiles((acc_f, acc_b), 0, nw_ref, zl, yl_ref, n_l)
    if ctx_out:
        _head_out_tiles((acc_f, acc_b), n_l, nw_ref, zc, yc_ref, n_c)


def _gdn(u_l, u_c, gt_l, gt_c, conv_w, a_log, dt_bias, nw, lay, ctx_out):
    b, l, _ = u_l.shape
    lc = u_c.shape[1]
    nh = lay["gh"]
    taps = conv_w.shape[0]
    blk = lambda off: off // HEAD_DIM
    offs = [lay["GQ"], lay["GQ"] + lay["DG"], lay["GQ"] + 2 * lay["DG"], lay["GZ"]]
    smem = pl.BlockSpec(memory_space=pltpu.SMEM)
    in_specs = [smem, smem]
    in_specs += [_head_block(l, blk(o)) for o in offs]
    in_specs += [_head_block(lc, blk(o)) for o in offs]
    in_specs += [_gate_block(l // LANES, j, nh) for j in range(4)]
    in_specs += [_gate_block(lc // LANES, j, nh) for j in range(4)]
    in_specs += [pl.BlockSpec((taps, HEAD_DIM), lambda b_, h, j=j: (0, j * nh + h)) for j in range(3)]
    in_specs += [pl.BlockSpec((1, HEAD_DIM), lambda b_, h: (0, 0))]
    out_specs = [pl.BlockSpec((1, l, HEAD_DIM), lambda b_, h: (b_, 0, h))]
    out_shape = [jax.ShapeDtypeStruct((b, l, nh * HEAD_DIM), BF16)]
    if ctx_out:
        out_specs.append(pl.BlockSpec((1, lc, HEAD_DIM), lambda b_, h: (b_, 0, h)))
        out_shape.append(jax.ShapeDtypeStruct((b, lc, nh * HEAD_DIM), BF16))
    lt = l + lc
    seq_f32 = pltpu.VMEM((lt, HEAD_DIM), F32)
    dir_bf16 = pltpu.VMEM((2, lt, HEAD_DIM), BF16)
    scratch = [pltpu.VMEM((2, HEAD_DIM, HEAD_DIM), F32), pltpu.VMEM((4, lt // LANES, LANES), F32),
               seq_f32, seq_f32, seq_f32,
               pltpu.VMEM((2, lt, HEAD_DIM), F32), dir_bf16, dir_bf16, dir_bf16, dir_bf16,
               seq_f32, seq_f32]
    res = pl.pallas_call(
        functools.partial(_gdn_kernel, n_heads=nh, ctx_out=ctx_out),
        grid=(b, nh),
        in_specs=in_specs, out_specs=out_specs, out_shape=out_shape,
        scratch_shapes=scratch,
        compiler_params=_cparams(("parallel", "parallel"), 56),
        name="gdn",
    )(a_log.reshape(-1), dt_bias.reshape(-1), *([u_l] * 4), *([u_c] * 4), *([gt_l] * 4), *([gt_c] * 4),
      *([conv_w] * 3), nw)
    return (res[0], res[1]) if ctx_out else (res[0], None)


def _dft_tables(l):
    k = jnp.arange(l, dtype=jnp.int32)
    ang = ((k[:, None] * k[None, :]) % (2 * l)).astype(F32) * (math.pi / l)
    alt = jnp.where(k % 2 == 0, 1.0, -1.0).astype(F32)
    sin_f = jnp.sin(ang).at[0, :].set(alt)
    return jnp.cos(ang).astype(BF16), sin_f.astype(BF16), sin_f.T.astype(BF16)


def _hy_prep_kernel(p_ref, w_ref, x1_ref, x2_ref, v_ref, vb_ref, *, period, dh):
    y = _conv_rows(p_ref[0], w_ref[...], period)
    x1_ref[0] = y[:, :dh]
    x2_ref[0] = y[:, dh:2 * dh]
    v = y[:, 2 * dh:]
    v_ref[0] = v
    vb_ref[0] = v.astype(BF16)


def _hy_prep(u, conv_w, dh, grid_mask):
    b, l, _ = u.shape
    taps = conv_w.shape[0]
    tr = _pick_tile(l, 256, CHUNK)
    period = CHUNK if grid_mask else l
    if not grid_mask:
        tr = l
    o_spec = pl.BlockSpec((1, tr, dh), lambda i, m: (i, m, 0))
    return pl.pallas_call(
        functools.partial(_hy_prep_kernel, period=period, dh=dh),
        grid=(b, l // tr),
        in_specs=[pl.BlockSpec((1, tr, 3 * dh), lambda i, m: (i, m, 0)),
                  pl.BlockSpec((taps, 3 * dh), lambda i, m: (0, 0))],
        out_specs=[o_spec] * 4,
        out_shape=[jax.ShapeDtypeStruct((b, l, dh), F32)] * 3 + [jax.ShapeDtypeStruct((b, l, dh), BF16)],
        compiler_params=_cparams(("parallel", "parallel"), 40),
        name="hy_prep",
    )(u, conv_w)


def _filter_kernel(feats_ref, w1_ref, b1_ref, fr_ref, w2_ref, b2_ref, w3c_ref, w3a_ref, dl_ref,
                   hs_ref, hd_ref, nyq_ref, hid_ref):
    @pl.when((pl.program_id(0) == 0) & (pl.program_id(1) == 0))
    def _():
        hid = jnp.sin(fr_ref[...] * (_dot_hi(feats_ref[...], w1_ref[...]) + b1_ref[...]))
        hid_ref[...] = jnp.sin(fr_ref[...] * (_dot_hi(hid, w2_ref[...]) + b2_ref[...]))

    hid = hid_ref[...]
    win = jnp.exp(-feats_ref[:, 0:1] * dl_ref[...])

    def one(w3_ref):
        f = _dot_hi(hid, w3_ref[...]) * win
        return f / (jnp.sum(jnp.abs(f), axis=0, keepdims=True) + NORM_EPS)

    h_c = one(w3c_ref)
    h_a = one(w3a_ref)
    h_sum = h_c + h_a
    sign = jnp.where(lax.broadcasted_iota(jnp.int32, h_sum.shape, 0) % 2 == 0, 1.0, -1.0)
    nyq_ref[0] = jnp.broadcast_to(jnp.sum(h_sum * sign, axis=0, keepdims=True), nyq_ref.shape[1:])
    hs_ref[0] = h_sum.astype(BF16)
    hd_ref[0] = (h_c - h_a).astype(BF16)


def _hy_filters(l, w1, b1, freq, w2, b2, w3, dh):
    pos = jnp.arange(l, dtype=F32)
    t = pos / max(l - 1, 1)
    ang = 2.0 * math.pi * pos / l
    bands = jnp.linspace(1e-4, FILTER_BANDS - 1, FILTER_BANDS, dtype=F32)
    feats = jnp.concatenate([t[:, None], jnp.cos(ang[:, None] * bands), -jnp.sin(ang[:, None] * bands)], axis=-1)
    n_emb, n_hid = w1.shape
    feats = jnp.pad(feats, ((0, 0), (0, LANES - n_emb)))
    pc = LANES - n_hid
    w1p = jnp.pad(w1, ((0, LANES - n_emb), (0, pc)))
    w2p = jnp.pad(w2, ((0, pc), (0, pc)))
    w3p = jnp.pad(w3, ((0, pc), (0, 0)))
    row = lambda a: jnp.pad(a, (0, pc))[None]
    deltas = jnp.abs(jnp.linspace(MIN_DECAY, MAX_DECAY, dh, dtype=F32))[None]
    nct = dh // LANES
    const = lambda shape: pl.BlockSpec(shape, lambda o, c: (0, 0))
    o_spec = pl.BlockSpec((1, l, LANES), lambda o, c: (o, 0, c))
    return pl.pallas_call(
        _filter_kernel,
        grid=(HYENA_ORDER, nct),
        in_specs=[const((l, LANES)), const((LANES, LANES)), const((1, LANES)), const((1, LANES)),
                  const((LANES, LANES)), const((1, LANES)),
                  pl.BlockSpec((LANES, LANES), lambda o, c: (0, o * 2 * nct + c)),
                  pl.BlockSpec((LANES, LANES), lambda o, c: (0, o * 2 * nct + nct + c)),
                  pl.BlockSpec((1, LANES), lambda o, c: (0, c))],
        out_specs=[o_spec, o_spec, pl.BlockSpec((1, 8, LANES), lambda o, c: (o, 0, c))],
        out_shape=[jax.ShapeDtypeStruct((HYENA_ORDER, l, dh), BF16)] * 2
        + [jax.ShapeDtypeStruct((HYENA_ORDER, 8, dh), F32)],
        scratch_shapes=[pltpu.VMEM((l, LANES), F32)],
        compiler_params=_cparams(("arbitrary", "arbitrary"), 48),
        name="hy_filter",
    )(feats, w1p, row(b1), row(freq), w2p, row(b2), w3p, w3p, deltas)


def _spectrum_kernel(c_ref, s_ref, hs_ref, hd_ref, nyq_ref, a_ref, a2_ref, hsin_ref, *, l, tk):
    a = jnp.dot(c_ref[...], hs_ref[0], preferred_element_type=F32)
    s = jnp.dot(s_ref[...], hd_ref[0], preferred_element_type=F32)
    is0 = (lax.broadcasted_iota(jnp.int32, a.shape, 0) + pl.program_id(1) * tk) == 0
    wk = jnp.where(is0, 0.5 / l, 1.0 / l)
    a_ref[0] = a * wk
    a2_ref[0] = jnp.where(is0, nyq_ref[0, 0:1, :], a) * wk
    hsin_ref[0] = jnp.where(is0, 0.0, s) * wk


def _hy_spectrum(tabs, hs, hd, nyq):
    cos_t, sin_f, _ = tabs
    _, l, dh = hs.shape
    tk = _pick_tile(l, 512, 8)
    tab = pl.BlockSpec((tk, l), lambda o, k: (k, 0))
    filt = pl.BlockSpec((1, l, dh), lambda o, k: (o, 0, 0))
    o_spec = pl.BlockSpec((1, tk, dh), lambda o, k: (o, k, 0))
    return pl.pallas_call(
        functools.partial(_spectrum_kernel, l=l, tk=tk),
        grid=(HYENA_ORDER, l // tk),
        in_specs=[tab, tab, filt, filt, pl.BlockSpec((1, 8, dh), lambda o, k: (o, 0, 0))],
        out_specs=[o_spec] * 3,
        out_shape=[jax.ShapeDtypeStruct((HYENA_ORDER, l, dh), F32)] * 3,
        compiler_params=_cparams(("parallel", "parallel"), 48),
        name="hy_spectrum",
    )(cos_t, sin_f, hs, hd, nyq)


def _hy_fwd_kernel(c_ref, s_ref, v_ref, a_ref, a2_ref, hsin_ref, yc_ref, ys_ref):
    v = v_ref[0]
    xc = jnp.dot(c_ref[...], v, preferred_element_type=F32)
    xs = jnp.dot(s_ref[...], v, preferred_element_type=F32)
    hsin = hsin_ref[0]
    yc_ref[0] = (xc * a_ref[0] - xs * hsin).astype(BF16)
    ys_ref[0] = (xc * hsin + xs * a2_ref[0]).astype(BF16)


def _hy_fwd(tabs, vb, spec, order):
    cos_t, sin_f, _ = tabs
    b, l, dh = vb.shape
    tk = _pick_tile(l, 512, 8)
    tab = pl.BlockSpec((tk, l), lambda k, i: (k, 0))
    filt = pl.BlockSpec((1, tk, dh), lambda k, i: (order, k, 0))
    o_spec = pl.BlockSpec((1, tk, dh), lambda k, i: (i, k, 0))
    return pl.pallas_call(
        _hy_fwd_kernel,
        grid=(l // tk, b),
        in_specs=[tab, tab, pl.BlockSpec((1, l, dh), lambda k, i: (i, 0, 0)), filt, filt, filt],
        out_specs=[o_spec] * 2,
        out_shape=[jax.ShapeDtypeStruct((b, l, dh), BF16)] * 2,
        compiler_params=_cparams(("parallel", "parallel"), 48),
        name="hy_fwd",
    )(cos_t, sin_f, vb, *spec)


def _hy_inv_kernel(c_ref, s_ref, yc_ref, ys_ref, x_ref, v_ref, sk_ref, *rest, last):
    conv = jnp.dot(c_ref[...], yc_ref[0], preferred_element_type=F32)
    conv = conv + jnp.dot(s_ref[...], ys_ref[0], preferred_element_type=F32)
    y = x_ref[0] * (conv + sk_ref[0] * v_ref[0])
    if last:
        z_ref, o_ref = rest
        o_ref[0] = (y * _silu(z_ref[0])).astype(BF16)
    else:
        of_ref, ob_ref = rest
        of_ref[0] = y
        ob_ref[0] = y.astype(BF16)


def _hy_inv(tabs, yc, ys, x, v, skip, order, u=None, z_blk=None):
    cos_t, _, sin_i = tabs
    b, l, dh = yc.shape
    last = u is not None
    tt = _pick_tile(l, 512, 8)
    tab = pl.BlockSpec((tt, l), lambda t, i: (t, 0))
    full = pl.BlockSpec((1, l, dh), lambda t, i: (i, 0, 0))
    tile = pl.BlockSpec((1, tt, dh), lambda t, i: (i, t, 0))
    in_specs = [tab, tab, full, full, tile, tile, pl.BlockSpec((1, 1, dh), lambda t, i: (order, 0, 0))]
    args = [cos_t, sin_i, yc, ys, x, v, skip[:, None, :]]
    if last:
        in_specs.append(pl.BlockSpec((1, tt, dh), lambda t, i: (i, t, z_blk)))
        args.append(u)
        out_specs = tile
        out_shape = jax.ShapeDtypeStruct((b, l, dh), BF16)
    else:
        out_specs = [tile, tile]
        out_shape = [jax.ShapeDtypeStruct((b, l, dh), F32), jax.ShapeDtypeStruct((b, l, dh), BF16)]
    return pl.pallas_call(
        functools.partial(_hy_inv_kernel, last=last),
        grid=(l // tt, b),
        in_specs=in_specs, out_specs=out_specs, out_shape=out_shape,
        compiler_params=_cparams(("parallel", "parallel"), 48),
        name="hy_inv",
    )(*args)


def _hyena(u, tabs, conv_w, w1, b1, freq, w2, b2, w3, skip, lay, grid_mask):
    dh = lay["DH"]
    l = u.shape[1]
    x1, x2, v, vb = _hy_prep(u, conv_w, dh, grid_mask)
    spec = _hy_spectrum(tabs, *_hy_filters(l, w1, b1, freq, w2, b2, w3, dh))
    yc, ys = _hy_fwd(tabs, vb, spec, 0)
    y1, y1b = _hy_inv(tabs, yc, ys, x1, v, skip, 0)
    yc, ys = _hy_fwd(tabs, y1b, spec, 1)
    return _hy_inv(tabs, yc, ys, x2, y1, skip, 1, u=u, z_blk=lay["HZ"] // dh)


def _layout(d):
    dg, dh, dm = 3 * d // 8, d // 4, 3 * d // 8
    lay = {"DG": dg, "DH": dh, "DM": dm, "gh": dg // HEAD_DIM, "mh": dm // HEAD_DIM}
    lay["HZ"] = 3 * dh
    lay["GQ"] = 4 * dh
    lay["GZ"] = lay["GQ"] + 3 * dg
    lay["GAB"] = lay["GZ"] + dg
    lay["MQ"] = lay["GAB"] + LANES
    lay["MO"] = lay["MQ"] + 3 * dm
    lay["MZ"] = lay["MO"] + dm
    lay["MG"] = lay["MZ"] + dm
    lay["NP"] = lay["MG"] + LANES
    return lay


def _pack_w_in(w, lay):
    dg, dh, dm, gh, mh = lay["DG"], lay["DH"], lay["DM"], lay["gh"], lay["mh"]
    sizes = (3 * dg, dg, 4 * gh, 3 * dh, dh, 3 * dm, dm, dm, 4 * mh)
    offs = [0]
    for s in sizes:
        offs.append(offs[-1] + s)
    seg = [w[:, offs[i]:offs[i + 1]] for i in range(len(sizes))]
    g_qkv, g_z, g_ab, h_p, h_z, m_qkv, m_o, m_z, m_g = seg
    padl = lambda a: jnp.pad(a, ((0, 0), (0, LANES - a.shape[1])))
    return jnp.concatenate([h_p, h_z, g_qkv, g_z, padl(g_ab), m_qkv, m_o, m_z, padl(m_g)], axis=1).astype(BF16)


def _gate_rows(u, off, n):
    b, l, _ = u.shape
    return jnp.transpose(u[:, :, off:off + n], (0, 2, 1)).reshape(b, n, l // LANES, LANES)


def kernel(x, c, ctx, c_ctx, norm_w, mod_w, mod_b, w_in, gdn_conv, gdn_a_log, gdn_dt_bias, gdn_norm, hy_conv,
           hy_w1, hy_b1, hy_freq, hy_w2, hy_b2, hy_w3, hy_skip, ml_gate_bias, ml_norm, w_out, final_norm):
    b, l, d = x.shape
    lc = ctx.shape[1]
    depth = norm_w.shape[0]
    lay = _layout(d)
    dg, dh = lay["DG"], lay["DH"]
    assert b < COND_ROWS and l % (2 * LANES) == 0 and lc % (2 * LANES) == 0 and d % 1024 == 0
    cond = jnp.zeros((COND_ROWS, d), F32).at[:b].set(c).at[b].set(c_ctx)
    tabs_l = _dft_tables(l)
    tabs_c = _dft_tables(lc)
    for layer in range(depth):
        last = layer == depth - 1
        mods = _adaln(cond, mod_w[layer], mod_b[layer][None])
        sh, sc, gt = mods[:, :d], mods[:, d:2 * d], mods[:, 2 * d:]
        lat = lambda m: m[:b, None, :]
        cx = lambda m: jnp.broadcast_to(m[b][None, None, :], (b, 1, d))
        wp = _pack_w_in(w_in[layer], lay)
        nw = norm_w[layer][None]
        u_l = _inproj(x, nw, lat(sc), lat(sh), wp)
        u_c = _inproj(ctx, nw, cx(sc), cx(sh), wp)
        g_rows = lambda u, off, n: _gate_rows(u, off, n)
        yg_l, yg_c = _gdn(u_l, u_c, g_rows(u_l, lay["GAB"], 4 * lay["gh"]), g_rows(u_c, lay["GAB"], 4 * lay["gh"]),
                          gdn_conv[layer], gdn_a_log[layer], gdn_dt_bias[layer], gdn_norm[layer][None], lay,
                          not last)
        ym_l, ym_c = _mlstm(u_l, u_c, g_rows(u_l, lay["MG"], 4 * lay["mh"]), g_rows(u_c, lay["MG"], 4 * lay["mh"]),
                            ml_gate_bias[layer], ml_norm[layer][None], lay, not last)
        hy = (hy_conv[layer], hy_w1[layer], hy_b1[layer], hy_freq[layer], hy_w2[layer], hy_b2[layer],
              hy_w3[layer], hy_skip[layer])
        yh_l = _hyena(u_l, tabs_l, *hy, lay, True)
        wo = w_out[layer].astype(BF16)
        wg, wh, wm = wo[:dg], wo[dg:dg + dh], wo[dg + dh:]
        fw = final_norm[None]
        x = _outproj(x, yg_l, yh_l, ym_l, wg, wh, wm, lat(gt), fw, last)
        if not last:
            yh_c = _hyena(u_c, tabs_c, *hy, lay, False)
            ctx = _outproj(ctx, yg_c, yh_c, ym_c, wg, wh, wm, cx(gt), fw, False)
    return x
```

```python
import functools
import math

import jax
import jax.numpy as jnp
from jax import lax
from jax.experimental import pallas as pl
from jax.experimental.pallas import tpu as pltpu

HEAD_DIM = 128
CHUNK = 64
LANES = 128
NORM_EPS = 1e-6
HYENA_ORDER = 2
FILTER_BANDS = 16
DECAY_TARGET = 1e-2
MIN_DECAY = math.log(DECAY_TARGET) / 1.5
MAX_DECAY = math.log(DECAY_TARGET) / 0.3
COND_ROWS = 16

F32 = jnp.float32
BF16 = jnp.bfloat16
HI = lax.Precision.HIGHEST


def _cparams(sem, vmem_mb):
    return pltpu.CompilerParams(dimension_semantics=sem, vmem_limit_bytes=vmem_mb << 20)


def _dot(a, b):
    return jnp.dot(a.astype(BF16), b.astype(BF16), preferred_element_type=F32)


def _dot_hi(a, b):
    return jnp.dot(a, b, precision=HI, preferred_element_type=F32)


def _dot_nt(a, b):
    return lax.dot_general(a.astype(BF16), b.astype(BF16), (((1,), (1,)), ((), ())),
                           preferred_element_type=F32)


def _dot_tn(a, b):
    return lax.dot_general(a.astype(BF16), b.astype(BF16), (((0,), (0,)), ((), ())),
                           preferred_element_type=F32)


def _silu(x):
    return x * jax.nn.sigmoid(x)


def _softplus(x):
    return jnp.maximum(x, 0.0) + jnp.log(1.0 + jnp.exp(-jnp.abs(x)))


def _pick_tile(n, cap, unit):
    t = (min(n, cap) // unit) * unit
    while n % t:
        t -= unit
    return t


def _adaln_kernel(c_ref, w_ref, b_ref, o_ref):
    o_ref[...] = _dot_hi(_silu(c_ref[...]), w_ref[...]) + b_ref[...]


def _adaln(cond, w, b):
    d, n = w.shape
    tn = _pick_tile(n, 768, LANES)
    return pl.pallas_call(
        _adaln_kernel,
        grid=(n // tn,),
        in_specs=[pl.BlockSpec((COND_ROWS, d), lambda j: (0, 0)),
                  pl.BlockSpec((d, tn), lambda j: (0, j)),
                  pl.BlockSpec((1, tn), lambda j: (0, j))],
        out_specs=pl.BlockSpec((COND_ROWS, tn), lambda j: (0, j)),
        out_shape=jax.ShapeDtypeStruct((COND_ROWS, n), F32),
        compiler_params=_cparams(("parallel",), 40),
        name="adaln",
    )(cond, w, b)


def _inproj_kernel(x_ref, nw_ref, sc_ref, sh_ref, w_ref, o_ref, xn_ref):
    @pl.when(pl.program_id(2) == 0)
    def _():
        x = x_ref[0]
        r = lax.rsqrt(jnp.mean(x * x, axis=-1, keepdims=True) + NORM_EPS)
        y = (x * r * nw_ref[...]) * (1.0 + sc_ref[0]) + sh_ref[0]
        xn_ref[...] = y.astype(BF16)

    o_ref[0] = jnp.dot(xn_ref[...], w_ref[...], preferred_element_type=F32)


def _inproj(x, nw, sc, sh, wp):
    b, l, d = x.shape
    n = wp.shape[1]
    tm = _pick_tile(l, 1024, 8)
    tn = _pick_tile(n, 1280, LANES)
    return pl.pallas_call(
        _inproj_kernel,
        grid=(b, l // tm, n // tn),
        in_specs=[pl.BlockSpec((1, tm, d), lambda i, m, j: (i, m, 0)),
                  pl.BlockSpec((1, d), lambda i, m, j: (0, 0)),
                  pl.BlockSpec((1, 1, d), lambda i, m, j: (i, 0, 0)),
                  pl.BlockSpec((1, 1, d), lambda i, m, j: (i, 0, 0)),
                  pl.BlockSpec((d, tn), lambda i, m, j: (0, j))],
        out_specs=pl.BlockSpec((1, tm, tn), lambda i, m, j: (i, m, j)),
        out_shape=jax.ShapeDtypeStruct((b, l, n), F32),
        scratch_shapes=[pltpu.VMEM((tm, d), BF16)],
        compiler_params=_cparams(("parallel", "parallel", "arbitrary"), 56),
        name="inproj",
    )(x, nw, sc, sh, wp)


def _outproj_kernel(x_ref, yg_ref, yh_ref, ym_ref, wg_ref, wh_ref, wm_ref, gt_ref, fw_ref, o_ref, *, final):
    acc = jnp.dot(yg_ref[0], wg_ref[...], preferred_element_type=F32)
    acc = acc + jnp.dot(yh_ref[0], wh_ref[...], preferred_element_type=F32)
    acc = acc + jnp.dot(ym_ref[0], wm_ref[...], preferred_element_type=F32)
    xn = x_ref[0] + gt_ref[0] * acc
    if final:
        r = lax.rsqrt(jnp.mean(xn * xn, axis=-1, keepdims=True) + NORM_EPS)
        xn = xn * r * fw_ref[...]
    o_ref[0] = xn


def _outproj(x, yg, yh, ym, wg, wh, wm, gt, fw, final):
    b, l, d = x.shape
    tm = _pick_tile(l, 512, 8)
    row = lambda w: pl.BlockSpec((1, tm, w), lambda i, m: (i, m, 0))
    full = lambda a: pl.BlockSpec(a.shape, lambda i, m: (0, 0))
    return pl.pallas_call(
        functools.partial(_outproj_kernel, final=final),
        grid=(b, l // tm),
        in_specs=[row(d), row(yg.shape[2]), row(yh.shape[2]), row(ym.shape[2]),
                  full(wg), full(wh), full(wm),
                  pl.BlockSpec((1, 1, d), lambda i, m: (i, 0, 0)),
                  pl.BlockSpec((1, d), lambda i, m: (0, 0))],
        out_specs=row(d),
        out_shape=jax.ShapeDtypeStruct((b, l, d), F32),
        compiler_params=_cparams(("parallel", "parallel"), 48),
        name="outproj",
    )(x, yg, yh, ym, wg, wh, wm, gt, fw)


def _conv_rows(x, w, period):
    rows = x.shape[0]
    taps = w.shape[0]
    pad = taps // 2
    pos = lax.broadcasted_iota(jnp.int32, x.shape, 0) % period
    y = None
    for j in range(taps):
        off = j - pad
        if off == 0:
            term = x * w[j:j + 1]
        else:
            shifted = pltpu.roll(x, (-off) % rows, axis=0)
            ok = (pos >= -off) if off < 0 else (pos < period - off)
            term = jnp.where(ok, shifted, 0.0) * w[j:j + 1]
        y = term if y is None else y + term
    return y


def _seg_cumsum(x, reverse):
    lane = lax.broadcasted_iota(jnp.int32, x.shape, 1) % CHUNK
    s = 1
    while s < CHUNK:
        if reverse:
            shifted = pltpu.roll(x, LANES - s, axis=1)
            ok = lane < CHUNK - s
        else:
            shifted = pltpu.roll(x, s, axis=1)
            ok = lane >= s
        x = x + jnp.where(ok, shifted, 0.0)
        s *= 2
    return x


def _chunk_masks(d):
    ii = lax.broadcasted_iota(jnp.int32, (CHUNK, CHUNK), 0)
    jj = lax.broadcasted_iota(jnp.int32, (CHUNK, CHUNK), 1)
    eye = ii == jj
    if d == 0:
        return eye, jj <= ii, jj < ii
    return eye, jj >= ii, jj > ii


def _pair_masks(d):
    ii = lax.broadcasted_iota(jnp.int32, (LANES, LANES), 0)
    jj = lax.broadcasted_iota(jnp.int32, (LANES, LANES), 1)
    lo = (ii // CHUNK) * CHUNK
    eye = ii == jj
    if d == 0:
        return eye, (jj >= lo) & (jj <= ii), (jj >= lo) & (jj < ii)
    return eye, (jj < lo + CHUNK) & (jj >= ii), (jj < lo + CHUNK) & (jj > ii)


def _to_col(row, eye):
    return jnp.sum(jnp.where(eye, jnp.broadcast_to(row, eye.shape), 0.0), axis=1, keepdims=True)


def _split_bf16(a):
    hi = a.astype(BF16)
    return hi, (a - hi.astype(F32)).astype(BF16)


def _col_bcast(row, eye):
    x = jnp.where(eye, jnp.broadcast_to(row, eye.shape), 0.0)
    hi = x.astype(BF16)
    mid, lo = _split_bf16(x - hi.astype(F32))
    ones = jnp.ones((3 * eye.shape[1], LANES), BF16)
    return jnp.dot(jnp.concatenate([hi, mid, lo], axis=1), ones, preferred_element_type=F32)


def _dot3(a, b):
    a_hi, a_lo = _split_bf16(a)
    b_hi, b_lo = _split_bf16(b)
    return jnp.dot(jnp.concatenate([a_hi, a_hi, a_lo], axis=1), jnp.concatenate([b_hi, b_lo, b_hi], axis=0),
                   preferred_element_type=F32)


def _dot2(a, b):
    a_hi, a_lo = _split_bf16(a)
    b_hi = b.astype(BF16)
    return jnp.dot(jnp.concatenate([a_hi, a_lo], axis=1), jnp.concatenate([b_hi, b_hi], axis=0),
                   preferred_element_type=F32)


def _row_slice(rows_ref, idx, p, half):
    return rows_ref[idx, pl.ds(p, 1), half * CHUNK:(half + 1) * CHUNK]


def _head_out_tiles(acc_refs, base, nw_ref, z_ref, y_ref, n_rows, og_ref=None):
    tile = _pick_tile(n_rows, 256, 8)

    def body(i, carry):
        r = pl.multiple_of(i * tile, tile)
        o = acc_refs[0][pl.ds(base + r, tile), :]
        for acc_ref in acc_refs[1:]:
            o = o + acc_ref[pl.ds(base + r, tile), :]
        if og_ref is not None:
            o = jax.nn.sigmoid(og_ref[0, pl.ds(r, tile), :]) * o
        o = o * lax.rsqrt(jnp.mean(o * o, axis=-1, keepdims=True) + NORM_EPS) * nw_ref[...]
        y_ref[0, pl.ds(r, tile), :] = (o * _silu(z_ref[0, pl.ds(r, tile), :])).astype(y_ref.dtype)
        return carry

    lax.fori_loop(0, n_rows // tile, body, 0)


def _ml_rows(g_refs, bias_ref, h, n_heads, rows_ref):
    li_f = g_refs[0][0, 0] + bias_ref[h]
    lf_f = -_softplus(-(g_refs[1][0, 0] + bias_ref[n_heads + h]))
    li_b = g_refs[2][0, 0] + bias_ref[2 * n_heads + h]
    lf_b = -_softplus(-(g_refs[3][0, 0] + bias_ref[3 * n_heads + h]))
    rows_ref[0] = _seg_cumsum(lf_f, False)
    rows_ref[1] = li_f
    rows_ref[2] = _seg_cumsum(lf_b, True)
    rows_ref[3] = li_b


def _ml_prep_cols(q_ref, k_ref, rows_ref, chains):
    eye = _pair_masks(0)[0]
    lane_i = lax.broadcasted_iota(jnp.int32, (1, LANES), 1)
    st = []
    for d, p in chains:
        rows = pl.ds(pl.multiple_of(p * LANES, LANES), LANES)
        k = k_ref[0, rows, :] * (HEAD_DIM ** -0.5)
        b_row = rows_ref[2 * d, pl.ds(p, 1), :]
        li_row = rows_ref[2 * d + 1, pl.ds(p, 1), :]
        last = CHUNK - 1 if d == 0 else 0
        b_tot = (b_row[:, last:last + 1], b_row[:, CHUNK + last:CHUNK + last + 1])
        end_row = jnp.where(lane_i < CHUNK, b_tot[0], b_tot[1]) - b_row + li_row
        e_max = (jnp.max(end_row[:, :CHUNK], axis=1, keepdims=True),
                 jnp.max(end_row[:, CHUNK:], axis=1, keepdims=True))
        st.append(dict(d=d, rows=rows, k=k, b_row=b_row, li_row=li_row, b_tot=b_tot, e_max=e_max,
                       b_cb=_col_bcast(b_row, eye), end_cb=_col_bcast(end_row, eye),
                       qk=_dot_nt(q_ref[0, rows, :], k)))
    return st


def _ml_prep_intra(v_ref, st):
    ones = jnp.ones((LANES, LANES), BF16)
    for c in st:
        _, incl, _ = _pair_masks(c["d"])
        c["v_ones"] = jnp.concatenate([v_ref[0, c["rows"], :].astype(BF16), ones], axis=1)
        dlog = jnp.where(incl, c["b_cb"] - c["b_row"] + c["li_row"], -jnp.inf)
        c["rowmax"] = jnp.max(dlog, axis=1, keepdims=True)
        p_hi, p_lo = _split_bf16(jnp.exp(dlog - c["rowmax"]) * c["qk"])
        rhs = jnp.concatenate([c["v_ones"], jnp.concatenate([jnp.zeros_like(ones), ones], axis=1)], axis=0)
        c["pv_ps"] = jnp.dot(jnp.concatenate([p_hi, p_lo], axis=1), rhs, preferred_element_type=F32)


def _ml_prep_state(st):
    row_i = lax.broadcasted_iota(jnp.int32, (LANES, 1), 0)
    lane_sq = lax.broadcasted_iota(jnp.int32, (LANES, LANES), 1)
    out = []
    for c in st:
        kw_t = (c["k"] * jnp.exp(c["end_cb"] - jnp.where(row_i < CHUNK, c["e_max"][0], c["e_max"][1]))).T
        kw_t2 = jnp.concatenate([jnp.where(lane_sq < CHUNK, kw_t, 0.0), jnp.where(lane_sq < CHUNK, 0.0, kw_t)],
                                axis=0)
        d_cn = jnp.dot(kw_t2.astype(BF16), c["v_ones"], preferred_element_type=F32)
        out.append((c["b_cb"], c["rowmax"], c["pv_ps"], d_cn, *c["b_tot"], *c["e_max"]))
    return tuple(out)


def _ml_steps_state(q_ref, cnst, chains):
    st = []
    for d, p, half in chains:
        r0 = pl.ds(pl.multiple_of(p * LANES + half * CHUNK, CHUNK), CHUNK)
        cn = cnst[d]
        st.append((r0, cn, _dot(q_ref[0, r0, :], cn)))
    return st


def _ml_steps_update(pre, acc_refs, cnst, mst, chains, st):
    for (d, p, half), (r0, cn, q_cn) in zip(chains, st):
        b_cb, rowmax, pv_ps, d_cn = pre[d][:4]
        b_tot = pre[d][4 + half]
        e_max = pre[d][6 + half]
        sl = slice(half * CHUNK, (half + 1) * CHUNK)
        m_s = mst[d, 0:1, 0:1]
        inter = b_cb[sl] + m_s
        m_i = jnp.maximum(inter, rowmax[sl])
        w_inter = jnp.exp(inter - m_i)
        s_intra = jnp.exp(rowmax[sl] - m_i)
        num = w_inter * q_cn[:, :HEAD_DIM] + s_intra * pv_ps[sl, :HEAD_DIM]
        den = w_inter * q_cn[:, HEAD_DIM:] + s_intra * pv_ps[sl, HEAD_DIM:]
        acc_refs[d][r0, :] = num / jnp.maximum(jnp.abs(den), jnp.exp(-m_i))
        carry_log = b_tot + m_s
        m_new = jnp.maximum(carry_log, e_max)
        cnst[d] = jnp.exp(carry_log - m_new) * cn + jnp.exp(e_max - m_new) * d_cn[half * LANES:(half + 1) * LANES]
        mst[d] = jnp.broadcast_to(m_new, mst.shape[1:])


def _ml_scan(q_ref, k_ref, v_ref, rows_ref, acc_refs, cnst, mst, n_pairs):
    def chains_of(t):
        return [(0, t), (1, n_pairs - 1 - t), (0, t + 1), (1, n_pairs - 2 - t)]

    def prep(chains):
        st = _ml_prep_cols(q_ref, k_ref, rows_ref, chains)
        _ml_prep_intra(v_ref, st)
        return _ml_prep_state(st)

    def body(i, pre):
        t = 2 * i
        st = _ml_prep_cols(q_ref, k_ref, rows_ref, chains_of(jnp.minimum(t + 2, n_pairs - 2)))
        nxt = None
        for j, tt in enumerate((t, t + 1)):
            pf, pb = tt, n_pairs - 1 - tt
            pre_j = pre[2 * j:2 * j + 2]
            for halves in ([(0, pf, 0), (1, pb, 1)], [(0, pf, 1), (1, pb, 0)]):
                held = _ml_steps_state(q_ref, cnst, halves)
                if j == 0 and halves[0][2] == 0:
                    _ml_prep_intra(v_ref, st)
                elif j == 0:
                    nxt = _ml_prep_state(st)
                _ml_steps_update(pre_j, acc_refs, cnst, mst, halves, held)
        return nxt

    lax.fori_loop(0, n_pairs // 2, body, prep(chains_of(0)))


def _mlstm_kernel(bias_ref, ql, kl, vl, ol, zl, qc, kc, vc, oc, zc, gl0, gl1, gl2, gl3, gc0, gc1, gc2, gc3,
                  nw_ref, *rest, n_heads, ctx_out):
    if ctx_out:
        yl_ref, yc_ref = rest[:2]
        rest = rest[2:]
    else:
        yl_ref, yc_ref = rest[0], None
        rest = rest[1:]
    cnst, mst, rows_l, rows_c, acc_lf, acc_lb, acc_cf, acc_cb = rest
    h = pl.program_id(1)
    cnst[...] = jnp.zeros(cnst.shape, F32)
    mst[...] = jnp.zeros(mst.shape, F32)
    _ml_rows((gc0, gc1, gc2, gc3), bias_ref, h, n_heads, rows_c)
    _ml_scan(qc, kc, vc, rows_c, (acc_cf, acc_cb), cnst, mst, rows_c.shape[1])
    _ml_rows((gl0, gl1, gl2, gl3), bias_ref, h, n_heads, rows_l)
    _ml_scan(ql, kl, vl, rows_l, (acc_lf, acc_lb), cnst, mst, rows_l.shape[1])
    _head_out_tiles((acc_lf, acc_lb), 0, nw_ref, zl, yl_ref, acc_lf.shape[0], og_ref=ol)
    if ctx_out:
        _head_out_tiles((acc_cf, acc_cb), 0, nw_ref, zc, yc_ref, acc_cf.shape[0], og_ref=oc)


def _head_block(l, blk):
    return pl.BlockSpec((1, l, HEAD_DIM), lambda b, h, blk=blk: (b, 0, blk + h))


def _gate_block(n_pairs, j, n_heads):
    return pl.BlockSpec((1, 1, n_pairs, LANES), lambda b, h, j=j: (b, j * n_heads + h, 0, 0))


def _mlstm(u_l, u_c, gt_l, gt_c, bias, nw, lay, ctx_out):
    b, l, _ = u_l.shape
    lc = u_c.shape[1]
    nh = lay["mh"]
    blk = lambda off: off // HEAD_DIM
    offs = [lay["MQ"], lay["MQ"] + lay["DM"], lay["MQ"] + 2 * lay["DM"], lay["MO"], lay["MZ"]]
    in_specs = [pl.BlockSpec(memory_space=pltpu.SMEM)]
    in_specs += [_head_block(l, blk(o)) for o in offs]
    in_specs += [_head_block(lc, blk(o)) for o in offs]
    in_specs += [_gate_block(l // LANES, j, nh) for j in range(4)]
    in_specs += [_gate_block(lc // LANES, j, nh) for j in range(4)]
    in_specs += [pl.BlockSpec((1, HEAD_DIM), lambda b_, h: (0, 0))]
    out_specs = [pl.BlockSpec((1, l, HEAD_DIM), lambda b_, h: (b_, 0, h))]
    out_shape = [jax.ShapeDtypeStruct((b, l, nh * HEAD_DIM), BF16)]
    if ctx_out:
        out_specs.append(pl.BlockSpec((1, lc, HEAD_DIM), lambda b_, h: (b_, 0, h)))
        out_shape.append(jax.ShapeDtypeStruct((b, lc, nh * HEAD_DIM), BF16))
    scratch = [pltpu.VMEM((2, HEAD_DIM, 2 * HEAD_DIM), F32), pltpu.VMEM((2, 8, LANES), F32),
               pltpu.VMEM((4, l // LANES, LANES), F32), pltpu.VMEM((4, lc // LANES, LANES), F32),
               pltpu.VMEM((l, HEAD_DIM), F32), pltpu.VMEM((l, HEAD_DIM), F32),
               pltpu.VMEM((lc, HEAD_DIM), F32), pltpu.VMEM((lc, HEAD_DIM), F32)]
    res = pl.pallas_call(
        functools.partial(_mlstm_kernel, n_heads=nh, ctx_out=ctx_out),
        grid=(b, nh),
        in_specs=in_specs, out_specs=out_specs, out_shape=out_shape,
        scratch_shapes=scratch,
        compiler_params=_cparams(("parallel", "parallel"), 48),
        name="mlstm",
    )(bias.reshape(-1), *([u_l] * 5), *([u_c] * 5), *([gt_l] * 4), *([gt_c] * 4), nw)
    return (res[0], res[1]) if ctx_out else (res[0], None)


def _gdn_rows(g_refs, alog_ref, dtb_ref, h, n_heads, rows_ref, base):
    n = g_refs[0].shape[2]
    g_f = -jnp.exp(alog_ref[h]) * _softplus(g_refs[0][0, 0] + dtb_ref[h])
    g_b = -jnp.exp(alog_ref[n_heads + h]) * _softplus(g_refs[1][0, 0] + dtb_ref[n_heads + h])
    rows_ref[0, base:base + n] = _seg_cumsum(g_f, False)
    rows_ref[1, base:base + n] = jax.nn.sigmoid(g_refs[2][0, 0])
    rows_ref[2, base:base + n] = _seg_cumsum(g_b, True)
    rows_ref[3, base:base + n] = jax.nn.sigmoid(g_refs[3][0, 0])


def _gdn_conv(src_refs, w_refs, dst_refs, base, n_units, unit_rows, period):
    def body(i, carry):
        r = pl.multiple_of(i * unit_rows, unit_rows)
        for idx in range(3):
            t = _silu(_conv_rows(src_refs[idx][0, pl.ds(r, unit_rows), :], w_refs[idx][...], period))
            if idx < 2:
                t = t * lax.rsqrt(jnp.sum(t * t, axis=-1, keepdims=True) + NORM_EPS)
            if idx == 0:
                t = t * (HEAD_DIM ** -0.5)
            dst_refs[idx][pl.ds(base + r, unit_rows), :] = t
        return carry

    lax.fori_loop(0, n_units, body, 0)


N_NEUMANN = CHUNK.bit_length() - 2


def _gdn_prep_start(qs, ks, rows_ref, aqk_s, chains):
    eye = _pair_masks(0)[0]
    eye_f = jnp.where(eye, 1.0, 0.0)
    st = []
    for d, p in chains:
        rows = pl.ds(pl.multiple_of(p * LANES, LANES), LANES)
        k = ks[rows, :]
        _, incl, strict = _pair_masks(d)
        g_row = rows_ref[2 * d, pl.ds(p, 1), :]
        beta_col = _to_col(rows_ref[2 * d + 1, pl.ds(p, 1), :], eye)
        g_col = _to_col(g_row, eye)
        dec = jnp.exp(jnp.where(incl, g_col - g_row, -jnp.inf))
        aqk_s[d, rows, :] = (_dot_nt(qs[rows, :], k) * dec).astype(BF16)
        m_low = jnp.where(strict, beta_col * _dot_nt(k, k) * dec, 0.0)
        st.append(dict(d=d, rows=rows, g_row=g_row, g_col=g_col, beta_col=beta_col, pw=m_low, t_inv=eye_f - m_low))
    return st


def _gdn_prep_neumann(st, after_square=None, after_product=None):
    eye_f = jnp.where(_pair_masks(0)[0], 1.0, 0.0)
    for c in st:
        c["pw"] = _dot3(c["pw"], c["pw"])
    if after_square is not None:
        after_square()
    for c in st:
        c["t_inv"] = _dot3(c["t_inv"], eye_f + c["pw"])
    if after_product is not None:
        after_product()


def _gdn_prep_finish(qs, ks, vs, ut_s, w_s, qd_s, ket_s, st):
    row_i = lax.broadcasted_iota(jnp.int32, (LANES, 1), 0)
    for c in st:
        d, rows, g_row, g_col, beta_col = c["d"], c["rows"], c["g_row"], c["g_col"], c["beta_col"]
        k = ks[rows, :]
        e_g = jnp.exp(g_col)
        sol = _dot3(c["t_inv"], jnp.concatenate([beta_col * vs[rows, :], (beta_col * e_g) * k], axis=1))
        ut_s[d, rows, :] = sol[:, :HEAD_DIM]
        w_s[d, rows, :] = sol[:, HEAD_DIM:].astype(BF16)
        qd_s[d, rows, :] = (qs[rows, :] * e_g).astype(BF16)
        last = CHUNK - 1 if d == 0 else 0
        g_tot_col = jnp.where(row_i < CHUNK, g_row[:, last:last + 1], g_row[:, CHUNK + last:CHUNK + last + 1])
        ket_s[d, rows, :] = (k * jnp.exp(g_tot_col - g_col)).T.astype(BF16)


def _gdn_steps_state(w_s, qd_s, sst, chains):
    st = []
    for d, p, half in chains:
        r0 = pl.ds(pl.multiple_of(p * LANES + half * CHUNK, CHUNK), CHUNK)
        s = sst[d]
        lhs = jnp.concatenate([w_s[d, r0, :], qd_s[d, r0, :]], axis=0)
        st.append((r0, s, jnp.dot(lhs, s.astype(BF16), preferred_element_type=F32)))
    return st


def _gdn_steps_update(rows_ref, ut_s, ket_s, aqk_s, sst, acc_refs, chains, st):
    for (d, p, half), (r0, s, ws_qs) in zip(chains, st):
        rp = pl.ds(pl.multiple_of(p * LANES, LANES), LANES)
        lane = half * CHUNK + (CHUNK - 1 if d == 0 else 0)
        g_tot = rows_ref[2 * d, pl.ds(p, 1), lane:lane + 1]
        u = (ut_s[d, r0, :] - ws_qs[:CHUNK]).astype(BF16)
        zero = jnp.zeros_like(u)
        u_pad = jnp.concatenate([u, zero] if half == 0 else [zero, u], axis=0)
        lhs2 = jnp.concatenate([aqk_s[d, r0, :], ket_s[d, rp, :]], axis=0)
        au_ku = jnp.dot(lhs2, u_pad, preferred_element_type=F32)
        acc_refs[d][r0, :] = ws_qs[CHUNK:] + au_ku[:CHUNK]
        sst[d] = jnp.exp(g_tot) * s + au_ku[CHUNK:]


def _gdn_kernel(alog_ref, dtb_ref, ql, kl, vl, zl, qc, kc, vc, zc, gl0, gl1, gl2, gl3, gc0, gc1, gc2, gc3,
                wq, wk, wv, nw_ref, *rest, n_heads, ctx_out):
    if ctx_out:
        yl_ref, yc_ref = rest[:2]
        rest = rest[2:]
    else:
        yl_ref, yc_ref = rest[0], None
        rest = rest[1:]
    sst, rows, qs, ks, vs, ut_s, w_s, qd_s, ket_s, aqk_s, acc_f, acc_b = rest
    h = pl.program_id(1)
    n_l = ql.shape[1]
    n_c = qc.shape[1]
    nl = n_l // LANES
    nc = n_c // LANES
    sst[...] = jnp.zeros(sst.shape, F32)
    _gdn_rows((gl0, gl1, gl2, gl3), alog_ref, dtb_ref, h, n_heads, rows, 0)
    _gdn_rows((gc0, gc1, gc2, gc3), alog_ref, dtb_ref, h, n_heads, rows, nl)
    conv_rows = _pick_tile(n_l, 4 * CHUNK, CHUNK)
    _gdn_conv((ql, kl, vl), (wq, wk, wv), (qs, ks, vs), 0, n_l // conv_rows, conv_rows, CHUNK)
    _gdn_conv((qc, kc, vc), (wq, wk, wv), (qs, ks, vs), n_l, 1, n_c, n_c)
    prepped = (ut_s, w_s, qd_s, ket_s, aqk_s)
    n_steps = nl + nc

    def fwd_pair(t):
        return jnp.where(t < nc, nl + t, t - nc)

    def bwd_pair(t):
        return n_steps - 1 - t

    def prep(chains, steps=()):
        st = _gdn_prep_start(qs, ks, rows, aqk_s, chains)
        for i in range(N_NEUMANN):
            if i < len(steps):
                held = []
                first = lambda c=steps[i]: held.append(_gdn_steps_state(w_s, qd_s, sst, c))
                second = lambda c=steps[i]: _gdn_steps_update(rows, ut_s, ket_s, aqk_s, sst, (acc_f, acc_b), c,
                                                              held[0])
                _gdn_prep_neumann(st, first, second)
            else:
                _gdn_prep_neumann(st)
        _gdn_prep_finish(qs, ks, vs, ut_s, w_s, qd_s, ket_s, st)

    def chains_of(t):
        return [(0, fwd_pair(t)), (1, bwd_pair(t)), (0, fwd_pair(t + 1)), (1, bwd_pair(t + 1))]

    prep(chains_of(0))

    def step(i, carry):
        t = 2 * i
        halves = []
        for tt in (t, t + 1):
            pf, pb = fwd_pair(tt), bwd_pair(tt)
            halves.append([(0, pf, 0), (1, pb, 1)])
            halves.append([(0, pf, 1), (1, pb, 0)])
        prep(chains_of(jnp.minimum(t + 2, n_steps - 2)), steps=halves)
        return carry

    lax.fori_loop(0, n_steps // 2, step, 0)
    _head_out_tiles((acc_f, acc_b), 0, nw_ref, zl, yl_ref, n_l)
    if ctx_out:
        _head_out_tiles((acc_f, acc_b), n_l, nw_ref, zc, yc_ref, n_c)


def _gdn(u_l, u_c, gt_l, gt_c, conv_w, a_log, dt_bias, nw, lay, ctx_out):
    b, l, _ = u_l.shape
    lc = u_c.shape[1]
    nh = lay["gh"]
    taps = conv_w.shape[0]
    blk = lambda off: off // HEAD_DIM
    offs = [lay["GQ"], lay["GQ"] + lay["DG"], lay["GQ"] + 2 * lay["DG"], lay["GZ"]]
    smem = pl.BlockSpec(memory_space=pltpu.SMEM)
    in_specs = [smem, smem]
    in_specs += [_head_block(l, blk(o)) for o in offs]
    in_specs += [_head_block(lc, blk(o)) for o in offs]
    in_specs += [_gate_block(l // LANES, j, nh) for j in range(4)]
    in_specs += [_gate_block(lc // LANES, j, nh) for j in range(4)]
    in_specs += [pl.BlockSpec((taps, HEAD_DIM), lambda b_, h, j=j: (0, j * nh + h)) for j in range(3)]
    in_specs += [pl.BlockSpec((1, HEAD_DIM), lambda b_, h: (0, 0))]
    out_specs = [pl.BlockSpec((1, l, HEAD_DIM), lambda b_, h: (b_, 0, h))]
    out_shape = [jax.ShapeDtypeStruct((b, l, nh * HEAD_DIM), BF16)]
    if ctx_out:
        out_specs.append(pl.BlockSpec((1, lc, HEAD_DIM), lambda b_, h: (b_, 0, h)))
        out_shape.append(jax.ShapeDtypeStruct((b, lc, nh * HEAD_DIM), BF16))
    lt = l + lc
    seq_f32 = pltpu.VMEM((lt, HEAD_DIM), F32)
    dir_bf16 = pltpu.VMEM((2, lt, HEAD_DIM), BF16)
    scratch = [pltpu.VMEM((2, HEAD_DIM, HEAD_DIM), F32), pltpu.VMEM((4, lt // LANES, LANES), F32),
               seq_f32, seq_f32, seq_f32,
               pltpu.VMEM((2, lt, HEAD_DIM), F32), dir_bf16, dir_bf16, dir_bf16, dir_bf16,
               seq_f32, seq_f32]
    res = pl.pallas_call(
        functools.partial(_gdn_kernel, n_heads=nh, ctx_out=ctx_out),
        grid=(b, nh),
        in_specs=in_specs, out_specs=out_specs, out_shape=out_shape,
        scratch_shapes=scratch,
        compiler_params=_cparams(("parallel", "parallel"), 56),
        name="gdn",
    )(a_log.reshape(-1), dt_bias.reshape(-1), *([u_l] * 4), *([u_c] * 4), *([gt_l] * 4), *([gt_c] * 4),
      *([conv_w] * 3), nw)
    return (res[0], res[1]) if ctx_out else (res[0], None)


def _dft_tables(l):
    k = jnp.arange(l, dtype=jnp.int32)
    ang = lambda t: ((k[:, None] * t[None, :]) % (2 * l)).astype(F32) * (math.pi / l)
    ang_a = ang(jnp.arange(l // CHUNK, dtype=jnp.int32) * CHUNK)[:, :, None]
    ang_b = ang(jnp.arange(CHUNK, dtype=jnp.int32))[:, None, :]
    cos_t = (jnp.cos(ang_a) * jnp.cos(ang_b) - jnp.sin(ang_a) * jnp.sin(ang_b)).reshape(l, l)
    sin_t = (jnp.sin(ang_a) * jnp.cos(ang_b) + jnp.cos(ang_a) * jnp.sin(ang_b)).reshape(l, l)
    alt = jnp.where(k % 2 == 0, 1.0, -1.0).astype(F32)
    sin_f = sin_t.at[0, :].set(alt)
    return cos_t.astype(BF16), sin_f.astype(BF16), sin_f.T.astype(BF16)


def _hy_prep_kernel(p_ref, w_ref, x1_ref, x2_ref, v_ref, vb_ref, *, period, dh):
    y = _conv_rows(p_ref[0], w_ref[...], period)
    x1_ref[0] = y[:, :dh]
    x2_ref[0] = y[:, dh:2 * dh]
    v = y[:, 2 * dh:]
    v_ref[0] = v
    vb_ref[0] = v.astype(BF16)


def _hy_prep(u, conv_w, dh, grid_mask):
    b, l, _ = u.shape
    taps = conv_w.shape[0]
    tr = _pick_tile(l, 256, CHUNK)
    period = CHUNK if grid_mask else l
    if not grid_mask:
        tr = l
    o_spec = pl.BlockSpec((1, tr, dh), lambda i, m: (i, m, 0))
    return pl.pallas_call(
        functools.partial(_hy_prep_kernel, period=period, dh=dh),
        grid=(b, l // tr),
        in_specs=[pl.BlockSpec((1, tr, 3 * dh), lambda i, m: (i, m, 0)),
                  pl.BlockSpec((taps, 3 * dh), lambda i, m: (0, 0))],
        out_specs=[o_spec] * 4,
        out_shape=[jax.ShapeDtypeStruct((b, l, dh), F32)] * 3 + [jax.ShapeDtypeStruct((b, l, dh), BF16)],
        compiler_params=_cparams(("parallel", "parallel"), 40),
        name="hy_prep",
    )(u, conv_w)


def _filter_kernel(feats_ref, w1_ref, b1_ref, fr_ref, w2_ref, b2_ref, w3c_ref, w3a_ref, dl_ref,
                   hs_ref, hd_ref, nyq_ref, hid_ref):
    @pl.when((pl.program_id(0) == 0) & (pl.program_id(1) == 0))
    def _():
        hid = jnp.sin(fr_ref[...] * (_dot_hi(feats_ref[...], w1_ref[...]) + b1_ref[...]))
        hid_ref[...] = jnp.sin(fr_ref[...] * (_dot_hi(hid, w2_ref[...]) + b2_ref[...]))

    hid = hid_ref[...]
    win = jnp.exp(-feats_ref[:, 0:1] * dl_ref[...])

    def one(w3_ref):
        f = _dot_hi(hid, w3_ref[...]) * win
        return f / (jnp.sum(jnp.abs(f), axis=0, keepdims=True) + NORM_EPS)

    h_c = one(w3c_ref)
    h_a = one(w3a_ref)
    h_sum = h_c + h_a
    sign = jnp.where(lax.broadcasted_iota(jnp.int32, h_sum.shape, 0) % 2 == 0, 1.0, -1.0)
    nyq_ref[0] = jnp.broadcast_to(jnp.sum(h_sum * sign, axis=0, keepdims=True), nyq_ref.shape[1:])
    hs_ref[0] = h_sum.astype(BF16)
    hd_ref[0] = (h_c - h_a).astype(BF16)


def _hy_filters(l, w1, b1, freq, w2, b2, w3, dh):
    pos = jnp.arange(l, dtype=F32)
    t = pos / max(l - 1, 1)
    ang = 2.0 * math.pi * pos / l
    bands = jnp.linspace(1e-4, FILTER_BANDS - 1, FILTER_BANDS, dtype=F32)
    feats = jnp.concatenate([t[:, None], jnp.cos(ang[:, None] * bands), -jnp.sin(ang[:, None] * bands)], axis=-1)
    n_emb, n_hid = w1.shape
    feats = jnp.pad(feats, ((0, 0), (0, LANES - n_emb)))
    pc = LANES - n_hid
    w1p = jnp.pad(w1, ((0, LANES - n_emb), (0, pc)))
    w2p = jnp.pad(w2, ((0, pc), (0, pc)))
    w3p = jnp.pad(w3, ((0, pc), (0, 0)))
    row = lambda a: jnp.pad(a, (0, pc))[None]
    deltas = jnp.abs(jnp.linspace(MIN_DECAY, MAX_DECAY, dh, dtype=F32))[None]
    nct = dh // LANES
    const = lambda shape: pl.BlockSpec(shape, lambda o, c: (0, 0))
    o_spec = pl.BlockSpec((1, l, LANES), lambda o, c: (o, 0, c))
    return pl.pallas_call(
        _filter_kernel,
        grid=(HYENA_ORDER, nct),
        in_specs=[const((l, LANES)), const((LANES, LANES)), const((1, LANES)), const((1, LANES)),
                  const((LANES, LANES)), const((1, LANES)),
                  pl.BlockSpec((LANES, LANES), lambda o, c: (0, o * 2 * nct + c)),
                  pl.BlockSpec((LANES, LANES), lambda o, c: (0, o * 2 * nct + nct + c)),
                  pl.BlockSpec((1, LANES), lambda o, c: (0, c))],
        out_specs=[o_spec, o_spec, pl.BlockSpec((1, 8, LANES), lambda o, c: (o, 0, c))],
        out_shape=[jax.ShapeDtypeStruct((HYENA_ORDER, l, dh), BF16)] * 2
        + [jax.ShapeDtypeStruct((HYENA_ORDER, 8, dh), F32)],
        scratch_shapes=[pltpu.VMEM((l, LANES), F32)],
        compiler_params=_cparams(("arbitrary", "arbitrary"), 48),
        name="hy_filter",
    )(feats, w1p, row(b1), row(freq), w2p, row(b2), w3p, w3p, deltas)


def _spectrum_kernel(c_ref, s_ref, hs_ref, hd_ref, nyq_ref, a_ref, a2_ref, hsin_ref, *, l, tk):
    a = jnp.dot(c_ref[...], hs_ref[0], preferred_element_type=F32)
    s = jnp.dot(s_ref[...], hd_ref[0], preferred_element_type=F32)
    is0 = (lax.broadcasted_iota(jnp.int32, a.shape, 0) + pl.program_id(1) * tk) == 0
    wk = jnp.where(is0, 0.5 / l, 1.0 / l)
    a_ref[0] = a * wk
    a2_ref[0] = jnp.where(is0, nyq_ref[0, 0:1, :], a) * wk
    hsin_ref[0] = jnp.where(is0, 0.0, s) * wk


def _hy_spectrum(tabs, hs, hd, nyq):
    cos_t, sin_f, _ = tabs
    _, l, dh = hs.shape
    tk = _pick_tile(l, 512, 8)
    tab = pl.BlockSpec((tk, l), lambda o, k: (k, 0))
    filt = pl.BlockSpec((1, l, dh), lambda o, k: (o, 0, 0))
    o_spec = pl.BlockSpec((1, tk, dh), lambda o, k: (o, k, 0))
    return pl.pallas_call(
        functools.partial(_spectrum_kernel, l=l, tk=tk),
        grid=(HYENA_ORDER, l // tk),
        in_specs=[tab, tab, filt, filt, pl.BlockSpec((1, 8, dh), lambda o, k: (o, 0, 0))],
        out_specs=[o_spec] * 3,
        out_shape=[jax.ShapeDtypeStruct((HYENA_ORDER, l, dh), F32)] * 3,
        compiler_params=_cparams(("parallel", "parallel"), 48),
        name="hy_spectrum",
    )(cos_t, sin_f, hs, hd, nyq)


def _hy_fwd_kernel(c_ref, s_ref, v_ref, a_ref, a2_ref, hsin_ref, yc_ref, ys_ref):
    v = v_ref[0]
    xc = jnp.dot(c_ref[...], v, preferred_element_type=F32)
    xs = jnp.dot(s_ref[...], v, preferred_element_type=F32)
    hsin = hsin_ref[0]
    yc_ref[0] = (xc * a_ref[0] - xs * hsin).astype(BF16)
    ys_ref[0] = (xc * hsin + xs * a2_ref[0]).astype(BF16)


def _hy_fwd(tabs, vb, spec, order):
    cos_t, sin_f, _ = tabs
    b, l, dh = vb.shape
    tk = _pick_tile(l, 512, 8)
    tab = pl.BlockSpec((tk, l), lambda k, i: (k, 0))
    filt = pl.BlockSpec((1, tk, dh), lambda k, i: (order, k, 0))
    o_spec = pl.BlockSpec((1, tk, dh), lambda k, i: (i, k, 0))
    return pl.pallas_call(
        _hy_fwd_kernel,
        grid=(l // tk, b),
        in_specs=[tab, tab, pl.BlockSpec((1, l, dh), lambda k, i: (i, 0, 0)), filt, filt, filt],
        out_specs=[o_spec] * 2,
        out_shape=[jax.ShapeDtypeStruct((b, l, dh), BF16)] * 2,
        compiler_params=_cparams(("parallel", "parallel"), 48),
        name="hy_fwd",
    )(cos_t, sin_f, vb, *spec)


def _hy_inv_kernel(c_ref, s_ref, yc_ref, ys_ref, x_ref, v_ref, sk_ref, *rest, last):
    conv = jnp.dot(c_ref[...], yc_ref[0], preferred_element_type=F32)
    conv = conv + jnp.dot(s_ref[...], ys_ref[0], preferred_element_type=F32)
    y = x_ref[0] * (conv + sk_ref[0] * v_ref[0])
    if last:
        z_ref, o_ref = rest
        o_ref[0] = (y * _silu(z_ref[0])).astype(BF16)
    else:
        of_ref, ob_ref = rest
        of_ref[0] = y
        ob_ref[0] = y.astype(BF16)


def _hy_inv(tabs, yc, ys, x, v, skip, order, u=None, z_blk=None):
    cos_t, _, sin_i = tabs
    b, l, dh = yc.shape
    last = u is not None
    tt = _pick_tile(l, 512, 8)
    tab = pl.BlockSpec((tt, l), lambda t, i: (t, 0))
    full = pl.BlockSpec((1, l, dh), lambda t, i: (i, 0, 0))
    tile = pl.BlockSpec((1, tt, dh), lambda t, i: (i, t, 0))
    in_specs = [tab, tab, full, full, tile, tile, pl.BlockSpec((1, 1, dh), lambda t, i: (order, 0, 0))]
    args = [cos_t, sin_i, yc, ys, x, v, skip[:, None, :]]
    if last:
        in_specs.append(pl.BlockSpec((1, tt, dh), lambda t, i: (i, t, z_blk)))
        args.append(u)
        out_specs = tile
        out_shape = jax.ShapeDtypeStruct((b, l, dh), BF16)
    else:
        out_specs = [tile, tile]
        out_shape = [jax.ShapeDtypeStruct((b, l, dh), F32), jax.ShapeDtypeStruct((b, l, dh), BF16)]
    return pl.pallas_call(
        functools.partial(_hy_inv_kernel, last=last),
        grid=(l // tt, b),
        in_specs=in_specs, out_specs=out_specs, out_shape=out_shape,
        compiler_params=_cparams(("parallel", "parallel"), 48),
        name="hy_inv",
    )(*args)


def _hyena(u, tabs, conv_w, w1, b1, freq, w2, b2, w3, skip, lay, grid_mask):
    dh = lay["DH"]
    l = u.shape[1]
    x1, x2, v, vb = _hy_prep(u, conv_w, dh, grid_mask)
    spec = _hy_spectrum(tabs, *_hy_filters(l, w1, b1, freq, w2, b2, w3, dh))
    yc, ys = _hy_fwd(tabs, vb, spec, 0)
    y1, y1b = _hy_inv(tabs, yc, ys, x1, v, skip, 0)
    yc, ys = _hy_fwd(tabs, y1b, spec, 1)
    return _hy_inv(tabs, yc, ys, x2, y1, skip, 1, u=u, z_blk=lay["HZ"] // dh)


def _layout(d):
    dg, dh, dm = 3 * d // 8, d // 4, 3 * d // 8
    lay = {"DG": dg, "DH": dh, "DM": dm, "gh": dg // HEAD_DIM, "mh": dm // HEAD_DIM}
    lay["HZ"] = 3 * dh
    lay["GQ"] = 4 * dh
    lay["GZ"] = lay["GQ"] + 3 * dg
    lay["GAB"] = lay["GZ"] + dg
    lay["MQ"] = lay["GAB"] + LANES
    lay["MO"] = lay["MQ"] + 3 * dm
    lay["MZ"] = lay["MO"] + dm
    lay["MG"] = lay["MZ"] + dm
    lay["NP"] = lay["MG"] + LANES
    return lay


def _pack_w_in(w, lay):
    dg, dh, dm, gh, mh = lay["DG"], lay["DH"], lay["DM"], lay["gh"], lay["mh"]
    sizes = (3 * dg, dg, 4 * gh, 3 * dh, dh, 3 * dm, dm, dm, 4 * mh)
    offs = [0]
    for s in sizes:
        offs.append(offs[-1] + s)
    seg = [w[:, offs[i]:offs[i + 1]] for i in range(len(sizes))]
    g_qkv, g_z, g_ab, h_p, h_z, m_qkv, m_o, m_z, m_g = seg
    padl = lambda a: jnp.pad(a, ((0, 0), (0, LANES - a.shape[1])))
    return jnp.concatenate([h_p, h_z, g_qkv, g_z, padl(g_ab), m_qkv, m_o, m_z, padl(m_g)], axis=1).astype(BF16)


def _gate_rows(u, off, n):
    b, l, _ = u.shape
    return jnp.transpose(u[:, :, off:off + n], (0, 2, 1)).reshape(b, n, l // LANES, LANES)


def kernel(x, c, ctx, c_ctx, norm_w, mod_w, mod_b, w_in, gdn_conv, gdn_a_log, gdn_dt_bias, gdn_norm, hy_conv,
           hy_w1, hy_b1, hy_freq, hy_w2, hy_b2, hy_w3, hy_skip, ml_gate_bias, ml_norm, w_out, final_norm):
    b, l, d = x.shape
    lc = ctx.shape[1]
    depth = norm_w.shape[0]
    lay = _layout(d)
    dg, dh = lay["DG"], lay["DH"]
    assert b < COND_ROWS and l % (2 * LANES) == 0 and lc % (2 * LANES) == 0 and d % 1024 == 0
    cond = jnp.zeros((COND_ROWS, d), F32).at[:b].set(c).at[b].set(c_ctx)
    tabs_l = _dft_tables(l)
    tabs_c = _dft_tables(lc)
    for layer in range(depth):
        last = layer == depth - 1
        mods = _adaln(cond, mod_w[layer], mod_b[layer][None])
        sh, sc, gt = mods[:, :d], mods[:, d:2 * d], mods[:, 2 * d:]
        lat = lambda m: m[:b, None, :]
        cx = lambda m: jnp.broadcast_to(m[b][None, None, :], (b, 1, d))
        wp = _pack_w_in(w_in[layer], lay)
        nw = norm_w[layer][None]
        u_l = _inproj(x, nw, lat(sc), lat(sh), wp)
        u_c = _inproj(ctx, nw, cx(sc), cx(sh), wp)
        g_rows = lambda u, off, n: _gate_rows(u, off, n)
        yg_l, yg_c = _gdn(u_l, u_c, g_rows(u_l, lay["GAB"], 4 * lay["gh"]), g_rows(u_c, lay["GAB"], 4 * lay["gh"]),
                          gdn_conv[layer], gdn_a_log[layer], gdn_dt_bias[layer], gdn_norm[layer][None], lay,
                          not last)
        ym_l, ym_c = _mlstm(u_l, u_c, g_rows(u_l, lay["MG"], 4 * lay["mh"]), g_rows(u_c, lay["MG"], 4 * lay["mh"]),
                            ml_gate_bias[layer], ml_norm[layer][None], lay, not last)
        hy = (hy_conv[layer], hy_w1[layer], hy_b1[layer], hy_freq[layer], hy_w2[layer], hy_b2[layer],
              hy_w3[layer], hy_skip[layer])
        yh_l = _hyena(u_l, tabs_l, *hy, lay, True)
        wo = w_out[layer].astype(BF16)
        wg, wh, wm = wo[:dg], wo[dg:dg + dh], wo[dg + dh:]
        fw = final_norm[None]
        x = _outproj(x, yg_l, yh_l, ym_l, wg, wh, wm, lat(gt), fw, last)
        if not last:
            yh_c = _hyena(u_c, tabs_c, *hy, lay, False)
            ctx = _outproj(ctx, yg_c, yh_c, ym_c, wg, wh, wm, cx(gt), fw, False)
    return x
```

```python
import functools
import math

import jax
import jax.numpy as jnp
from jax import lax
from jax.experimental import pallas as pl
from jax.experimental.pallas import tpu as pltpu

HEAD_DIM = 128
CHUNK = 64
LANES = 128
NORM_EPS = 1e-6
HYENA_ORDER = 2
HY_BLOCK = 512
FILTER_BANDS = 16
DECAY_TARGET = 1e-2
MIN_DECAY = math.log(DECAY_TARGET) / 1.5
MAX_DECAY = math.log(DECAY_TARGET) / 0.3
COND_ROWS = 16

F32 = jnp.float32
BF16 = jnp.bfloat16
HI = lax.Precision.HIGHEST


def _cparams(sem, vmem_mb):
    return pltpu.CompilerParams(dimension_semantics=sem, vmem_limit_bytes=vmem_mb << 20)


def _dot(a, b):
    return jnp.dot(a.astype(BF16), b.astype(BF16), preferred_element_type=F32)


def _dot_hi(a, b):
    return jnp.dot(a, b, precision=HI, preferred_element_type=F32)


def _dot_nt(a, b):
    return lax.dot_general(a.astype(BF16), b.astype(BF16), (((1,), (1,)), ((), ())),
                           preferred_element_type=F32)


def _dot_tn(a, b):
    return lax.dot_general(a.astype(BF16), b.astype(BF16), (((0,), (0,)), ((), ())),
                           preferred_element_type=F32)


def _silu(x):
    return x * jax.nn.sigmoid(x)


def _softplus(x):
    return jnp.maximum(x, 0.0) + jnp.log(1.0 + jnp.exp(-jnp.abs(x)))


def _pick_tile(n, cap, unit):
    t = (min(n, cap) // unit) * unit
    while n % t:
        t -= unit
    return t


def _adaln_kernel(c_ref, w_ref, b_ref, o_ref):
    o_ref[...] = _dot_hi(_silu(c_ref[...]), w_ref[...]) + b_ref[...]


def _adaln(cond, w, b):
    d, n = w.shape
    tn = _pick_tile(n, 768, LANES)
    return pl.pallas_call(
        _adaln_kernel,
        grid=(n // tn,),
        in_specs=[pl.BlockSpec((COND_ROWS, d), lambda j: (0, 0)),
                  pl.BlockSpec((d, tn), lambda j: (0, j)),
                  pl.BlockSpec((1, tn), lambda j: (0, j))],
        out_specs=pl.BlockSpec((COND_ROWS, tn), lambda j: (0, j)),
        out_shape=jax.ShapeDtypeStruct((COND_ROWS, n), F32),
        compiler_params=_cparams(("parallel",), 40),
        name="adaln",
    )(cond, w, b)


def _inproj_kernel(x_ref, nw_ref, sc_ref, sh_ref, w_ref, o_ref, xn_ref):
    @pl.when(pl.program_id(2) == 0)
    def _():
        x = x_ref[0]
        r = lax.rsqrt(jnp.mean(x * x, axis=-1, keepdims=True) + NORM_EPS)
        y = (x * r * nw_ref[...]) * (1.0 + sc_ref[0]) + sh_ref[0]
        xn_ref[...] = y.astype(BF16)

    o_ref[0] = jnp.dot(xn_ref[...], w_ref[...], preferred_element_type=F32)


def _inproj(x, nw, sc, sh, wp):
    b, l, d = x.shape
    n = wp.shape[1]
    tm = _pick_tile(l, 1024, 8)
    tn = _pick_tile(n, 1280, LANES)
    return pl.pallas_call(
        _inproj_kernel,
        grid=(b, l // tm, n // tn),
        in_specs=[pl.BlockSpec((1, tm, d), lambda i, m, j: (i, m, 0)),
                  pl.BlockSpec((1, d), lambda i, m, j: (0, 0)),
                  pl.BlockSpec((1, 1, d), lambda i, m, j: (i, 0, 0)),
                  pl.BlockSpec((1, 1, d), lambda i, m, j: (i, 0, 0)),
                  pl.BlockSpec((d, tn), lambda i, m, j: (0, j))],
        out_specs=pl.BlockSpec((1, tm, tn), lambda i, m, j: (i, m, j)),
        out_shape=jax.ShapeDtypeStruct((b, l, n), F32),
        scratch_shapes=[pltpu.VMEM((tm, d), BF16)],
        compiler_params=_cparams(("parallel", "parallel", "arbitrary"), 56),
        name="inproj",
    )(x, nw, sc, sh, wp)


def _outproj_kernel(x_ref, yg_ref, yh_ref, ym_ref, wg_ref, wh_ref, wm_ref, gt_ref, fw_ref, o_ref, *, final):
    acc = jnp.dot(yg_ref[0], wg_ref[...], preferred_element_type=F32)
    acc = acc + jnp.dot(yh_ref[0], wh_ref[...], preferred_element_type=F32)
    acc = acc + jnp.dot(ym_ref[0], wm_ref[...], preferred_element_type=F32)
    xn = x_ref[0] + gt_ref[0] * acc
    if final:
        r = lax.rsqrt(jnp.mean(xn * xn, axis=-1, keepdims=True) + NORM_EPS)
        xn = xn * r * fw_ref[...]
    o_ref[0] = xn


def _outproj(x, yg, yh, ym, wg, wh, wm, gt, fw, final):
    b, l, d = x.shape
    tm = _pick_tile(l, 512, 8)
    row = lambda w: pl.BlockSpec((1, tm, w), lambda i, m: (i, m, 0))
    full = lambda a: pl.BlockSpec(a.shape, lambda i, m: (0, 0))
    return pl.pallas_call(
        functools.partial(_outproj_kernel, final=final),
        grid=(b, l // tm),
        in_specs=[row(d), row(yg.shape[2]), row(yh.shape[2]), row(ym.shape[2]),
                  full(wg), full(wh), full(wm),
                  pl.BlockSpec((1, 1, d), lambda i, m: (i, 0, 0)),
                  pl.BlockSpec((1, d), lambda i, m: (0, 0))],
        out_specs=row(d),
        out_shape=jax.ShapeDtypeStruct((b, l, d), F32),
        compiler_params=_cparams(("parallel", "parallel"), 48),
        name="outproj",
    )(x, yg, yh, ym, wg, wh, wm, gt, fw)


def _conv_rows(x, w, period):
    rows = x.shape[0]
    taps = w.shape[0]
    pad = taps // 2
    pos = lax.broadcasted_iota(jnp.int32, x.shape, 0) % period
    y = None
    for j in range(taps):
        off = j - pad
        if off == 0:
            term = x * w[j:j + 1]
        else:
            shifted = pltpu.roll(x, (-off) % rows, axis=0)
            ok = (pos >= -off) if off < 0 else (pos < period - off)
            term = jnp.where(ok, shifted, 0.0) * w[j:j + 1]
        y = term if y is None else y + term
    return y


def _seg_cumsum(x, reverse):
    lane = lax.broadcasted_iota(jnp.int32, x.shape, 1) % CHUNK
    s = 1
    while s < CHUNK:
        if reverse:
            shifted = pltpu.roll(x, LANES - s, axis=1)
            ok = lane < CHUNK - s
        else:
            shifted = pltpu.roll(x, s, axis=1)
            ok = lane >= s
        x = x + jnp.where(ok, shifted, 0.0)
        s *= 2
    return x


def _chunk_masks(d):
    ii = lax.broadcasted_iota(jnp.int32, (CHUNK, CHUNK), 0)
    jj = lax.broadcasted_iota(jnp.int32, (CHUNK, CHUNK), 1)
    eye = ii == jj
    if d == 0:
        return eye, jj <= ii, jj < ii
    return eye, jj >= ii, jj > ii


def _pair_masks(d):
    ii = lax.broadcasted_iota(jnp.int32, (LANES, LANES), 0)
    jj = lax.broadcasted_iota(jnp.int32, (LANES, LANES), 1)
    lo = (ii // CHUNK) * CHUNK
    eye = ii == jj
    if d == 0:
        return eye, (jj >= lo) & (jj <= ii), (jj >= lo) & (jj < ii)
    return eye, (jj < lo + CHUNK) & (jj >= ii), (jj < lo + CHUNK) & (jj > ii)


def _to_col(row, eye):
    return jnp.sum(jnp.where(eye, jnp.broadcast_to(row, eye.shape), 0.0), axis=1, keepdims=True)


def _split_bf16(a):
    hi = a.astype(BF16)
    return hi, (a - hi.astype(F32)).astype(BF16)


def _col_bcast(row, eye):
    x = jnp.where(eye, jnp.broadcast_to(row, eye.shape), 0.0)
    hi = x.astype(BF16)
    mid, lo = _split_bf16(x - hi.astype(F32))
    ones = jnp.ones((3 * eye.shape[1], LANES), BF16)
    return jnp.dot(jnp.concatenate([hi, mid, lo], axis=1), ones, preferred_element_type=F32)


def _dot3(a, b):
    a_hi, a_lo = _split_bf16(a)
    b_hi, b_lo = _split_bf16(b)
    return jnp.dot(jnp.concatenate([a_hi, a_hi, a_lo], axis=1), jnp.concatenate([b_hi, b_lo, b_hi], axis=0),
                   preferred_element_type=F32)


def _row_slice(rows_ref, idx, p, half):
    return rows_ref[idx, pl.ds(p, 1), half * CHUNK:(half + 1) * CHUNK]


def _head_out_tiles(acc_refs, base, nw_ref, z_ref, y_ref, n_rows, og_ref=None):
    tile = _pick_tile(n_rows, 256, 8)

    def body(i, carry):
        r = pl.multiple_of(i * tile, tile)
        o = acc_refs[0][pl.ds(base + r, tile), :]
        for acc_ref in acc_refs[1:]:
            o = o + acc_ref[pl.ds(base + r, tile), :]
        if og_ref is not None:
            o = jax.nn.sigmoid(og_ref[0, pl.ds(r, tile), :]) * o
        o = o * lax.rsqrt(jnp.mean(o * o, axis=-1, keepdims=True) + NORM_EPS) * nw_ref[...]
        y_ref[0, pl.ds(r, tile), :] = (o * _silu(z_ref[0, pl.ds(r, tile), :])).astype(y_ref.dtype)
        return carry

    lax.fori_loop(0, n_rows // tile, body, 0)


def _ml_rows(g_refs, bias_ref, h, n_heads, rows_ref):
    li_f = g_refs[0][0, 0] + bias_ref[h]
    lf_f = -_softplus(-(g_refs[1][0, 0] + bias_ref[n_heads + h]))
    li_b = g_refs[2][0, 0] + bias_ref[2 * n_heads + h]
    lf_b = -_softplus(-(g_refs[3][0, 0] + bias_ref[3 * n_heads + h]))
    rows_ref[0] = _seg_cumsum(lf_f, False)
    rows_ref[1] = li_f
    rows_ref[2] = _seg_cumsum(lf_b, True)
    rows_ref[3] = li_b


def _ml_prep_cols(q_ref, k_ref, rows_ref, chains):
    eye = _pair_masks(0)[0]
    lane_i = lax.broadcasted_iota(jnp.int32, (1, LANES), 1)
    st = []
    for d, p in chains:
        rows = pl.ds(pl.multiple_of(p * LANES, LANES), LANES)
        k = k_ref[0, rows, :] * (HEAD_DIM ** -0.5)
        b_row = rows_ref[2 * d, pl.ds(p, 1), :]
        li_row = rows_ref[2 * d + 1, pl.ds(p, 1), :]
        last = CHUNK - 1 if d == 0 else 0
        b_tot = (b_row[:, last:last + 1], b_row[:, CHUNK + last:CHUNK + last + 1])
        end_row = jnp.where(lane_i < CHUNK, b_tot[0], b_tot[1]) - b_row + li_row
        e_max = (jnp.max(end_row[:, :CHUNK], axis=1, keepdims=True),
                 jnp.max(end_row[:, CHUNK:], axis=1, keepdims=True))
        st.append(dict(d=d, rows=rows, k=k, b_row=b_row, li_row=li_row, b_tot=b_tot, e_max=e_max,
                       b_cb=_col_bcast(b_row, eye), end_cb=_col_bcast(end_row, eye),
                       qk=_dot_nt(q_ref[0, rows, :], k)))
    return st


def _ml_prep_intra(v_ref, st):
    ones = jnp.ones((LANES, LANES), BF16)
    for c in st:
        _, incl, _ = _pair_masks(c["d"])
        c["v_ones"] = jnp.concatenate([v_ref[0, c["rows"], :].astype(BF16), ones], axis=1)
        dlog = jnp.where(incl, c["b_cb"] - c["b_row"] + c["li_row"], -jnp.inf)
        c["rowmax"] = jnp.max(dlog, axis=1, keepdims=True)
        p_hi, p_lo = _split_bf16(jnp.exp(dlog - c["rowmax"]) * c["qk"])
        rhs = jnp.concatenate([c["v_ones"], jnp.concatenate([jnp.zeros_like(ones), ones], axis=1)], axis=0)
        c["pv_ps"] = jnp.dot(jnp.concatenate([p_hi, p_lo], axis=1), rhs, preferred_element_type=F32)


def _ml_prep_state(st):
    row_i = lax.broadcasted_iota(jnp.int32, (LANES, 1), 0)
    lane_sq = lax.broadcasted_iota(jnp.int32, (LANES, LANES), 1)
    out = []
    for c in st:
        kw_t = (c["k"] * jnp.exp(c["end_cb"] - jnp.where(row_i < CHUNK, c["e_max"][0], c["e_max"][1]))).T
        kw_t2 = jnp.concatenate([jnp.where(lane_sq < CHUNK, kw_t, 0.0), jnp.where(lane_sq < CHUNK, 0.0, kw_t)],
                                axis=0)
        d_cn = jnp.dot(kw_t2.astype(BF16), c["v_ones"], preferred_element_type=F32)
        out.append((c["b_cb"], c["rowmax"], c["pv_ps"], d_cn, *c["b_tot"], *c["e_max"]))
    return tuple(out)


def _ml_steps_state(q_ref, cnst, chains):
    st = []
    for d, p, half in chains:
        r0 = pl.ds(pl.multiple_of(p * LANES + half * CHUNK, CHUNK), CHUNK)
        cn = cnst[d]
        st.append((r0, cn, _dot(q_ref[0, r0, :], cn)))
    return st


def _ml_steps_update(pre, acc_refs, cnst, mst, chains, st):
    for (d, p, half), (r0, cn, q_cn) in zip(chains, st):
        b_cb, rowmax, pv_ps, d_cn = pre[d][:4]
        b_tot = pre[d][4 + half]
        e_max = pre[d][6 + half]
        sl = slice(half * CHUNK, (half + 1) * CHUNK)
        m_s = mst[d, 0:1, 0:1]
        inter = b_cb[sl] + m_s
        m_i = jnp.maximum(inter, rowmax[sl])
        w_inter = jnp.exp(inter - m_i)
        s_intra = jnp.exp(rowmax[sl] - m_i)
        num = w_inter * q_cn[:, :HEAD_DIM] + s_intra * pv_ps[sl, :HEAD_DIM]
        den = w_inter * q_cn[:, HEAD_DIM:] + s_intra * pv_ps[sl, HEAD_DIM:]
        acc_refs[d][r0, :] = num / jnp.maximum(jnp.abs(den), jnp.exp(-m_i))
        carry_log = b_tot + m_s
        m_new = jnp.maximum(carry_log, e_max)
        cnst[d] = jnp.exp(carry_log - m_new) * cn + jnp.exp(e_max - m_new) * d_cn[half * LANES:(half + 1) * LANES]
        mst[d] = jnp.broadcast_to(m_new, mst.shape[1:])


def _ml_scan(q_ref, k_ref, v_ref, rows_ref, acc_refs, cnst, mst, n_pairs):
    def chains_of(t):
        return [(0, t), (1, n_pairs - 1 - t), (0, t + 1), (1, n_pairs - 2 - t)]

    def prep(chains):
        st = _ml_prep_cols(q_ref, k_ref, rows_ref, chains)
        _ml_prep_intra(v_ref, st)
        return _ml_prep_state(st)

    def body(i, pre):
        t = 2 * i
        st = _ml_prep_cols(q_ref, k_ref, rows_ref, chains_of(jnp.minimum(t + 2, n_pairs - 2)))
        nxt = None
        for j, tt in enumerate((t, t + 1)):
            pf, pb = tt, n_pairs - 1 - tt
            pre_j = pre[2 * j:2 * j + 2]
            for halves in ([(0, pf, 0), (1, pb, 1)], [(0, pf, 1), (1, pb, 0)]):
                held = _ml_steps_state(q_ref, cnst, halves)
                if j == 0 and halves[0][2] == 0:
                    _ml_prep_intra(v_ref, st)
                elif j == 0:
                    nxt = _ml_prep_state(st)
                _ml_steps_update(pre_j, acc_refs, cnst, mst, halves, held)
        return nxt

    lax.fori_loop(0, n_pairs // 2, body, prep(chains_of(0)))


def _mlstm_kernel(bias_ref, ql, kl, vl, ol, zl, qc, kc, vc, oc, zc, gl0, gl1, gl2, gl3, gc0, gc1, gc2, gc3,
                  nw_ref, *rest, n_heads, ctx_out):
    if ctx_out:
        yl_ref, yc_ref = rest[:2]
        rest = rest[2:]
    else:
        yl_ref, yc_ref = rest[0], None
        rest = rest[1:]
    cnst, mst, rows_l, rows_c, acc_lf, acc_lb, acc_cf, acc_cb = rest
    h = pl.program_id(1)
    cnst[...] = jnp.zeros(cnst.shape, F32)
    mst[...] = jnp.zeros(mst.shape, F32)
    _ml_rows((gc0, gc1, gc2, gc3), bias_ref, h, n_heads, rows_c)
    _ml_scan(qc, kc, vc, rows_c, (acc_cf, acc_cb), cnst, mst, rows_c.shape[1])
    _ml_rows((gl0, gl1, gl2, gl3), bias_ref, h, n_heads, rows_l)
    _ml_scan(ql, kl, vl, rows_l, (acc_lf, acc_lb), cnst, mst, rows_l.shape[1])
    _head_out_tiles((acc_lf, acc_lb), 0, nw_ref, zl, yl_ref, acc_lf.shape[0], og_ref=ol)
    if ctx_out:
        _head_out_tiles((acc_cf, acc_cb), 0, nw_ref, zc, yc_ref, acc_cf.shape[0], og_ref=oc)


def _head_block(l, blk):
    return pl.BlockSpec((1, l, HEAD_DIM), lambda b, h, blk=blk: (b, 0, blk + h))


def _gate_block(n_pairs, j, n_heads):
    return pl.BlockSpec((1, 1, n_pairs, LANES), lambda b, h, j=j: (b, j * n_heads + h, 0, 0))


def _mlstm(u_l, u_c, gt_l, gt_c, bias, nw, lay, ctx_out):
    b, l, _ = u_l.shape
    lc = u_c.shape[1]
    nh = lay["mh"]
    blk = lambda off: off // HEAD_DIM
    offs = [lay["MQ"], lay["MQ"] + lay["DM"], lay["MQ"] + 2 * lay["DM"], lay["MO"], lay["MZ"]]
    in_specs = [pl.BlockSpec(memory_space=pltpu.SMEM)]
    in_specs += [_head_block(l, blk(o)) for o in offs]
    in_specs += [_head_block(lc, blk(o)) for o in offs]
    in_specs += [_gate_block(l // LANES, j, nh) for j in range(4)]
    in_specs += [_gate_block(lc // LANES, j, nh) for j in range(4)]
    in_specs += [pl.BlockSpec((1, HEAD_DIM), lambda b_, h: (0, 0))]
    out_specs = [pl.BlockSpec((1, l, HEAD_DIM), lambda b_, h: (b_, 0, h))]
    out_shape = [jax.ShapeDtypeStruct((b, l, nh * HEAD_DIM), BF16)]
    if ctx_out:
        out_specs.append(pl.BlockSpec((1, lc, HEAD_DIM), lambda b_, h: (b_, 0, h)))
        out_shape.append(jax.ShapeDtypeStruct((b, lc, nh * HEAD_DIM), BF16))
    scratch = [pltpu.VMEM((2, HEAD_DIM, 2 * HEAD_DIM), F32), pltpu.VMEM((2, 8, LANES), F32),
               pltpu.VMEM((4, l // LANES, LANES), F32), pltpu.VMEM((4, lc // LANES, LANES), F32),
               pltpu.VMEM((l, HEAD_DIM), F32), pltpu.VMEM((l, HEAD_DIM), F32),
               pltpu.VMEM((lc, HEAD_DIM), F32), pltpu.VMEM((lc, HEAD_DIM), F32)]
    res = pl.pallas_call(
        functools.partial(_mlstm_kernel, n_heads=nh, ctx_out=ctx_out),
        grid=(b, nh),
        in_specs=in_specs, out_specs=out_specs, out_shape=out_shape,
        scratch_shapes=scratch,
        compiler_params=_cparams(("parallel", "parallel"), 48),
        name="mlstm",
    )(bias.reshape(-1), *([u_l] * 5), *([u_c] * 5), *([gt_l] * 4), *([gt_c] * 4), nw)
    return (res[0], res[1]) if ctx_out else (res[0], None)


def _gdn_rows(g_refs, alog_ref, dtb_ref, h, n_heads, rows_ref, base):
    n = g_refs[0].shape[2]
    g_f = -jnp.exp(alog_ref[h]) * _softplus(g_refs[0][0, 0] + dtb_ref[h])
    g_b = -jnp.exp(alog_ref[n_heads + h]) * _softplus(g_refs[1][0, 0] + dtb_ref[n_heads + h])
    rows_ref[0, base:base + n] = _seg_cumsum(g_f, False)
    rows_ref[1, base:base + n] = jax.nn.sigmoid(g_refs[2][0, 0])
    rows_ref[2, base:base + n] = _seg_cumsum(g_b, True)
    rows_ref[3, base:base + n] = jax.nn.sigmoid(g_refs[3][0, 0])


def _gdn_conv(src_refs, w_refs, dst_refs, base, n_units, unit_rows, period):
    def body(i, carry):
        r = pl.multiple_of(i * unit_rows, unit_rows)
        for idx in range(3):
            t = _silu(_conv_rows(src_refs[idx][0, pl.ds(r, unit_rows), :], w_refs[idx][...], period))
            if idx < 2:
                t = t * lax.rsqrt(jnp.sum(t * t, axis=-1, keepdims=True) + NORM_EPS)
            if idx == 0:
                t = t * (HEAD_DIM ** -0.5)
            dst_refs[idx][pl.ds(base + r, unit_rows), :] = t
        return carry

    lax.fori_loop(0, n_units, body, 0)


N_NEUMANN = CHUNK.bit_length() - 2


def _gdn_prep_start(qs, ks, rows_ref, aqk_s, chains):
    eye = _pair_masks(0)[0]
    eye_f = jnp.where(eye, 1.0, 0.0)
    st = []
    for d, p in chains:
        rows = pl.ds(pl.multiple_of(p * LANES, LANES), LANES)
        k = ks[rows, :]
        _, incl, strict = _pair_masks(d)
        g_row = rows_ref[2 * d, pl.ds(p, 1), :]
        beta_col = _to_col(rows_ref[2 * d + 1, pl.ds(p, 1), :], eye)
        g_col = _to_col(g_row, eye)
        dec = jnp.exp(jnp.where(incl, g_col - g_row, -jnp.inf))
        aqk_s[d, rows, :] = (_dot_nt(qs[rows, :], k) * dec).astype(BF16)
        m_low = jnp.where(strict, beta_col * _dot_nt(k, k) * dec, 0.0)
        st.append(dict(d=d, rows=rows, g_row=g_row, g_col=g_col, beta_col=beta_col, pw=m_low, t_inv=eye_f - m_low))
    return st


def _gdn_prep_neumann(st, after_square=None, after_product=None):
    eye_f = jnp.where(_pair_masks(0)[0], 1.0, 0.0)
    for c in st:
        c["pw"] = _dot3(c["pw"], c["pw"])
    if after_square is not None:
        after_square()
    for c in st:
        c["t_inv"] = _dot3(c["t_inv"], eye_f + c["pw"])
    if after_product is not None:
        after_product()


def _gdn_prep_finish(qs, ks, vs, ut_s, w_s, qd_s, ket_s, st):
    row_i = lax.broadcasted_iota(jnp.int32, (LANES, 1), 0)
    for c in st:
        d, rows, g_row, g_col, beta_col = c["d"], c["rows"], c["g_row"], c["g_col"], c["beta_col"]
        k = ks[rows, :]
        e_g = jnp.exp(g_col)
        sol = _dot3(c["t_inv"], jnp.concatenate([beta_col * vs[rows, :], (beta_col * e_g) * k], axis=1))
        ut_s[d, rows, :] = sol[:, :HEAD_DIM]
        w_s[d, rows, :] = sol[:, HEAD_DIM:].astype(BF16)
        qd_s[d, rows, :] = (qs[rows, :] * e_g).astype(BF16)
        last = CHUNK - 1 if d == 0 else 0
        g_tot_col = jnp.where(row_i < CHUNK, g_row[:, last:last + 1], g_row[:, CHUNK + last:CHUNK + last + 1])
        ket_s[d, rows, :] = (k * jnp.exp(g_tot_col - g_col)).T.astype(BF16)


def _gdn_steps_state(w_s, qd_s, sst, chains):
    st = []
    for d, p, half in chains:
        r0 = pl.ds(pl.multiple_of(p * LANES + half * CHUNK, CHUNK), CHUNK)
        s = sst[d]
        lhs = jnp.concatenate([w_s[d, r0, :], qd_s[d, r0, :]], axis=0)
        st.append((r0, s, jnp.dot(lhs, s.astype(BF16), preferred_element_type=F32)))
    return st


def _gdn_steps_update(rows_ref, ut_s, ket_s, aqk_s, sst, acc_refs, chains, st):
    for (d, p, half), (r0, s, ws_qs) in zip(chains, st):
        rp = pl.ds(pl.multiple_of(p * LANES, LANES), LANES)
        lane = half * CHUNK + (CHUNK - 1 if d == 0 else 0)
        g_tot = rows_ref[2 * d, pl.ds(p, 1), lane:lane + 1]
        u = (ut_s[d, r0, :] - ws_qs[:CHUNK]).astype(BF16)
        zero = jnp.zeros_like(u)
        u_pad = jnp.concatenate([u, zero] if half == 0 else [zero, u], axis=0)
        lhs2 = jnp.concatenate([aqk_s[d, r0, :], ket_s[d, rp, :]], axis=0)
        au_ku = jnp.dot(lhs2, u_pad, preferred_element_type=F32)
        acc_refs[d][r0, :] = ws_qs[CHUNK:] + au_ku[:CHUNK]
        sst[d] = jnp.exp(g_tot) * s + au_ku[CHUNK:]


def _gdn_kernel(alog_ref, dtb_ref, ql, kl, vl, zl, qc, kc, vc, zc, gl0, gl1, gl2, gl3, gc0, gc1, gc2, gc3,
                wq, wk, wv, nw_ref, *rest, n_heads, ctx_out):
    if ctx_out:
        yl_ref, yc_ref = rest[:2]
        rest = rest[2:]
    else:
        yl_ref, yc_ref = rest[0], None
        rest = rest[1:]
    sst, rows, qs, ks, vs, ut_s, w_s, qd_s, ket_s, aqk_s, acc_f, acc_b = rest
    h = pl.program_id(1)
    n_l = ql.shape[1]
    n_c = qc.shape[1]
    nl = n_l // LANES
    nc = n_c // LANES
    sst[...] = jnp.zeros(sst.shape, F32)
    _gdn_rows((gl0, gl1, gl2, gl3), alog_ref, dtb_ref, h, n_heads, rows, 0)
    _gdn_rows((gc0, gc1, gc2, gc3), alog_ref, dtb_ref, h, n_heads, rows, nl)
    conv_rows = _pick_tile(n_l, 4 * CHUNK, CHUNK)
    _gdn_conv((ql, kl, vl), (wq, wk, wv), (qs, ks, vs), 0, n_l // conv_rows, conv_rows, CHUNK)
    _gdn_conv((qc, kc, vc), (wq, wk, wv), (qs, ks, vs), n_l, 1, n_c, n_c)
    prepped = (ut_s, w_s, qd_s, ket_s, aqk_s)
    n_steps = nl + nc

    def fwd_pair(t):
        return jnp.where(t < nc, nl + t, t - nc)

    def bwd_pair(t):
        return n_steps - 1 - t

    def prep(chains, steps=()):
        st = _gdn_prep_start(qs, ks, rows, aqk_s, chains)
        for i in range(N_NEUMANN):
            if i < len(steps):
                held = []
                first = lambda c=steps[i]: held.append(_gdn_steps_state(w_s, qd_s, sst, c))
                second = lambda c=steps[i]: _gdn_steps_update(rows, ut_s, ket_s, aqk_s, sst, (acc_f, acc_b), c,
                                                              held[0])
                _gdn_prep_neumann(st, first, second)
            else:
                _gdn_prep_neumann(st)
        _gdn_prep_finish(qs, ks, vs, ut_s, w_s, qd_s, ket_s, st)

    def chains_of(t):
        return [(0, fwd_pair(t)), (1, bwd_pair(t)), (0, fwd_pair(t + 1)), (1, bwd_pair(t + 1))]

    prep(chains_of(0))

    def step(i, carry):
        t = 2 * i
        halves = []
        for tt in (t, t + 1):
            pf, pb = fwd_pair(tt), bwd_pair(tt)
            halves.append([(0, pf, 0), (1, pb, 1)])
            halves.append([(0, pf, 1), (1, pb, 0)])
        prep(chains_of(jnp.minimum(t + 2, n_steps - 2)), steps=halves)
        return carry

    lax.fori_loop(0, n_steps // 2, step, 0)
    _head_out_tiles((acc_f, acc_b), 0, nw_ref, zl, yl_ref, n_l)
    if ctx_out:
        _head_out_tiles((acc_f, acc_b), n_l, nw_ref, zc, yc_ref, n_c)


def _gdn(u_l, u_c, gt_l, gt_c, conv_w, a_log, dt_bias, nw, lay, ctx_out):
    b, l, _ = u_l.shape
    lc = u_c.shape[1]
    nh = lay["gh"]
    taps = conv_w.shape[0]
    blk = lambda off: off // HEAD_DIM
    offs = [lay["GQ"], lay["GQ"] + lay["DG"], lay["GQ"] + 2 * lay["DG"], lay["GZ"]]
    smem = pl.BlockSpec(memory_space=pltpu.SMEM)
    in_specs = [smem, smem]
    in_specs += [_head_block(l, blk(o)) for o in offs]
    in_specs += [_head_block(lc, blk(o)) for o in offs]
    in_specs += [_gate_block(l // LANES, j, nh) for j in range(4)]
    in_specs += [_gate_block(lc // LANES, j, nh) for j in range(4)]
    in_specs += [pl.BlockSpec((taps, HEAD_DIM), lambda b_, h, j=j: (0, j * nh + h)) for j in range(3)]
    in_specs += [pl.BlockSpec((1, HEAD_DIM), lambda b_, h: (0, 0))]
    out_specs = [pl.BlockSpec((1, l, HEAD_DIM), lambda b_, h: (b_, 0, h))]
    out_shape = [jax.ShapeDtypeStruct((b, l, nh * HEAD_DIM), BF16)]
    if ctx_out:
        out_specs.append(pl.BlockSpec((1, lc, HEAD_DIM), lambda b_, h: (b_, 0, h)))
        out_shape.append(jax.ShapeDtypeStruct((b, lc, nh * HEAD_DIM), BF16))
    lt = l + lc
    seq_f32 = pltpu.VMEM((lt, HEAD_DIM), F32)
    dir_bf16 = pltpu.VMEM((2, lt, HEAD_DIM), BF16)
    scratch = [pltpu.VMEM((2, HEAD_DIM, HEAD_DIM), F32), pltpu.VMEM((4, lt // LANES, LANES), F32),
               seq_f32, seq_f32, seq_f32,
               pltpu.VMEM((2, lt, HEAD_DIM), F32), dir_bf16, dir_bf16, dir_bf16, dir_bf16,
               seq_f32, seq_f32]
    res = pl.pallas_call(
        functools.partial(_gdn_kernel, n_heads=nh, ctx_out=ctx_out),
        grid=(b, nh),
        in_specs=in_specs, out_specs=out_specs, out_shape=out_shape,
        scratch_shapes=scratch,
        compiler_params=_cparams(("parallel", "parallel"), 56),
        name="gdn",
    )(a_log.reshape(-1), dt_bias.reshape(-1), *([u_l] * 4), *([u_c] * 4), *([gt_l] * 4), *([gt_c] * 4),
      *([conv_w] * 3), nw)
    return (res[0], res[1]) if ctx_out else (res[0], None)


def _dft_tables(l):
    k = jnp.arange(l, dtype=jnp.int32)
    ang = lambda t: ((k[:, None] * t[None, :]) % (2 * l)).astype(F32) * (math.pi / l)
    ang_a = ang(jnp.arange(l // CHUNK, dtype=jnp.int32) * CHUNK)[:, :, None]
    ang_b = ang(jnp.arange(CHUNK, dtype=jnp.int32))[:, None, :]
    cos_t = (jnp.cos(ang_a) * jnp.cos(ang_b) - jnp.sin(ang_a) * jnp.sin(ang_b)).reshape(l, l)
    sin_t = (jnp.sin(ang_a) * jnp.cos(ang_b) + jnp.cos(ang_a) * jnp.sin(ang_b)).reshape(l, l)
    alt = jnp.where(k % 2 == 0, 1.0, -1.0).astype(F32)
    sin_f = sin_t.at[0, :].set(alt)
    return cos_t.astype(BF16), sin_f.astype(BF16), sin_f.T.astype(BF16)


def _hy_prep_kernel(p_ref, w_ref, x1_ref, x2_ref, v_ref, vb_ref, *, period, dh):
    y = _conv_rows(p_ref[0], w_ref[...], period)
    x1_ref[0] = y[:, :dh]
    x2_ref[0] = y[:, dh:2 * dh]
    v = y[:, 2 * dh:]
    v_ref[0] = v
    vb_ref[0] = v.astype(BF16)


def _hy_prep(u, conv_w, dh, grid_mask):
    b, l, _ = u.shape
    taps = conv_w.shape[0]
    tr = _pick_tile(l, 256, CHUNK)
    period = CHUNK if grid_mask else l
    if not grid_mask:
        tr = l
    o_spec = pl.BlockSpec((1, tr, dh), lambda i, m: (i, m, 0))
    return pl.pallas_call(
        functools.partial(_hy_prep_kernel, period=period, dh=dh),
        grid=(b, l // tr),
        in_specs=[pl.BlockSpec((1, tr, 3 * dh), lambda i, m: (i, m, 0)),
                  pl.BlockSpec((taps, 3 * dh), lambda i, m: (0, 0))],
        out_specs=[o_spec] * 4,
        out_shape=[jax.ShapeDtypeStruct((b, l, dh), F32)] * 3 + [jax.ShapeDtypeStruct((b, l, dh), BF16)],
        compiler_params=_cparams(("parallel", "parallel"), 40),
        name="hy_prep",
    )(u, conv_w)


def _filter_kernel(feats_ref, featr_ref, w1_ref, b1_ref, fr_ref, w2_ref, b2_ref, w3c_ref, w3a_ref, dl_ref,
                   h2_ref, h2r_ref, hid_ref, hidr_ref):
    @pl.when((pl.program_id(0) == 0) & (pl.program_id(1) == 0))
    def _():
        for f_ref, h_ref in ((feats_ref, hid_ref), (featr_ref, hidr_ref)):
            hid = jnp.sin(fr_ref[...] * (_dot_hi(f_ref[...], w1_ref[...]) + b1_ref[...]))
            h_ref[...] = jnp.sin(fr_ref[...] * (_dot_hi(hid, w2_ref[...]) + b2_ref[...]))

    def raw(h_ref, f_ref, w3_ref):
        return _dot_hi(h_ref[...], w3_ref[...]) * jnp.exp(-f_ref[:, 0:1] * dl_ref[...])

    c_f = raw(hid_ref, feats_ref, w3c_ref)
    a_f = raw(hid_ref, feats_ref, w3a_ref)
    den_c = jnp.sum(jnp.abs(c_f), axis=0, keepdims=True) + NORM_EPS
    den_a = jnp.sum(jnp.abs(a_f), axis=0, keepdims=True) + NORM_EPS
    c_f = c_f / den_c
    a_f = a_f / den_a
    row0 = lax.broadcasted_iota(jnp.int32, c_f.shape, 0) == 0
    centre = c_f[0:1] + a_f[0:1]
    h2_ref[0, 0] = jnp.where(row0, 0.0, raw(hidr_ref, featr_ref, w3a_ref) / den_a)
    h2_ref[0, 1] = jnp.where(row0, centre, c_f)
    h2r_ref[0, 0] = jnp.where(row0, 0.0, raw(hidr_ref, featr_ref, w3c_ref) / den_c)
    h2r_ref[0, 1] = jnp.where(row0, centre, a_f)


def _filter_feats(pos, l, n_emb):
    t = pos / max(l - 1, 1)
    ang = 2.0 * math.pi * pos / l
    bands = jnp.linspace(1e-4, FILTER_BANDS - 1, FILTER_BANDS, dtype=F32)
    feats = jnp.concatenate([t[:, None], jnp.cos(ang[:, None] * bands), -jnp.sin(ang[:, None] * bands)], axis=-1)
    return jnp.pad(feats, ((0, 0), (0, LANES - n_emb)))


def _hy_filters(l, w1, b1, freq, w2, b2, w3, dh):
    n_emb, n_hid = w1.shape
    pos = jnp.arange(l, dtype=F32)
    feats = _filter_feats(pos, l, n_emb)
    feats_r = _filter_feats(l - pos, l, n_emb)
    pc = LANES - n_hid
    w1p = jnp.pad(w1, ((0, LANES - n_emb), (0, pc)))
    w2p = jnp.pad(w2, ((0, pc), (0, pc)))
    w3p = jnp.pad(w3, ((0, pc), (0, 0)))
    row = lambda a: jnp.pad(a, (0, pc))[None]
    deltas = jnp.abs(jnp.linspace(MIN_DECAY, MAX_DECAY, dh, dtype=F32))[None]
    nct = dh // LANES
    const = lambda shape: pl.BlockSpec(shape, lambda o, c: (0, 0))
    o_spec = pl.BlockSpec((1, 2, l, LANES), lambda o, c: (o, 0, 0, c))
    return pl.pallas_call(
        _filter_kernel,
        grid=(HYENA_ORDER, nct),
        in_specs=[const((l, LANES)), const((l, LANES)), const((LANES, LANES)), const((1, LANES)),
                  const((1, LANES)), const((LANES, LANES)), const((1, LANES)),
                  pl.BlockSpec((LANES, LANES), lambda o, c: (0, o * 2 * nct + c)),
                  pl.BlockSpec((LANES, LANES), lambda o, c: (0, o * 2 * nct + nct + c)),
                  pl.BlockSpec((1, LANES), lambda o, c: (0, c))],
        out_specs=[o_spec, o_spec],
        out_shape=[jax.ShapeDtypeStruct((HYENA_ORDER, 2, l, dh), F32)] * 2,
        scratch_shapes=[pltpu.VMEM((l, LANES), F32)] * 2,
        compiler_params=_cparams(("arbitrary", "arbitrary"), 56),
        name="hy_filter",
    )(feats, feats_r, w1p, row(b1), row(freq), w2p, row(b2), w3p, w3p, deltas)


def _spectrum_kernel(c_ref, s_ref, h2_ref, h2r_ref, a_ref, nyq_ref, hsin_ref, *, p):
    pos = h2_ref[0, 0]
    neg = h2r_ref[0, 0]
    is0 = lax.broadcasted_iota(jnp.int32, pos.shape, 0) == 0
    h_sum = jnp.where(is0, pos, pos + neg)
    sign = jnp.where(lax.broadcasted_iota(jnp.int32, pos.shape, 0) % 2 == 0, 1.0, -1.0)
    nyq = jnp.sum(h_sum * sign, axis=0, keepdims=True)
    a = jnp.dot(c_ref[...], h_sum.astype(BF16), preferred_element_type=F32)
    s = jnp.dot(s_ref[...], (pos - neg).astype(BF16), preferred_element_type=F32)
    wk = jnp.where(is0, 0.5 / p, 1.0 / p)
    a_ref[0, 0] = a * wk
    nyq_ref[0, 0] = jnp.broadcast_to((nyq - a[0:1]) * (0.5 / p), nyq_ref.shape[2:])
    hsin_ref[0, 0] = jnp.where(is0, 0.0, s) * wk


def _hy_spectrum(tabs, h2, h2r, p):
    cos_t, sin_f, _ = tabs
    n_ord, _, l, dh = h2.shape
    nb = l // p
    n_win = 2 * nb - 1
    h2 = h2.reshape(n_ord, 2 * nb, p, dh)
    h2r = h2r.reshape(n_ord, 2 * nb, p, dh)
    tab = pl.BlockSpec((p, p), lambda o, m: (0, 0))
    o_spec = pl.BlockSpec((1, 1, p, dh), lambda o, m: (o, m, 0, 0))
    return pl.pallas_call(
        functools.partial(_spectrum_kernel, p=p),
        grid=(n_ord, n_win),
        in_specs=[tab, tab,
                  pl.BlockSpec((1, 1, p, dh), lambda o, m: (o, m + 1, 0, 0)),
                  pl.BlockSpec((1, 1, p, dh), lambda o, m: (o, 2 * nb - 1 - m, 0, 0))],
        out_specs=[o_spec, pl.BlockSpec((1, 1, 8, dh), lambda o, m: (o, m, 0, 0)), o_spec],
        out_shape=[jax.ShapeDtypeStruct((n_ord, n_win, p, dh), F32),
                   jax.ShapeDtypeStruct((n_ord, n_win, 8, dh), F32),
                   jax.ShapeDtypeStruct((n_ord, n_win, p, dh), F32)],
        compiler_params=_cparams(("parallel", "parallel"), 48),
        name="hy_spectrum",
    )(cos_t, sin_f, h2, h2r)


def _hy_block_kernel(c_ref, sf_ref, si_ref, v_ref, a_ref, nyq_ref, hsin_ref, x_ref, vp_ref, sk_ref, *rest,
                     nb, last):
    if last:
        z_ref, o_ref, xc_s, xs_s, yc_s, ys_s = rest
    else:
        of_ref, ob_ref, xc_s, xs_s, yc_s, ys_s = rest
    p = c_ref.shape[0]
    i = pl.program_id(2)

    @pl.when(i == 0)
    def _():
        for j in range(nb):
            xj = v_ref[0, j * p:(j + 1) * p, :]
            xc_s[j] = jnp.dot(c_ref[...], xj, preferred_element_type=F32)
            xs_s[j] = jnp.dot(sf_ref[...], xj, preferred_element_type=F32)

    tile = _pick_tile(p, 64, 8)
    nyq_fix = jnp.zeros((1, LANES), F32)
    for j in range(nb):
        nyq_fix = nyq_fix + xs_s[j, 0:1, :] * nyq_ref[0, i - j + (nb - 1), 0:1, :]

    def rows_body(r, carry):
        rs = pl.ds(pl.multiple_of(r * tile, tile), tile)
        yc = jnp.zeros((tile, LANES), F32)
        ys = jnp.zeros((tile, LANES), F32)
        for j in range(nb):
            w = i - j + (nb - 1)
            xc = xc_s[j, rs, :]
            xs = xs_s[j, rs, :]
            a = a_ref[0, w, rs, :]
            hsin = hsin_ref[0, w, rs, :]
            yc = yc + (xc * a - xs * hsin)
            ys = ys + (xc * hsin + xs * a)
        row = lax.broadcasted_iota(jnp.int32, ys.shape, 0) + r * tile
        yc_s[rs, :] = yc.astype(BF16)
        ys_s[rs, :] = (ys + jnp.where(row == 0, nyq_fix, 0.0)).astype(BF16)
        return carry

    lax.fori_loop(0, p // tile, rows_body, 0)
    conv = jnp.dot(c_ref[...], yc_s[...], preferred_element_type=F32)
    conv = conv + jnp.dot(si_ref[...], ys_s[...], preferred_element_type=F32)
    y = x_ref[0] * (conv + sk_ref[0] * vp_ref[0])
    if last:
        o_ref[0] = (y * _silu(z_ref[0])).astype(BF16)
    else:
        of_ref[0] = y
        ob_ref[0] = y.astype(BF16)


def _hy_block(tabs, vb, spec, x, v_prev, skip, order, p, u=None, z_blk=None):
    b, l, dh = vb.shape
    nb = l // p
    n_win = 2 * nb - 1
    last = u is not None
    tab = pl.BlockSpec((p, p), lambda c, i, t: (0, 0))
    filt = pl.BlockSpec((1, n_win, p, LANES), lambda c, i, t: (order, 0, 0, c))
    tile = pl.BlockSpec((1, p, LANES), lambda c, i, t: (i, t, c))
    nyq = pl.BlockSpec((1, n_win, 8, LANES), lambda c, i, t: (order, 0, 0, c))
    in_specs = [tab, tab, tab, pl.BlockSpec((1, l, LANES), lambda c, i, t: (i, 0, c)), filt, nyq, filt,
                tile, tile, pl.BlockSpec((1, 1, LANES), lambda c, i, t: (order, 0, c))]
    args = [*tabs, vb, *spec, x, v_prev, skip[:, None, :]]
    if last:
        in_specs.append(pl.BlockSpec((1, p, LANES), lambda c, i, t: (i, t, z_blk + c)))
        args.append(u)
        out_specs = tile
        out_shape = jax.ShapeDtypeStruct((b, l, dh), BF16)
    else:
        out_specs = [tile, tile]
        out_shape = [jax.ShapeDtypeStruct((b, l, dh), F32), jax.ShapeDtypeStruct((b, l, dh), BF16)]
    return pl.pallas_call(
        functools.partial(_hy_block_kernel, nb=nb, last=last),
        grid=(dh // LANES, b, nb),
        in_specs=in_specs, out_specs=out_specs, out_shape=out_shape,
        scratch_shapes=[pltpu.VMEM((nb, p, LANES), F32)] * 2 + [pltpu.VMEM((p, LANES), BF16)] * 2,
        compiler_params=_cparams(("parallel", "parallel", "arbitrary"), 56),
        name="hy_block",
    )(*args)


def _hyena(u, conv_w, w1, b1, freq, w2, b2, w3, skip, lay, grid_mask):
    dh = lay["DH"]
    l = u.shape[1]
    p = min(HY_BLOCK, l)
    tabs = _dft_tables(p)
    x1, x2, v, vb = _hy_prep(u, conv_w, dh, grid_mask)
    spec = _hy_spectrum(tabs, *_hy_filters(l, w1, b1, freq, w2, b2, w3, dh), p)
    y1, y1b = _hy_block(tabs, vb, spec, x1, v, skip, 0, p)
    return _hy_block(tabs, y1b, spec, x2, y1, skip, 1, p, u=u, z_blk=lay["HZ"] // LANES)


def _layout(d):
    dg, dh, dm = 3 * d // 8, d // 4, 3 * d // 8
    lay = {"DG": dg, "DH": dh, "DM": dm, "gh": dg // HEAD_DIM, "mh": dm // HEAD_DIM}
    lay["HZ"] = 3 * dh
    lay["GQ"] = 4 * dh
    lay["GZ"] = lay["GQ"] + 3 * dg
    lay["GAB"] = lay["GZ"] + dg
    lay["MQ"] = lay["GAB"] + LANES
    lay["MO"] = lay["MQ"] + 3 * dm
    lay["MZ"] = lay["MO"] + dm
    lay["MG"] = lay["MZ"] + dm
    lay["NP"] = lay["MG"] + LANES
    return lay


def _pack_w_in(w, lay):
    dg, dh, dm, gh, mh = lay["DG"], lay["DH"], lay["DM"], lay["gh"], lay["mh"]
    sizes = (3 * dg, dg, 4 * gh, 3 * dh, dh, 3 * dm, dm, dm, 4 * mh)
    offs = [0]
    for s in sizes:
        offs.append(offs[-1] + s)
    seg = [w[:, offs[i]:offs[i + 1]] for i in range(len(sizes))]
    g_qkv, g_z, g_ab, h_p, h_z, m_qkv, m_o, m_z, m_g = seg
    padl = lambda a: jnp.pad(a, ((0, 0), (0, LANES - a.shape[1])))
    return jnp.concatenate([h_p, h_z, g_qkv, g_z, padl(g_ab), m_qkv, m_o, m_z, padl(m_g)], axis=1).astype(BF16)


def _gate_rows(u, off, n):
    b, l, _ = u.shape
    return jnp.transpose(u[:, :, off:off + n], (0, 2, 1)).reshape(b, n, l // LANES, LANES)


def kernel(x, c, ctx, c_ctx, norm_w, mod_w, mod_b, w_in, gdn_conv, gdn_a_log, gdn_dt_bias, gdn_norm, hy_conv,
           hy_w1, hy_b1, hy_freq, hy_w2, hy_b2, hy_w3, hy_skip, ml_gate_bias, ml_norm, w_out, final_norm):
    b, l, d = x.shape
    lc = ctx.shape[1]
    depth = norm_w.shape[0]
    lay = _layout(d)
    dg, dh = lay["DG"], lay["DH"]
    assert b < COND_ROWS and l % (2 * LANES) == 0 and lc % (2 * LANES) == 0 and d % 1024 == 0
    assert l % min(HY_BLOCK, l) == 0
    cond = jnp.zeros((COND_ROWS, d), F32).at[:b].set(c).at[b].set(c_ctx)
    for layer in range(depth):
        last = layer == depth - 1
        mods = _adaln(cond, mod_w[layer], mod_b[layer][None])
        sh, sc, gt = mods[:, :d], mods[:, d:2 * d], mods[:, 2 * d:]
        lat = lambda m: m[:b, None, :]
        cx = lambda m: jnp.broadcast_to(m[b][None, None, :], (b, 1, d))
        wp = _pack_w_in(w_in[layer], lay)
        nw = norm_w[layer][None]
        u_l = _inproj(x, nw, lat(sc), lat(sh), wp)
        u_c = _inproj(ctx, nw, cx(sc), cx(sh), wp)
        g_rows = lambda u, off, n: _gate_rows(u, off, n)
        yg_l, yg_c = _gdn(u_l, u_c, g_rows(u_l, lay["GAB"], 4 * lay["gh"]), g_rows(u_c, lay["GAB"], 4 * lay["gh"]),
                          gdn_conv[layer], gdn_a_log[layer], gdn_dt_bias[layer], gdn_norm[layer][None], lay,
                          not last)
        ym_l, ym_c = _mlstm(u_l, u_c, g_rows(u_l, lay["MG"], 4 * lay["mh"]), g_rows(u_c, lay["MG"], 4 * lay["mh"]),
                            ml_gate_bias[layer], ml_norm[layer][None], lay, not last)
        hy = (hy_conv[layer], hy_w1[layer], hy_b1[layer], hy_freq[layer], hy_w2[layer], hy_b2[layer],
              hy_w3[layer], hy_skip[layer])
        yh_l = _hyena(u_l, *hy, lay, True)
        wo = w_out[layer].astype(BF16)
        wg, wh, wm = wo[:dg], wo[dg:dg + dh], wo[dg + dh:]
        fw = final_norm[None]
        x = _outproj(x, yg_l, yh_l, ym_l, wg, wh, wm, lat(gt), fw, last)
        if not last:
            yh_c = _hyena(u_c, *hy, lay, False)
            ctx = _outproj(ctx, yg_c, yh_c, ym_c, wg, wh, wm, cx(gt), fw, False)
    return x
```

```python
import functools
import math

import jax
import jax.numpy as jnp
from jax import lax
from jax.experimental import pallas as pl
from jax.experimental.pallas import tpu as pltpu

HEAD_DIM = 128
CHUNK = 64
LANES = 128
NORM_EPS = 1e-6
HYENA_ORDER = 2
HY_BLOCK = 512
FILTER_BANDS = 16
DECAY_TARGET = 1e-2
MIN_DECAY = math.log(DECAY_TARGET) / 1.5
MAX_DECAY = math.log(DECAY_TARGET) / 0.3
COND_ROWS = 16

F32 = jnp.float32
BF16 = jnp.bfloat16
HI = lax.Precision.HIGHEST


def _cparams(sem, vmem_mb):
    return pltpu.CompilerParams(dimension_semantics=sem, vmem_limit_bytes=vmem_mb << 20)


def _dot(a, b):
    return jnp.dot(a.astype(BF16), b.astype(BF16), preferred_element_type=F32)


def _dot_hi(a, b):
    return jnp.dot(a, b, precision=HI, preferred_element_type=F32)


def _dot_nt(a, b):
    return lax.dot_general(a.astype(BF16), b.astype(BF16), (((1,), (1,)), ((), ())),
                           preferred_element_type=F32)


def _dot_tn(a, b):
    return lax.dot_general(a.astype(BF16), b.astype(BF16), (((0,), (0,)), ((), ())),
                           preferred_element_type=F32)


def _silu(x):
    return x * jax.nn.sigmoid(x)


def _softplus(x):
    return jnp.maximum(x, 0.0) + jnp.log(1.0 + jnp.exp(-jnp.abs(x)))


def _pick_tile(n, cap, unit):
    t = (min(n, cap) // unit) * unit
    while n % t:
        t -= unit
    return t


def _adaln_kernel(c_ref, w_ref, b_ref, o_ref):
    o_ref[...] = _dot_hi(_silu(c_ref[...]), w_ref[...]) + b_ref[...]


def _adaln(cond, w, b):
    d, n = w.shape
    tn = _pick_tile(n, 768, LANES)
    return pl.pallas_call(
        _adaln_kernel,
        grid=(n // tn,),
        in_specs=[pl.BlockSpec((COND_ROWS, d), lambda j: (0, 0)),
                  pl.BlockSpec((d, tn), lambda j: (0, j)),
                  pl.BlockSpec((1, tn), lambda j: (0, j))],
        out_specs=pl.BlockSpec((COND_ROWS, tn), lambda j: (0, j)),
        out_shape=jax.ShapeDtypeStruct((COND_ROWS, n), F32),
        compiler_params=_cparams(("parallel",), 40),
        name="adaln",
    )(cond, w, b)


def _inproj_kernel(x_ref, nw_ref, sc_ref, sh_ref, w_ref, o_ref, xn_ref):
    @pl.when(pl.program_id(2) == 0)
    def _():
        x = x_ref[0]
        r = lax.rsqrt(jnp.mean(x * x, axis=-1, keepdims=True) + NORM_EPS)
        y = (x * r * nw_ref[...]) * (1.0 + sc_ref[0]) + sh_ref[0]
        xn_ref[...] = y.astype(BF16)

    o_ref[0] = jnp.dot(xn_ref[...], w_ref[...], preferred_element_type=F32)


def _inproj(x, nw, sc, sh, wp):
    b, l, d = x.shape
    n = wp.shape[1]
    tm = _pick_tile(l, 1024, 8)
    tn = _pick_tile(n, 1280, LANES)
    return pl.pallas_call(
        _inproj_kernel,
        grid=(b, l // tm, n // tn),
        in_specs=[pl.BlockSpec((1, tm, d), lambda i, m, j: (i, m, 0)),
                  pl.BlockSpec((1, d), lambda i, m, j: (0, 0)),
                  pl.BlockSpec((1, 1, d), lambda i, m, j: (i, 0, 0)),
                  pl.BlockSpec((1, 1, d), lambda i, m, j: (i, 0, 0)),
                  pl.BlockSpec((d, tn), lambda i, m, j: (0, j))],
        out_specs=pl.BlockSpec((1, tm, tn), lambda i, m, j: (i, m, j)),
        out_shape=jax.ShapeDtypeStruct((b, l, n), F32),
        scratch_shapes=[pltpu.VMEM((tm, d), BF16)],
        compiler_params=_cparams(("parallel", "parallel", "arbitrary"), 56),
        name="inproj",
    )(x, nw, sc, sh, wp)


def _outproj_kernel(x_ref, yg_ref, yh_ref, ym_ref, wg_ref, wh_ref, wm_ref, gt_ref, fw_ref, o_ref, *, final):
    acc = jnp.dot(yg_ref[0], wg_ref[...], preferred_element_type=F32)
    acc = acc + jnp.dot(yh_ref[0], wh_ref[...], preferred_element_type=F32)
    acc = acc + jnp.dot(ym_ref[0], wm_ref[...], preferred_element_type=F32)
    xn = x_ref[0] + gt_ref[0] * acc
    if final:
        r = lax.rsqrt(jnp.mean(xn * xn, axis=-1, keepdims=True) + NORM_EPS)
        xn = xn * r * fw_ref[...]
    o_ref[0] = xn


def _outproj(x, yg, yh, ym, wg, wh, wm, gt, fw, final):
    b, l, d = x.shape
    tm = _pick_tile(l, 512, 8)
    row = lambda w: pl.BlockSpec((1, tm, w), lambda i, m: (i, m, 0))
    full = lambda a: pl.BlockSpec(a.shape, lambda i, m: (0, 0))
    return pl.pallas_call(
        functools.partial(_outproj_kernel, final=final),
        grid=(b, l // tm),
        in_specs=[row(d), row(yg.shape[2]), row(yh.shape[2]), row(ym.shape[2]),
                  full(wg), full(wh), full(wm),
                  pl.BlockSpec((1, 1, d), lambda i, m: (i, 0, 0)),
                  pl.BlockSpec((1, d), lambda i, m: (0, 0))],
        out_specs=row(d),
        out_shape=jax.ShapeDtypeStruct((b, l, d), F32),
        compiler_params=_cparams(("parallel", "parallel"), 48),
        name="outproj",
    )(x, yg, yh, ym, wg, wh, wm, gt, fw)


def _conv_rows(x, w, period):
    rows = x.shape[0]
    taps = w.shape[0]
    pad = taps // 2
    pos = lax.broadcasted_iota(jnp.int32, x.shape, 0) % period
    y = None
    for j in range(taps):
        off = j - pad
        if off == 0:
            term = x * w[j:j + 1]
        else:
            shifted = pltpu.roll(x, (-off) % rows, axis=0)
            ok = (pos >= -off) if off < 0 else (pos < period - off)
            term = jnp.where(ok, shifted, 0.0) * w[j:j + 1]
        y = term if y is None else y + term
    return y


def _seg_cumsum(x, reverse):
    lane = lax.broadcasted_iota(jnp.int32, x.shape, 1) % CHUNK
    s = 1
    while s < CHUNK:
        if reverse:
            shifted = pltpu.roll(x, LANES - s, axis=1)
            ok = lane < CHUNK - s
        else:
            shifted = pltpu.roll(x, s, axis=1)
            ok = lane >= s
        x = x + jnp.where(ok, shifted, 0.0)
        s *= 2
    return x


def _chunk_masks(d):
    ii = lax.broadcasted_iota(jnp.int32, (CHUNK, CHUNK), 0)
    jj = lax.broadcasted_iota(jnp.int32, (CHUNK, CHUNK), 1)
    eye = ii == jj
    if d == 0:
        return eye, jj <= ii, jj < ii
    return eye, jj >= ii, jj > ii


def _pair_masks(d):
    ii = lax.broadcasted_iota(jnp.int32, (LANES, LANES), 0)
    jj = lax.broadcasted_iota(jnp.int32, (LANES, LANES), 1)
    lo = (ii // CHUNK) * CHUNK
    eye = ii == jj
    if d == 0:
        return eye, (jj >= lo) & (jj <= ii), (jj >= lo) & (jj < ii)
    return eye, (jj < lo + CHUNK) & (jj >= ii), (jj < lo + CHUNK) & (jj > ii)


def _to_col(row, eye):
    return jnp.sum(jnp.where(eye, jnp.broadcast_to(row, eye.shape), 0.0), axis=1, keepdims=True)


def _split_bf16(a):
    hi = a.astype(BF16)
    return hi, (a - hi.astype(F32)).astype(BF16)


def _col_bcast(row, eye):
    x = jnp.where(eye, jnp.broadcast_to(row, eye.shape), 0.0)
    hi = x.astype(BF16)
    mid, lo = _split_bf16(x - hi.astype(F32))
    ones = jnp.ones((3 * eye.shape[1], LANES), BF16)
    return jnp.dot(jnp.concatenate([hi, mid, lo], axis=1), ones, preferred_element_type=F32)


def _dot3(a, b):
    a_hi, a_lo = _split_bf16(a)
    b_hi, b_lo = _split_bf16(b)
    return jnp.dot(jnp.concatenate([a_hi, a_hi, a_lo], axis=1), jnp.concatenate([b_hi, b_lo, b_hi], axis=0),
                   preferred_element_type=F32)


def _row_slice(rows_ref, idx, p, half):
    return rows_ref[idx, pl.ds(p, 1), half * CHUNK:(half + 1) * CHUNK]


def _head_out_tiles(acc_refs, base, nw_ref, z_ref, y_ref, n_rows, og_ref=None):
    tile = _pick_tile(n_rows, 256, 8)

    def body(i, carry):
        r = pl.multiple_of(i * tile, tile)
        o = acc_refs[0][pl.ds(base + r, tile), :]
        for acc_ref in acc_refs[1:]:
            o = o + acc_ref[pl.ds(base + r, tile), :]
        if og_ref is not None:
            o = jax.nn.sigmoid(og_ref[0, pl.ds(r, tile), :]) * o
        o = o * lax.rsqrt(jnp.mean(o * o, axis=-1, keepdims=True) + NORM_EPS) * nw_ref[...]
        y_ref[0, pl.ds(r, tile), :] = (o * _silu(z_ref[0, pl.ds(r, tile), :])).astype(y_ref.dtype)
        return carry

    lax.fori_loop(0, n_rows // tile, body, 0)


def _ml_rows(g_refs, bias_ref, h, n_heads, rows_ref):
    li_f = g_refs[0][0, 0] + bias_ref[h]
    lf_f = -_softplus(-(g_refs[1][0, 0] + bias_ref[n_heads + h]))
    li_b = g_refs[2][0, 0] + bias_ref[2 * n_heads + h]
    lf_b = -_softplus(-(g_refs[3][0, 0] + bias_ref[3 * n_heads + h]))
    rows_ref[0] = _seg_cumsum(lf_f, False)
    rows_ref[1] = li_f
    rows_ref[2] = _seg_cumsum(lf_b, True)
    rows_ref[3] = li_b


def _ml_prep_cols(q_ref, k_ref, rows_ref, chains):
    eye = _pair_masks(0)[0]
    lane_i = lax.broadcasted_iota(jnp.int32, (1, LANES), 1)
    st = []
    for d, p in chains:
        rows = pl.ds(pl.multiple_of(p * LANES, LANES), LANES)
        k = k_ref[0, rows, :] * (HEAD_DIM ** -0.5)
        b_row = rows_ref[2 * d, pl.ds(p, 1), :]
        li_row = rows_ref[2 * d + 1, pl.ds(p, 1), :]
        last = CHUNK - 1 if d == 0 else 0
        b_tot = (b_row[:, last:last + 1], b_row[:, CHUNK + last:CHUNK + last + 1])
        end_row = jnp.where(lane_i < CHUNK, b_tot[0], b_tot[1]) - b_row + li_row
        e_max = (jnp.max(end_row[:, :CHUNK], axis=1, keepdims=True),
                 jnp.max(end_row[:, CHUNK:], axis=1, keepdims=True))
        st.append(dict(d=d, rows=rows, k=k, b_row=b_row, li_row=li_row, b_tot=b_tot, e_max=e_max,
                       b_cb=_col_bcast(b_row, eye), end_cb=_col_bcast(end_row, eye),
                       qk=_dot_nt(q_ref[0, rows, :], k)))
    return st


def _ml_prep_intra(v_ref, st):
    ones = jnp.ones((LANES, LANES), BF16)
    for c in st:
        _, incl, _ = _pair_masks(c["d"])
        c["v_ones"] = jnp.concatenate([v_ref[0, c["rows"], :].astype(BF16), ones], axis=1)
        dlog = jnp.where(incl, c["b_cb"] - c["b_row"] + c["li_row"], -jnp.inf)
        c["rowmax"] = jnp.max(dlog, axis=1, keepdims=True)
        p_hi, p_lo = _split_bf16(jnp.exp(dlog - c["rowmax"]) * c["qk"])
        rhs = jnp.concatenate([c["v_ones"], jnp.concatenate([jnp.zeros_like(ones), ones], axis=1)], axis=0)
        c["pv_ps"] = jnp.dot(jnp.concatenate([p_hi, p_lo], axis=1), rhs, preferred_element_type=F32)


def _ml_prep_state(st):
    row_i = lax.broadcasted_iota(jnp.int32, (LANES, 1), 0)
    lane_sq = lax.broadcasted_iota(jnp.int32, (LANES, LANES), 1)
    out = []
    for c in st:
        kw_t = (c["k"] * jnp.exp(c["end_cb"] - jnp.where(row_i < CHUNK, c["e_max"][0], c["e_max"][1]))).T
        kw_t2 = jnp.concatenate([jnp.where(lane_sq < CHUNK, kw_t, 0.0), jnp.where(lane_sq < CHUNK, 0.0, kw_t)],
                                axis=0)
        d_cn = jnp.dot(kw_t2.astype(BF16), c["v_ones"], preferred_element_type=F32)
        out.append((c["b_cb"], c["rowmax"], c["pv_ps"], d_cn, *c["b_tot"], *c["e_max"]))
    return tuple(out)


def _ml_steps_state(q_ref, cnst, chains):
    st = []
    for d, p, half in chains:
        r0 = pl.ds(pl.multiple_of(p * LANES + half * CHUNK, CHUNK), CHUNK)
        cn = cnst[d]
        st.append((r0, cn, _dot(q_ref[0, r0, :], cn)))
    return st


def _ml_steps_update(pre, acc_refs, cnst, mst, chains, st):
    for (d, p, half), (r0, cn, q_cn) in zip(chains, st):
        b_cb, rowmax, pv_ps, d_cn = pre[d][:4]
        b_tot = pre[d][4 + half]
        e_max = pre[d][6 + half]
        sl = slice(half * CHUNK, (half + 1) * CHUNK)
        m_s = mst[d, 0:1, 0:1]
        inter = b_cb[sl] + m_s
        m_i = jnp.maximum(inter, rowmax[sl])
        w_inter = jnp.exp(inter - m_i)
        s_intra = jnp.exp(rowmax[sl] - m_i)
        num = w_inter * q_cn[:, :HEAD_DIM] + s_intra * pv_ps[sl, :HEAD_DIM]
        den = w_inter * q_cn[:, HEAD_DIM:] + s_intra * pv_ps[sl, HEAD_DIM:]
        acc_refs[d][r0, :] = num / jnp.maximum(jnp.abs(den), jnp.exp(-m_i))
        carry_log = b_tot + m_s
        m_new = jnp.maximum(carry_log, e_max)
        cnst[d] = jnp.exp(carry_log - m_new) * cn + jnp.exp(e_max - m_new) * d_cn[half * LANES:(half + 1) * LANES]
        mst[d] = jnp.broadcast_to(m_new, mst.shape[1:])


def _ml_scan(q_ref, k_ref, v_ref, rows_ref, acc_refs, cnst, mst, n_pairs):
    def chains_of(t):
        return [(0, t), (1, n_pairs - 1 - t), (0, t + 1), (1, n_pairs - 2 - t)]

    def prep(chains):
        st = _ml_prep_cols(q_ref, k_ref, rows_ref, chains)
        _ml_prep_intra(v_ref, st)
        return _ml_prep_state(st)

    def body(i, pre):
        t = 2 * i
        st = _ml_prep_cols(q_ref, k_ref, rows_ref, chains_of(jnp.minimum(t + 2, n_pairs - 2)))
        nxt = None
        for j, tt in enumerate((t, t + 1)):
            pf, pb = tt, n_pairs - 1 - tt
            pre_j = pre[2 * j:2 * j + 2]
            for halves in ([(0, pf, 0), (1, pb, 1)], [(0, pf, 1), (1, pb, 0)]):
                held = _ml_steps_state(q_ref, cnst, halves)
                if j == 0 and halves[0][2] == 0:
                    _ml_prep_intra(v_ref, st)
                elif j == 0:
                    nxt = _ml_prep_state(st)
                _ml_steps_update(pre_j, acc_refs, cnst, mst, halves, held)
        return nxt

    lax.fori_loop(0, n_pairs // 2, body, prep(chains_of(0)))


def _mlstm_kernel(bias_ref, ql, kl, vl, ol, zl, qc, kc, vc, oc, zc, gl0, gl1, gl2, gl3, gc0, gc1, gc2, gc3,
                  nw_ref, *rest, n_heads, ctx_out):
    if ctx_out:
        yl_ref, yc_ref = rest[:2]
        rest = rest[2:]
    else:
        yl_ref, yc_ref = rest[0], None
        rest = rest[1:]
    cnst, mst, rows_l, rows_c, acc_lf, acc_lb, acc_cf, acc_cb = rest
    h = pl.program_id(1)
    cnst[...] = jnp.zeros(cnst.shape, F32)
    mst[...] = jnp.zeros(mst.shape, F32)
    _ml_rows((gc0, gc1, gc2, gc3), bias_ref, h, n_heads, rows_c)
    _ml_scan(qc, kc, vc, rows_c, (acc_cf, acc_cb), cnst, mst, rows_c.shape[1])
    _ml_rows((gl0, gl1, gl2, gl3), bias_ref, h, n_heads, rows_l)
    _ml_scan(ql, kl, vl, rows_l, (acc_lf, acc_lb), cnst, mst, rows_l.shape[1])
    _head_out_tiles((acc_lf, acc_lb), 0, nw_ref, zl, yl_ref, acc_lf.shape[0], og_ref=ol)
    if ctx_out:
        _head_out_tiles((acc_cf, acc_cb), 0, nw_ref, zc, yc_ref, acc_cf.shape[0], og_ref=oc)


def _head_block(l, blk):
    return pl.BlockSpec((1, l, HEAD_DIM), lambda b, h, blk=blk: (b, 0, blk + h))


def _gate_block(n_pairs, j, n_heads):
    return pl.BlockSpec((1, 1, n_pairs, LANES), lambda b, h, j=j: (b, j * n_heads + h, 0, 0))


def _mlstm(u_l, u_c, gt_l, gt_c, bias, nw, lay, ctx_out):
    b, l, _ = u_l.shape
    lc = u_c.shape[1]
    nh = lay["mh"]
    blk = lambda off: off // HEAD_DIM
    offs = [lay["MQ"], lay["MQ"] + lay["DM"], lay["MQ"] + 2 * lay["DM"], lay["MO"], lay["MZ"]]
    in_specs = [pl.BlockSpec(memory_space=pltpu.SMEM)]
    in_specs += [_head_block(l, blk(o)) for o in offs]
    in_specs += [_head_block(lc, blk(o)) for o in offs]
    in_specs += [_gate_block(l // LANES, j, nh) for j in range(4)]
    in_specs += [_gate_block(lc // LANES, j, nh) for j in range(4)]
    in_specs += [pl.BlockSpec((1, HEAD_DIM), lambda b_, h: (0, 0))]
    out_specs = [pl.BlockSpec((1, l, HEAD_DIM), lambda b_, h: (b_, 0, h))]
    out_shape = [jax.ShapeDtypeStruct((b, l, nh * HEAD_DIM), BF16)]
    if ctx_out:
        out_specs.append(pl.BlockSpec((1, lc, HEAD_DIM), lambda b_, h: (b_, 0, h)))
        out_shape.append(jax.ShapeDtypeStruct((b, lc, nh * HEAD_DIM), BF16))
    scratch = [pltpu.VMEM((2, HEAD_DIM, 2 * HEAD_DIM), F32), pltpu.VMEM((2, 8, LANES), F32),
               pltpu.VMEM((4, l // LANES, LANES), F32), pltpu.VMEM((4, lc // LANES, LANES), F32),
               pltpu.VMEM((l, HEAD_DIM), F32), pltpu.VMEM((l, HEAD_DIM), F32),
               pltpu.VMEM((lc, HEAD_DIM), F32), pltpu.VMEM((lc, HEAD_DIM), F32)]
    res = pl.pallas_call(
        functools.partial(_mlstm_kernel, n_heads=nh, ctx_out=ctx_out),
        grid=(b, nh),
        in_specs=in_specs, out_specs=out_specs, out_shape=out_shape,
        scratch_shapes=scratch,
        compiler_params=_cparams(("parallel", "parallel"), 48),
        name="mlstm",
    )(bias.reshape(-1), *([u_l] * 5), *([u_c] * 5), *([gt_l] * 4), *([gt_c] * 4), nw)
    return (res[0], res[1]) if ctx_out else (res[0], None)


def _gdn_rows(g_refs, alog_ref, dtb_ref, h, n_heads, rows_ref, base):
    n = g_refs[0].shape[2]
    g_f = -jnp.exp(alog_ref[h]) * _softplus(g_refs[0][0, 0] + dtb_ref[h])
    g_b = -jnp.exp(alog_ref[n_heads + h]) * _softplus(g_refs[1][0, 0] + dtb_ref[n_heads + h])
    rows_ref[0, base:base + n] = _seg_cumsum(g_f, False)
    rows_ref[1, base:base + n] = jax.nn.sigmoid(g_refs[2][0, 0])
    rows_ref[2, base:base + n] = _seg_cumsum(g_b, True)
    rows_ref[3, base:base + n] = jax.nn.sigmoid(g_refs[3][0, 0])


def _gdn_conv(src_refs, w_refs, dst_refs, base, n_units, unit_rows, period):
    def body(i, carry):
        r = pl.multiple_of(i * unit_rows, unit_rows)
        for idx in range(3):
            t = _silu(_conv_rows(src_refs[idx][0, pl.ds(r, unit_rows), :], w_refs[idx][...], period))
            if idx < 2:
                t = t * lax.rsqrt(jnp.sum(t * t, axis=-1, keepdims=True) + NORM_EPS)
            if idx == 0:
                t = t * (HEAD_DIM ** -0.5)
            dst_refs[idx][pl.ds(base + r, unit_rows), :] = t
        return carry

    lax.fori_loop(0, n_units, body, 0)


N_NEUMANN = CHUNK.bit_length() - 2


def _gdn_prep_start(qs, ks, rows_ref, aqk_s, chains):
    eye = _pair_masks(0)[0]
    eye_f = jnp.where(eye, 1.0, 0.0)
    st = []
    for d, p in chains:
        rows = pl.ds(pl.multiple_of(p * LANES, LANES), LANES)
        k = ks[rows, :]
        _, incl, strict = _pair_masks(d)
        g_row = rows_ref[2 * d, pl.ds(p, 1), :]
        beta_col = _to_col(rows_ref[2 * d + 1, pl.ds(p, 1), :], eye)
        g_col = _to_col(g_row, eye)
        dec = jnp.exp(jnp.where(incl, g_col - g_row, -jnp.inf))
        aqk_s[d, rows, :] = (_dot_nt(qs[rows, :], k) * dec).astype(BF16)
        m_low = jnp.where(strict, beta_col * _dot_nt(k, k) * dec, 0.0)
        st.append(dict(d=d, rows=rows, g_row=g_row, g_col=g_col, beta_col=beta_col, pw=m_low, t_inv=eye_f - m_low))
    return st


def _gdn_prep_neumann(st, after_square=None, after_product=None):
    eye_f = jnp.where(_pair_masks(0)[0], 1.0, 0.0)
    for c in st:
        c["pw"] = _dot3(c["pw"], c["pw"])
    if after_square is not None:
        after_square()
    for c in st:
        c["t_inv"] = _dot3(c["t_inv"], eye_f + c["pw"])
    if after_product is not None:
        after_product()


def _gdn_prep_finish(qs, ks, vs, ut_s, w_s, qd_s, ket_s, st):
    row_i = lax.broadcasted_iota(jnp.int32, (LANES, 1), 0)
    for c in st:
        d, rows, g_row, g_col, beta_col = c["d"], c["rows"], c["g_row"], c["g_col"], c["beta_col"]
        k = ks[rows, :]
        e_g = jnp.exp(g_col)
        sol = _dot3(c["t_inv"], jnp.concatenate([beta_col * vs[rows, :], (beta_col * e_g) * k], axis=1))
        ut_s[d, rows, :] = sol[:, :HEAD_DIM]
        w_s[d, rows, :] = sol[:, HEAD_DIM:].astype(BF16)
        qd_s[d, rows, :] = (qs[rows, :] * e_g).astype(BF16)
        last = CHUNK - 1 if d == 0 else 0
        g_tot_col = jnp.where(row_i < CHUNK, g_row[:, last:last + 1], g_row[:, CHUNK + last:CHUNK + last + 1])
        ket_s[d, rows, :] = (k * jnp.exp(g_tot_col - g_col)).T.astype(BF16)


def _gdn_steps_state(w_s, qd_s, sst, chains):
    st = []
    for d, p, half in chains:
        r0 = pl.ds(pl.multiple_of(p * LANES + half * CHUNK, CHUNK), CHUNK)
        s = sst[d]
        lhs = jnp.concatenate([w_s[d, r0, :], qd_s[d, r0, :]], axis=0)
        st.append((r0, s, jnp.dot(lhs, s.astype(BF16), preferred_element_type=F32)))
    return st


def _gdn_steps_update(rows_ref, ut_s, ket_s, aqk_s, sst, acc_refs, chains, st):
    for (d, p, half), (r0, s, ws_qs) in zip(chains, st):
        rp = pl.ds(pl.multiple_of(p * LANES, LANES), LANES)
        lane = half * CHUNK + (CHUNK - 1 if d == 0 else 0)
        g_tot = rows_ref[2 * d, pl.ds(p, 1), lane:lane + 1]
        u = (ut_s[d, r0, :] - ws_qs[:CHUNK]).astype(BF16)
        zero = jnp.zeros_like(u)
        u_pad = jnp.concatenate([u, zero] if half == 0 else [zero, u], axis=0)
        lhs2 = jnp.concatenate([aqk_s[d, r0, :], ket_s[d, rp, :]], axis=0)
        au_ku = jnp.dot(lhs2, u_pad, preferred_element_type=F32)
        acc_refs[d][r0, :] = ws_qs[CHUNK:] + au_ku[:CHUNK]
        sst[d] = jnp.exp(g_tot) * s + au_ku[CHUNK:]


def _gdn_kernel(alog_ref, dtb_ref, ql, kl, vl, zl, qc, kc, vc, zc, gl0, gl1, gl2, gl3, gc0, gc1, gc2, gc3,
                wq, wk, wv, nw_ref, *rest, n_heads, ctx_out):
    if ctx_out:
        yl_ref, yc_ref = rest[:2]
        rest = rest[2:]
    else:
        yl_ref, yc_ref = rest[0], None
        rest = rest[1:]
    sst, rows, qs, ks, vs, ut_s, w_s, qd_s, ket_s, aqk_s, acc_f, acc_b = rest
    h = pl.program_id(1)
    n_l = ql.shape[1]
    n_c = qc.shape[1]
    nl = n_l // LANES
    nc = n_c // LANES
    sst[...] = jnp.zeros(sst.shape, F32)
    _gdn_rows((gl0, gl1, gl2, gl3), alog_ref, dtb_ref, h, n_heads, rows, 0)
    _gdn_rows((gc0, gc1, gc2, gc3), alog_ref, dtb_ref, h, n_heads, rows, nl)
    conv_rows = _pick_tile(n_l, 4 * CHUNK, CHUNK)
    _gdn_conv((ql, kl, vl), (wq, wk, wv), (qs, ks, vs), 0, n_l // conv_rows, conv_rows, CHUNK)
    _gdn_conv((qc, kc, vc), (wq, wk, wv), (qs, ks, vs), n_l, 1, n_c, n_c)
    prepped = (ut_s, w_s, qd_s, ket_s, aqk_s)
    n_steps = nl + nc

    def fwd_pair(t):
        return jnp.where(t < nc, nl + t, t - nc)

    def bwd_pair(t):
        return n_steps - 1 - t

    def prep(chains, steps=()):
        st = _gdn_prep_start(qs, ks, rows, aqk_s, chains)
        for i in range(N_NEUMANN):
            if i < len(steps):
                held = []
                first = lambda c=steps[i]: held.append(_gdn_steps_state(w_s, qd_s, sst, c))
                second = lambda c=steps[i]: _gdn_steps_update(rows, ut_s, ket_s, aqk_s, sst, (acc_f, acc_b), c,
                                                              held[0])
                _gdn_prep_neumann(st, first, second)
            else:
                _gdn_prep_neumann(st)
        _gdn_prep_finish(qs, ks, vs, ut_s, w_s, qd_s, ket_s, st)

    def chains_of(t):
        return [(0, fwd_pair(t)), (1, bwd_pair(t)), (0, fwd_pair(t + 1)), (1, bwd_pair(t + 1))]

    prep(chains_of(0))

    def step(i, carry):
        t = 2 * i
        halves = []
        for tt in (t, t + 1):
            pf, pb = fwd_pair(tt), bwd_pair(tt)
            halves.append([(0, pf, 0), (1, pb, 1)])
            halves.append([(0, pf, 1), (1, pb, 0)])
        prep(chains_of(jnp.minimum(t + 2, n_steps - 2)), steps=halves)
        return carry

    lax.fori_loop(0, n_steps // 2, step, 0)
    _head_out_tiles((acc_f, acc_b), 0, nw_ref, zl, yl_ref, n_l)
    if ctx_out:
        _head_out_tiles((acc_f, acc_b), n_l, nw_ref, zc, yc_ref, n_c)


def _gdn(u_l, u_c, gt_l, gt_c, conv_w, a_log, dt_bias, nw, lay, ctx_out):
    b, l, _ = u_l.shape
    lc = u_c.shape[1]
    nh = lay["gh"]
    taps = conv_w.shape[0]
    blk = lambda off: off // HEAD_DIM
    offs = [lay["GQ"], lay["GQ"] + lay["DG"], lay["GQ"] + 2 * lay["DG"], lay["GZ"]]
    smem = pl.BlockSpec(memory_space=pltpu.SMEM)
    in_specs = [smem, smem]
    in_specs += [_head_block(l, blk(o)) for o in offs]
    in_specs += [_head_block(lc, blk(o)) for o in offs]
    in_specs += [_gate_block(l // LANES, j, nh) for j in range(4)]
    in_specs += [_gate_block(lc // LANES, j, nh) for j in range(4)]
    in_specs += [pl.BlockSpec((taps, HEAD_DIM), lambda b_, h, j=j: (0, j * nh + h)) for j in range(3)]
    in_specs += [pl.BlockSpec((1, HEAD_DIM), lambda b_, h: (0, 0))]
    out_specs = [pl.BlockSpec((1, l, HEAD_DIM), lambda b_, h: (b_, 0, h))]
    out_shape = [jax.ShapeDtypeStruct((b, l, nh * HEAD_DIM), BF16)]
    if ctx_out:
        out_specs.append(pl.BlockSpec((1, lc, HEAD_DIM), lambda b_, h: (b_, 0, h)))
        out_shape.append(jax.ShapeDtypeStruct((b, lc, nh * HEAD_DIM), BF16))
    lt = l + lc
    seq_f32 = pltpu.VMEM((lt, HEAD_DIM), F32)
    dir_bf16 = pltpu.VMEM((2, lt, HEAD_DIM), BF16)
    scratch = [pltpu.VMEM((2, HEAD_DIM, HEAD_DIM), F32), pltpu.VMEM((4, lt // LANES, LANES), F32),
               seq_f32, seq_f32, seq_f32,
               pltpu.VMEM((2, lt, HEAD_DIM), F32), dir_bf16, dir_bf16, dir_bf16, dir_bf16,
               seq_f32, seq_f32]
    res = pl.pallas_call(
        functools.partial(_gdn_kernel, n_heads=nh, ctx_out=ctx_out),
        grid=(b, nh),
        in_specs=in_specs, out_specs=out_specs, out_shape=out_shape,
        scratch_shapes=scratch,
        compiler_params=_cparams(("parallel", "parallel"), 56),
        name="gdn",
    )(a_log.reshape(-1), dt_bias.reshape(-1), *([u_l] * 4), *([u_c] * 4), *([gt_l] * 4), *([gt_c] * 4),
      *([conv_w] * 3), nw)
    return (res[0], res[1]) if ctx_out else (res[0], None)


def _dft_tables(l):
    k = jnp.arange(l, dtype=jnp.int32)
    ang = lambda t: ((k[:, None] * t[None, :]) % (2 * l)).astype(F32) * (math.pi / l)
    ang_a = ang(jnp.arange(l // CHUNK, dtype=jnp.int32) * CHUNK)[:, :, None]
    ang_b = ang(jnp.arange(CHUNK, dtype=jnp.int32))[:, None, :]
    cos_t = (jnp.cos(ang_a) * jnp.cos(ang_b) - jnp.sin(ang_a) * jnp.sin(ang_b)).reshape(l, l)
    sin_t = (jnp.sin(ang_a) * jnp.cos(ang_b) + jnp.cos(ang_a) * jnp.sin(ang_b)).reshape(l, l)
    alt = jnp.where(k % 2 == 0, 1.0, -1.0).astype(F32)
    sin_f = sin_t.at[0, :].set(alt)
    return cos_t.astype(BF16), sin_f.astype(BF16), sin_f.T.astype(BF16)


def _filter_kernel(feats_ref, featr_ref, w1_ref, b1_ref, fr_ref, w2_ref, b2_ref, w3c_ref, w3a_ref, dl_ref,
                   h2_ref, h2r_ref, hid_ref, hidr_ref):
    @pl.when((pl.program_id(0) == 0) & (pl.program_id(1) == 0))
    def _():
        for f_ref, h_ref in ((feats_ref, hid_ref), (featr_ref, hidr_ref)):
            hid = jnp.sin(fr_ref[...] * (_dot_hi(f_ref[...], w1_ref[...]) + b1_ref[...]))
            h_ref[...] = jnp.sin(fr_ref[...] * (_dot_hi(hid, w2_ref[...]) + b2_ref[...]))

    def raw(h_ref, f_ref, w3_ref):
        return _dot_hi(h_ref[...], w3_ref[...]) * jnp.exp(-f_ref[:, 0:1] * dl_ref[...])

    c_f = raw(hid_ref, feats_ref, w3c_ref)
    a_f = raw(hid_ref, feats_ref, w3a_ref)
    den_c = jnp.sum(jnp.abs(c_f), axis=0, keepdims=True) + NORM_EPS
    den_a = jnp.sum(jnp.abs(a_f), axis=0, keepdims=True) + NORM_EPS
    c_f = c_f / den_c
    a_f = a_f / den_a
    row0 = lax.broadcasted_iota(jnp.int32, c_f.shape, 0) == 0
    centre = c_f[0:1] + a_f[0:1]
    h2_ref[0, 0] = jnp.where(row0, 0.0, raw(hidr_ref, featr_ref, w3a_ref) / den_a)
    h2_ref[0, 1] = jnp.where(row0, centre, c_f)
    h2r_ref[0, 0] = jnp.where(row0, 0.0, raw(hidr_ref, featr_ref, w3c_ref) / den_c)
    h2r_ref[0, 1] = jnp.where(row0, centre, a_f)


def _filter_feats(pos, l, n_emb):
    t = pos / max(l - 1, 1)
    ang = 2.0 * math.pi * pos / l
    bands = jnp.linspace(1e-4, FILTER_BANDS - 1, FILTER_BANDS, dtype=F32)
    feats = jnp.concatenate([t[:, None], jnp.cos(ang[:, None] * bands), -jnp.sin(ang[:, None] * bands)], axis=-1)
    return jnp.pad(feats, ((0, 0), (0, LANES - n_emb)))


def _hy_filters(l, w1, b1, freq, w2, b2, w3, dh):
    n_emb, n_hid = w1.shape
    pos = jnp.arange(l, dtype=F32)
    feats = _filter_feats(pos, l, n_emb)
    feats_r = _filter_feats(l - pos, l, n_emb)
    pc = LANES - n_hid
    w1p = jnp.pad(w1, ((0, LANES - n_emb), (0, pc)))
    w2p = jnp.pad(w2, ((0, pc), (0, pc)))
    w3p = jnp.pad(w3, ((0, pc), (0, 0)))
    row = lambda a: jnp.pad(a, (0, pc))[None]
    deltas = jnp.abs(jnp.linspace(MIN_DECAY, MAX_DECAY, dh, dtype=F32))[None]
    nct = dh // LANES
    const = lambda shape: pl.BlockSpec(shape, lambda o, c: (0, 0))
    o_spec = pl.BlockSpec((1, 2, l, LANES), lambda o, c: (o, 0, 0, c))
    return pl.pallas_call(
        _filter_kernel,
        grid=(HYENA_ORDER, nct),
        in_specs=[const((l, LANES)), const((l, LANES)), const((LANES, LANES)), const((1, LANES)),
                  const((1, LANES)), const((LANES, LANES)), const((1, LANES)),
                  pl.BlockSpec((LANES, LANES), lambda o, c: (0, o * 2 * nct + c)),
                  pl.BlockSpec((LANES, LANES), lambda o, c: (0, o * 2 * nct + nct + c)),
                  pl.BlockSpec((1, LANES), lambda o, c: (0, c))],
        out_specs=[o_spec, o_spec],
        out_shape=[jax.ShapeDtypeStruct((HYENA_ORDER, 2, l, dh), F32)] * 2,
        scratch_shapes=[pltpu.VMEM((l, LANES), F32)] * 2,
        compiler_params=_cparams(("arbitrary", "arbitrary"), 56),
        name="hy_filter",
    )(feats, feats_r, w1p, row(b1), row(freq), w2p, row(b2), w3p, w3p, deltas)


def _spectrum_kernel(c_ref, s_ref, h2_ref, h2r_ref, a_ref, nyq_ref, hsin_ref, *, p):
    pos = h2_ref[0, 0]
    neg = h2r_ref[0, 0]
    is0 = lax.broadcasted_iota(jnp.int32, pos.shape, 0) == 0
    h_sum = jnp.where(is0, pos, pos + neg)
    sign = jnp.where(lax.broadcasted_iota(jnp.int32, pos.shape, 0) % 2 == 0, 1.0, -1.0)
    nyq = jnp.sum(h_sum * sign, axis=0, keepdims=True)
    a = jnp.dot(c_ref[...], h_sum.astype(BF16), preferred_element_type=F32)
    s = jnp.dot(s_ref[...], (pos - neg).astype(BF16), preferred_element_type=F32)
    wk = jnp.where(is0, 0.5 / p, 1.0 / p)
    a_ref[0, 0] = a * wk
    nyq_ref[0, 0] = jnp.broadcast_to((nyq - a[0:1]) * (0.5 / p), nyq_ref.shape[2:])
    hsin_ref[0, 0] = jnp.where(is0, 0.0, s) * wk


def _hy_spectrum(tabs, h2, h2r, p):
    cos_t, sin_f, _ = tabs
    n_ord, _, l, dh = h2.shape
    nb = l // p
    n_win = 2 * nb - 1
    h2 = h2.reshape(n_ord, 2 * nb, p, dh)
    h2r = h2r.reshape(n_ord, 2 * nb, p, dh)
    tab = pl.BlockSpec((p, p), lambda o, m: (0, 0))
    o_spec = pl.BlockSpec((1, 1, p, dh), lambda o, m: (o, m, 0, 0))
    return pl.pallas_call(
        functools.partial(_spectrum_kernel, p=p),
        grid=(n_ord, n_win),
        in_specs=[tab, tab,
                  pl.BlockSpec((1, 1, p, dh), lambda o, m: (o, m + 1, 0, 0)),
                  pl.BlockSpec((1, 1, p, dh), lambda o, m: (o, 2 * nb - 1 - m, 0, 0))],
        out_specs=[o_spec, pl.BlockSpec((1, 1, 8, dh), lambda o, m: (o, m, 0, 0)), o_spec],
        out_shape=[jax.ShapeDtypeStruct((n_ord, n_win, p, dh), F32),
                   jax.ShapeDtypeStruct((n_ord, n_win, 8, dh), F32),
                   jax.ShapeDtypeStruct((n_ord, n_win, p, dh), F32)],
        compiler_params=_cparams(("parallel", "parallel"), 48),
        name="hy_spectrum",
    )(cos_t, sin_f, h2, h2r)


HY_ROWS = 8


def _hy_block_kernel(c_ref, sf_ref, si_ref, a_ref, nyq_ref, hsin_ref, sk_ref, ug_ref, wg_ref, *rest,
                     nb, period, first):
    if first:
        uv_ref, wv_ref, of_ref, ob_ref, xc_s, xs_s, yc_s, ys_s, v_s = rest
    else:
        vb_ref, vp_ref, z_ref, o_ref, xc_s, xs_s, yc_s, ys_s = rest
    p = c_ref.shape[0]
    for j in range(nb):
        rows = slice(j * p, (j + 1) * p)
        if first:
            vj = _conv_rows(uv_ref[0, rows, :], wv_ref[...], period)
            v_s[rows, :] = vj
            xj = vj.astype(BF16)
        else:
            xj = vb_ref[0, rows, :]
        xc_s[j] = jnp.dot(c_ref[...], xj, preferred_element_type=F32)
        xs_s[j] = jnp.dot(sf_ref[...], xj, preferred_element_type=F32)

    nyq_fix = []
    for i in range(nb):
        fix = jnp.zeros((1, LANES), F32)
        for j in range(nb):
            fix = fix + xs_s[j, 0:1, :] * nyq_ref[0, i - j + (nb - 1), 0:1, :]
        nyq_fix.append(fix)

    def rows_body(r, carry):
        rs = pl.ds(pl.multiple_of(r * HY_ROWS, HY_ROWS), HY_ROWS)
        xc = [xc_s[j, rs, :] for j in range(nb)]
        xs = [xs_s[j, rs, :] for j in range(nb)]
        yc = [None] * nb
        ys = [None] * nb
        for w in range(2 * nb - 1):
            a = a_ref[0, w, rs, :]
            hsin = hsin_ref[0, w, rs, :]
            for i in range(nb):
                j = i - (w - (nb - 1))
                if 0 <= j < nb:
                    tc = xc[j] * a - xs[j] * hsin
                    ts = xc[j] * hsin + xs[j] * a
                    yc[i] = tc if yc[i] is None else yc[i] + tc
                    ys[i] = ts if ys[i] is None else ys[i] + ts
        row0 = (lax.broadcasted_iota(jnp.int32, (HY_ROWS, LANES), 0) + r * HY_ROWS) == 0
        for i in range(nb):
            yc_s[i, rs, :] = yc[i]
            ys_s[i, rs, :] = ys[i] + jnp.where(row0, nyq_fix[i], 0.0)
        return carry

    lax.fori_loop(0, p // HY_ROWS, rows_body, 0)
    for i in range(nb):
        rows = slice(i * p, (i + 1) * p)
        conv = jnp.dot(c_ref[...], yc_s[i].astype(BF16), preferred_element_type=F32)
        conv = conv + jnp.dot(si_ref[...], ys_s[i].astype(BF16), preferred_element_type=F32)
        xg = _conv_rows(ug_ref[0, rows, :], wg_ref[...], period)
        if first:
            y = xg * (conv + sk_ref[0] * v_s[rows, :])
            of_ref[0, rows, :] = y
            ob_ref[0, rows, :] = y.astype(BF16)
        else:
            y = xg * (conv + sk_ref[0] * vp_ref[0, rows, :])
            o_ref[0, rows, :] = (y * _silu(z_ref[0, rows, :])).astype(BF16)


def _hy_block(tabs, spec, u, conv_w, skip, order, p, dh, period, y_prev=None):
    b, l, _ = u.shape
    nb = l // p
    n_win = 2 * nb - 1
    nct = dh // LANES
    taps = conv_w.shape[0]
    first = y_prev is None
    tab = pl.BlockSpec((p, p), lambda c, i: (0, 0))
    filt = pl.BlockSpec((1, n_win, p, LANES), lambda c, i: (order, 0, 0, c))
    nyq = pl.BlockSpec((1, n_win, 8, LANES), lambda c, i: (order, 0, 0, c))
    u_cols = lambda k: pl.BlockSpec((1, l, LANES), lambda c, i, k=k: (i, 0, k * nct + c))
    w_cols = lambda k: pl.BlockSpec((taps, LANES), lambda c, i, k=k: (0, k * nct + c))
    col = pl.BlockSpec((1, l, LANES), lambda c, i: (i, 0, c))
    in_specs = [tab, tab, tab, filt, nyq, filt, pl.BlockSpec((1, 1, LANES), lambda c, i: (order, 0, c)),
                u_cols(order), w_cols(order)]
    args = [*tabs, *spec, skip[:, None, :], u, conv_w]
    scratch = [pltpu.VMEM((nb, p, LANES), F32)] * 4
    if first:
        in_specs += [u_cols(2), w_cols(2)]
        args += [u, conv_w]
        out_specs = [col, col]
        out_shape = [jax.ShapeDtypeStruct((b, l, dh), F32), jax.ShapeDtypeStruct((b, l, dh), BF16)]
        scratch.append(pltpu.VMEM((l, LANES), F32))
    else:
        in_specs += [col, col, u_cols(3)]
        args += [y_prev[1], y_prev[0], u]
        out_specs = col
        out_shape = jax.ShapeDtypeStruct((b, l, dh), BF16)
    return pl.pallas_call(
        functools.partial(_hy_block_kernel, nb=nb, period=period, first=first),
        grid=(nct, b),
        in_specs=in_specs, out_specs=out_specs, out_shape=out_shape,
        scratch_shapes=scratch,
        compiler_params=_cparams(("parallel", "parallel"), 56),
        name="hy_block",
    )(*args)


def _hyena(u, conv_w, w1, b1, freq, w2, b2, w3, skip, lay, grid_mask):
    assert HYENA_ORDER == 2
    dh = lay["DH"]
    l = u.shape[1]
    p = min(HY_BLOCK, l)
    period = CHUNK if grid_mask else l
    assert p % period == 0
    tabs = _dft_tables(p)
    spec = _hy_spectrum(tabs, *_hy_filters(l, w1, b1, freq, w2, b2, w3, dh), p)
    y1 = _hy_block(tabs, spec, u, conv_w, skip, 0, p, dh, period)
    return _hy_block(tabs, spec, u, conv_w, skip, 1, p, dh, period, y_prev=y1)


def _layout(d):
    dg, dh, dm = 3 * d // 8, d // 4, 3 * d // 8
    lay = {"DG": dg, "DH": dh, "DM": dm, "gh": dg // HEAD_DIM, "mh": dm // HEAD_DIM}
    lay["HZ"] = 3 * dh
    lay["GQ"] = 4 * dh
    lay["GZ"] = lay["GQ"] + 3 * dg
    lay["GAB"] = lay["GZ"] + dg
    lay["MQ"] = lay["GAB"] + LANES
    lay["MO"] = lay["MQ"] + 3 * dm
    lay["MZ"] = lay["MO"] + dm
    lay["MG"] = lay["MZ"] + dm
    lay["NP"] = lay["MG"] + LANES
    return lay


def _pack_w_in(w, lay):
    dg, dh, dm, gh, mh = lay["DG"], lay["DH"], lay["DM"], lay["gh"], lay["mh"]
    sizes = (3 * dg, dg, 4 * gh, 3 * dh, dh, 3 * dm, dm, dm, 4 * mh)
    offs = [0]
    for s in sizes:
        offs.append(offs[-1] + s)
    seg = [w[:, offs[i]:offs[i + 1]] for i in range(len(sizes))]
    g_qkv, g_z, g_ab, h_p, h_z, m_qkv, m_o, m_z, m_g = seg
    padl = lambda a: jnp.pad(a, ((0, 0), (0, LANES - a.shape[1])))
    return jnp.concatenate([h_p, h_z, g_qkv, g_z, padl(g_ab), m_qkv, m_o, m_z, padl(m_g)], axis=1).astype(BF16)


def _gate_rows(u, off, n):
    b, l, _ = u.shape
    return jnp.transpose(u[:, :, off:off + n], (0, 2, 1)).reshape(b, n, l // LANES, LANES)


def kernel(x, c, ctx, c_ctx, norm_w, mod_w, mod_b, w_in, gdn_conv, gdn_a_log, gdn_dt_bias, gdn_norm, hy_conv,
           hy_w1, hy_b1, hy_freq, hy_w2, hy_b2, hy_w3, hy_skip, ml_gate_bias, ml_norm, w_out, final_norm):
    b, l, d = x.shape
    lc = ctx.shape[1]
    depth = norm_w.shape[0]
    lay = _layout(d)
    dg, dh = lay["DG"], lay["DH"]
    assert b < COND_ROWS and l % (2 * LANES) == 0 and lc % (2 * LANES) == 0 and d % 1024 == 0
    assert l % min(HY_BLOCK, l) == 0
    cond = jnp.zeros((COND_ROWS, d), F32).at[:b].set(c).at[b].set(c_ctx)
    for layer in range(depth):
        last = layer == depth - 1
        mods = _adaln(cond, mod_w[layer], mod_b[layer][None])
        sh, sc, gt = mods[:, :d], mods[:, d:2 * d], mods[:, 2 * d:]
        lat = lambda m: m[:b, None, :]
        cx = lambda m: jnp.broadcast_to(m[b][None, None, :], (b, 1, d))
        wp = _pack_w_in(w_in[layer], lay)
        nw = norm_w[layer][None]
        u_l = _inproj(x, nw, lat(sc), lat(sh), wp)
        u_c = _inproj(ctx, nw, cx(sc), cx(sh), wp)
        g_rows = lambda u, off, n: _gate_rows(u, off, n)
        yg_l, yg_c = _gdn(u_l, u_c, g_rows(u_l, lay["GAB"], 4 * lay["gh"]), g_rows(u_c, lay["GAB"], 4 * lay["gh"]),
                          gdn_conv[layer], gdn_a_log[layer], gdn_dt_bias[layer], gdn_norm[layer][None], lay,
                          not last)
        ym_l, ym_c = _mlstm(u_l, u_c, g_rows(u_l, lay["MG"], 4 * lay["mh"]), g_rows(u_c, lay["MG"], 4 * lay["mh"]),
                            ml_gate_bias[layer], ml_norm[layer][None], lay, not last)
        hy = (hy_conv[layer], hy_w1[layer], hy_b1[layer], hy_freq[layer], hy_w2[layer], hy_b2[layer],
              hy_w3[layer], hy_skip[layer])
        yh_l = _hyena(u_l, *hy, lay, True)
        wo = w_out[layer].astype(BF16)
        wg, wh, wm = wo[:dg], wo[dg:dg + dh], wo[dg + dh:]
        fw = final_norm[None]
        x = _outproj(x, yg_l, yh_l, ym_l, wg, wh, wm, lat(gt), fw, last)
        if not last:
            yh_c = _hyena(u_c, *hy, lay, False)
            ctx = _outproj(ctx, yg_c, yh_c, ym_c, wg, wh, wm, cx(gt), fw, False)
    return x
```

```python
import functools
import math

import jax
import jax.numpy as jnp
from jax import lax
from jax.experimental import pallas as pl
from jax.experimental.pallas import tpu as pltpu

HEAD_DIM = 128
CHUNK = 64
LANES = 128
NORM_EPS = 1e-6
HYENA_ORDER = 2
HY_BLOCK = 512
FILTER_BANDS = 16
DECAY_TARGET = 1e-2
MIN_DECAY = math.log(DECAY_TARGET) / 1.5
MAX_DECAY = math.log(DECAY_TARGET) / 0.3
COND_ROWS = 16

F32 = jnp.float32
BF16 = jnp.bfloat16
HI = lax.Precision.HIGHEST


def _cparams(sem, vmem_mb):
    return pltpu.CompilerParams(dimension_semantics=sem, vmem_limit_bytes=vmem_mb << 20)


def _dot(a, b):
    return jnp.dot(a.astype(BF16), b.astype(BF16), preferred_element_type=F32)


def _dot_hi(a, b):
    return jnp.dot(a, b, precision=HI, preferred_element_type=F32)


def _dot_nt(a, b):
    return lax.dot_general(a.astype(BF16), b.astype(BF16), (((1,), (1,)), ((), ())),
                           preferred_element_type=F32)


def _dot_tn(a, b):
    return lax.dot_general(a.astype(BF16), b.astype(BF16), (((0,), (0,)), ((), ())),
                           preferred_element_type=F32)


def _silu(x):
    return x * jax.nn.sigmoid(x)


def _softplus(x):
    return jnp.maximum(x, 0.0) + jnp.log(1.0 + jnp.exp(-jnp.abs(x)))


def _pick_tile(n, cap, unit):
    t = (min(n, cap) // unit) * unit
    while n % t:
        t -= unit
    return t


def _adaln_kernel(c_ref, w_ref, b_ref, o_ref):
    o_ref[...] = _dot_hi(_silu(c_ref[...]), w_ref[...]) + b_ref[...]


def _adaln(cond, w, b):
    d, n = w.shape
    tn = _pick_tile(n, 768, LANES)
    return pl.pallas_call(
        _adaln_kernel,
        grid=(n // tn,),
        in_specs=[pl.BlockSpec((COND_ROWS, d), lambda j: (0, 0)),
                  pl.BlockSpec((d, tn), lambda j: (0, j)),
                  pl.BlockSpec((1, tn), lambda j: (0, j))],
        out_specs=pl.BlockSpec((COND_ROWS, tn), lambda j: (0, j)),
        out_shape=jax.ShapeDtypeStruct((COND_ROWS, n), F32),
        compiler_params=_cparams(("parallel",), 40),
        name="adaln",
    )(cond, w, b)


def _inproj_kernel(x_ref, nw_ref, sc_ref, sh_ref, w_ref, o_ref, xn_ref):
    @pl.when(pl.program_id(2) == 0)
    def _():
        x = x_ref[0]
        r = lax.rsqrt(jnp.mean(x * x, axis=-1, keepdims=True) + NORM_EPS)
        y = (x * r * nw_ref[...]) * (1.0 + sc_ref[0]) + sh_ref[0]
        xn_ref[...] = y.astype(BF16)

    o_ref[0] = jnp.dot(xn_ref[...], w_ref[...], preferred_element_type=F32)


def _inproj(x, nw, sc, sh, wp):
    b, l, d = x.shape
    n = wp.shape[1]
    tm = _pick_tile(l, 1024, 8)
    tn = _pick_tile(n, 1280, LANES)
    return pl.pallas_call(
        _inproj_kernel,
        grid=(b, l // tm, n // tn),
        in_specs=[pl.BlockSpec((1, tm, d), lambda i, m, j: (i, m, 0)),
                  pl.BlockSpec((1, d), lambda i, m, j: (0, 0)),
                  pl.BlockSpec((1, 1, d), lambda i, m, j: (i, 0, 0)),
                  pl.BlockSpec((1, 1, d), lambda i, m, j: (i, 0, 0)),
                  pl.BlockSpec((d, tn), lambda i, m, j: (0, j))],
        out_specs=pl.BlockSpec((1, tm, tn), lambda i, m, j: (i, m, j)),
        out_shape=jax.ShapeDtypeStruct((b, l, n), F32),
        scratch_shapes=[pltpu.VMEM((tm, d), BF16)],
        compiler_params=_cparams(("parallel", "parallel", "arbitrary"), 56),
        name="inproj",
    )(x, nw, sc, sh, wp)


def _outproj_kernel(x_ref, yg_ref, yh_ref, ym_ref, wg_ref, wh_ref, wm_ref, gt_ref, fw_ref, o_ref, *, final):
    acc = jnp.dot(yg_ref[0], wg_ref[...], preferred_element_type=F32)
    acc = acc + jnp.dot(yh_ref[0], wh_ref[...], preferred_element_type=F32)
    acc = acc + jnp.dot(ym_ref[0], wm_ref[...], preferred_element_type=F32)
    xn = x_ref[0] + gt_ref[0] * acc
    if final:
        r = lax.rsqrt(jnp.mean(xn * xn, axis=-1, keepdims=True) + NORM_EPS)
        xn = xn * r * fw_ref[...]
    o_ref[0] = xn


def _outproj(x, yg, yh, ym, wg, wh, wm, gt, fw, final):
    b, l, d = x.shape
    tm = _pick_tile(l, 512, 8)
    row = lambda w: pl.BlockSpec((1, tm, w), lambda i, m: (i, m, 0))
    full = lambda a: pl.BlockSpec(a.shape, lambda i, m: (0, 0))
    return pl.pallas_call(
        functools.partial(_outproj_kernel, final=final),
        grid=(b, l // tm),
        in_specs=[row(d), row(yg.shape[2]), row(yh.shape[2]), row(ym.shape[2]),
                  full(wg), full(wh), full(wm),
                  pl.BlockSpec((1, 1, d), lambda i, m: (i, 0, 0)),
                  pl.BlockSpec((1, d), lambda i, m: (0, 0))],
        out_specs=row(d),
        out_shape=jax.ShapeDtypeStruct((b, l, d), F32),
        compiler_params=_cparams(("parallel", "parallel"), 48),
        name="outproj",
    )(x, yg, yh, ym, wg, wh, wm, gt, fw)


def _conv_rows(x, w, period):
    rows = x.shape[0]
    taps = w.shape[0]
    pad = taps // 2
    pos = lax.broadcasted_iota(jnp.int32, x.shape, 0) % period
    y = None
    for j in range(taps):
        off = j - pad
        if off == 0:
            term = x * w[j:j + 1]
        else:
            shifted = pltpu.roll(x, (-off) % rows, axis=0)
            ok = (pos >= -off) if off < 0 else (pos < period - off)
            term = jnp.where(ok, shifted, 0.0) * w[j:j + 1]
        y = term if y is None else y + term
    return y


def _seg_cumsum(x, reverse):
    lane = lax.broadcasted_iota(jnp.int32, x.shape, 1) % CHUNK
    s = 1
    while s < CHUNK:
        if reverse:
            shifted = pltpu.roll(x, LANES - s, axis=1)
            ok = lane < CHUNK - s
        else:
            shifted = pltpu.roll(x, s, axis=1)
            ok = lane >= s
        x = x + jnp.where(ok, shifted, 0.0)
        s *= 2
    return x


def _chunk_masks(d):
    ii = lax.broadcasted_iota(jnp.int32, (CHUNK, CHUNK), 0)
    jj = lax.broadcasted_iota(jnp.int32, (CHUNK, CHUNK), 1)
    eye = ii == jj
    if d == 0:
        return eye, jj <= ii, jj < ii
    return eye, jj >= ii, jj > ii


def _pair_masks(d):
    ii = lax.broadcasted_iota(jnp.int32, (LANES, LANES), 0)
    jj = lax.broadcasted_iota(jnp.int32, (LANES, LANES), 1)
    lo = (ii // CHUNK) * CHUNK
    eye = ii == jj
    if d == 0:
        return eye, (jj >= lo) & (jj <= ii), (jj >= lo) & (jj < ii)
    return eye, (jj < lo + CHUNK) & (jj >= ii), (jj < lo + CHUNK) & (jj > ii)


def _to_col(row, eye):
    return jnp.sum(jnp.where(eye, jnp.broadcast_to(row, eye.shape), 0.0), axis=1, keepdims=True)


def _split_bf16(a):
    hi = a.astype(BF16)
    return hi, (a - hi.astype(F32)).astype(BF16)


def _col_bcast(row, eye):
    x = jnp.where(eye, jnp.broadcast_to(row, eye.shape), 0.0)
    hi = x.astype(BF16)
    mid, lo = _split_bf16(x - hi.astype(F32))
    ones = jnp.ones((3 * eye.shape[1], LANES), BF16)
    return jnp.dot(jnp.concatenate([hi, mid, lo], axis=1), ones, preferred_element_type=F32)


def _dot3(a, b):
    a_hi, a_lo = _split_bf16(a)
    b_hi, b_lo = _split_bf16(b)
    return jnp.dot(jnp.concatenate([a_hi, a_hi, a_lo], axis=1), jnp.concatenate([b_hi, b_lo, b_hi], axis=0),
                   preferred_element_type=F32)


def _row_slice(rows_ref, idx, p, half):
    return rows_ref[idx, pl.ds(p, 1), half * CHUNK:(half + 1) * CHUNK]


def _head_out_tiles(acc_refs, base, nw_ref, z_ref, y_ref, n_rows, og_ref=None):
    tile = _pick_tile(n_rows, 256, 8)

    def body(i, carry):
        r = pl.multiple_of(i * tile, tile)
        o = acc_refs[0][pl.ds(base + r, tile), :]
        for acc_ref in acc_refs[1:]:
            o = o + acc_ref[pl.ds(base + r, tile), :]
        if og_ref is not None:
            o = jax.nn.sigmoid(og_ref[0, pl.ds(r, tile), :]) * o
        o = o * lax.rsqrt(jnp.mean(o * o, axis=-1, keepdims=True) + NORM_EPS) * nw_ref[...]
        y_ref[0, pl.ds(r, tile), :] = (o * _silu(z_ref[0, pl.ds(r, tile), :])).astype(y_ref.dtype)
        return carry

    lax.fori_loop(0, n_rows // tile, body, 0)


def _ml_rows(g_refs, bias_ref, h, n_heads, rows_ref):
    li_f = g_refs[0][0, 0] + bias_ref[h]
    lf_f = -_softplus(-(g_refs[1][0, 0] + bias_ref[n_heads + h]))
    li_b = g_refs[2][0, 0] + bias_ref[2 * n_heads + h]
    lf_b = -_softplus(-(g_refs[3][0, 0] + bias_ref[3 * n_heads + h]))
    rows_ref[0] = _seg_cumsum(lf_f, False)
    rows_ref[1] = li_f
    rows_ref[2] = _seg_cumsum(lf_b, True)
    rows_ref[3] = li_b


def _ml_prep_cols(q_ref, k_ref, rows_ref, chains):
    eye = _pair_masks(0)[0]
    lane_i = lax.broadcasted_iota(jnp.int32, (1, LANES), 1)
    st = []
    for d, p in chains:
        rows = pl.ds(pl.multiple_of(p * LANES, LANES), LANES)
        k = k_ref[0, rows, :] * (HEAD_DIM ** -0.5)
        b_row = rows_ref[2 * d, pl.ds(p, 1), :]
        li_row = rows_ref[2 * d + 1, pl.ds(p, 1), :]
        last = CHUNK - 1 if d == 0 else 0
        b_tot = (b_row[:, last:last + 1], b_row[:, CHUNK + last:CHUNK + last + 1])
        end_row = jnp.where(lane_i < CHUNK, b_tot[0], b_tot[1]) - b_row + li_row
        e_max = (jnp.max(end_row[:, :CHUNK], axis=1, keepdims=True),
                 jnp.max(end_row[:, CHUNK:], axis=1, keepdims=True))
        st.append(dict(d=d, rows=rows, k=k, b_row=b_row, li_row=li_row, b_tot=b_tot, e_max=e_max,
                       b_cb=_col_bcast(b_row, eye), end_cb=_col_bcast(end_row, eye),
                       qk=_dot_nt(q_ref[0, rows, :], k)))
    return st


def _ml_prep_intra(v_ref, st):
    ones = jnp.ones((LANES, LANES), BF16)
    for c in st:
        _, incl, _ = _pair_masks(c["d"])
        c["v_ones"] = jnp.concatenate([v_ref[0, c["rows"], :].astype(BF16), ones], axis=1)
        dlog = jnp.where(incl, c["b_cb"] - c["b_row"] + c["li_row"], -jnp.inf)
        c["rowmax"] = jnp.max(dlog, axis=1, keepdims=True)
        p_hi, p_lo = _split_bf16(jnp.exp(dlog - c["rowmax"]) * c["qk"])
        rhs = jnp.concatenate([c["v_ones"], jnp.concatenate([jnp.zeros_like(ones), ones], axis=1)], axis=0)
        c["pv_ps"] = jnp.dot(jnp.concatenate([p_hi, p_lo], axis=1), rhs, preferred_element_type=F32)


def _ml_prep_state(st):
    row_i = lax.broadcasted_iota(jnp.int32, (LANES, 1), 0)
    lane_sq = lax.broadcasted_iota(jnp.int32, (LANES, LANES), 1)
    out = []
    for c in st:
        kw_t = (c["k"] * jnp.exp(c["end_cb"] - jnp.where(row_i < CHUNK, c["e_max"][0], c["e_max"][1]))).T
        kw_t2 = jnp.concatenate([jnp.where(lane_sq < CHUNK, kw_t, 0.0), jnp.where(lane_sq < CHUNK, 0.0, kw_t)],
                                axis=0)
        d_cn = jnp.dot(kw_t2.astype(BF16), c["v_ones"], preferred_element_type=F32)
        out.append((c["b_cb"], c["rowmax"], c["pv_ps"], d_cn, *c["b_tot"], *c["e_max"]))
    return tuple(out)


def _ml_steps_state(q_ref, cnst, chains):
    st = []
    for d, p, half in chains:
        r0 = pl.ds(pl.multiple_of(p * LANES + half * CHUNK, CHUNK), CHUNK)
        cn = cnst[d]
        st.append((r0, cn, _dot(q_ref[0, r0, :], cn)))
    return st


def _ml_steps_update(pre, acc_refs, cnst, mst, chains, st):
    for (d, p, half), (r0, cn, q_cn) in zip(chains, st):
        b_cb, rowmax, pv_ps, d_cn = pre[d][:4]
        b_tot = pre[d][4 + half]
        e_max = pre[d][6 + half]
        sl = slice(half * CHUNK, (half + 1) * CHUNK)
        m_s = mst[d, 0:1, 0:1]
        inter = b_cb[sl] + m_s
        m_i = jnp.maximum(inter, rowmax[sl])
        w_inter = jnp.exp(inter - m_i)
        s_intra = jnp.exp(rowmax[sl] - m_i)
        num = w_inter * q_cn[:, :HEAD_DIM] + s_intra * pv_ps[sl, :HEAD_DIM]
        den = w_inter * q_cn[:, HEAD_DIM:] + s_intra * pv_ps[sl, HEAD_DIM:]
        acc_refs[d][r0, :] = num / jnp.maximum(jnp.abs(den), jnp.exp(-m_i))
        carry_log = b_tot + m_s
        m_new = jnp.maximum(carry_log, e_max)
        cnst[d] = jnp.exp(carry_log - m_new) * cn + jnp.exp(e_max - m_new) * d_cn[half * LANES:(half + 1) * LANES]
        mst[d] = jnp.broadcast_to(m_new, mst.shape[1:])


def _ml_scan(q_ref, k_ref, v_ref, rows_ref, acc_refs, cnst, mst, n_pairs):
    def chains_of(t):
        return [(0, t), (1, n_pairs - 1 - t), (0, t + 1), (1, n_pairs - 2 - t)]

    def prep(chains):
        st = _ml_prep_cols(q_ref, k_ref, rows_ref, chains)
        _ml_prep_intra(v_ref, st)
        return _ml_prep_state(st)

    def body(i, pre):
        t = 2 * i
        st = _ml_prep_cols(q_ref, k_ref, rows_ref, chains_of(jnp.minimum(t + 2, n_pairs - 2)))
        nxt = None
        for j, tt in enumerate((t, t + 1)):
            pf, pb = tt, n_pairs - 1 - tt
            pre_j = pre[2 * j:2 * j + 2]
            for halves in ([(0, pf, 0), (1, pb, 1)], [(0, pf, 1), (1, pb, 0)]):
                held = _ml_steps_state(q_ref, cnst, halves)
                if j == 0 and halves[0][2] == 0:
                    _ml_prep_intra(v_ref, st)
                elif j == 0:
                    nxt = _ml_prep_state(st)
                _ml_steps_update(pre_j, acc_refs, cnst, mst, halves, held)
        return nxt

    lax.fori_loop(0, n_pairs // 2, body, prep(chains_of(0)))


def _mlstm_kernel(bias_ref, ql, kl, vl, ol, zl, qc, kc, vc, oc, zc, gl0, gl1, gl2, gl3, gc0, gc1, gc2, gc3,
                  nw_ref, *rest, n_heads, ctx_out):
    if ctx_out:
        yl_ref, yc_ref = rest[:2]
        rest = rest[2:]
    else:
        yl_ref, yc_ref = rest[0], None
        rest = rest[1:]
    cnst, mst, rows_l, rows_c, acc_lf, acc_lb, acc_cf, acc_cb = rest
    h = pl.program_id(1)
    cnst[...] = jnp.zeros(cnst.shape, F32)
    mst[...] = jnp.zeros(mst.shape, F32)
    _ml_rows((gc0, gc1, gc2, gc3), bias_ref, h, n_heads, rows_c)
    _ml_scan(qc, kc, vc, rows_c, (acc_cf, acc_cb), cnst, mst, rows_c.shape[1])
    _ml_rows((gl0, gl1, gl2, gl3), bias_ref, h, n_heads, rows_l)
    _ml_scan(ql, kl, vl, rows_l, (acc_lf, acc_lb), cnst, mst, rows_l.shape[1])
    _head_out_tiles((acc_lf, acc_lb), 0, nw_ref, zl, yl_ref, acc_lf.shape[0], og_ref=ol)
    if ctx_out:
        _head_out_tiles((acc_cf, acc_cb), 0, nw_ref, zc, yc_ref, acc_cf.shape[0], og_ref=oc)


def _head_block(l, blk):
    return pl.BlockSpec((1, l, HEAD_DIM), lambda b, h, blk=blk: (b, 0, blk + h))


def _gate_block(n_pairs, j, n_heads):
    return pl.BlockSpec((1, 1, n_pairs, LANES), lambda b, h, j=j: (b, j * n_heads + h, 0, 0))


def _mlstm(u_l, u_c, gt_l, gt_c, bias, nw, lay, ctx_out):
    b, l, _ = u_l.shape
    lc = u_c.shape[1]
    nh = lay["mh"]
    blk = lambda off: off // HEAD_DIM
    offs = [lay["MQ"], lay["MQ"] + lay["DM"], lay["MQ"] + 2 * lay["DM"], lay["MO"], lay["MZ"]]
    in_specs = [pl.BlockSpec(memory_space=pltpu.SMEM)]
    in_specs += [_head_block(l, blk(o)) for o in offs]
    in_specs += [_head_block(lc, blk(o)) for o in offs]
    in_specs += [_gate_block(l // LANES, j, nh) for j in range(4)]
    in_specs += [_gate_block(lc // LANES, j, nh) for j in range(4)]
    in_specs += [pl.BlockSpec((1, HEAD_DIM), lambda b_, h: (0, 0))]
    out_specs = [pl.BlockSpec((1, l, HEAD_DIM), lambda b_, h: (b_, 0, h))]
    out_shape = [jax.ShapeDtypeStruct((b, l, nh * HEAD_DIM), BF16)]
    if ctx_out:
        out_specs.append(pl.BlockSpec((1, lc, HEAD_DIM), lambda b_, h: (b_, 0, h)))
        out_shape.append(jax.ShapeDtypeStruct((b, lc, nh * HEAD_DIM), BF16))
    scratch = [pltpu.VMEM((2, HEAD_DIM, 2 * HEAD_DIM), F32), pltpu.VMEM((2, 8, LANES), F32),
               pltpu.VMEM((4, l // LANES, LANES), F32), pltpu.VMEM((4, lc // LANES, LANES), F32),
               pltpu.VMEM((l, HEAD_DIM), F32), pltpu.VMEM((l, HEAD_DIM), F32),
               pltpu.VMEM((lc, HEAD_DIM), F32), pltpu.VMEM((lc, HEAD_DIM), F32)]
    res = pl.pallas_call(
        functools.partial(_mlstm_kernel, n_heads=nh, ctx_out=ctx_out),
        grid=(b, nh),
        in_specs=in_specs, out_specs=out_specs, out_shape=out_shape,
        scratch_shapes=scratch,
        compiler_params=_cparams(("parallel", "parallel"), 48),
        name="mlstm",
    )(bias.reshape(-1), *([u_l] * 5), *([u_c] * 5), *([gt_l] * 4), *([gt_c] * 4), nw)
    return (res[0], res[1]) if ctx_out else (res[0], None)


def _gdn_rows(g_refs, alog_ref, dtb_ref, h, n_heads, rows_ref, base):
    n = g_refs[0].shape[2]
    g_f = -jnp.exp(alog_ref[h]) * _softplus(g_refs[0][0, 0] + dtb_ref[h])
    g_b = -jnp.exp(alog_ref[n_heads + h]) * _softplus(g_refs[1][0, 0] + dtb_ref[n_heads + h])
    rows_ref[0, base:base + n] = _seg_cumsum(g_f, False)
    rows_ref[1, base:base + n] = jax.nn.sigmoid(g_refs[2][0, 0])
    rows_ref[2, base:base + n] = _seg_cumsum(g_b, True)
    rows_ref[3, base:base + n] = jax.nn.sigmoid(g_refs[3][0, 0])


def _gdn_conv(src_refs, w_refs, dst_refs, base, n_units, unit_rows, period):
    def body(i, carry):
        r = pl.multiple_of(i * unit_rows, unit_rows)
        for idx in range(3):
            t = _silu(_conv_rows(src_refs[idx][0, pl.ds(r, unit_rows), :], w_refs[idx][...], period))
            if idx < 2:
                t = t * lax.rsqrt(jnp.sum(t * t, axis=-1, keepdims=True) + NORM_EPS)
            if idx == 0:
                t = t * (HEAD_DIM ** -0.5)
            dst_refs[idx][pl.ds(base + r, unit_rows), :] = t
        return carry

    lax.fori_loop(0, n_units, body, 0)


N_NEUMANN = CHUNK.bit_length() - 2


def _side_masks(d):
    ii = lax.broadcasted_iota(jnp.int32, (CHUNK, LANES), 0)
    lane = lax.broadcasted_iota(jnp.int32, (CHUNK, LANES), 1)
    jj = lane % CHUNK
    first = lane < CHUNK
    if d == 0:
        return first, jj == ii, jj <= ii, jj < ii
    return first, jj == ii, jj >= ii, jj > ii


def _side(x, first):
    return jnp.where(first, x[:CHUNK], x[CHUNK:])


def _block_diag(x, first):
    zero = jnp.zeros_like(x)
    return jnp.concatenate([jnp.where(first, x, zero), jnp.where(first, zero, x)], axis=0)


def _dot3_side(a, b, first):
    a_hi, a_lo = _split_bf16(a)
    b_hi, b_lo = _split_bf16(b)
    rhs = jnp.concatenate([_block_diag(b_hi, first), _block_diag(b_lo, first), _block_diag(b_hi, first)], axis=0)
    return jnp.dot(jnp.concatenate([a_hi, a_hi, a_lo], axis=1), rhs, preferred_element_type=F32)


def _gdn_gates(rows_ref, d, p):
    eye = _pair_masks(0)[0]
    g_row = rows_ref[2 * d, pl.ds(p, 1), :]
    return dict(d=d, p=p, rows=pl.ds(pl.multiple_of(p * LANES, LANES), LANES), g_row=g_row,
                g_col=_to_col(g_row, eye), beta_col=_to_col(rows_ref[2 * d + 1, pl.ds(p, 1), :], eye))


def _gdn_prep_start(qs, ks, rows_ref, chains):
    st = []
    for d, p in chains:
        c = _gdn_gates(rows_ref, d, p)
        rows, g_row, g_col, beta_col = c["rows"], c["g_row"], c["g_col"], c["beta_col"]
        k = ks[rows, :]
        first, eye_s, incl, strict = _side_masks(d)
        dec = jnp.exp(jnp.where(incl, _side(g_col, first) - g_row, -jnp.inf))
        aqk = _block_diag(_side(_dot_nt(qs[rows, :], k), first) * dec, first).astype(BF16)
        m_low = jnp.where(strict, _side(beta_col, first) * _side(_dot_nt(k, k), first) * dec, 0.0)
        st.append(dict(c, aqk=aqk, pw=m_low, t_inv=jnp.where(eye_s, 1.0, 0.0) - m_low))
    return st


def _gdn_prep_neumann(st, after_square=None, after_product=None):
    first, eye_s, _, _ = _side_masks(0)
    eye_f = jnp.where(eye_s, 1.0, 0.0)
    for c in st:
        c["pw"] = _dot3_side(c["pw"], c["pw"], first)
    if after_square is not None:
        after_square()
    for c in st:
        c["t_inv"] = _dot3_side(c["t_inv"], eye_f + c["pw"], first)
    if after_product is not None:
        after_product()


NQ_ROWS = HEAD_DIM + CHUNK


def _gdn_prep_solve(qs, ks, vs, st):
    row_i = lax.broadcasted_iota(jnp.int32, (LANES, 1), 0)
    first = _side_masks(0)[0]
    for c in st:
        d, rows, g_row, g_col, beta_col = c["d"], c["rows"], c["g_row"], c["g_col"], c["beta_col"]
        k = ks[rows, :]
        e_g = jnp.exp(g_col)
        c["w_ut"] = _dot3(_block_diag(c["t_inv"], first),
                          jnp.concatenate([(beta_col * e_g) * k, beta_col * vs[rows, :]], axis=1)).astype(BF16)
        c["q_dec"] = qs[rows, :] * e_g
        last = CHUNK - 1 if d == 0 else 0
        g_tot_col = jnp.where(row_i < CHUNK, g_row[:, last:last + 1], g_row[:, CHUNK + last:CHUNK + last + 1])
        c["k_end_t"] = (k * jnp.exp(g_tot_col - g_col)).T.astype(BF16)


def _gdn_prep_finish(nq_s, r_s, acc_refs, st):
    row_i = lax.broadcasted_iota(jnp.int32, (LANES, 1), 0)
    for c in st:
        d, w_ut, q_dec = c["d"], c["w_ut"], c["q_dec"]
        zero = jnp.zeros_like(w_ut)
        for half in (0, 1):
            sl = slice(half * CHUNK, (half + 1) * CHUNK)
            lhs = jnp.concatenate([c["k_end_t"], c["aqk"][sl]], axis=0)
            mine = (row_i < CHUNK) if half == 0 else (row_i >= CHUNK)
            prod = jnp.dot(lhs, jnp.where(mine, w_ut, zero), preferred_element_type=F32)
            n_q = jnp.concatenate([prod[:HEAD_DIM, :HEAD_DIM], q_dec[sl] - prod[HEAD_DIM:, :HEAD_DIM]], axis=0)
            chunk = 2 * c["p"] + half
            nq_s[d, pl.ds(pl.multiple_of(chunk * NQ_ROWS, NQ_ROWS), NQ_ROWS), :] = n_q.astype(BF16)
            r_s[d, pl.ds(pl.multiple_of(chunk * HEAD_DIM, HEAD_DIM), HEAD_DIM), :] = prod[:HEAD_DIM, HEAD_DIM:]
            acc_refs[d][pl.ds(pl.multiple_of(chunk * CHUNK, CHUNK), CHUNK), :] = prod[HEAD_DIM:, HEAD_DIM:]


def _gdn_steps_state(nq_s, sst, chains):
    st = []
    for d, p, half in chains:
        chunk = 2 * p + half
        s = sst[d]
        n_q = nq_s[d, pl.ds(pl.multiple_of(chunk * NQ_ROWS, NQ_ROWS), NQ_ROWS), :]
        st.append((chunk, s, jnp.dot(n_q, s.astype(BF16), preferred_element_type=F32)))
    return st


def _gdn_steps_update(rows_ref, r_s, sst, acc_refs, chains, st):
    for (d, p, half), (chunk, s, prod) in zip(chains, st):
        lane = half * CHUNK + (CHUNK - 1 if d == 0 else 0)
        g_tot = rows_ref[2 * d, pl.ds(p, 1), lane:lane + 1]
        r0 = pl.ds(pl.multiple_of(chunk * CHUNK, CHUNK), CHUNK)
        acc_refs[d][r0, :] = acc_refs[d][r0, :] + prod[HEAD_DIM:]
        r_c = r_s[d, pl.ds(pl.multiple_of(chunk * HEAD_DIM, HEAD_DIM), HEAD_DIM), :]
        sst[d] = jnp.exp(g_tot) * s - prod[:HEAD_DIM] + r_c


def _gdn_kernel(alog_ref, dtb_ref, ql, kl, vl, zl, qc, kc, vc, zc, gl0, gl1, gl2, gl3, gc0, gc1, gc2, gc3,
                wq, wk, wv, nw_ref, *rest, n_heads, ctx_out):
    if ctx_out:
        yl_ref, yc_ref = rest[:2]
        rest = rest[2:]
    else:
        yl_ref, yc_ref = rest[0], None
        rest = rest[1:]
    sst, rows, qs, ks, vs, nq_s, r_s, acc_f, acc_b, nm_s, aq_s = rest
    h = pl.program_id(1)
    n_l = ql.shape[1]
    n_c = qc.shape[1]
    nl = n_l // LANES
    nc = n_c // LANES
    sst[...] = jnp.zeros(sst.shape, F32)
    _gdn_rows((gl0, gl1, gl2, gl3), alog_ref, dtb_ref, h, n_heads, rows, 0)
    _gdn_rows((gc0, gc1, gc2, gc3), alog_ref, dtb_ref, h, n_heads, rows, nl)
    conv_rows = _pick_tile(n_l, 4 * CHUNK, CHUNK)
    _gdn_conv((ql, kl, vl), (wq, wk, wv), (qs, ks, vs), 0, n_l // conv_rows, conv_rows, CHUNK)
    _gdn_conv((qc, kc, vc), (wq, wk, wv), (qs, ks, vs), n_l, 1, n_c, n_c)
    n_steps = nl + nc
    accs = (acc_f, acc_b)

    def fwd_pair(t):
        return jnp.where(t < nc, nl + t, t - nc)

    def bwd_pair(t):
        return n_steps - 1 - t

    n_trips = n_steps // 2
    n_early = (N_NEUMANN + 1) // 2

    def chains_of(trip):
        t = 2 * jnp.minimum(trip, n_trips - 1)
        return [(0, fwd_pair(t)), (1, bwd_pair(t)), (0, fwd_pair(t + 1)), (1, bwd_pair(t + 1))]

    def neumann(st, half=None):
        if half is None:
            _gdn_prep_neumann(st)
            return
        held = []
        _gdn_prep_neumann(st, lambda: held.append(_gdn_steps_state(nq_s, sst, half)),
                          lambda: _gdn_steps_update(rows, r_s, sst, accs, half, held[0]))

    def finish(st):
        _gdn_prep_solve(qs, ks, vs, st)
        _gdn_prep_finish(nq_s, r_s, accs, st)

    def halves_of(trip):
        out = []
        for t in (2 * trip, 2 * trip + 1):
            pf, pb = fwd_pair(t), bwd_pair(t)
            out += [[(0, pf, 0), (1, pb, 1)], [(0, pf, 1), (1, pb, 0)]]
        return out

    st = _gdn_prep_start(qs, ks, rows, chains_of(0))
    for _ in range(N_NEUMANN):
        neumann(st)
    finish(st)
    st = _gdn_prep_start(qs, ks, rows, chains_of(1))
    for _ in range(n_early):
        neumann(st)

    def park(st_part):
        for n, c in enumerate(st_part):
            nm_s[2 * n] = c["pw"]
            nm_s[2 * n + 1] = c["t_inv"]
            aq_s[n] = c["aqk"]

    park(st)

    def trip(i, carry):
        halves = halves_of(i)
        st_late = [dict(_gdn_gates(rows, d, p), pw=nm_s[2 * n], t_inv=nm_s[2 * n + 1], aqk=aq_s[n])
                   for n, (d, p) in enumerate(chains_of(i + 1))]
        st_early = _gdn_prep_start(qs, ks, rows, chains_of(i + 2))
        n_late = N_NEUMANN - n_early
        slot = 0
        for n in range(n_early):
            neumann(st_early, halves[slot] if slot < len(halves) else None)
            slot += 1
            if n < n_late:
                neumann(st_late, halves[slot] if slot < len(halves) else None)
                slot += 1
            if n == n_late - 1:
                _gdn_prep_solve(qs, ks, vs, st_late)
        _gdn_prep_finish(nq_s, r_s, accs, st_late)
        park(st_early)
        return carry

    lax.fori_loop(0, n_trips - 1, trip, 0)
    for half in halves_of(n_trips - 1):
        _gdn_steps_update(rows, r_s, sst, accs, half, _gdn_steps_state(nq_s, sst, half))
    _head_out_tiles((acc_f, acc_b), 0, nw_ref, zl, yl_ref, n_l)
    if ctx_out:
        _head_out_tiles((acc_f, acc_b), n_l, nw_ref, zc, yc_ref, n_c)


def _gdn(u_l, u_c, gt_l, gt_c, conv_w, a_log, dt_bias, nw, lay, ctx_out):
    b, l, _ = u_l.shape
    lc = u_c.shape[1]
    nh = lay["gh"]
    taps = conv_w.shape[0]
    blk = lambda off: off // HEAD_DIM
    offs = [lay["GQ"], lay["GQ"] + lay["DG"], lay["GQ"] + 2 * lay["DG"], lay["GZ"]]
    smem = pl.BlockSpec(memory_space=pltpu.SMEM)
    in_specs = [smem, smem]
    in_specs += [_head_block(l, blk(o)) for o in offs]
    in_specs += [_head_block(lc, blk(o)) for o in offs]
    in_specs += [_gate_block(l // LANES, j, nh) for j in range(4)]
    in_specs += [_gate_block(lc // LANES, j, nh) for j in range(4)]
    in_specs += [pl.BlockSpec((taps, HEAD_DIM), lambda b_, h, j=j: (0, j * nh + h)) for j in range(3)]
    in_specs += [pl.BlockSpec((1, HEAD_DIM), lambda b_, h: (0, 0))]
    out_specs = [pl.BlockSpec((1, l, HEAD_DIM), lambda b_, h: (b_, 0, h))]
    out_shape = [jax.ShapeDtypeStruct((b, l, nh * HEAD_DIM), BF16)]
    if ctx_out:
        out_specs.append(pl.BlockSpec((1, lc, HEAD_DIM), lambda b_, h: (b_, 0, h)))
        out_shape.append(jax.ShapeDtypeStruct((b, lc, nh * HEAD_DIM), BF16))
    lt = l + lc
    seq_f32 = pltpu.VMEM((lt, HEAD_DIM), F32)
    n_chunks = lt // CHUNK
    scratch = [pltpu.VMEM((2, HEAD_DIM, HEAD_DIM), F32), pltpu.VMEM((4, lt // LANES, LANES), F32),
               seq_f32, seq_f32, seq_f32,
               pltpu.VMEM((2, n_chunks * NQ_ROWS, HEAD_DIM), BF16), pltpu.VMEM((2, n_chunks * HEAD_DIM, HEAD_DIM), F32),
               seq_f32, seq_f32, pltpu.VMEM((8, CHUNK, LANES), F32), pltpu.VMEM((4, LANES, LANES), BF16)]
    res = pl.pallas_call(
        functools.partial(_gdn_kernel, n_heads=nh, ctx_out=ctx_out),
        grid=(b, nh),
        in_specs=in_specs, out_specs=out_specs, out_shape=out_shape,
        scratch_shapes=scratch,
        compiler_params=_cparams(("parallel", "parallel"), 56),
        name="gdn",
    )(a_log.reshape(-1), dt_bias.reshape(-1), *([u_l] * 4), *([u_c] * 4), *([gt_l] * 4), *([gt_c] * 4),
      *([conv_w] * 3), nw)
    return (res[0], res[1]) if ctx_out else (res[0], None)


def _dft_tables(l):
    k = jnp.arange(l, dtype=jnp.int32)
    ang = lambda t: ((k[:, None] * t[None, :]) % (2 * l)).astype(F32) * (math.pi / l)
    ang_a = ang(jnp.arange(l // CHUNK, dtype=jnp.int32) * CHUNK)[:, :, None]
    ang_b = ang(jnp.arange(CHUNK, dtype=jnp.int32))[:, None, :]
    cos_t = (jnp.cos(ang_a) * jnp.cos(ang_b) - jnp.sin(ang_a) * jnp.sin(ang_b)).reshape(l, l)
    sin_t = (jnp.sin(ang_a) * jnp.cos(ang_b) + jnp.cos(ang_a) * jnp.sin(ang_b)).reshape(l, l)
    alt = jnp.where(k % 2 == 0, 1.0, -1.0).astype(F32)
    sin_f = sin_t.at[0, :].set(alt)
    return cos_t.astype(BF16), sin_f.astype(BF16), sin_f.T.astype(BF16)


def _filter_kernel(feats_ref, featr_ref, w1_ref, b1_ref, fr_ref, w2_ref, b2_ref, w3c_ref, w3a_ref, dl_ref,
                   h2_ref, h2r_ref, hid_ref, hidr_ref):
    @pl.when((pl.program_id(0) == 0) & (pl.program_id(1) == 0))
    def _():
        for f_ref, h_ref in ((feats_ref, hid_ref), (featr_ref, hidr_ref)):
            hid = jnp.sin(fr_ref[...] * (_dot_hi(f_ref[...], w1_ref[...]) + b1_ref[...]))
            h_ref[...] = jnp.sin(fr_ref[...] * (_dot_hi(hid, w2_ref[...]) + b2_ref[...]))

    def raw(h_ref, f_ref, w3_ref):
        return _dot3(h_ref[...], w3_ref[...]) * jnp.exp(-f_ref[:, 0:1] * dl_ref[...])

    c_f = raw(hid_ref, feats_ref, w3c_ref)
    a_f = raw(hid_ref, feats_ref, w3a_ref)
    den_c = jnp.sum(jnp.abs(c_f), axis=0, keepdims=True) + NORM_EPS
    den_a = jnp.sum(jnp.abs(a_f), axis=0, keepdims=True) + NORM_EPS
    c_f = c_f / den_c
    a_f = a_f / den_a
    row0 = lax.broadcasted_iota(jnp.int32, c_f.shape, 0) == 0
    centre = c_f[0:1] + a_f[0:1]
    h2_ref[0, 0] = jnp.where(row0, 0.0, raw(hidr_ref, featr_ref, w3a_ref) / den_a)
    h2_ref[0, 1] = jnp.where(row0, centre, c_f)
    h2r_ref[0, 0] = jnp.where(row0, 0.0, raw(hidr_ref, featr_ref, w3c_ref) / den_c)
    h2r_ref[0, 1] = jnp.where(row0, centre, a_f)


def _filter_feats(pos, l, n_emb):
    t = pos / max(l - 1, 1)
    ang = 2.0 * math.pi * pos / l
    bands = jnp.linspace(1e-4, FILTER_BANDS - 1, FILTER_BANDS, dtype=F32)
    feats = jnp.concatenate([t[:, None], jnp.cos(ang[:, None] * bands), -jnp.sin(ang[:, None] * bands)], axis=-1)
    return jnp.pad(feats, ((0, 0), (0, LANES - n_emb)))


def _hy_filters(l, w1, b1, freq, w2, b2, w3, dh):
    n_emb, n_hid = w1.shape
    pos = jnp.arange(l, dtype=F32)
    feats = _filter_feats(pos, l, n_emb)
    feats_r = _filter_feats(l - pos, l, n_emb)
    pc = LANES - n_hid
    w1p = jnp.pad(w1, ((0, LANES - n_emb), (0, pc)))
    w2p = jnp.pad(w2, ((0, pc), (0, pc)))
    w3p = jnp.pad(w3, ((0, pc), (0, 0)))
    row = lambda a: jnp.pad(a, (0, pc))[None]
    deltas = jnp.abs(jnp.linspace(MIN_DECAY, MAX_DECAY, dh, dtype=F32))[None]
    nct = dh // LANES
    const = lambda shape: pl.BlockSpec(shape, lambda o, c: (0, 0))
    o_spec = pl.BlockSpec((1, 2, l, LANES), lambda o, c: (o, 0, 0, c))
    return pl.pallas_call(
        _filter_kernel,
        grid=(HYENA_ORDER, nct),
        in_specs=[const((l, LANES)), const((l, LANES)), const((LANES, LANES)), const((1, LANES)),
                  const((1, LANES)), const((LANES, LANES)), const((1, LANES)),
                  pl.BlockSpec((LANES, LANES), lambda o, c: (0, o * 2 * nct + c)),
                  pl.BlockSpec((LANES, LANES), lambda o, c: (0, o * 2 * nct + nct + c)),
                  pl.BlockSpec((1, LANES), lambda o, c: (0, c))],
        out_specs=[o_spec, o_spec],
        out_shape=[jax.ShapeDtypeStruct((HYENA_ORDER, 2, l, dh), F32)] * 2,
        scratch_shapes=[pltpu.VMEM((l, LANES), F32)] * 2,
        compiler_params=_cparams(("arbitrary", "arbitrary"), 56),
        name="hy_filter",
    )(feats, feats_r, w1p, row(b1), row(freq), w2p, row(b2), w3p, w3p, deltas)


def _spectrum_kernel(c_ref, s_ref, h2_ref, h2r_ref, a_ref, nyq_ref, hsin_ref, *, p):
    pos = h2_ref[0, 0]
    neg = h2r_ref[0, 0]
    is0 = lax.broadcasted_iota(jnp.int32, pos.shape, 0) == 0
    h_sum = jnp.where(is0, pos, pos + neg)
    sign = jnp.where(lax.broadcasted_iota(jnp.int32, pos.shape, 0) % 2 == 0, 1.0, -1.0)
    nyq = jnp.sum(h_sum * sign, axis=0, keepdims=True)
    a = jnp.dot(c_ref[...], h_sum.astype(BF16), preferred_element_type=F32)
    s = jnp.dot(s_ref[...], (pos - neg).astype(BF16), preferred_element_type=F32)
    wk = jnp.where(is0, 0.5 / p, 1.0 / p)
    a_ref[0, 0] = a * wk
    nyq_ref[0, 0] = jnp.broadcast_to((nyq - a[0:1]) * (0.5 / p), nyq_ref.shape[2:])
    hsin_ref[0, 0] = jnp.where(is0, 0.0, s) * wk


def _hy_spectrum(tabs, h2, h2r, p):
    cos_t, sin_f, _ = tabs
    n_ord, _, l, dh = h2.shape
    nb = l // p
    n_win = 2 * nb - 1
    h2 = h2.reshape(n_ord, 2 * nb, p, dh)
    h2r = h2r.reshape(n_ord, 2 * nb, p, dh)
    tab = pl.BlockSpec((p, p), lambda o, m: (0, 0))
    o_spec = pl.BlockSpec((1, 1, p, dh), lambda o, m: (o, m, 0, 0))
    return pl.pallas_call(
        functools.partial(_spectrum_kernel, p=p),
        grid=(n_ord, n_win),
        in_specs=[tab, tab,
                  pl.BlockSpec((1, 1, p, dh), lambda o, m: (o, m + 1, 0, 0)),
                  pl.BlockSpec((1, 1, p, dh), lambda o, m: (o, 2 * nb - 1 - m, 0, 0))],
        out_specs=[o_spec, pl.BlockSpec((1, 1, 8, dh), lambda o, m: (o, m, 0, 0)), o_spec],
        out_shape=[jax.ShapeDtypeStruct((n_ord, n_win, p, dh), F32),
                   jax.ShapeDtypeStruct((n_ord, n_win, 8, dh), F32),
                   jax.ShapeDtypeStruct((n_ord, n_win, p, dh), F32)],
        compiler_params=_cparams(("parallel", "parallel"), 48),
        name="hy_spectrum",
    )(cos_t, sin_f, h2, h2r)


HY_ROWS = 8


def _hy_block_kernel(c_ref, sf_ref, si_ref, a_ref, nyq_ref, hsin_ref, sk_ref, ug_ref, wg_ref, *rest,
                     nb, period, first):
    if first:
        uv_ref, wv_ref, of_ref, ob_ref, xc_s, xs_s, yc_s, ys_s, v_s = rest
    else:
        vb_ref, vp_ref, z_ref, o_ref, xc_s, xs_s, yc_s, ys_s = rest
    p = c_ref.shape[0]
    for j in range(nb):
        rows = slice(j * p, (j + 1) * p)
        if first:
            vj = _conv_rows(uv_ref[0, rows, :], wv_ref[...], period)
            v_s[rows, :] = vj
            xj = vj.astype(BF16)
        else:
            xj = vb_ref[0, rows, :]
        xc_s[j] = jnp.dot(c_ref[...], xj, preferred_element_type=F32)
        xs_s[j] = jnp.dot(sf_ref[...], xj, preferred_element_type=F32)

    nyq_fix = []
    for i in range(nb):
        fix = jnp.zeros((1, LANES), F32)
        for j in range(nb):
            fix = fix + xs_s[j, 0:1, :] * nyq_ref[0, i - j + (nb - 1), 0:1, :]
        nyq_fix.append(fix)

    def rows_body(r, carry):
        rs = pl.ds(pl.multiple_of(r * HY_ROWS, HY_ROWS), HY_ROWS)
        xc = [xc_s[j, rs, :] for j in range(nb)]
        xs = [xs_s[j, rs, :] for j in range(nb)]
        yc = [None] * nb
        ys = [None] * nb
        for w in range(2 * nb - 1):
            a = a_ref[0, w, rs, :]
            hsin = hsin_ref[0, w, rs, :]
            for i in range(nb):
                j = i - (w - (nb - 1))
                if 0 <= j < nb:
                    tc = xc[j] * a - xs[j] * hsin
                    ts = xc[j] * hsin + xs[j] * a
                    yc[i] = tc if yc[i] is None else yc[i] + tc
                    ys[i] = ts if ys[i] is None else ys[i] + ts
        row0 = (lax.broadcasted_iota(jnp.int32, (HY_ROWS, LANES), 0) + r * HY_ROWS) == 0
        for i in range(nb):
            yc_s[i, rs, :] = yc[i]
            ys_s[i, rs, :] = ys[i] + jnp.where(row0, nyq_fix[i], 0.0)
        return carry

    lax.fori_loop(0, p // HY_ROWS, rows_body, 0)
    for i in range(nb):
        rows = slice(i * p, (i + 1) * p)
        conv = jnp.dot(c_ref[...], yc_s[i].astype(BF16), preferred_element_type=F32)
        conv = conv + jnp.dot(si_ref[...], ys_s[i].astype(BF16), preferred_element_type=F32)
        xg = _conv_rows(ug_ref[0, rows, :], wg_ref[...], period)
        if first:
            y = xg * (conv + sk_ref[0] * v_s[rows, :])
            of_ref[0, rows, :] = y
            ob_ref[0, rows, :] = y.astype(BF16)
        else:
            y = xg * (conv + sk_ref[0] * vp_ref[0, rows, :])
            o_ref[0, rows, :] = (y * _silu(z_ref[0, rows, :])).astype(BF16)


def _hy_block(tabs, spec, u, conv_w, skip, order, p, dh, period, y_prev=None):
    b, l, _ = u.shape
    nb = l // p
    n_win = 2 * nb - 1
    nct = dh // LANES
    taps = conv_w.shape[0]
    first = y_prev is None
    tab = pl.BlockSpec((p, p), lambda c, i: (0, 0))
    filt = pl.BlockSpec((1, n_win, p, LANES), lambda c, i: (order, 0, 0, c))
    nyq = pl.BlockSpec((1, n_win, 8, LANES), lambda c, i: (order, 0, 0, c))
    u_cols = lambda k: pl.BlockSpec((1, l, LANES), lambda c, i, k=k: (i, 0, k * nct + c))
    w_cols = lambda k: pl.BlockSpec((taps, LANES), lambda c, i, k=k: (0, k * nct + c))
    col = pl.BlockSpec((1, l, LANES), lambda c, i: (i, 0, c))
    in_specs = [tab, tab, tab, filt, nyq, filt, pl.BlockSpec((1, 1, LANES), lambda c, i: (order, 0, c)),
                u_cols(order), w_cols(order)]
    args = [*tabs, *spec, skip[:, None, :], u, conv_w]
    scratch = [pltpu.VMEM((nb, p, LANES), F32)] * 4
    if first:
        in_specs += [u_cols(2), w_cols(2)]
        args += [u, conv_w]
        out_specs = [col, col]
        out_shape = [jax.ShapeDtypeStruct((b, l, dh), F32), jax.ShapeDtypeStruct((b, l, dh), BF16)]
        scratch.append(pltpu.VMEM((l, LANES), F32))
    else:
        in_specs += [col, col, u_cols(3)]
        args += [y_prev[1], y_prev[0], u]
        out_specs = col
        out_shape = jax.ShapeDtypeStruct((b, l, dh), BF16)
    return pl.pallas_call(
        functools.partial(_hy_block_kernel, nb=nb, period=period, first=first),
        grid=(nct, b),
        in_specs=in_specs, out_specs=out_specs, out_shape=out_shape,
        scratch_shapes=scratch,
        compiler_params=_cparams(("parallel", "parallel"), 56),
        name="hy_block",
    )(*args)


def _hyena(u, conv_w, w1, b1, freq, w2, b2, w3, skip, lay, grid_mask):
    assert HYENA_ORDER == 2
    dh = lay["DH"]
    l = u.shape[1]
    p = min(HY_BLOCK, l)
    period = CHUNK if grid_mask else l
    assert p % period == 0
    tabs = _dft_tables(p)
    spec = _hy_spectrum(tabs, *_hy_filters(l, w1, b1, freq, w2, b2, w3, dh), p)
    y1 = _hy_block(tabs, spec, u, conv_w, skip, 0, p, dh, period)
    return _hy_block(tabs, spec, u, conv_w, skip, 1, p, dh, period, y_prev=y1)


def _layout(d):
    dg, dh, dm = 3 * d // 8, d // 4, 3 * d // 8
    lay = {"DG": dg, "DH": dh, "DM": dm, "gh": dg // HEAD_DIM, "mh": dm // HEAD_DIM}
    lay["HZ"] = 3 * dh
    lay["GQ"] = 4 * dh
    lay["GZ"] = lay["GQ"] + 3 * dg
    lay["GAB"] = lay["GZ"] + dg
    lay["MQ"] = lay["GAB"] + LANES
    lay["MO"] = lay["MQ"] + 3 * dm
    lay["MZ"] = lay["MO"] + dm
    lay["MG"] = lay["MZ"] + dm
    lay["NP"] = lay["MG"] + LANES
    return lay


def _pack_w_in(w, lay):
    dg, dh, dm, gh, mh = lay["DG"], lay["DH"], lay["DM"], lay["gh"], lay["mh"]
    sizes = (3 * dg, dg, 4 * gh, 3 * dh, dh, 3 * dm, dm, dm, 4 * mh)
    offs = [0]
    for s in sizes:
        offs.append(offs[-1] + s)
    seg = [w[:, offs[i]:offs[i + 1]] for i in range(len(sizes))]
    g_qkv, g_z, g_ab, h_p, h_z, m_qkv, m_o, m_z, m_g = seg
    padl = lambda a: jnp.pad(a, ((0, 0), (0, LANES - a.shape[1])))
    return jnp.concatenate([h_p, h_z, g_qkv, g_z, padl(g_ab), m_qkv, m_o, m_z, padl(m_g)], axis=1).astype(BF16)


def _gate_rows(u, off, n):
    b, l, _ = u.shape
    return jnp.transpose(u[:, :, off:off + n], (0, 2, 1)).reshape(b, n, l // LANES, LANES)


def kernel(x, c, ctx, c_ctx, norm_w, mod_w, mod_b, w_in, gdn_conv, gdn_a_log, gdn_dt_bias, gdn_norm, hy_conv,
           hy_w1, hy_b1, hy_freq, hy_w2, hy_b2, hy_w3, hy_skip, ml_gate_bias, ml_norm, w_out, final_norm):
    b, l, d = x.shape
    lc = ctx.shape[1]
    depth = norm_w.shape[0]
    lay = _layout(d)
    dg, dh = lay["DG"], lay["DH"]
    assert b < COND_ROWS and l % (2 * LANES) == 0 and lc % (2 * LANES) == 0 and d % 1024 == 0
    assert l % min(HY_BLOCK, l) == 0
    cond = jnp.zeros((COND_ROWS, d), F32).at[:b].set(c).at[b].set(c_ctx)
    for layer in range(depth):
        last = layer == depth - 1
        mods = _adaln(cond, mod_w[layer], mod_b[layer][None])
        sh, sc, gt = mods[:, :d], mods[:, d:2 * d], mods[:, 2 * d:]
        lat = lambda m: m[:b, None, :]
        cx = lambda m: jnp.broadcast_to(m[b][None, None, :], (b, 1, d))
        wp = _pack_w_in(w_in[layer], lay)
        nw = norm_w[layer][None]
        u_l = _inproj(x, nw, lat(sc), lat(sh), wp)
        u_c = _inproj(ctx, nw, cx(sc), cx(sh), wp)
        g_rows = lambda u, off, n: _gate_rows(u, off, n)
        yg_l, yg_c = _gdn(u_l, u_c, g_rows(u_l, lay["GAB"], 4 * lay["gh"]), g_rows(u_c, lay["GAB"], 4 * lay["gh"]),
                          gdn_conv[layer], gdn_a_log[layer], gdn_dt_bias[layer], gdn_norm[layer][None], lay,
                          not last)
        ym_l, ym_c = _mlstm(u_l, u_c, g_rows(u_l, lay["MG"], 4 * lay["mh"]), g_rows(u_c, lay["MG"], 4 * lay["mh"]),
                            ml_gate_bias[layer], ml_norm[layer][None], lay, not last)
        hy = (hy_conv[layer], hy_w1[layer], hy_b1[layer], hy_freq[layer], hy_w2[layer], hy_b2[layer],
              hy_w3[layer], hy_skip[layer])
        yh_l = _hyena(u_l, *hy, lay, True)
        wo = w_out[layer].astype(BF16)
        wg, wh, wm = wo[:dg], wo[dg:dg + dh], wo[dg + dh:]
        fw = final_norm[None]
        x = _outproj(x, yg_l, yh_l, ym_l, wg, wh, wm, lat(gt), fw, last)
        if not last:
            yh_c = _hyena(u_c, *hy, lay, False)
            ctx = _outproj(ctx, yg_c, yh_c, ym_c, wg, wh, wm, cx(gt), fw, False)
    return x
```

```python
import functools
import math

import jax
import jax.numpy as jnp
from jax import lax
from jax.experimental import pallas as pl
from jax.experimental.pallas import tpu as pltpu

HEAD_DIM = 128
CHUNK = 64
LANES = 128
NORM_EPS = 1e-6
HYENA_ORDER = 2
HY_BLOCK = 512
FILTER_BANDS = 16
DECAY_TARGET = 1e-2
MIN_DECAY = math.log(DECAY_TARGET) / 1.5
MAX_DECAY = math.log(DECAY_TARGET) / 0.3
COND_ROWS = 16

F32 = jnp.float32
BF16 = jnp.bfloat16
HI = lax.Precision.HIGHEST


def _cparams(sem, vmem_mb):
    return pltpu.CompilerParams(dimension_semantics=sem, vmem_limit_bytes=vmem_mb << 20)


def _dot(a, b):
    return jnp.dot(a.astype(BF16), b.astype(BF16), preferred_element_type=F32)


def _dot_hi(a, b):
    return jnp.dot(a, b, precision=HI, preferred_element_type=F32)


def _dot_nt(a, b):
    return lax.dot_general(a.astype(BF16), b.astype(BF16), (((1,), (1,)), ((), ())),
                           preferred_element_type=F32)


def _dot_tn(a, b):
    return lax.dot_general(a.astype(BF16), b.astype(BF16), (((0,), (0,)), ((), ())),
                           preferred_element_type=F32)


def _silu(x):
    return x * jax.nn.sigmoid(x)


def _softplus(x):
    return jnp.maximum(x, 0.0) + jnp.log(1.0 + jnp.exp(-jnp.abs(x)))


def _pick_tile(n, cap, unit):
    t = (min(n, cap) // unit) * unit
    while n % t:
        t -= unit
    return t


def _adaln_kernel(c_ref, w_ref, b_ref, o_ref):
    o_ref[...] = _dot_hi(_silu(c_ref[...]), w_ref[...]) + b_ref[...]


def _adaln(cond, w, b):
    d, n = w.shape
    tn = _pick_tile(n, 768, LANES)
    return pl.pallas_call(
        _adaln_kernel,
        grid=(n // tn,),
        in_specs=[pl.BlockSpec((COND_ROWS, d), lambda j: (0, 0)),
                  pl.BlockSpec((d, tn), lambda j: (0, j)),
                  pl.BlockSpec((1, tn), lambda j: (0, j))],
        out_specs=pl.BlockSpec((COND_ROWS, tn), lambda j: (0, j)),
        out_shape=jax.ShapeDtypeStruct((COND_ROWS, n), F32),
        compiler_params=_cparams(("parallel",), 40),
        name="adaln",
    )(cond, w, b)


def _inproj_kernel(x_ref, nw_ref, sc_ref, sh_ref, w_ref, o_ref, xn_ref):
    @pl.when(pl.program_id(2) == 0)
    def _():
        x = x_ref[0]
        r = lax.rsqrt(jnp.mean(x * x, axis=-1, keepdims=True) + NORM_EPS)
        y = (x * r * nw_ref[...]) * (1.0 + sc_ref[0]) + sh_ref[0]
        xn_ref[...] = y.astype(BF16)

    o_ref[0] = jnp.dot(xn_ref[...], w_ref[...], preferred_element_type=F32)


def _inproj(x, nw, sc, sh, wp):
    b, l, d = x.shape
    n = wp.shape[1]
    tm = _pick_tile(l, 1024, 8)
    tn = _pick_tile(n, 1280, LANES)
    return pl.pallas_call(
        _inproj_kernel,
        grid=(b, l // tm, n // tn),
        in_specs=[pl.BlockSpec((1, tm, d), lambda i, m, j: (i, m, 0)),
                  pl.BlockSpec((1, d), lambda i, m, j: (0, 0)),
                  pl.BlockSpec((1, 1, d), lambda i, m, j: (i, 0, 0)),
                  pl.BlockSpec((1, 1, d), lambda i, m, j: (i, 0, 0)),
                  pl.BlockSpec((d, tn), lambda i, m, j: (0, j))],
        out_specs=pl.BlockSpec((1, tm, tn), lambda i, m, j: (i, m, j)),
        out_shape=jax.ShapeDtypeStruct((b, l, n), F32),
        scratch_shapes=[pltpu.VMEM((tm, d), BF16)],
        compiler_params=_cparams(("parallel", "parallel", "arbitrary"), 56),
        name="inproj",
    )(x, nw, sc, sh, wp)


def _outproj_kernel(x_ref, yg_ref, yh_ref, ym_ref, wg_ref, wh_ref, wm_ref, gt_ref, fw_ref, o_ref, *, final):
    acc = jnp.dot(yg_ref[0], wg_ref[...], preferred_element_type=F32)
    acc = acc + jnp.dot(yh_ref[0], wh_ref[...], preferred_element_type=F32)
    acc = acc + jnp.dot(ym_ref[0], wm_ref[...], preferred_element_type=F32)
    xn = x_ref[0] + gt_ref[0] * acc
    if final:
        r = lax.rsqrt(jnp.mean(xn * xn, axis=-1, keepdims=True) + NORM_EPS)
        xn = xn * r * fw_ref[...]
    o_ref[0] = xn


def _outproj(x, yg, yh, ym, wg, wh, wm, gt, fw, final):
    b, l, d = x.shape
    tm = _pick_tile(l, 512, 8)
    row = lambda w: pl.BlockSpec((1, tm, w), lambda i, m: (i, m, 0))
    full = lambda a: pl.BlockSpec(a.shape, lambda i, m: (0, 0))
    return pl.pallas_call(
        functools.partial(_outproj_kernel, final=final),
        grid=(b, l // tm),
        in_specs=[row(d), row(yg.shape[2]), row(yh.shape[2]), row(ym.shape[2]),
                  full(wg), full(wh), full(wm),
                  pl.BlockSpec((1, 1, d), lambda i, m: (i, 0, 0)),
                  pl.BlockSpec((1, d), lambda i, m: (0, 0))],
        out_specs=row(d),
        out_shape=jax.ShapeDtypeStruct((b, l, d), F32),
        compiler_params=_cparams(("parallel", "parallel"), 48),
        name="outproj",
    )(x, yg, yh, ym, wg, wh, wm, gt, fw)


def _conv_rows(x, w, period):
    rows = x.shape[0]
    taps = w.shape[0]
    pad = taps // 2
    pos = lax.broadcasted_iota(jnp.int32, x.shape, 0) % period
    y = None
    for j in range(taps):
        off = j - pad
        if off == 0:
            term = x * w[j:j + 1]
        else:
            shifted = pltpu.roll(x, (-off) % rows, axis=0)
            ok = (pos >= -off) if off < 0 else (pos < period - off)
            term = jnp.where(ok, shifted, 0.0) * w[j:j + 1]
        y = term if y is None else y + term
    return y


def _seg_cumsum(x, reverse):
    lane = lax.broadcasted_iota(jnp.int32, x.shape, 1) % CHUNK
    s = 1
    while s < CHUNK:
        if reverse:
            shifted = pltpu.roll(x, LANES - s, axis=1)
            ok = lane < CHUNK - s
        else:
            shifted = pltpu.roll(x, s, axis=1)
            ok = lane >= s
        x = x + jnp.where(ok, shifted, 0.0)
        s *= 2
    return x


def _chunk_masks(d):
    ii = lax.broadcasted_iota(jnp.int32, (CHUNK, CHUNK), 0)
    jj = lax.broadcasted_iota(jnp.int32, (CHUNK, CHUNK), 1)
    eye = ii == jj
    if d == 0:
        return eye, jj <= ii, jj < ii
    return eye, jj >= ii, jj > ii


def _pair_masks(d):
    ii = lax.broadcasted_iota(jnp.int32, (LANES, LANES), 0)
    jj = lax.broadcasted_iota(jnp.int32, (LANES, LANES), 1)
    lo = (ii // CHUNK) * CHUNK
    eye = ii == jj
    if d == 0:
        return eye, (jj >= lo) & (jj <= ii), (jj >= lo) & (jj < ii)
    return eye, (jj < lo + CHUNK) & (jj >= ii), (jj < lo + CHUNK) & (jj > ii)


def _to_col(row, eye):
    return jnp.sum(jnp.where(eye, jnp.broadcast_to(row, eye.shape), 0.0), axis=1, keepdims=True)


def _split_bf16(a):
    hi = a.astype(BF16)
    return hi, (a - hi.astype(F32)).astype(BF16)


def _col_bcast(row, eye):
    x = jnp.where(eye, jnp.broadcast_to(row, eye.shape), 0.0)
    hi = x.astype(BF16)
    mid, lo = _split_bf16(x - hi.astype(F32))
    ones = jnp.ones((3 * eye.shape[1], LANES), BF16)
    return jnp.dot(jnp.concatenate([hi, mid, lo], axis=1), ones, preferred_element_type=F32)


def _dot3(a, b):
    a_hi, a_lo = _split_bf16(a)
    b_hi, b_lo = _split_bf16(b)
    return jnp.dot(jnp.concatenate([a_hi, a_hi, a_lo], axis=1), jnp.concatenate([b_hi, b_lo, b_hi], axis=0),
                   preferred_element_type=F32)


def _row_slice(rows_ref, idx, p, half):
    return rows_ref[idx, pl.ds(p, 1), half * CHUNK:(half + 1) * CHUNK]


def _head_out_tiles(acc_refs, base, nw_ref, z_ref, y_ref, n_rows, og_ref=None):
    tile = _pick_tile(n_rows, 256, 8)

    def body(i, carry):
        r = pl.multiple_of(i * tile, tile)
        o = acc_refs[0][pl.ds(base + r, tile), :]
        for acc_ref in acc_refs[1:]:
            o = o + acc_ref[pl.ds(base + r, tile), :]
        if og_ref is not None:
            o = jax.nn.sigmoid(og_ref[0, pl.ds(r, tile), :]) * o
        o = o * lax.rsqrt(jnp.mean(o * o, axis=-1, keepdims=True) + NORM_EPS) * nw_ref[...]
        y_ref[0, pl.ds(r, tile), :] = (o * _silu(z_ref[0, pl.ds(r, tile), :])).astype(y_ref.dtype)
        return carry

    lax.fori_loop(0, n_rows // tile, body, 0)


def _ml_rows(g_refs, bias_ref, h, n_heads, rows_ref):
    li_f = g_refs[0][0, 0] + bias_ref[h]
    lf_f = -_softplus(-(g_refs[1][0, 0] + bias_ref[n_heads + h]))
    li_b = g_refs[2][0, 0] + bias_ref[2 * n_heads + h]
    lf_b = -_softplus(-(g_refs[3][0, 0] + bias_ref[3 * n_heads + h]))
    rows_ref[0] = _seg_cumsum(lf_f, False)
    rows_ref[1] = li_f
    rows_ref[2] = _seg_cumsum(lf_b, True)
    rows_ref[3] = li_b


def _ml_prep_cols(q_ref, k_ref, rows_ref, chains):
    eye = _pair_masks(0)[0]
    lane_i = lax.broadcasted_iota(jnp.int32, (1, LANES), 1)
    st = []
    for d, p in chains:
        rows = pl.ds(pl.multiple_of(p * LANES, LANES), LANES)
        k = k_ref[0, rows, :] * (HEAD_DIM ** -0.5)
        b_row = rows_ref[2 * d, pl.ds(p, 1), :]
        li_row = rows_ref[2 * d + 1, pl.ds(p, 1), :]
        last = CHUNK - 1 if d == 0 else 0
        b_tot = (b_row[:, last:last + 1], b_row[:, CHUNK + last:CHUNK + last + 1])
        end_row = jnp.where(lane_i < CHUNK, b_tot[0], b_tot[1]) - b_row + li_row
        e_max = (jnp.max(end_row[:, :CHUNK], axis=1, keepdims=True),
                 jnp.max(end_row[:, CHUNK:], axis=1, keepdims=True))
        st.append(dict(d=d, rows=rows, k=k, b_row=b_row, li_row=li_row, b_tot=b_tot, e_max=e_max,
                       b_cb=_col_bcast(b_row, eye), end_cb=_col_bcast(end_row, eye),
                       qk=_dot_nt(q_ref[0, rows, :], k)))
    return st


def _ml_prep_intra(st):
    for c in st:
        _, incl, _ = _pair_masks(c["d"])
        dlog = jnp.where(incl, c["b_cb"] - c["b_row"] + c["li_row"], -jnp.inf)
        c["rowmax"] = jnp.max(dlog, axis=1, keepdims=True)
        c["p_hi"], c["p_lo"] = _split_bf16(jnp.exp(dlog - c["rowmax"]) * c["qk"])


def _ml_prep_state(st):
    row_i = lax.broadcasted_iota(jnp.int32, (LANES, 1), 0)
    lane_sq = lax.broadcasted_iota(jnp.int32, (LANES, LANES), 1)
    out = []
    for c in st:
        kw_t = (c["k"] * jnp.exp(c["end_cb"] - jnp.where(row_i < CHUNK, c["e_max"][0], c["e_max"][1]))).T
        kw_t2 = jnp.concatenate([jnp.where(lane_sq < CHUNK, kw_t, 0.0), jnp.where(lane_sq < CHUNK, 0.0, kw_t)],
                                axis=0).astype(BF16)
        out.append((c["b_cb"], c["rowmax"], c["p_hi"], c["p_lo"], kw_t2, *c["b_tot"], *c["e_max"]))
    return tuple(out)


def _ml_steps_state(q_ref, v_ref, cnst, pre, chains):
    ones = jnp.ones((LANES, LANES), BF16)
    st = []
    for d, p, half in chains:
        _, _, p_hi, p_lo, kw_t2 = pre[d][:5]
        sl = slice(half * CHUNK, (half + 1) * CHUNK)
        v_ones = jnp.concatenate([v_ref[0, pl.ds(pl.multiple_of(p * LANES, LANES), LANES), :].astype(BF16), ones],
                                 axis=1)
        rhs = jnp.concatenate([v_ones, jnp.concatenate([jnp.zeros_like(ones), ones], axis=1)], axis=0)
        pv_ps = jnp.dot(jnp.concatenate([p_hi[sl], p_lo[sl]], axis=1), rhs, preferred_element_type=F32)
        d_cn = jnp.dot(kw_t2[half * LANES:(half + 1) * LANES], v_ones, preferred_element_type=F32)
        st.append((pv_ps, d_cn))
    for n, (d, p, half) in enumerate(chains):
        r0 = pl.ds(pl.multiple_of(p * LANES + half * CHUNK, CHUNK), CHUNK)
        cn = cnst[d]
        st[n] = (r0, cn, _dot(q_ref[0, r0, :], cn), *st[n])
    return st


def _ml_steps_update(pre, acc_refs, cnst, mst, chains, st):
    for (d, p, half), (r0, cn, q_cn, pv_ps, d_cn) in zip(chains, st):
        b_cb, rowmax = pre[d][:2]
        b_tot = pre[d][5 + half]
        e_max = pre[d][7 + half]
        sl = slice(half * CHUNK, (half + 1) * CHUNK)
        m_s = mst[d, 0:1, 0:1]
        inter = b_cb[sl] + m_s
        m_i = jnp.maximum(inter, rowmax[sl])
        w_inter = jnp.exp(inter - m_i)
        s_intra = jnp.exp(rowmax[sl] - m_i)
        num = w_inter * q_cn[:, :HEAD_DIM] + s_intra * pv_ps[:, :HEAD_DIM]
        den = w_inter * q_cn[:, HEAD_DIM:] + s_intra * pv_ps[:, HEAD_DIM:]
        acc_refs[d][r0, :] = num / jnp.maximum(jnp.abs(den), jnp.exp(-m_i))
        carry_log = b_tot + m_s
        m_new = jnp.maximum(carry_log, e_max)
        cnst[d] = jnp.exp(carry_log - m_new) * cn + jnp.exp(e_max - m_new) * d_cn
        mst[d] = jnp.broadcast_to(m_new, mst.shape[1:])


def _ml_scan(q_ref, k_ref, v_ref, rows_ref, acc_refs, cnst, mst, n_pairs):
    def chains_of(t):
        return [(0, t), (1, n_pairs - 1 - t), (0, t + 1), (1, n_pairs - 2 - t)]

    def prep(chains):
        st = _ml_prep_cols(q_ref, k_ref, rows_ref, chains)
        _ml_prep_intra(st)
        return _ml_prep_state(st)

    def body(i, pre):
        t = 2 * i
        st = _ml_prep_cols(q_ref, k_ref, rows_ref, chains_of(jnp.minimum(t + 2, n_pairs - 2)))
        nxt = None
        for j, tt in enumerate((t, t + 1)):
            pf, pb = tt, n_pairs - 1 - tt
            pre_j = pre[2 * j:2 * j + 2]
            for halves in ([(0, pf, 0), (1, pb, 1)], [(0, pf, 1), (1, pb, 0)]):
                held = _ml_steps_state(q_ref, v_ref, cnst, pre_j, halves)
                if j == 0 and halves[0][2] == 0:
                    _ml_prep_intra(st)
                elif j == 0:
                    nxt = _ml_prep_state(st)
                _ml_steps_update(pre_j, acc_refs, cnst, mst, halves, held)
        return nxt

    lax.fori_loop(0, n_pairs // 2, body, prep(chains_of(0)))


def _mlstm_kernel(bias_ref, ql, kl, vl, ol, zl, qc, kc, vc, oc, zc, gl0, gl1, gl2, gl3, gc0, gc1, gc2, gc3,
                  nw_ref, *rest, n_heads, ctx_out):
    if ctx_out:
        yl_ref, yc_ref = rest[:2]
        rest = rest[2:]
    else:
        yl_ref, yc_ref = rest[0], None
        rest = rest[1:]
    cnst, mst, rows_l, rows_c, acc_lf, acc_lb, acc_cf, acc_cb = rest
    h = pl.program_id(1)
    cnst[...] = jnp.zeros(cnst.shape, F32)
    mst[...] = jnp.zeros(mst.shape, F32)
    _ml_rows((gc0, gc1, gc2, gc3), bias_ref, h, n_heads, rows_c)
    _ml_scan(qc, kc, vc, rows_c, (acc_cf, acc_cb), cnst, mst, rows_c.shape[1])
    _ml_rows((gl0, gl1, gl2, gl3), bias_ref, h, n_heads, rows_l)
    _ml_scan(ql, kl, vl, rows_l, (acc_lf, acc_lb), cnst, mst, rows_l.shape[1])
    _head_out_tiles((acc_lf, acc_lb), 0, nw_ref, zl, yl_ref, acc_lf.shape[0], og_ref=ol)
    if ctx_out:
        _head_out_tiles((acc_cf, acc_cb), 0, nw_ref, zc, yc_ref, acc_cf.shape[0], og_ref=oc)


def _head_block(l, blk):
    return pl.BlockSpec((1, l, HEAD_DIM), lambda b, h, blk=blk: (b, 0, blk + h))


def _gate_block(n_pairs, j, n_heads):
    return pl.BlockSpec((1, 1, n_pairs, LANES), lambda b, h, j=j: (b, j * n_heads + h, 0, 0))


def _mlstm(u_l, u_c, gt_l, gt_c, bias, nw, lay, ctx_out):
    b, l, _ = u_l.shape
    lc = u_c.shape[1]
    nh = lay["mh"]
    blk = lambda off: off // HEAD_DIM
    offs = [lay["MQ"], lay["MQ"] + lay["DM"], lay["MQ"] + 2 * lay["DM"], lay["MO"], lay["MZ"]]
    in_specs = [pl.BlockSpec(memory_space=pltpu.SMEM)]
    in_specs += [_head_block(l, blk(o)) for o in offs]
    in_specs += [_head_block(lc, blk(o)) for o in offs]
    in_specs += [_gate_block(l // LANES, j, nh) for j in range(4)]
    in_specs += [_gate_block(lc // LANES, j, nh) for j in range(4)]
    in_specs += [pl.BlockSpec((1, HEAD_DIM), lambda b_, h: (0, 0))]
    out_specs = [pl.BlockSpec((1, l, HEAD_DIM), lambda b_, h: (b_, 0, h))]
    out_shape = [jax.ShapeDtypeStruct((b, l, nh * HEAD_DIM), BF16)]
    if ctx_out:
        out_specs.append(pl.BlockSpec((1, lc, HEAD_DIM), lambda b_, h: (b_, 0, h)))
        out_shape.append(jax.ShapeDtypeStruct((b, lc, nh * HEAD_DIM), BF16))
    scratch = [pltpu.VMEM((2, HEAD_DIM, 2 * HEAD_DIM), F32), pltpu.VMEM((2, 8, LANES), F32),
               pltpu.VMEM((4, l // LANES, LANES), F32), pltpu.VMEM((4, lc // LANES, LANES), F32),
               pltpu.VMEM((l, HEAD_DIM), F32), pltpu.VMEM((l, HEAD_DIM), F32),
               pltpu.VMEM((lc, HEAD_DIM), F32), pltpu.VMEM((lc, HEAD_DIM), F32)]
    res = pl.pallas_call(
        functools.partial(_mlstm_kernel, n_heads=nh, ctx_out=ctx_out),
        grid=(b, nh),
        in_specs=in_specs, out_specs=out_specs, out_shape=out_shape,
        scratch_shapes=scratch,
        compiler_params=_cparams(("parallel", "parallel"), 48),
        name="mlstm",
    )(bias.reshape(-1), *([u_l] * 5), *([u_c] * 5), *([gt_l] * 4), *([gt_c] * 4), nw)
    return (res[0], res[1]) if ctx_out else (res[0], None)


def _gdn_rows(g_refs, alog_ref, dtb_ref, h, n_heads, rows_ref, base):
    n = g_refs[0].shape[2]
    g_f = -jnp.exp(alog_ref[h]) * _softplus(g_refs[0][0, 0] + dtb_ref[h])
    g_b = -jnp.exp(alog_ref[n_heads + h]) * _softplus(g_refs[1][0, 0] + dtb_ref[n_heads + h])
    rows_ref[0, base:base + n] = _seg_cumsum(g_f, False)
    rows_ref[1, base:base + n] = jax.nn.sigmoid(g_refs[2][0, 0])
    rows_ref[2, base:base + n] = _seg_cumsum(g_b, True)
    rows_ref[3, base:base + n] = jax.nn.sigmoid(g_refs[3][0, 0])


def _masked_taps(w, period):
    taps = w.shape[0]
    pad = taps // 2
    pos = lax.broadcasted_iota(jnp.int32, (period, w.shape[1]), 0)
    out = []
    for j in range(taps):
        off = j - pad
        ok = (pos >= -off) if off < 0 else (pos < period - off)
        out.append(jnp.where(ok, w[j:j + 1], 0.0))
    return out


def _conv_rows_pre(x, wm_ref, first):
    rows = x.shape[0]
    taps = wm_ref.shape[0] // 3
    period = wm_ref.shape[1]
    pad = taps // 2
    y = None
    for j in range(taps):
        off = j - pad
        shifted = x if off == 0 else pltpu.roll(x, (-off) % rows, axis=0)
        term = shifted * jnp.concatenate([wm_ref[first + j]] * (rows // period), axis=0)
        y = term if y is None else y + term
    return y


def _gdn_conv(src_refs, w_refs, dst_refs, base, n_units, unit_rows, period, wm_ref=None):
    taps = w_refs[0].shape[0]
    if wm_ref is not None:
        for idx in range(3):
            for j, wm in enumerate(_masked_taps(w_refs[idx][...], period)):
                wm_ref[idx * taps + j] = wm

    def body(i, carry):
        r = pl.multiple_of(i * unit_rows, unit_rows)
        for idx in range(3):
            x = src_refs[idx][0, pl.ds(r, unit_rows), :]
            if wm_ref is not None:
                t = _silu(_conv_rows_pre(x, wm_ref, idx * taps))
            else:
                t = _silu(_conv_rows(x, w_refs[idx][...], period))
            if idx < 2:
                t = t * lax.rsqrt(jnp.sum(t * t, axis=-1, keepdims=True) + NORM_EPS)
            if idx == 0:
                t = t * (HEAD_DIM ** -0.5)
            dst_refs[idx][pl.ds(base + r, unit_rows), :] = t
        return carry

    lax.fori_loop(0, n_units, body, 0)


N_NEUMANN = CHUNK.bit_length() - 2


def _side_masks(d):
    ii = lax.broadcasted_iota(jnp.int32, (CHUNK, LANES), 0)
    lane = lax.broadcasted_iota(jnp.int32, (CHUNK, LANES), 1)
    jj = lane % CHUNK
    first = lane < CHUNK
    if d == 0:
        return first, jj == ii, jj <= ii, jj < ii
    return first, jj == ii, jj >= ii, jj > ii


def _side(x, first):
    return jnp.where(first, x[:CHUNK], x[CHUNK:])


def _block_diag(x, first):
    zero = jnp.zeros_like(x)
    return jnp.concatenate([jnp.where(first, x, zero), jnp.where(first, zero, x)], axis=0)


def _dot3_side(a, b, first):
    a_hi, a_lo = _split_bf16(a)
    b_hi, b_lo = _split_bf16(b)
    rhs = jnp.concatenate([_block_diag(b_hi, first), _block_diag(b_lo, first), _block_diag(b_hi, first)], axis=0)
    return jnp.dot(jnp.concatenate([a_hi, a_hi, a_lo], axis=1), rhs, preferred_element_type=F32)


def _gdn_gates(rows_ref, d, p):
    eye = _pair_masks(0)[0]
    g_row = rows_ref[2 * d, pl.ds(p, 1), :]
    return dict(d=d, p=p, rows=pl.ds(pl.multiple_of(p * LANES, LANES), LANES), g_row=g_row,
                g_col=_to_col(g_row, eye), beta_col=_to_col(rows_ref[2 * d + 1, pl.ds(p, 1), :], eye))


def _gdn_prep_start(qs, ks, rows_ref, chains):
    st = []
    for d, p in chains:
        c = _gdn_gates(rows_ref, d, p)
        rows, g_row, g_col, beta_col = c["rows"], c["g_row"], c["g_col"], c["beta_col"]
        k = ks[rows, :]
        first, eye_s, incl, strict = _side_masks(d)
        dec = jnp.exp(jnp.where(incl, _side(g_col, first) - g_row, -jnp.inf))
        aqk = _block_diag(_side(_dot_nt(qs[rows, :], k), first) * dec, first).astype(BF16)
        m_low = jnp.where(strict, _side(beta_col, first) * _side(_dot_nt(k, k), first) * dec, 0.0)
        st.append(dict(c, aqk=aqk, pw=m_low, t_inv=jnp.where(eye_s, 1.0, 0.0) - m_low))
    return st


def _gdn_prep_neumann(st, after_square=None, after_product=None):
    first, eye_s, _, _ = _side_masks(0)
    eye_f = jnp.where(eye_s, 1.0, 0.0)
    for c in st:
        c["pw"] = _dot3_side(c["pw"], c["pw"], first)
    if after_square is not None:
        after_square()
    for c in st:
        c["t_inv"] = _dot3_side(c["t_inv"], eye_f + c["pw"], first)
    if after_product is not None:
        after_product()


NQ_ROWS = HEAD_DIM + CHUNK


def _gdn_prep_solve(qs, ks, vs, st):
    row_i = lax.broadcasted_iota(jnp.int32, (LANES, 1), 0)
    first = _side_masks(0)[0]
    for c in st:
        d, rows, g_row, g_col, beta_col = c["d"], c["rows"], c["g_row"], c["g_col"], c["beta_col"]
        k = ks[rows, :]
        e_g = jnp.exp(g_col)
        c["w_ut"] = _dot3(_block_diag(c["t_inv"], first),
                          jnp.concatenate([(beta_col * e_g) * k, beta_col * vs[rows, :]], axis=1)).astype(BF16)
        c["q_dec"] = qs[rows, :] * e_g
        last = CHUNK - 1 if d == 0 else 0
        g_tot_col = jnp.where(row_i < CHUNK, g_row[:, last:last + 1], g_row[:, CHUNK + last:CHUNK + last + 1])
        c["k_end_t"] = (k * jnp.exp(g_tot_col - g_col)).T.astype(BF16)


def _gdn_prep_finish(nq_s, r_s, acc_refs, st):
    row_i = lax.broadcasted_iota(jnp.int32, (LANES, 1), 0)
    for c in st:
        d, w_ut, q_dec = c["d"], c["w_ut"], c["q_dec"]
        zero = jnp.zeros_like(w_ut)
        for half in (0, 1):
            sl = slice(half * CHUNK, (half + 1) * CHUNK)
            lhs = jnp.concatenate([c["k_end_t"], c["aqk"][sl]], axis=0)
            mine = (row_i < CHUNK) if half == 0 else (row_i >= CHUNK)
            prod = jnp.dot(lhs, jnp.where(mine, w_ut, zero), preferred_element_type=F32)
            n_q = jnp.concatenate([prod[:HEAD_DIM, :HEAD_DIM], q_dec[sl] - prod[HEAD_DIM:, :HEAD_DIM]], axis=0)
            chunk = 2 * c["p"] + half
            nq_s[d, pl.ds(pl.multiple_of(chunk * NQ_ROWS, NQ_ROWS), NQ_ROWS), :] = n_q.astype(BF16)
            r_s[d, pl.ds(pl.multiple_of(chunk * HEAD_DIM, HEAD_DIM), HEAD_DIM), :] = prod[:HEAD_DIM, HEAD_DIM:]
            acc_refs[d][pl.ds(pl.multiple_of(chunk * CHUNK, CHUNK), CHUNK), :] = prod[HEAD_DIM:, HEAD_DIM:]


def _gdn_steps_state(nq_s, sst, chains):
    st = []
    for d, p, half in chains:
        chunk = 2 * p + half
        s = sst[d]
        n_q = nq_s[d, pl.ds(pl.multiple_of(chunk * NQ_ROWS, NQ_ROWS), NQ_ROWS), :]
        st.append((chunk, s, jnp.dot(n_q, s.astype(BF16), preferred_element_type=F32)))
    return st


def _gdn_steps_update(rows_ref, r_s, sst, acc_refs, chains, st):
    for (d, p, half), (chunk, s, prod) in zip(chains, st):
        lane = half * CHUNK + (CHUNK - 1 if d == 0 else 0)
        g_tot = rows_ref[2 * d, pl.ds(p, 1), lane:lane + 1]
        r0 = pl.ds(pl.multiple_of(chunk * CHUNK, CHUNK), CHUNK)
        acc_refs[d][r0, :] = acc_refs[d][r0, :] + prod[HEAD_DIM:]
        r_c = r_s[d, pl.ds(pl.multiple_of(chunk * HEAD_DIM, HEAD_DIM), HEAD_DIM), :]
        sst[d] = jnp.exp(g_tot) * s - prod[:HEAD_DIM] + r_c


def _gdn_kernel(alog_ref, dtb_ref, ql, kl, vl, zl, qc, kc, vc, zc, gl0, gl1, gl2, gl3, gc0, gc1, gc2, gc3,
                wq, wk, wv, nw_ref, *rest, n_heads, ctx_out):
    if ctx_out:
        yl_ref, yc_ref = rest[:2]
        rest = rest[2:]
    else:
        yl_ref, yc_ref = rest[0], None
        rest = rest[1:]
    sst, rows, qs, ks, vs, nq_s, r_s, acc_f, acc_b, nm_s, aq_s, wm_s = rest
    h = pl.program_id(1)
    n_l = ql.shape[1]
    n_c = qc.shape[1]
    nl = n_l // LANES
    nc = n_c // LANES
    sst[...] = jnp.zeros(sst.shape, F32)
    _gdn_rows((gl0, gl1, gl2, gl3), alog_ref, dtb_ref, h, n_heads, rows, 0)
    _gdn_rows((gc0, gc1, gc2, gc3), alog_ref, dtb_ref, h, n_heads, rows, nl)
    conv_rows = _pick_tile(n_l, 4 * CHUNK, CHUNK)
    _gdn_conv((ql, kl, vl), (wq, wk, wv), (qs, ks, vs), 0, n_l // conv_rows, conv_rows, CHUNK, wm_ref=wm_s)
    _gdn_conv((qc, kc, vc), (wq, wk, wv), (qs, ks, vs), n_l, 1, n_c, n_c)
    n_steps = nl + nc
    accs = (acc_f, acc_b)

    def fwd_pair(t):
        return jnp.where(t < nc, nl + t, t - nc)

    def bwd_pair(t):
        return n_steps - 1 - t

    n_trips = n_steps // 2
    n_early = (N_NEUMANN + 1) // 2

    def chains_of(trip):
        t = 2 * jnp.minimum(trip, n_trips - 1)
        return [(0, fwd_pair(t)), (1, bwd_pair(t)), (0, fwd_pair(t + 1)), (1, bwd_pair(t + 1))]

    def neumann(st, half=None):
        if half is None:
            _gdn_prep_neumann(st)
            return
        held = []
        _gdn_prep_neumann(st, lambda: held.append(_gdn_steps_state(nq_s, sst, half)),
                          lambda: _gdn_steps_update(rows, r_s, sst, accs, half, held[0]))

    def finish(st):
        _gdn_prep_solve(qs, ks, vs, st)
        _gdn_prep_finish(nq_s, r_s, accs, st)

    def halves_of(trip):
        out = []
        for t in (2 * trip, 2 * trip + 1):
            pf, pb = fwd_pair(t), bwd_pair(t)
            out += [[(0, pf, 0), (1, pb, 1)], [(0, pf, 1), (1, pb, 0)]]
        return out

    st = _gdn_prep_start(qs, ks, rows, chains_of(0))
    for _ in range(N_NEUMANN):
        neumann(st)
    finish(st)
    st = _gdn_prep_start(qs, ks, rows, chains_of(1))
    for _ in range(n_early):
        neumann(st)

    def park(st_part):
        for n, c in enumerate(st_part):
            nm_s[2 * n] = c["pw"]
            nm_s[2 * n + 1] = c["t_inv"]
            aq_s[n] = c["aqk"]

    park(st)

    def trip(i, carry):
        halves = halves_of(i)
        st_late = [dict(_gdn_gates(rows, d, p), pw=nm_s[2 * n], t_inv=nm_s[2 * n + 1], aqk=aq_s[n])
                   for n, (d, p) in enumerate(chains_of(i + 1))]
        st_early = _gdn_prep_start(qs, ks, rows, chains_of(i + 2))
        n_late = N_NEUMANN - n_early
        slot = 0
        for n in range(n_early):
            neumann(st_early, halves[slot] if slot < len(halves) else None)
            slot += 1
            if n < n_late:
                neumann(st_late, halves[slot] if slot < len(halves) else None)
                slot += 1
            if n == n_late - 1:
                _gdn_prep_solve(qs, ks, vs, st_late)
        _gdn_prep_finish(nq_s, r_s, accs, st_late)
        park(st_early)
        return carry

    lax.fori_loop(0, n_trips - 1, trip, 0)
    for half in halves_of(n_trips - 1):
        _gdn_steps_update(rows, r_s, sst, accs, half, _gdn_steps_state(nq_s, sst, half))
    _head_out_tiles((acc_f, acc_b), 0, nw_ref, zl, yl_ref, n_l)
    if ctx_out:
        _head_out_tiles((acc_f, acc_b), n_l, nw_ref, zc, yc_ref, n_c)


def _gdn(u_l, u_c, gt_l, gt_c, conv_w, a_log, dt_bias, nw, lay, ctx_out):
    b, l, _ = u_l.shape
    lc = u_c.shape[1]
    nh = lay["gh"]
    taps = conv_w.shape[0]
    blk = lambda off: off // HEAD_DIM
    offs = [lay["GQ"], lay["GQ"] + lay["DG"], lay["GQ"] + 2 * lay["DG"], lay["GZ"]]
    smem = pl.BlockSpec(memory_space=pltpu.SMEM)
    in_specs = [smem, smem]
    in_specs += [_head_block(l, blk(o)) for o in offs]
    in_specs += [_head_block(lc, blk(o)) for o in offs]
    in_specs += [_gate_block(l // LANES, j, nh) for j in range(4)]
    in_specs += [_gate_block(lc // LANES, j, nh) for j in range(4)]
    in_specs += [pl.BlockSpec((taps, HEAD_DIM), lambda b_, h, j=j: (0, j * nh + h)) for j in range(3)]
    in_specs += [pl.BlockSpec((1, HEAD_DIM), lambda b_, h: (0, 0))]
    out_specs = [pl.BlockSpec((1, l, HEAD_DIM), lambda b_, h: (b_, 0, h))]
    out_shape = [jax.ShapeDtypeStruct((b, l, nh * HEAD_DIM), BF16)]
    if ctx_out:
        out_specs.append(pl.BlockSpec((1, lc, HEAD_DIM), lambda b_, h: (b_, 0, h)))
        out_shape.append(jax.ShapeDtypeStruct((b, lc, nh * HEAD_DIM), BF16))
    lt = l + lc
    seq_f32 = pltpu.VMEM((lt, HEAD_DIM), F32)
    n_chunks = lt // CHUNK
    scratch = [pltpu.VMEM((2, HEAD_DIM, HEAD_DIM), F32), pltpu.VMEM((4, lt // LANES, LANES), F32),
               seq_f32, seq_f32, seq_f32,
               pltpu.VMEM((2, n_chunks * NQ_ROWS, HEAD_DIM), BF16), pltpu.VMEM((2, n_chunks * HEAD_DIM, HEAD_DIM), F32),
               seq_f32, seq_f32, pltpu.VMEM((8, CHUNK, LANES), F32), pltpu.VMEM((4, LANES, LANES), BF16),
               pltpu.VMEM((3 * taps, CHUNK, HEAD_DIM), F32)]
    res = pl.pallas_call(
        functools.partial(_gdn_kernel, n_heads=nh, ctx_out=ctx_out),
        grid=(b, nh),
        in_specs=in_specs, out_specs=out_specs, out_shape=out_shape,
        scratch_shapes=scratch,
        compiler_params=_cparams(("parallel", "parallel"), 56),
        name="gdn",
    )(a_log.reshape(-1), dt_bias.reshape(-1), *([u_l] * 4), *([u_c] * 4), *([gt_l] * 4), *([gt_c] * 4),
      *([conv_w] * 3), nw)
    return (res[0], res[1]) if ctx_out else (res[0], None)


def _dft_tables(l):
    k = jnp.arange(l, dtype=jnp.int32)
    ang = lambda t: ((k[:, None] * t[None, :]) % (2 * l)).astype(F32) * (math.pi / l)
    ang_a = ang(jnp.arange(l // CHUNK, dtype=jnp.int32) * CHUNK)[:, :, None]
    ang_b = ang(jnp.arange(CHUNK, dtype=jnp.int32))[:, None, :]
    cos_t = (jnp.cos(ang_a) * jnp.cos(ang_b) - jnp.sin(ang_a) * jnp.sin(ang_b)).reshape(l, l)
    sin_t = (jnp.sin(ang_a) * jnp.cos(ang_b) + jnp.cos(ang_a) * jnp.sin(ang_b)).reshape(l, l)
    alt = jnp.where(k % 2 == 0, 1.0, -1.0).astype(F32)
    sin_f = sin_t.at[0, :].set(alt)
    return cos_t.astype(BF16), sin_f.astype(BF16), sin_f.T.astype(BF16)


def _filter_kernel(feats_ref, featr_ref, w1_ref, b1_ref, fr_ref, w2_ref, b2_ref, w3c_ref, w3a_ref, dl_ref,
                   h2_ref, h2r_ref, hid_ref, hidr_ref):
    @pl.when((pl.program_id(0) == 0) & (pl.program_id(1) == 0))
    def _():
        for f_ref, h_ref in ((feats_ref, hid_ref), (featr_ref, hidr_ref)):
            hid = jnp.sin(fr_ref[...] * (_dot_hi(f_ref[...], w1_ref[...]) + b1_ref[...]))
            h_ref[...] = jnp.sin(fr_ref[...] * (_dot_hi(hid, w2_ref[...]) + b2_ref[...]))

    def raw(h_ref, f_ref, w3_ref):
        return _dot3(h_ref[...], w3_ref[...]) * jnp.exp(-f_ref[:, 0:1] * dl_ref[...])

    c_f = raw(hid_ref, feats_ref, w3c_ref)
    a_f = raw(hid_ref, feats_ref, w3a_ref)
    den_c = jnp.sum(jnp.abs(c_f), axis=0, keepdims=True) + NORM_EPS
    den_a = jnp.sum(jnp.abs(a_f), axis=0, keepdims=True) + NORM_EPS
    c_f = c_f / den_c
    a_f = a_f / den_a
    row0 = lax.broadcasted_iota(jnp.int32, c_f.shape, 0) == 0
    centre = c_f[0:1] + a_f[0:1]
    h2_ref[0, 0] = jnp.where(row0, 0.0, raw(hidr_ref, featr_ref, w3a_ref) / den_a)
    h2_ref[0, 1] = jnp.where(row0, centre, c_f)
    h2r_ref[0, 0] = jnp.where(row0, 0.0, raw(hidr_ref, featr_ref, w3c_ref) / den_c)
    h2r_ref[0, 1] = jnp.where(row0, centre, a_f)


def _filter_feats(pos, l, n_emb):
    t = pos / max(l - 1, 1)
    ang = 2.0 * math.pi * pos / l
    bands = jnp.linspace(1e-4, FILTER_BANDS - 1, FILTER_BANDS, dtype=F32)
    feats = jnp.concatenate([t[:, None], jnp.cos(ang[:, None] * bands), -jnp.sin(ang[:, None] * bands)], axis=-1)
    return jnp.pad(feats, ((0, 0), (0, LANES - n_emb)))


def _hy_filters(l, w1, b1, freq, w2, b2, w3, dh):
    n_emb, n_hid = w1.shape
    pos = jnp.arange(l, dtype=F32)
    feats = _filter_feats(pos, l, n_emb)
    feats_r = _filter_feats(l - pos, l, n_emb)
    pc = LANES - n_hid
    w1p = jnp.pad(w1, ((0, LANES - n_emb), (0, pc)))
    w2p = jnp.pad(w2, ((0, pc), (0, pc)))
    w3p = jnp.pad(w3, ((0, pc), (0, 0)))
    row = lambda a: jnp.pad(a, (0, pc))[None]
    deltas = jnp.abs(jnp.linspace(MIN_DECAY, MAX_DECAY, dh, dtype=F32))[None]
    nct = dh // LANES
    const = lambda shape: pl.BlockSpec(shape, lambda o, c: (0, 0))
    o_spec = pl.BlockSpec((1, 2, l, LANES), lambda o, c: (o, 0, 0, c))
    return pl.pallas_call(
        _filter_kernel,
        grid=(HYENA_ORDER, nct),
        in_specs=[const((l, LANES)), const((l, LANES)), const((LANES, LANES)), const((1, LANES)),
                  const((1, LANES)), const((LANES, LANES)), const((1, LANES)),
                  pl.BlockSpec((LANES, LANES), lambda o, c: (0, o * 2 * nct + c)),
                  pl.BlockSpec((LANES, LANES), lambda o, c: (0, o * 2 * nct + nct + c)),
                  pl.BlockSpec((1, LANES), lambda o, c: (0, c))],
        out_specs=[o_spec, o_spec],
        out_shape=[jax.ShapeDtypeStruct((HYENA_ORDER, 2, l, dh), F32)] * 2,
        scratch_shapes=[pltpu.VMEM((l, LANES), F32)] * 2,
        compiler_params=_cparams(("arbitrary", "arbitrary"), 56),
        name="hy_filter",
    )(feats, feats_r, w1p, row(b1), row(freq), w2p, row(b2), w3p, w3p, deltas)


def _spectrum_kernel(c_ref, s_ref, h2_ref, h2r_ref, a_ref, nyq_ref, hsin_ref, *, p):
    pos = h2_ref[0, 0]
    neg = h2r_ref[0, 0]
    is0 = lax.broadcasted_iota(jnp.int32, pos.shape, 0) == 0
    h_sum = jnp.where(is0, pos, pos + neg)
    sign = jnp.where(lax.broadcasted_iota(jnp.int32, pos.shape, 0) % 2 == 0, 1.0, -1.0)
    nyq = jnp.sum(h_sum * sign, axis=0, keepdims=True)
    a = jnp.dot(c_ref[...], h_sum.astype(BF16), preferred_element_type=F32)
    s = jnp.dot(s_ref[...], (pos - neg).astype(BF16), preferred_element_type=F32)
    wk = jnp.where(is0, 0.5 / p, 1.0 / p)
    a_ref[0, 0] = a * wk
    nyq_ref[0, 0] = jnp.broadcast_to((nyq - a[0:1]) * (0.5 / p), nyq_ref.shape[2:])
    hsin_ref[0, 0] = jnp.where(is0, 0.0, s) * wk


def _hy_spectrum(tabs, h2, h2r, p):
    cos_t, sin_f, _ = tabs
    n_ord, _, l, dh = h2.shape
    nb = l // p
    n_win = 2 * nb - 1
    h2 = h2.reshape(n_ord, 2 * nb, p, dh)
    h2r = h2r.reshape(n_ord, 2 * nb, p, dh)
    tab = pl.BlockSpec((p, p), lambda o, m: (0, 0))
    o_spec = pl.BlockSpec((1, 1, p, dh), lambda o, m: (o, m, 0, 0))
    return pl.pallas_call(
        functools.partial(_spectrum_kernel, p=p),
        grid=(n_ord, n_win),
        in_specs=[tab, tab,
                  pl.BlockSpec((1, 1, p, dh), lambda o, m: (o, m + 1, 0, 0)),
                  pl.BlockSpec((1, 1, p, dh), lambda o, m: (o, 2 * nb - 1 - m, 0, 0))],
        out_specs=[o_spec, pl.BlockSpec((1, 1, 8, dh), lambda o, m: (o, m, 0, 0)), o_spec],
        out_shape=[jax.ShapeDtypeStruct((n_ord, n_win, p, dh), F32),
                   jax.ShapeDtypeStruct((n_ord, n_win, 8, dh), F32),
                   jax.ShapeDtypeStruct((n_ord, n_win, p, dh), F32)],
        compiler_params=_cparams(("parallel", "parallel"), 48),
        name="hy_spectrum",
    )(cos_t, sin_f, h2, h2r)


HY_ROWS = 8


def _hy_block_kernel(c_ref, sf_ref, si_ref, a_ref, nyq_ref, hsin_ref, sk_ref, ug_ref, wg_ref, *rest,
                     nb, period, first):
    if first:
        uv_ref, wv_ref, of_ref, ob_ref, xc_s, xs_s, yc_s, ys_s, v_s = rest
    else:
        vb_ref, vp_ref, z_ref, o_ref, xc_s, xs_s, yc_s, ys_s = rest
    p = c_ref.shape[0]
    for j in range(nb):
        rows = slice(j * p, (j + 1) * p)
        if first:
            vj = _conv_rows(uv_ref[0, rows, :], wv_ref[...], period)
            v_s[rows, :] = vj
            xj = vj.astype(BF16)
        else:
            xj = vb_ref[0, rows, :]
        xc_s[j] = jnp.dot(c_ref[...], xj, preferred_element_type=F32)
        xs_s[j] = jnp.dot(sf_ref[...], xj, preferred_element_type=F32)

    nyq_fix = []
    for i in range(nb):
        fix = jnp.zeros((1, LANES), F32)
        for j in range(nb):
            fix = fix + xs_s[j, 0:1, :] * nyq_ref[0, i - j + (nb - 1), 0:1, :]
        nyq_fix.append(fix)

    def rows_body(r, carry):
        rs = pl.ds(pl.multiple_of(r * HY_ROWS, HY_ROWS), HY_ROWS)
        xc = [xc_s[j, rs, :] for j in range(nb)]
        xs = [xs_s[j, rs, :] for j in range(nb)]
        yc = [None] * nb
        ys = [None] * nb
        for w in range(2 * nb - 1):
            a = a_ref[0, w, rs, :]
            hsin = hsin_ref[0, w, rs, :]
            for i in range(nb):
                j = i - (w - (nb - 1))
                if 0 <= j < nb:
                    tc = xc[j] * a - xs[j] * hsin
                    ts = xc[j] * hsin + xs[j] * a
                    yc[i] = tc if yc[i] is None else yc[i] + tc
                    ys[i] = ts if ys[i] is None else ys[i] + ts
        row0 = (lax.broadcasted_iota(jnp.int32, (HY_ROWS, LANES), 0) + r * HY_ROWS) == 0
        for i in range(nb):
            yc_s[i, rs, :] = yc[i]
            ys_s[i, rs, :] = ys[i] + jnp.where(row0, nyq_fix[i], 0.0)
        return carry

    lax.fori_loop(0, p // HY_ROWS, rows_body, 0)
    for i in range(nb):
        rows = slice(i * p, (i + 1) * p)
        conv = jnp.dot(c_ref[...], yc_s[i].astype(BF16), preferred_element_type=F32)
        conv = conv + jnp.dot(si_ref[...], ys_s[i].astype(BF16), preferred_element_type=F32)
        xg = _conv_rows(ug_ref[0, rows, :], wg_ref[...], period)
        if first:
            y = xg * (conv + sk_ref[0] * v_s[rows, :])
            of_ref[0, rows, :] = y
            ob_ref[0, rows, :] = y.astype(BF16)
        else:
            y = xg * (conv + sk_ref[0] * vp_ref[0, rows, :])
            o_ref[0, rows, :] = (y * _silu(z_ref[0, rows, :])).astype(BF16)


def _hy_block(tabs, spec, u, conv_w, skip, order, p, dh, period, y_prev=None):
    b, l, _ = u.shape
    nb = l // p
    n_win = 2 * nb - 1
    nct = dh // LANES
    taps = conv_w.shape[0]
    first = y_prev is None
    tab = pl.BlockSpec((p, p), lambda c, i: (0, 0))
    filt = pl.BlockSpec((1, n_win, p, LANES), lambda c, i: (order, 0, 0, c))
    nyq = pl.BlockSpec((1, n_win, 8, LANES), lambda c, i: (order, 0, 0, c))
    u_cols = lambda k: pl.BlockSpec((1, l, LANES), lambda c, i, k=k: (i, 0, k * nct + c))
    w_cols = lambda k: pl.BlockSpec((taps, LANES), lambda c, i, k=k: (0, k * nct + c))
    col = pl.BlockSpec((1, l, LANES), lambda c, i: (i, 0, c))
    in_specs = [tab, tab, tab, filt, nyq, filt, pl.BlockSpec((1, 1, LANES), lambda c, i: (order, 0, c)),
                u_cols(order), w_cols(order)]
    args = [*tabs, *spec, skip[:, None, :], u, conv_w]
    scratch = [pltpu.VMEM((nb, p, LANES), F32)] * 4
    if first:
        in_specs += [u_cols(2), w_cols(2)]
        args += [u, conv_w]
        out_specs = [col, col]
        out_shape = [jax.ShapeDtypeStruct((b, l, dh), F32), jax.ShapeDtypeStruct((b, l, dh), BF16)]
        scratch.append(pltpu.VMEM((l, LANES), F32))
    else:
        in_specs += [col, col, u_cols(3)]
        args += [y_prev[1], y_prev[0], u]
        out_specs = col
        out_shape = jax.ShapeDtypeStruct((b, l, dh), BF16)
    return pl.pallas_call(
        functools.partial(_hy_block_kernel, nb=nb, period=period, first=first),
        grid=(nct, b),
        in_specs=in_specs, out_specs=out_specs, out_shape=out_shape,
        scratch_shapes=scratch,
        compiler_params=_cparams(("parallel", "parallel"), 56),
        name="hy_block",
    )(*args)


def _hyena(u, conv_w, w1, b1, freq, w2, b2, w3, skip, lay, grid_mask):
    assert HYENA_ORDER == 2
    dh = lay["DH"]
    l = u.shape[1]
    p = min(HY_BLOCK, l)
    period = CHUNK if grid_mask else l
    assert p % period == 0
    tabs = _dft_tables(p)
    spec = _hy_spectrum(tabs, *_hy_filters(l, w1, b1, freq, w2, b2, w3, dh), p)
    y1 = _hy_block(tabs, spec, u, conv_w, skip, 0, p, dh, period)
    return _hy_block(tabs, spec, u, conv_w, skip, 1, p, dh, period, y_prev=y1)


def _layout(d):
    dg, dh, dm = 3 * d // 8, d // 4, 3 * d // 8
    lay = {"DG": dg, "DH": dh, "DM": dm, "gh": dg // HEAD_DIM, "mh": dm // HEAD_DIM}
    lay["HZ"] = 3 * dh
    lay["GQ"] = 4 * dh
    lay["GZ"] = lay["GQ"] + 3 * dg
    lay["GAB"] = lay["GZ"] + dg
    lay["MQ"] = lay["GAB"] + LANES
    lay["MO"] = lay["MQ"] + 3 * dm
    lay["MZ"] = lay["MO"] + dm
    lay["MG"] = lay["MZ"] + dm
    lay["NP"] = lay["MG"] + LANES
    return lay


def _pack_w_in(w, lay):
    dg, dh, dm, gh, mh = lay["DG"], lay["DH"], lay["DM"], lay["gh"], lay["mh"]
    sizes = (3 * dg, dg, 4 * gh, 3 * dh, dh, 3 * dm, dm, dm, 4 * mh)
    offs = [0]
    for s in sizes:
        offs.append(offs[-1] + s)
    seg = [w[:, offs[i]:offs[i + 1]] for i in range(len(sizes))]
    g_qkv, g_z, g_ab, h_p, h_z, m_qkv, m_o, m_z, m_g = seg
    padl = lambda a: jnp.pad(a, ((0, 0), (0, LANES - a.shape[1])))
    return jnp.concatenate([h_p, h_z, g_qkv, g_z, padl(g_ab), m_qkv, m_o, m_z, padl(m_g)], axis=1).astype(BF16)


def _gate_rows(u, off, n):
    b, l, _ = u.shape
    return jnp.transpose(u[:, :, off:off + n], (0, 2, 1)).reshape(b, n, l // LANES, LANES)


def kernel(x, c, ctx, c_ctx, norm_w, mod_w, mod_b, w_in, gdn_conv, gdn_a_log, gdn_dt_bias, gdn_norm, hy_conv,
           hy_w1, hy_b1, hy_freq, hy_w2, hy_b2, hy_w3, hy_skip, ml_gate_bias, ml_norm, w_out, final_norm):
    b, l, d = x.shape
    lc = ctx.shape[1]
    depth = norm_w.shape[0]
    lay = _layout(d)
    dg, dh = lay["DG"], lay["DH"]
    assert b < COND_ROWS and l % (2 * LANES) == 0 and lc % (2 * LANES) == 0 and d % 1024 == 0
    assert l % min(HY_BLOCK, l) == 0
    cond = jnp.zeros((COND_ROWS, d), F32).at[:b].set(c).at[b].set(c_ctx)
    for layer in range(depth):
        last = layer == depth - 1
        mods = _adaln(cond, mod_w[layer], mod_b[layer][None])
        sh, sc, gt = mods[:, :d], mods[:, d:2 * d], mods[:, 2 * d:]
        lat = lambda m: m[:b, None, :]
        cx = lambda m: jnp.broadcast_to(m[b][None, None, :], (b, 1, d))
        wp = _pack_w_in(w_in[layer], lay)
        nw = norm_w[layer][None]
        u_l = _inproj(x, nw, lat(sc), lat(sh), wp)
        u_c = _inproj(ctx, nw, cx(sc), cx(sh), wp)
        g_rows = lambda u, off, n: _gate_rows(u, off, n)
        yg_l, yg_c = _gdn(u_l, u_c, g_rows(u_l, lay["GAB"], 4 * lay["gh"]), g_rows(u_c, lay["GAB"], 4 * lay["gh"]),
                          gdn_conv[layer], gdn_a_log[layer], gdn_dt_bias[layer], gdn_norm[layer][None], lay,
                          not last)
        ym_l, ym_c = _mlstm(u_l, u_c, g_rows(u_l, lay["MG"], 4 * lay["mh"]), g_rows(u_c, lay["MG"], 4 * lay["mh"]),
                            ml_gate_bias[layer], ml_norm[layer][None], lay, not last)
        hy = (hy_conv[layer], hy_w1[layer], hy_b1[layer], hy_freq[layer], hy_w2[layer], hy_b2[layer],
              hy_w3[layer], hy_skip[layer])
        yh_l = _hyena(u_l, *hy, lay, True)
        wo = w_out[layer].astype(BF16)
        wg, wh, wm = wo[:dg], wo[dg:dg + dh], wo[dg + dh:]
        fw = final_norm[None]
        x = _outproj(x, yg_l, yh_l, ym_l, wg, wh, wm, lat(gt), fw, last)
        if not last:
            yh_c = _hyena(u_c, *hy, lay, False)
            ctx = _outproj(ctx, yg_c, yh_c, ym_c, wg, wh, wm, cx(gt), fw, False)
    return x
```

```python
import functools
import math

import jax
import jax.numpy as jnp
from jax import lax
from jax.experimental import pallas as pl
from jax.experimental.pallas import tpu as pltpu

HEAD_DIM = 128
CHUNK = 64
LANES = 128
NORM_EPS = 1e-6
HYENA_ORDER = 2
HY_BLOCK = 512
FILTER_BANDS = 16
DECAY_TARGET = 1e-2
MIN_DECAY = math.log(DECAY_TARGET) / 1.5
MAX_DECAY = math.log(DECAY_TARGET) / 0.3
COND_ROWS = 16

F32 = jnp.float32
BF16 = jnp.bfloat16
HI = lax.Precision.HIGHEST


def _cparams(sem, vmem_mb):
    return pltpu.CompilerParams(dimension_semantics=sem, vmem_limit_bytes=vmem_mb << 20)


def _dot(a, b):
    return jnp.dot(a.astype(BF16), b.astype(BF16), preferred_element_type=F32)


def _dot_hi(a, b):
    return jnp.dot(a, b, precision=HI, preferred_element_type=F32)


def _dot_nt(a, b):
    return lax.dot_general(a.astype(BF16), b.astype(BF16), (((1,), (1,)), ((), ())),
                           preferred_element_type=F32)


def _dot_tn(a, b):
    return lax.dot_general(a.astype(BF16), b.astype(BF16), (((0,), (0,)), ((), ())),
                           preferred_element_type=F32)


def _silu(x):
    return x * jax.nn.sigmoid(x)


def _softplus(x):
    return jnp.maximum(x, 0.0) + jnp.log(1.0 + jnp.exp(-jnp.abs(x)))


def _pick_tile(n, cap, unit):
    t = (min(n, cap) // unit) * unit
    while n % t:
        t -= unit
    return t


def _adaln_kernel(c_ref, w_ref, b_ref, o_ref):
    o_ref[0] = _dot_hi(_silu(c_ref[...]), w_ref[0]) + b_ref[0]


def _adaln(cond, w, b):
    depth, d, n = w.shape
    tn = _pick_tile(n, 768, LANES)
    return pl.pallas_call(
        _adaln_kernel,
        grid=(depth, n // tn),
        in_specs=[pl.BlockSpec((COND_ROWS, d), lambda i, j: (0, 0)),
                  pl.BlockSpec((1, d, tn), lambda i, j: (i, 0, j)),
                  pl.BlockSpec((1, 1, tn), lambda i, j: (i, 0, j))],
        out_specs=pl.BlockSpec((1, COND_ROWS, tn), lambda i, j: (i, 0, j)),
        out_shape=jax.ShapeDtypeStruct((depth, COND_ROWS, n), F32),
        compiler_params=_cparams(("parallel", "parallel"), 40),
        name="adaln",
    )(cond, w, b)


INPROJ_NORM_ROWS = 256


def _inproj_kernel(x_ref, nw_ref, sc_ref, sh_ref, w_ref, o_ref, xn_ref):
    first = pl.program_id(2) == 0

    @pl.when(first)
    def _():
        rows = _pick_tile(xn_ref.shape[0], INPROJ_NORM_ROWS, 8)
        for s in range(xn_ref.shape[0] // rows):
            sl = slice(s * rows, (s + 1) * rows)
            x = x_ref[0, sl, :]
            r = lax.rsqrt(jnp.mean(x * x, axis=-1, keepdims=True) + NORM_EPS)
            y = (x * r * nw_ref[...]) * (1.0 + sc_ref[0]) + sh_ref[0]
            y = y.astype(BF16)
            xn_ref[sl, :] = y
            o_ref[0, sl, :] = jnp.dot(y, w_ref[...], preferred_element_type=F32)

    @pl.when(jnp.logical_not(first))
    def _():
        o_ref[0] = jnp.dot(xn_ref[...], w_ref[...], preferred_element_type=F32)


def _inproj(x, nw, sc, sh, wp):
    b, l, d = x.shape
    n = wp.shape[1]
    tm = _pick_tile(l, 1024, 8)
    tn = _pick_tile(n, 1280, LANES)
    return pl.pallas_call(
        _inproj_kernel,
        grid=(b, l // tm, n // tn),
        in_specs=[pl.BlockSpec((1, tm, d), lambda i, m, j: (i, m, 0)),
                  pl.BlockSpec((1, d), lambda i, m, j: (0, 0)),
                  pl.BlockSpec((1, 1, d), lambda i, m, j: (i, 0, 0)),
                  pl.BlockSpec((1, 1, d), lambda i, m, j: (i, 0, 0)),
                  pl.BlockSpec((d, tn), lambda i, m, j: (0, j))],
        out_specs=pl.BlockSpec((1, tm, tn), lambda i, m, j: (i, m, j)),
        out_shape=jax.ShapeDtypeStruct((b, l, n), F32),
        scratch_shapes=[pltpu.VMEM((tm, d), BF16)],
        compiler_params=_cparams(("parallel", "parallel", "arbitrary"), 56),
        name="inproj",
    )(x, nw, sc, sh, wp)


def _outproj_kernel(x_ref, yg_ref, yh_ref, ym_ref, wg_ref, wh_ref, wm_ref, gt_ref, fw_ref, o_ref, *, final):
    acc = jnp.dot(yg_ref[0], wg_ref[...], preferred_element_type=F32)
    acc = acc + jnp.dot(yh_ref[0], wh_ref[...], preferred_element_type=F32)
    acc = acc + jnp.dot(ym_ref[0], wm_ref[...], preferred_element_type=F32)
    xn = x_ref[0] + gt_ref[0] * acc
    if final:
        r = lax.rsqrt(jnp.mean(xn * xn, axis=-1, keepdims=True) + NORM_EPS)
        xn = xn * r * fw_ref[...]
    o_ref[0] = xn


def _outproj(x, yg, yh, ym, wg, wh, wm, gt, fw, final):
    b, l, d = x.shape
    tm = _pick_tile(l, 512, 8)
    row = lambda w: pl.BlockSpec((1, tm, w), lambda i, m: (i, m, 0))
    full = lambda a: pl.BlockSpec(a.shape, lambda i, m: (0, 0))
    return pl.pallas_call(
        functools.partial(_outproj_kernel, final=final),
        grid=(b, l // tm),
        in_specs=[row(d), row(yg.shape[2]), row(yh.shape[2]), row(ym.shape[2]),
                  full(wg), full(wh), full(wm),
                  pl.BlockSpec((1, 1, d), lambda i, m: (i, 0, 0)),
                  pl.BlockSpec((1, d), lambda i, m: (0, 0))],
        out_specs=row(d),
        out_shape=jax.ShapeDtypeStruct((b, l, d), F32),
        compiler_params=_cparams(("parallel", "parallel"), 48),
        name="outproj",
    )(x, yg, yh, ym, wg, wh, wm, gt, fw)


def _conv_rows(x, w, period):
    rows = x.shape[0]
    taps = w.shape[0]
    pad = taps // 2
    pos = lax.broadcasted_iota(jnp.int32, x.shape, 0) % period
    y = None
    for j in range(taps):
        off = j - pad
        if off == 0:
            term = x * w[j:j + 1]
        else:
            shifted = pltpu.roll(x, (-off) % rows, axis=0)
            ok = (pos >= -off) if off < 0 else (pos < period - off)
            term = jnp.where(ok, shifted, 0.0) * w[j:j + 1]
        y = term if y is None else y + term
    return y


def _seg_cumsum(x, reverse):
    lane = lax.broadcasted_iota(jnp.int32, x.shape, 1) % CHUNK
    s = 1
    while s < CHUNK:
        if reverse:
            shifted = pltpu.roll(x, LANES - s, axis=1)
            ok = lane < CHUNK - s
        else:
            shifted = pltpu.roll(x, s, axis=1)
            ok = lane >= s
        x = x + jnp.where(ok, shifted, 0.0)
        s *= 2
    return x


def _chunk_masks(d):
    ii = lax.broadcasted_iota(jnp.int32, (CHUNK, CHUNK), 0)
    jj = lax.broadcasted_iota(jnp.int32, (CHUNK, CHUNK), 1)
    eye = ii == jj
    if d == 0:
        return eye, jj <= ii, jj < ii
    return eye, jj >= ii, jj > ii


def _pair_masks(d):
    ii = lax.broadcasted_iota(jnp.int32, (LANES, LANES), 0)
    jj = lax.broadcasted_iota(jnp.int32, (LANES, LANES), 1)
    lo = (ii // CHUNK) * CHUNK
    eye = ii == jj
    if d == 0:
        return eye, (jj >= lo) & (jj <= ii), (jj >= lo) & (jj < ii)
    return eye, (jj < lo + CHUNK) & (jj >= ii), (jj < lo + CHUNK) & (jj > ii)


def _to_col(row, eye):
    return jnp.sum(jnp.where(eye, jnp.broadcast_to(row, eye.shape), 0.0), axis=1, keepdims=True)


def _split_bf16(a):
    hi = a.astype(BF16)
    return hi, (a - hi.astype(F32)).astype(BF16)


def _col_bcast(row, eye):
    x = jnp.where(eye, jnp.broadcast_to(row, eye.shape), 0.0)
    hi = x.astype(BF16)
    mid, lo = _split_bf16(x - hi.astype(F32))
    ones = jnp.ones((3 * eye.shape[1], LANES), BF16)
    return jnp.dot(jnp.concatenate([hi, mid, lo], axis=1), ones, preferred_element_type=F32)


def _dot3(a, b):
    a_hi, a_lo = _split_bf16(a)
    b_hi, b_lo = _split_bf16(b)
    return jnp.dot(jnp.concatenate([a_hi, a_hi, a_lo], axis=1), jnp.concatenate([b_hi, b_lo, b_hi], axis=0),
                   preferred_element_type=F32)


def _row_slice(rows_ref, idx, p, half):
    return rows_ref[idx, pl.ds(p, 1), half * CHUNK:(half + 1) * CHUNK]


def _head_out_tiles(acc_refs, base, nw_ref, z_ref, y_ref, n_rows, og_ref=None):
    tile = _pick_tile(n_rows, 256, 8)

    def body(i, carry):
        r = pl.multiple_of(i * tile, tile)
        o = acc_refs[0][pl.ds(base + r, tile), :]
        for acc_ref in acc_refs[1:]:
            o = o + acc_ref[pl.ds(base + r, tile), :]
        if og_ref is not None:
            o = jax.nn.sigmoid(og_ref[0, pl.ds(r, tile), :]) * o
        o = o * lax.rsqrt(jnp.mean(o * o, axis=-1, keepdims=True) + NORM_EPS) * nw_ref[...]
        y_ref[0, pl.ds(r, tile), :] = (o * _silu(z_ref[0, pl.ds(r, tile), :])).astype(y_ref.dtype)
        return carry

    lax.fori_loop(0, n_rows // tile, body, 0)


def _ml_rows(g_refs, bias_ref, h, n_heads, rows_ref):
    li_f = g_refs[0][0, 0] + bias_ref[h]
    lf_f = -_softplus(-(g_refs[1][0, 0] + bias_ref[n_heads + h]))
    li_b = g_refs[2][0, 0] + bias_ref[2 * n_heads + h]
    lf_b = -_softplus(-(g_refs[3][0, 0] + bias_ref[3 * n_heads + h]))
    rows_ref[0] = _seg_cumsum(lf_f, False)
    rows_ref[1] = li_f
    rows_ref[2] = _seg_cumsum(lf_b, True)
    rows_ref[3] = li_b


def _ml_prep_cols(q_ref, k_ref, rows_ref, chains):
    eye = _pair_masks(0)[0]
    lane_i = lax.broadcasted_iota(jnp.int32, (1, LANES), 1)
    st = []
    for d, p in chains:
        rows = pl.ds(pl.multiple_of(p * LANES, LANES), LANES)
        k = k_ref[0, rows, :] * (HEAD_DIM ** -0.5)
        b_row = rows_ref[2 * d, pl.ds(p, 1), :]
        li_row = rows_ref[2 * d + 1, pl.ds(p, 1), :]
        last = CHUNK - 1 if d == 0 else 0
        b_tot = (b_row[:, last:last + 1], b_row[:, CHUNK + last:CHUNK + last + 1])
        end_row = jnp.where(lane_i < CHUNK, b_tot[0], b_tot[1]) - b_row + li_row
        e_max = (jnp.max(end_row[:, :CHUNK], axis=1, keepdims=True),
                 jnp.max(end_row[:, CHUNK:], axis=1, keepdims=True))
        st.append(dict(d=d, rows=rows, k=k, b_row=b_row, li_row=li_row, b_tot=b_tot, e_max=e_max,
                       b_cb=_col_bcast(b_row, eye), end_cb=_col_bcast(end_row, eye),
                       qk=_dot_nt(q_ref[0, rows, :], k)))
    return st


def _ml_prep_intra(st):
    for c in st:
        _, incl, _ = _pair_masks(c["d"])
        dlog = jnp.where(incl, c["b_cb"] - c["b_row"] + c["li_row"], -jnp.inf)
        c["rowmax"] = jnp.max(dlog, axis=1, keepdims=True)
        c["p_hi"], c["p_lo"] = _split_bf16(jnp.exp(dlog - c["rowmax"]) * c["qk"])


def _ml_prep_state(st):
    row_i = lax.broadcasted_iota(jnp.int32, (LANES, 1), 0)
    lane_sq = lax.broadcasted_iota(jnp.int32, (LANES, LANES), 1)
    out = []
    for c in st:
        kw_t = (c["k"] * jnp.exp(c["end_cb"] - jnp.where(row_i < CHUNK, c["e_max"][0], c["e_max"][1]))).T
        kw_t2 = jnp.concatenate([jnp.where(lane_sq < CHUNK, kw_t, 0.0), jnp.where(lane_sq < CHUNK, 0.0, kw_t)],
                                axis=0).astype(BF16)
        out.append((c["b_cb"], c["rowmax"], c["p_hi"], c["p_lo"], kw_t2, *c["b_tot"], *c["e_max"]))
    return tuple(out)


def _ml_steps_state(q_ref, v_ref, cnst, pre, chains):
    ones = jnp.ones((LANES, LANES), BF16)
    st = []
    for d, p, half in chains:
        _, _, p_hi, p_lo, kw_t2 = pre[d][:5]
        sl = slice(half * CHUNK, (half + 1) * CHUNK)
        v_ones = jnp.concatenate([v_ref[0, pl.ds(pl.multiple_of(p * LANES, LANES), LANES), :].astype(BF16), ones],
                                 axis=1)
        rhs = jnp.concatenate([v_ones, jnp.concatenate([jnp.zeros_like(ones), ones], axis=1)], axis=0)
        pv_ps = jnp.dot(jnp.concatenate([p_hi[sl], p_lo[sl]], axis=1), rhs, preferred_element_type=F32)
        d_cn = jnp.dot(kw_t2[half * LANES:(half + 1) * LANES], v_ones, preferred_element_type=F32)
        st.append((pv_ps, d_cn))
    for n, (d, p, half) in enumerate(chains):
        r0 = pl.ds(pl.multiple_of(p * LANES + half * CHUNK, CHUNK), CHUNK)
        cn = cnst[d]
        st[n] = (r0, cn, _dot(q_ref[0, r0, :], cn), *st[n])
    return st


def _ml_steps_update(pre, acc_refs, cnst, mst, chains, st):
    for (d, p, half), (r0, cn, q_cn, pv_ps, d_cn) in zip(chains, st):
        b_cb, rowmax = pre[d][:2]
        b_tot = pre[d][5 + half]
        e_max = pre[d][7 + half]
        sl = slice(half * CHUNK, (half + 1) * CHUNK)
        m_s = mst[d, 0:1, 0:1]
        inter = b_cb[sl] + m_s
        m_i = jnp.maximum(inter, rowmax[sl])
        w_inter = jnp.exp(inter - m_i)
        s_intra = jnp.exp(rowmax[sl] - m_i)
        num = w_inter * q_cn[:, :HEAD_DIM] + s_intra * pv_ps[:, :HEAD_DIM]
        den = w_inter * q_cn[:, HEAD_DIM:] + s_intra * pv_ps[:, HEAD_DIM:]
        acc_refs[d][r0, :] = num / jnp.maximum(jnp.abs(den), jnp.exp(-m_i))
        carry_log = b_tot + m_s
        m_new = jnp.maximum(carry_log, e_max)
        cnst[d] = jnp.exp(carry_log - m_new) * cn + jnp.exp(e_max - m_new) * d_cn
        mst[d] = jnp.broadcast_to(m_new, mst.shape[1:])


def _ml_scan(q_ref, k_ref, v_ref, rows_ref, acc_refs, cnst, mst, n_pairs):
    def chains_of(t):
        return [(0, t), (1, n_pairs - 1 - t), (0, t + 1), (1, n_pairs - 2 - t)]

    def prep(chains):
        st = _ml_prep_cols(q_ref, k_ref, rows_ref, chains)
        _ml_prep_intra(st)
        return _ml_prep_state(st)

    def body(i, pre):
        t = 2 * i
        st = _ml_prep_cols(q_ref, k_ref, rows_ref, chains_of(jnp.minimum(t + 2, n_pairs - 2)))
        nxt = None
        for j, tt in enumerate((t, t + 1)):
            pf, pb = tt, n_pairs - 1 - tt
            pre_j = pre[2 * j:2 * j + 2]
            for halves in ([(0, pf, 0), (1, pb, 1)], [(0, pf, 1), (1, pb, 0)]):
                held = _ml_steps_state(q_ref, v_ref, cnst, pre_j, halves)
                if j == 0 and halves[0][2] == 0:
                    _ml_prep_intra(st)
                elif j == 0:
                    nxt = _ml_prep_state(st)
                _ml_steps_update(pre_j, acc_refs, cnst, mst, halves, held)
        return nxt

    lax.fori_loop(0, n_pairs // 2, body, prep(chains_of(0)))


def _mlstm_kernel(bias_ref, ql, kl, vl, ol, zl, qc, kc, vc, oc, zc, gl0, gl1, gl2, gl3, gc0, gc1, gc2, gc3,
                  nw_ref, *rest, n_heads, ctx_out):
    if ctx_out:
        yl_ref, yc_ref = rest[:2]
        rest = rest[2:]
    else:
        yl_ref, yc_ref = rest[0], None
        rest = rest[1:]
    cnst, mst, rows_l, rows_c, acc_lf, acc_lb, acc_cf, acc_cb = rest
    h = pl.program_id(1)
    cnst[...] = jnp.zeros(cnst.shape, F32)
    mst[...] = jnp.zeros(mst.shape, F32)
    _ml_rows((gc0, gc1, gc2, gc3), bias_ref, h, n_heads, rows_c)
    _ml_scan(qc, kc, vc, rows_c, (acc_cf, acc_cb), cnst, mst, rows_c.shape[1])
    _ml_rows((gl0, gl1, gl2, gl3), bias_ref, h, n_heads, rows_l)
    _ml_scan(ql, kl, vl, rows_l, (acc_lf, acc_lb), cnst, mst, rows_l.shape[1])
    _head_out_tiles((acc_lf, acc_lb), 0, nw_ref, zl, yl_ref, acc_lf.shape[0], og_ref=ol)
    if ctx_out:
        _head_out_tiles((acc_cf, acc_cb), 0, nw_ref, zc, yc_ref, acc_cf.shape[0], og_ref=oc)


def _head_block(l, blk):
    return pl.BlockSpec((1, l, HEAD_DIM), lambda b, h, blk=blk: (b, 0, blk + h))


def _gate_block(n_pairs, j, n_heads):
    return pl.BlockSpec((1, 1, n_pairs, LANES), lambda b, h, j=j: (b, j * n_heads + h, 0, 0))


def _mlstm(u_l, u_c, gt_l, gt_c, bias, nw, lay, ctx_out):
    b, l, _ = u_l.shape
    lc = u_c.shape[1]
    nh = lay["mh"]
    blk = lambda off: off // HEAD_DIM
    offs = [lay["MQ"], lay["MQ"] + lay["DM"], lay["MQ"] + 2 * lay["DM"], lay["MO"], lay["MZ"]]
    in_specs = [pl.BlockSpec(memory_space=pltpu.SMEM)]
    in_specs += [_head_block(l, blk(o)) for o in offs]
    in_specs += [_head_block(lc, blk(o)) for o in offs]
    in_specs += [_gate_block(l // LANES, j, nh) for j in range(4)]
    in_specs += [_gate_block(lc // LANES, j, nh) for j in range(4)]
    in_specs += [pl.BlockSpec((1, HEAD_DIM), lambda b_, h: (0, 0))]
    out_specs = [pl.BlockSpec((1, l, HEAD_DIM), lambda b_, h: (b_, 0, h))]
    out_shape = [jax.ShapeDtypeStruct((b, l, nh * HEAD_DIM), BF16)]
    if ctx_out:
        out_specs.append(pl.BlockSpec((1, lc, HEAD_DIM), lambda b_, h: (b_, 0, h)))
        out_shape.append(jax.ShapeDtypeStruct((b, lc, nh * HEAD_DIM), BF16))
    scratch = [pltpu.VMEM((2, HEAD_DIM, 2 * HEAD_DIM), F32), pltpu.VMEM((2, 8, LANES), F32),
               pltpu.VMEM((4, l // LANES, LANES), F32), pltpu.VMEM((4, lc // LANES, LANES), F32),
               pltpu.VMEM((l, HEAD_DIM), F32), pltpu.VMEM((l, HEAD_DIM), F32),
               pltpu.VMEM((lc, HEAD_DIM), F32), pltpu.VMEM((lc, HEAD_DIM), F32)]
    res = pl.pallas_call(
        functools.partial(_mlstm_kernel, n_heads=nh, ctx_out=ctx_out),
        grid=(b, nh),
        in_specs=in_specs, out_specs=out_specs, out_shape=out_shape,
        scratch_shapes=scratch,
        compiler_params=_cparams(("parallel", "parallel"), 48),
        name="mlstm",
    )(bias.reshape(-1), *([u_l] * 5), *([u_c] * 5), *([gt_l] * 4), *([gt_c] * 4), nw)
    return (res[0], res[1]) if ctx_out else (res[0], None)


def _gdn_rows(g_refs, alog_ref, dtb_ref, h, n_heads, rows_ref, base):
    n = g_refs[0].shape[2]
    g_f = -jnp.exp(alog_ref[h]) * _softplus(g_refs[0][0, 0] + dtb_ref[h])
    g_b = -jnp.exp(alog_ref[n_heads + h]) * _softplus(g_refs[1][0, 0] + dtb_ref[n_heads + h])
    rows_ref[0, base:base + n] = _seg_cumsum(g_f, False)
    rows_ref[1, base:base + n] = jax.nn.sigmoid(g_refs[2][0, 0])
    rows_ref[2, base:base + n] = _seg_cumsum(g_b, True)
    rows_ref[3, base:base + n] = jax.nn.sigmoid(g_refs[3][0, 0])


def _masked_taps(w, period):
    taps = w.shape[0]
    pad = taps // 2
    pos = lax.broadcasted_iota(jnp.int32, (period, w.shape[1]), 0)
    out = []
    for j in range(taps):
        off = j - pad
        ok = (pos >= -off) if off < 0 else (pos < period - off)
        out.append(jnp.where(ok, w[j:j + 1], 0.0))
    return out


def _conv_rows_pre(x, wm_ref, first):
    rows = x.shape[0]
    taps = wm_ref.shape[0] // 3
    period = wm_ref.shape[1]
    pad = taps // 2
    y = None
    for j in range(taps):
        off = j - pad
        shifted = x if off == 0 else pltpu.roll(x, (-off) % rows, axis=0)
        term = shifted * jnp.concatenate([wm_ref[first + j]] * (rows // period), axis=0)
        y = term if y is None else y + term
    return y


def _gdn_conv(src_refs, w_refs, dst_refs, base, n_units, unit_rows, period, wm_ref=None):
    taps = w_refs[0].shape[0]
    if wm_ref is not None:
        for idx in range(3):
            for j, wm in enumerate(_masked_taps(w_refs[idx][...], period)):
                wm_ref[idx * taps + j] = wm

    def body(i, carry):
        r = pl.multiple_of(i * unit_rows, unit_rows)
        for idx in range(3):
            x = src_refs[idx][0, pl.ds(r, unit_rows), :]
            if wm_ref is not None:
                t = _silu(_conv_rows_pre(x, wm_ref, idx * taps))
            else:
                t = _silu(_conv_rows(x, w_refs[idx][...], period))
            if idx < 2:
                t = t * lax.rsqrt(jnp.sum(t * t, axis=-1, keepdims=True) + NORM_EPS)
            if idx == 0:
                t = t * (HEAD_DIM ** -0.5)
            dst_refs[idx][pl.ds(base + r, unit_rows), :] = t
        return carry

    lax.fori_loop(0, n_units, body, 0)


N_NEUMANN = CHUNK.bit_length() - 2


def _side_masks(d):
    ii = lax.broadcasted_iota(jnp.int32, (CHUNK, LANES), 0)
    lane = lax.broadcasted_iota(jnp.int32, (CHUNK, LANES), 1)
    jj = lane % CHUNK
    first = lane < CHUNK
    if d == 0:
        return first, jj == ii, jj <= ii, jj < ii
    return first, jj == ii, jj >= ii, jj > ii


def _side(x, first):
    return jnp.where(first, x[:CHUNK], x[CHUNK:])


def _block_diag(x, first):
    zero = jnp.zeros_like(x)
    return jnp.concatenate([jnp.where(first, x, zero), jnp.where(first, zero, x)], axis=0)


def _dot3_side(a, b, first):
    a_hi, a_lo = _split_bf16(a)
    b_hi, b_lo = _split_bf16(b)
    rhs = jnp.concatenate([_block_diag(b_hi, first), _block_diag(b_lo, first), _block_diag(b_hi, first)], axis=0)
    return jnp.dot(jnp.concatenate([a_hi, a_hi, a_lo], axis=1), rhs, preferred_element_type=F32)


def _gdn_gates(rows_ref, d, p):
    eye = _pair_masks(0)[0]
    g_row = rows_ref[2 * d, pl.ds(p, 1), :]
    return dict(d=d, p=p, rows=pl.ds(pl.multiple_of(p * LANES, LANES), LANES), g_row=g_row,
                g_col=_to_col(g_row, eye), beta_col=_to_col(rows_ref[2 * d + 1, pl.ds(p, 1), :], eye))


def _gdn_prep_start(qs, ks, rows_ref, chains):
    st = []
    for d, p in chains:
        c = _gdn_gates(rows_ref, d, p)
        rows, g_row, g_col, beta_col = c["rows"], c["g_row"], c["g_col"], c["beta_col"]
        k = ks[rows, :]
        first, eye_s, incl, strict = _side_masks(d)
        dec = jnp.exp(jnp.where(incl, _side(g_col, first) - g_row, -jnp.inf))
        aqk = _block_diag(_side(_dot_nt(qs[rows, :], k), first) * dec, first).astype(BF16)
        m_low = jnp.where(strict, _side(beta_col, first) * _side(_dot_nt(k, k), first) * dec, 0.0)
        st.append(dict(c, aqk=aqk, pw=m_low, t_inv=jnp.where(eye_s, 1.0, 0.0) - m_low))
    return st


def _gdn_prep_neumann(st, after_square=None, after_product=None):
    first, eye_s, _, _ = _side_masks(0)
    eye_f = jnp.where(eye_s, 1.0, 0.0)
    for c in st:
        c["pw"] = _dot3_side(c["pw"], c["pw"], first)
    if after_square is not None:
        after_square()
    for c in st:
        c["t_inv"] = _dot3_side(c["t_inv"], eye_f + c["pw"], first)
    if after_product is not None:
        after_product()


NQ_ROWS = HEAD_DIM + CHUNK


def _gdn_prep_solve(qs, ks, vs, st):
    row_i = lax.broadcasted_iota(jnp.int32, (LANES, 1), 0)
    first = _side_masks(0)[0]
    for c in st:
        d, rows, g_row, g_col, beta_col = c["d"], c["rows"], c["g_row"], c["g_col"], c["beta_col"]
        k = ks[rows, :]
        e_g = jnp.exp(g_col)
        c["w_ut"] = _dot3(_block_diag(c["t_inv"], first),
                          jnp.concatenate([(beta_col * e_g) * k, beta_col * vs[rows, :]], axis=1)).astype(BF16)
        c["q_dec"] = qs[rows, :] * e_g
        last = CHUNK - 1 if d == 0 else 0
        g_tot_col = jnp.where(row_i < CHUNK, g_row[:, last:last + 1], g_row[:, CHUNK + last:CHUNK + last + 1])
        c["k_end_t"] = (k * jnp.exp(g_tot_col - g_col)).T.astype(BF16)


def _gdn_prep_finish(nq_s, r_s, acc_refs, st):
    row_i = lax.broadcasted_iota(jnp.int32, (LANES, 1), 0)
    for c in st:
        d, w_ut, q_dec = c["d"], c["w_ut"], c["q_dec"]
        zero = jnp.zeros_like(w_ut)
        for half in (0, 1):
            sl = slice(half * CHUNK, (half + 1) * CHUNK)
            lhs = jnp.concatenate([c["k_end_t"], c["aqk"][sl]], axis=0)
            mine = (row_i < CHUNK) if half == 0 else (row_i >= CHUNK)
            prod = jnp.dot(lhs, jnp.where(mine, w_ut, zero), preferred_element_type=F32)
            n_q = jnp.concatenate([prod[:HEAD_DIM, :HEAD_DIM], q_dec[sl] - prod[HEAD_DIM:, :HEAD_DIM]], axis=0)
            chunk = 2 * c["p"] + half
            nq_s[d, pl.ds(pl.multiple_of(chunk * NQ_ROWS, NQ_ROWS), NQ_ROWS), :] = n_q.astype(BF16)
            r_s[d, pl.ds(pl.multiple_of(chunk * HEAD_DIM, HEAD_DIM), HEAD_DIM), :] = prod[:HEAD_DIM, HEAD_DIM:]
            acc_refs[d][pl.ds(pl.multiple_of(chunk * CHUNK, CHUNK), CHUNK), :] = prod[HEAD_DIM:, HEAD_DIM:]


def _gdn_steps_state(nq_s, sst, chains):
    st = []
    for d, p, half in chains:
        chunk = 2 * p + half
        s = sst[d]
        n_q = nq_s[d, pl.ds(pl.multiple_of(chunk * NQ_ROWS, NQ_ROWS), NQ_ROWS), :]
        st.append((chunk, s, jnp.dot(n_q, s.astype(BF16), preferred_element_type=F32)))
    return st


def _gdn_steps_update(rows_ref, r_s, sst, acc_refs, chains, st):
    for (d, p, half), (chunk, s, prod) in zip(chains, st):
        lane = half * CHUNK + (CHUNK - 1 if d == 0 else 0)
        g_tot = rows_ref[2 * d, pl.ds(p, 1), lane:lane + 1]
        r0 = pl.ds(pl.multiple_of(chunk * CHUNK, CHUNK), CHUNK)
        acc_refs[d][r0, :] = acc_refs[d][r0, :] + prod[HEAD_DIM:]
        r_c = r_s[d, pl.ds(pl.multiple_of(chunk * HEAD_DIM, HEAD_DIM), HEAD_DIM), :]
        sst[d] = jnp.exp(g_tot) * s - prod[:HEAD_DIM] + r_c


def _gdn_kernel(alog_ref, dtb_ref, ql, kl, vl, zl, qc, kc, vc, zc, gl0, gl1, gl2, gl3, gc0, gc1, gc2, gc3,
                wq, wk, wv, nw_ref, *rest, n_heads, ctx_out):
    if ctx_out:
        yl_ref, yc_ref = rest[:2]
        rest = rest[2:]
    else:
        yl_ref, yc_ref = rest[0], None
        rest = rest[1:]
    sst, rows, qs, ks, vs, nq_s, r_s, acc_f, acc_b, nm_s, aq_s, wm_s = rest
    h = pl.program_id(1)
    n_l = ql.shape[1]
    n_c = qc.shape[1]
    nl = n_l // LANES
    nc = n_c // LANES
    sst[...] = jnp.zeros(sst.shape, F32)
    _gdn_rows((gl0, gl1, gl2, gl3), alog_ref, dtb_ref, h, n_heads, rows, 0)
    _gdn_rows((gc0, gc1, gc2, gc3), alog_ref, dtb_ref, h, n_heads, rows, nl)
    conv_rows = _pick_tile(n_l, 4 * CHUNK, CHUNK)
    _gdn_conv((ql, kl, vl), (wq, wk, wv), (qs, ks, vs), 0, n_l // conv_rows, conv_rows, CHUNK, wm_ref=wm_s)
    _gdn_conv((qc, kc, vc), (wq, wk, wv), (qs, ks, vs), n_l, 1, n_c, n_c)
    n_steps = nl + nc
    accs = (acc_f, acc_b)

    def fwd_pair(t):
        return jnp.where(t < nc, nl + t, t - nc)

    def bwd_pair(t):
        return n_steps - 1 - t

    n_trips = n_steps // 2
    n_early = (N_NEUMANN + 1) // 2

    def chains_of(trip):
        t = 2 * jnp.minimum(trip, n_trips - 1)
        return [(0, fwd_pair(t)), (1, bwd_pair(t)), (0, fwd_pair(t + 1)), (1, bwd_pair(t + 1))]

    def neumann(st, half=None):
        if half is None:
            _gdn_prep_neumann(st)
            return
        held = []
        _gdn_prep_neumann(st, lambda: held.append(_gdn_steps_state(nq_s, sst, half)),
                          lambda: _gdn_steps_update(rows, r_s, sst, accs, half, held[0]))

    def finish(st):
        _gdn_prep_solve(qs, ks, vs, st)
        _gdn_prep_finish(nq_s, r_s, accs, st)

    def halves_of(trip):
        out = []
        for t in (2 * trip, 2 * trip + 1):
            pf, pb = fwd_pair(t), bwd_pair(t)
            out += [[(0, pf, 0), (1, pb, 1)], [(0, pf, 1), (1, pb, 0)]]
        return out

    st = _gdn_prep_start(qs, ks, rows, chains_of(0))
    for _ in range(N_NEUMANN):
        neumann(st)
    finish(st)
    st = _gdn_prep_start(qs, ks, rows, chains_of(1))
    for _ in range(n_early):
        neumann(st)

    def park(st_part):
        for n, c in enumerate(st_part):
            nm_s[2 * n] = c["pw"]
            nm_s[2 * n + 1] = c["t_inv"]
            aq_s[n] = c["aqk"]

    park(st)

    def trip(i, carry):
        halves = halves_of(i)
        st_late = [dict(_gdn_gates(rows, d, p), pw=nm_s[2 * n], t_inv=nm_s[2 * n + 1], aqk=aq_s[n])
                   for n, (d, p) in enumerate(chains_of(i + 1))]
        st_early = _gdn_prep_start(qs, ks, rows, chains_of(i + 2))
        n_late = N_NEUMANN - n_early
        slot = 0
        for n in range(n_early):
            neumann(st_early, halves[slot] if slot < len(halves) else None)
            slot += 1
            if n < n_late:
                neumann(st_late, halves[slot] if slot < len(halves) else None)
                slot += 1
            if n == n_late - 1:
                _gdn_prep_solve(qs, ks, vs, st_late)
        _gdn_prep_finish(nq_s, r_s, accs, st_late)
        park(st_early)
        return carry

    lax.fori_loop(0, n_trips - 1, trip, 0)
    for half in halves_of(n_trips - 1):
        _gdn_steps_update(rows, r_s, sst, accs, half, _gdn_steps_state(nq_s, sst, half))
    _head_out_tiles((acc_f, acc_b), 0, nw_ref, zl, yl_ref, n_l)
    if ctx_out:
        _head_out_tiles((acc_f, acc_b), n_l, nw_ref, zc, yc_ref, n_c)


def _gdn(u_l, u_c, gt_l, gt_c, conv_w, a_log, dt_bias, nw, lay, ctx_out):
    b, l, _ = u_l.shape
    lc = u_c.shape[1]
    nh = lay["gh"]
    taps = conv_w.shape[0]
    blk = lambda off: off // HEAD_DIM
    offs = [lay["GQ"], lay["GQ"] + lay["DG"], lay["GQ"] + 2 * lay["DG"], lay["GZ"]]
    smem = pl.BlockSpec(memory_space=pltpu.SMEM)
    in_specs = [smem, smem]
    in_specs += [_head_block(l, blk(o)) for o in offs]
    in_specs += [_head_block(lc, blk(o)) for o in offs]
    in_specs += [_gate_block(l // LANES, j, nh) for j in range(4)]
    in_specs += [_gate_block(lc // LANES, j, nh) for j in range(4)]
    in_specs += [pl.BlockSpec((taps, HEAD_DIM), lambda b_, h, j=j: (0, j * nh + h)) for j in range(3)]
    in_specs += [pl.BlockSpec((1, HEAD_DIM), lambda b_, h: (0, 0))]
    out_specs = [pl.BlockSpec((1, l, HEAD_DIM), lambda b_, h: (b_, 0, h))]
    out_shape = [jax.ShapeDtypeStruct((b, l, nh * HEAD_DIM), BF16)]
    if ctx_out:
        out_specs.append(pl.BlockSpec((1, lc, HEAD_DIM), lambda b_, h: (b_, 0, h)))
        out_shape.append(jax.ShapeDtypeStruct((b, lc, nh * HEAD_DIM), BF16))
    lt = l + lc
    seq_f32 = pltpu.VMEM((lt, HEAD_DIM), F32)
    n_chunks = lt // CHUNK
    scratch = [pltpu.VMEM((2, HEAD_DIM, HEAD_DIM), F32), pltpu.VMEM((4, lt // LANES, LANES), F32),
               seq_f32, seq_f32, seq_f32,
               pltpu.VMEM((2, n_chunks * NQ_ROWS, HEAD_DIM), BF16), pltpu.VMEM((2, n_chunks * HEAD_DIM, HEAD_DIM), F32),
               seq_f32, seq_f32, pltpu.VMEM((8, CHUNK, LANES), F32), pltpu.VMEM((4, LANES, LANES), BF16),
               pltpu.VMEM((3 * taps, CHUNK, HEAD_DIM), F32)]
    res = pl.pallas_call(
        functools.partial(_gdn_kernel, n_heads=nh, ctx_out=ctx_out),
        grid=(b, nh),
        in_specs=in_specs, out_specs=out_specs, out_shape=out_shape,
        scratch_shapes=scratch,
        compiler_params=_cparams(("parallel", "parallel"), 56),
        name="gdn",
    )(a_log.reshape(-1), dt_bias.reshape(-1), *([u_l] * 4), *([u_c] * 4), *([gt_l] * 4), *([gt_c] * 4),
      *([conv_w] * 3), nw)
    return (res[0], res[1]) if ctx_out else (res[0], None)


def _dft_tables(l):
    k = jnp.arange(l, dtype=jnp.int32)
    ang = lambda t: ((k[:, None] * t[None, :]) % (2 * l)).astype(F32) * (math.pi / l)
    ang_a = ang(jnp.arange(l // CHUNK, dtype=jnp.int32) * CHUNK)[:, :, None]
    ang_b = ang(jnp.arange(CHUNK, dtype=jnp.int32))[:, None, :]
    cos_t = (jnp.cos(ang_a) * jnp.cos(ang_b) - jnp.sin(ang_a) * jnp.sin(ang_b)).reshape(l, l)
    sin_t = (jnp.sin(ang_a) * jnp.cos(ang_b) + jnp.cos(ang_a) * jnp.sin(ang_b)).reshape(l, l)
    alt = jnp.where(k % 2 == 0, 1.0, -1.0).astype(F32)
    sin_f = sin_t.at[0, :].set(alt)
    return cos_t.astype(BF16), sin_f.astype(BF16), sin_f.T.astype(BF16)


def _filter_kernel(feats_ref, featr_ref, w1_ref, b1_ref, fr_ref, w2_ref, b2_ref, w3c_ref, w3a_ref, dl_ref,
                   h2_ref, h2r_ref, hid_ref, hidr_ref):
    @pl.when((pl.program_id(0) == 0) & (pl.program_id(1) == 0))
    def _():
        for f_ref, h_ref in ((feats_ref, hid_ref), (featr_ref, hidr_ref)):
            hid = jnp.sin(fr_ref[...] * (_dot_hi(f_ref[...], w1_ref[...]) + b1_ref[...]))
            h_ref[...] = jnp.sin(fr_ref[...] * (_dot_hi(hid, w2_ref[...]) + b2_ref[...]))

    def raw(h_ref, f_ref, w3_ref):
        return _dot3(h_ref[...], w3_ref[...]) * jnp.exp(-f_ref[:, 0:1] * dl_ref[...])

    c_f = raw(hid_ref, feats_ref, w3c_ref)
    a_f = raw(hid_ref, feats_ref, w3a_ref)
    den_c = jnp.sum(jnp.abs(c_f), axis=0, keepdims=True) + NORM_EPS
    den_a = jnp.sum(jnp.abs(a_f), axis=0, keepdims=True) + NORM_EPS
    c_f = c_f / den_c
    a_f = a_f / den_a
    row0 = lax.broadcasted_iota(jnp.int32, c_f.shape, 0) == 0
    centre = c_f[0:1] + a_f[0:1]
    h2_ref[0, 0] = jnp.where(row0, 0.0, raw(hidr_ref, featr_ref, w3a_ref) / den_a)
    h2_ref[0, 1] = jnp.where(row0, centre, c_f)
    h2r_ref[0, 0] = jnp.where(row0, 0.0, raw(hidr_ref, featr_ref, w3c_ref) / den_c)
    h2r_ref[0, 1] = jnp.where(row0, centre, a_f)


def _filter_feats(pos, l, n_emb):
    t = pos / max(l - 1, 1)
    ang = 2.0 * math.pi * pos / l
    bands = jnp.linspace(1e-4, FILTER_BANDS - 1, FILTER_BANDS, dtype=F32)
    feats = jnp.concatenate([t[:, None], jnp.cos(ang[:, None] * bands), -jnp.sin(ang[:, None] * bands)], axis=-1)
    return jnp.pad(feats, ((0, 0), (0, LANES - n_emb)))


def _hy_filters(l, w1, b1, freq, w2, b2, w3, dh):
    n_emb, n_hid = w1.shape
    pos = jnp.arange(l, dtype=F32)
    feats = _filter_feats(pos, l, n_emb)
    feats_r = _filter_feats(l - pos, l, n_emb)
    pc = LANES - n_hid
    w1p = jnp.pad(w1, ((0, LANES - n_emb), (0, pc)))
    w2p = jnp.pad(w2, ((0, pc), (0, pc)))
    w3p = jnp.pad(w3, ((0, pc), (0, 0)))
    row = lambda a: jnp.pad(a, (0, pc))[None]
    deltas = jnp.abs(jnp.linspace(MIN_DECAY, MAX_DECAY, dh, dtype=F32))[None]
    nct = dh // LANES
    const = lambda shape: pl.BlockSpec(shape, lambda o, c: (0, 0))
    o_spec = pl.BlockSpec((1, 2, l, LANES), lambda o, c: (o, 0, 0, c))
    return pl.pallas_call(
        _filter_kernel,
        grid=(HYENA_ORDER, nct),
        in_specs=[const((l, LANES)), const((l, LANES)), const((LANES, LANES)), const((1, LANES)),
                  const((1, LANES)), const((LANES, LANES)), const((1, LANES)),
                  pl.BlockSpec((LANES, LANES), lambda o, c: (0, o * 2 * nct + c)),
                  pl.BlockSpec((LANES, LANES), lambda o, c: (0, o * 2 * nct + nct + c)),
                  pl.BlockSpec((1, LANES), lambda o, c: (0, c))],
        out_specs=[o_spec, o_spec],
        out_shape=[jax.ShapeDtypeStruct((HYENA_ORDER, 2, l, dh), F32)] * 2,
        scratch_shapes=[pltpu.VMEM((l, LANES), F32)] * 2,
        compiler_params=_cparams(("arbitrary", "arbitrary"), 56),
        name="hy_filter",
    )(feats, feats_r, w1p, row(b1), row(freq), w2p, row(b2), w3p, w3p, deltas)


def _spectrum_kernel(c_ref, s_ref, h2_ref, h2r_ref, a_ref, nyq_ref, hsin_ref, *, p):
    pos = h2_ref[0, 0]
    neg = h2r_ref[0, 0]
    is0 = lax.broadcasted_iota(jnp.int32, pos.shape, 0) == 0
    h_sum = jnp.where(is0, pos, pos + neg)
    sign = jnp.where(lax.broadcasted_iota(jnp.int32, pos.shape, 0) % 2 == 0, 1.0, -1.0)
    nyq = jnp.sum(h_sum * sign, axis=0, keepdims=True)
    a = jnp.dot(c_ref[...], h_sum.astype(BF16), preferred_element_type=F32)
    s = jnp.dot(s_ref[...], (pos - neg).astype(BF16), preferred_element_type=F32)
    wk = jnp.where(is0, 0.5 / p, 1.0 / p)
    a_ref[0, 0] = a * wk
    nyq_ref[0, 0] = jnp.broadcast_to((nyq - a[0:1]) * (0.5 / p), nyq_ref.shape[2:])
    hsin_ref[0, 0] = jnp.where(is0, 0.0, s) * wk


def _hy_spectrum(tabs, h2, h2r, p):
    cos_t, sin_f, _ = tabs
    n_ord, _, l, dh = h2.shape
    nb = l // p
    n_win = 2 * nb - 1
    h2 = h2.reshape(n_ord, 2 * nb, p, dh)
    h2r = h2r.reshape(n_ord, 2 * nb, p, dh)
    tab = pl.BlockSpec((p, p), lambda o, m: (0, 0))
    o_spec = pl.BlockSpec((1, 1, p, dh), lambda o, m: (o, m, 0, 0))
    return pl.pallas_call(
        functools.partial(_spectrum_kernel, p=p),
        grid=(n_ord, n_win),
        in_specs=[tab, tab,
                  pl.BlockSpec((1, 1, p, dh), lambda o, m: (o, m + 1, 0, 0)),
                  pl.BlockSpec((1, 1, p, dh), lambda o, m: (o, 2 * nb - 1 - m, 0, 0))],
        out_specs=[o_spec, pl.BlockSpec((1, 1, 8, dh), lambda o, m: (o, m, 0, 0)), o_spec],
        out_shape=[jax.ShapeDtypeStruct((n_ord, n_win, p, dh), F32),
                   jax.ShapeDtypeStruct((n_ord, n_win, 8, dh), F32),
                   jax.ShapeDtypeStruct((n_ord, n_win, p, dh), F32)],
        compiler_params=_cparams(("parallel", "parallel"), 48),
        name="hy_spectrum",
    )(cos_t, sin_f, h2, h2r)


HY_ROWS = 8


def _hy_block_kernel(c_ref, sf_ref, si_ref, a_ref, nyq_ref, hsin_ref, sk_ref, ug_ref, wg_ref, *rest,
                     nb, period, first):
    if first:
        uv_ref, wv_ref, of_ref, ob_ref, xc_s, xs_s, yc_s, ys_s, v_s = rest
    else:
        vb_ref, vp_ref, z_ref, o_ref, xc_s, xs_s, yc_s, ys_s = rest
    p = c_ref.shape[0]
    for j in range(nb):
        rows = slice(j * p, (j + 1) * p)
        if first:
            vj = _conv_rows(uv_ref[0, rows, :], wv_ref[...], period)
            v_s[rows, :] = vj
            xj = vj.astype(BF16)
        else:
            xj = vb_ref[0, rows, :]
        xc_s[j] = jnp.dot(c_ref[...], xj, preferred_element_type=F32)
        xs_s[j] = jnp.dot(sf_ref[...], xj, preferred_element_type=F32)

    nyq_fix = []
    for i in range(nb):
        fix = jnp.zeros((1, LANES), F32)
        for j in range(nb):
            fix = fix + xs_s[j, 0:1, :] * nyq_ref[0, i - j + (nb - 1), 0:1, :]
        nyq_fix.append(fix)

    def rows_body(r, carry):
        rs = pl.ds(pl.multiple_of(r * HY_ROWS, HY_ROWS), HY_ROWS)
        xc = [xc_s[j, rs, :] for j in range(nb)]
        xs = [xs_s[j, rs, :] for j in range(nb)]
        yc = [None] * nb
        ys = [None] * nb
        for w in range(2 * nb - 1):
            a = a_ref[0, w, rs, :]
            hsin = hsin_ref[0, w, rs, :]
            for i in range(nb):
                j = i - (w - (nb - 1))
                if 0 <= j < nb:
                    tc = xc[j] * a - xs[j] * hsin
                    ts = xc[j] * hsin + xs[j] * a
                    yc[i] = tc if yc[i] is None else yc[i] + tc
                    ys[i] = ts if ys[i] is None else ys[i] + ts
        row0 = (lax.broadcasted_iota(jnp.int32, (HY_ROWS, LANES), 0) + r * HY_ROWS) == 0
        for i in range(nb):
            yc_s[i, rs, :] = yc[i]
            ys_s[i, rs, :] = ys[i] + jnp.where(row0, nyq_fix[i], 0.0)
        return carry

    lax.fori_loop(0, p // HY_ROWS, rows_body, 0)
    for i in range(nb):
        rows = slice(i * p, (i + 1) * p)
        conv = jnp.dot(c_ref[...], yc_s[i].astype(BF16), preferred_element_type=F32)
        conv = conv + jnp.dot(si_ref[...], ys_s[i].astype(BF16), preferred_element_type=F32)
        xg = _conv_rows(ug_ref[0, rows, :], wg_ref[...], period)
        if first:
            y = xg * (conv + sk_ref[0] * v_s[rows, :])
            of_ref[0, rows, :] = y
            ob_ref[0, rows, :] = y.astype(BF16)
        else:
            y = xg * (conv + sk_ref[0] * vp_ref[0, rows, :])
            o_ref[0, rows, :] = (y * _silu(z_ref[0, rows, :])).astype(BF16)


def _hy_block(tabs, spec, u, conv_w, skip, order, p, dh, period, y_prev=None):
    b, l, _ = u.shape
    nb = l // p
    n_win = 2 * nb - 1
    nct = dh // LANES
    taps = conv_w.shape[0]
    first = y_prev is None
    tab = pl.BlockSpec((p, p), lambda c, i: (0, 0))
    filt = pl.BlockSpec((1, n_win, p, LANES), lambda c, i: (order, 0, 0, c))
    nyq = pl.BlockSpec((1, n_win, 8, LANES), lambda c, i: (order, 0, 0, c))
    u_cols = lambda k: pl.BlockSpec((1, l, LANES), lambda c, i, k=k: (i, 0, k * nct + c))
    w_cols = lambda k: pl.BlockSpec((taps, LANES), lambda c, i, k=k: (0, k * nct + c))
    col = pl.BlockSpec((1, l, LANES), lambda c, i: (i, 0, c))
    in_specs = [tab, tab, tab, filt, nyq, filt, pl.BlockSpec((1, 1, LANES), lambda c, i: (order, 0, c)),
                u_cols(order), w_cols(order)]
    args = [*tabs, *spec, skip[:, None, :], u, conv_w]
    scratch = [pltpu.VMEM((nb, p, LANES), F32)] * 4
    if first:
        in_specs += [u_cols(2), w_cols(2)]
        args += [u, conv_w]
        out_specs = [col, col]
        out_shape = [jax.ShapeDtypeStruct((b, l, dh), F32), jax.ShapeDtypeStruct((b, l, dh), BF16)]
        scratch.append(pltpu.VMEM((l, LANES), F32))
    else:
        in_specs += [col, col, u_cols(3)]
        args += [y_prev[1], y_prev[0], u]
        out_specs = col
        out_shape = jax.ShapeDtypeStruct((b, l, dh), BF16)
    return pl.pallas_call(
        functools.partial(_hy_block_kernel, nb=nb, period=period, first=first),
        grid=(nct, b),
        in_specs=in_specs, out_specs=out_specs, out_shape=out_shape,
        scratch_shapes=scratch,
        compiler_params=_cparams(("parallel", "parallel"), 56),
        name="hy_block",
    )(*args)


def _hyena(u, conv_w, w1, b1, freq, w2, b2, w3, skip, lay, grid_mask):
    assert HYENA_ORDER == 2
    dh = lay["DH"]
    l = u.shape[1]
    p = min(HY_BLOCK, l)
    period = CHUNK if grid_mask else l
    assert p % period == 0
    tabs = _dft_tables(p)
    spec = _hy_spectrum(tabs, *_hy_filters(l, w1, b1, freq, w2, b2, w3, dh), p)
    y1 = _hy_block(tabs, spec, u, conv_w, skip, 0, p, dh, period)
    return _hy_block(tabs, spec, u, conv_w, skip, 1, p, dh, period, y_prev=y1)


def _layout(d):
    dg, dh, dm = 3 * d // 8, d // 4, 3 * d // 8
    lay = {"DG": dg, "DH": dh, "DM": dm, "gh": dg // HEAD_DIM, "mh": dm // HEAD_DIM}
    lay["HZ"] = 3 * dh
    lay["GQ"] = 4 * dh
    lay["GZ"] = lay["GQ"] + 3 * dg
    lay["GAB"] = lay["GZ"] + dg
    lay["MQ"] = lay["GAB"] + LANES
    lay["MO"] = lay["MQ"] + 3 * dm
    lay["MZ"] = lay["MO"] + dm
    lay["MG"] = lay["MZ"] + dm
    lay["NP"] = lay["MG"] + LANES
    return lay


def _pack_w_in(w, lay):
    dg, dh, dm, gh, mh = lay["DG"], lay["DH"], lay["DM"], lay["gh"], lay["mh"]
    sizes = (3 * dg, dg, 4 * gh, 3 * dh, dh, 3 * dm, dm, dm, 4 * mh)
    offs = [0]
    for s in sizes:
        offs.append(offs[-1] + s)
    seg = [w[:, offs[i]:offs[i + 1]] for i in range(len(sizes))]
    g_qkv, g_z, g_ab, h_p, h_z, m_qkv, m_o, m_z, m_g = seg
    padl = lambda a: jnp.pad(a, ((0, 0), (0, LANES - a.shape[1])))
    return jnp.concatenate([h_p, h_z, g_qkv, g_z, padl(g_ab), m_qkv, m_o, m_z, padl(m_g)], axis=1).astype(BF16)


def _gate_rows(u, off, n):
    b, l, _ = u.shape
    return jnp.transpose(u[:, :, off:off + n], (0, 2, 1)).reshape(b, n, l // LANES, LANES)


def kernel(x, c, ctx, c_ctx, norm_w, mod_w, mod_b, w_in, gdn_conv, gdn_a_log, gdn_dt_bias, gdn_norm, hy_conv,
           hy_w1, hy_b1, hy_freq, hy_w2, hy_b2, hy_w3, hy_skip, ml_gate_bias, ml_norm, w_out, final_norm):
    b, l, d = x.shape
    lc = ctx.shape[1]
    depth = norm_w.shape[0]
    lay = _layout(d)
    dg, dh = lay["DG"], lay["DH"]
    assert b < COND_ROWS and l % (2 * LANES) == 0 and lc % (2 * LANES) == 0 and d % 1024 == 0
    assert l % min(HY_BLOCK, l) == 0
    cond = jnp.zeros((COND_ROWS, d), F32).at[:b].set(c).at[b].set(c_ctx)
    all_mods = _adaln(cond, mod_w, mod_b[:, None, :])
    for layer in range(depth):
        last = layer == depth - 1
        mods = all_mods[layer]
        sh, sc, gt = mods[:, :d], mods[:, d:2 * d], mods[:, 2 * d:]
        lat = lambda m: m[:b, None, :]
        cx = lambda m: jnp.broadcast_to(m[b][None, None, :], (b, 1, d))
        wp = _pack_w_in(w_in[layer], lay)
        nw = norm_w[layer][None]
        u_l = _inproj(x, nw, lat(sc), lat(sh), wp)
        u_c = _inproj(ctx, nw, cx(sc), cx(sh), wp)
        g_rows = lambda u, off, n: _gate_rows(u, off, n)
        yg_l, yg_c = _gdn(u_l, u_c, g_rows(u_l, lay["GAB"], 4 * lay["gh"]), g_rows(u_c, lay["GAB"], 4 * lay["gh"]),
                          gdn_conv[layer], gdn_a_log[layer], gdn_dt_bias[layer], gdn_norm[layer][None], lay,
                          not last)
        ym_l, ym_c = _mlstm(u_l, u_c, g_rows(u_l, lay["MG"], 4 * lay["mh"]), g_rows(u_c, lay["MG"], 4 * lay["mh"]),
                            ml_gate_bias[layer], ml_norm[layer][None], lay, not last)
        hy = (hy_conv[layer], hy_w1[layer], hy_b1[layer], hy_freq[layer], hy_w2[layer], hy_b2[layer],
              hy_w3[layer], hy_skip[layer])
        yh_l = _hyena(u_l, *hy, lay, True)
        wo = w_out[layer].astype(BF16)
        wg, wh, wm = wo[:dg], wo[dg:dg + dh], wo[dg + dh:]
        fw = final_norm[None]
        x = _outproj(x, yg_l, yh_l, ym_l, wg, wh, wm, lat(gt), fw, last)
        if not last:
            yh_c = _hyena(u_c, *hy, lay, False)
            ctx = _outproj(ctx, yg_c, yh_c, ym_c, wg, wh, wm, cx(gt), fw, False)
    return x
```

```python
import functools
import math

import jax
import jax.numpy as jnp
from jax import lax
from jax.experimental import pallas as pl
from jax.experimental.pallas import tpu as pltpu

HEAD_DIM = 128
CHUNK = 64
LANES = 128
NORM_EPS = 1e-6
HYENA_ORDER = 2
HY_BLOCK = 512
FILTER_BANDS = 16
DECAY_TARGET = 1e-2
MIN_DECAY = math.log(DECAY_TARGET) / 1.5
MAX_DECAY = math.log(DECAY_TARGET) / 0.3
COND_ROWS = 16

F32 = jnp.float32
BF16 = jnp.bfloat16
HI = lax.Precision.HIGHEST


def _cparams(sem, vmem_mb):
    return pltpu.CompilerParams(dimension_semantics=sem, vmem_limit_bytes=vmem_mb << 20)


def _dot(a, b):
    return jnp.dot(a.astype(BF16), b.astype(BF16), preferred_element_type=F32)


def _dot_hi(a, b):
    return jnp.dot(a, b, precision=HI, preferred_element_type=F32)


def _dot_nt(a, b):
    return lax.dot_general(a.astype(BF16), b.astype(BF16), (((1,), (1,)), ((), ())),
                           preferred_element_type=F32)


def _silu(x):
    return x * jax.nn.sigmoid(x)


def _softplus(x):
    return jnp.maximum(x, 0.0) + jnp.log(1.0 + jnp.exp(-jnp.abs(x)))


def _pick_tile(n, cap, unit):
    t = (min(n, cap) // unit) * unit
    while n % t:
        t -= unit
    return t


def _adaln_kernel(c_ref, w_ref, b_ref, o_ref):
    o_ref[0] = _dot_hi(_silu(c_ref[...]), w_ref[0]) + b_ref[0]


def _adaln(cond, w, b):
    depth, d, n = w.shape
    tn = _pick_tile(n, 768, LANES)
    return pl.pallas_call(
        _adaln_kernel,
        grid=(depth, n // tn),
        in_specs=[pl.BlockSpec((COND_ROWS, d), lambda i, j: (0, 0)),
                  pl.BlockSpec((1, d, tn), lambda i, j: (i, 0, j)),
                  pl.BlockSpec((1, 1, tn), lambda i, j: (i, 0, j))],
        out_specs=pl.BlockSpec((1, COND_ROWS, tn), lambda i, j: (i, 0, j)),
        out_shape=jax.ShapeDtypeStruct((depth, COND_ROWS, n), F32),
        compiler_params=_cparams(("parallel", "parallel"), 40),
        name="adaln",
    )(cond, w, b)


INPROJ_NORM_ROWS = 256


def _inproj_kernel(x_ref, nw_ref, sc_ref, sh_ref, w_ref, o_ref, xn_ref):
    first = pl.program_id(2) == 0

    @pl.when(first)
    def _():
        rows = _pick_tile(xn_ref.shape[0], INPROJ_NORM_ROWS, 8)
        for s in range(xn_ref.shape[0] // rows):
            sl = slice(s * rows, (s + 1) * rows)
            x = x_ref[0, sl, :]
            r = lax.rsqrt(jnp.mean(x * x, axis=-1, keepdims=True) + NORM_EPS)
            y = (x * r * nw_ref[...]) * (1.0 + sc_ref[0]) + sh_ref[0]
            y = y.astype(BF16)
            xn_ref[sl, :] = y
            o_ref[0, sl, :] = jnp.dot(y, w_ref[...], preferred_element_type=F32)

    @pl.when(jnp.logical_not(first))
    def _():
        o_ref[0] = jnp.dot(xn_ref[...], w_ref[...], preferred_element_type=F32)


def _inproj(x, nw, sc, sh, wp):
    b, l, d = x.shape
    n = wp.shape[1]
    tm = _pick_tile(l, 1024, 8)
    tn = _pick_tile(n, 1280, LANES)
    return pl.pallas_call(
        _inproj_kernel,
        grid=(b, l // tm, n // tn),
        in_specs=[pl.BlockSpec((1, tm, d), lambda i, m, j: (i, m, 0)),
                  pl.BlockSpec((1, d), lambda i, m, j: (0, 0)),
                  pl.BlockSpec((1, 1, d), lambda i, m, j: (i, 0, 0)),
                  pl.BlockSpec((1, 1, d), lambda i, m, j: (i, 0, 0)),
                  pl.BlockSpec((d, tn), lambda i, m, j: (0, j))],
        out_specs=pl.BlockSpec((1, tm, tn), lambda i, m, j: (i, m, j)),
        out_shape=jax.ShapeDtypeStruct((b, l, n), F32),
        scratch_shapes=[pltpu.VMEM((tm, d), BF16)],
        compiler_params=_cparams(("parallel", "parallel", "arbitrary"), 56),
        name="inproj",
    )(x, nw, sc, sh, wp)


def _outproj_kernel(x_ref, yg_ref, yh_ref, ym_ref, wg_ref, wh_ref, wm_ref, gt_ref, fw_ref, o_ref, *, final):
    acc = jnp.dot(yg_ref[0], wg_ref[...], preferred_element_type=F32)
    acc = acc + jnp.dot(yh_ref[0], wh_ref[...], preferred_element_type=F32)
    acc = acc + jnp.dot(ym_ref[0], wm_ref[...], preferred_element_type=F32)
    xn = x_ref[0] + gt_ref[0] * acc
    if final:
        r = lax.rsqrt(jnp.mean(xn * xn, axis=-1, keepdims=True) + NORM_EPS)
        xn = xn * r * fw_ref[...]
    o_ref[0] = xn


def _outproj(x, yg, yh, ym, wg, wh, wm, gt, fw, final):
    b, l, d = x.shape
    tm = _pick_tile(l, 512, 8)
    row = lambda w: pl.BlockSpec((1, tm, w), lambda i, m: (i, m, 0))
    full = lambda a: pl.BlockSpec(a.shape, lambda i, m: (0, 0))
    return pl.pallas_call(
        functools.partial(_outproj_kernel, final=final),
        grid=(b, l // tm),
        in_specs=[row(d), row(yg.shape[2]), row(yh.shape[2]), row(ym.shape[2]),
                  full(wg), full(wh), full(wm),
                  pl.BlockSpec((1, 1, d), lambda i, m: (i, 0, 0)),
                  pl.BlockSpec((1, d), lambda i, m: (0, 0))],
        out_specs=row(d),
        out_shape=jax.ShapeDtypeStruct((b, l, d), F32),
        compiler_params=_cparams(("parallel", "parallel"), 48),
        name="outproj",
    )(x, yg, yh, ym, wg, wh, wm, gt, fw)


def _conv_rows(x, w, period):
    rows = x.shape[0]
    taps = w.shape[0]
    pad = taps // 2
    pos = lax.broadcasted_iota(jnp.int32, x.shape, 0) % period
    y = None
    for j in range(taps):
        off = j - pad
        if off == 0:
            term = x * w[j:j + 1]
        else:
            shifted = pltpu.roll(x, (-off) % rows, axis=0)
            ok = (pos >= -off) if off < 0 else (pos < period - off)
            term = jnp.where(ok, shifted, 0.0) * w[j:j + 1]
        y = term if y is None else y + term
    return y


def _seg_cumsum(x, reverse):
    lane = lax.broadcasted_iota(jnp.int32, x.shape, 1) % CHUNK
    s = 1
    while s < CHUNK:
        if reverse:
            shifted = pltpu.roll(x, LANES - s, axis=1)
            ok = lane < CHUNK - s
        else:
            shifted = pltpu.roll(x, s, axis=1)
            ok = lane >= s
        x = x + jnp.where(ok, shifted, 0.0)
        s *= 2
    return x


def _pair_masks(d):
    ii = lax.broadcasted_iota(jnp.int32, (LANES, LANES), 0)
    jj = lax.broadcasted_iota(jnp.int32, (LANES, LANES), 1)
    lo = (ii // CHUNK) * CHUNK
    eye = ii == jj
    if d == 0:
        return eye, (jj >= lo) & (jj <= ii), (jj >= lo) & (jj < ii)
    return eye, (jj < lo + CHUNK) & (jj >= ii), (jj < lo + CHUNK) & (jj > ii)


def _to_col(row, eye):
    return jnp.sum(jnp.where(eye, jnp.broadcast_to(row, eye.shape), 0.0), axis=1, keepdims=True)


def _split_bf16(a):
    hi = a.astype(BF16)
    return hi, (a - hi.astype(F32)).astype(BF16)


def _col_bcast(row, eye):
    x = jnp.where(eye, jnp.broadcast_to(row, eye.shape), 0.0)
    hi = x.astype(BF16)
    mid, lo = _split_bf16(x - hi.astype(F32))
    ones = jnp.ones((3 * eye.shape[1], LANES), BF16)
    return jnp.dot(jnp.concatenate([hi, mid, lo], axis=1), ones, preferred_element_type=F32)


def _dot3(a, b):
    a_hi, a_lo = _split_bf16(a)
    b_hi, b_lo = _split_bf16(b)
    return jnp.dot(jnp.concatenate([a_hi, a_hi, a_lo], axis=1), jnp.concatenate([b_hi, b_lo, b_hi], axis=0),
                   preferred_element_type=F32)


def _head_out_tiles(acc_refs, base, nw_ref, z_ref, y_ref, n_rows, og_ref=None):
    tile = _pick_tile(n_rows, 256, 8)

    def body(i, carry):
        r = pl.multiple_of(i * tile, tile)
        o = acc_refs[0][pl.ds(base + r, tile), :]
        for acc_ref in acc_refs[1:]:
            o = o + acc_ref[pl.ds(base + r, tile), :]
        if og_ref is not None:
            o = jax.nn.sigmoid(og_ref[0, pl.ds(r, tile), :]) * o
        o = o * lax.rsqrt(jnp.mean(o * o, axis=-1, keepdims=True) + NORM_EPS) * nw_ref[...]
        y_ref[0, pl.ds(r, tile), :] = (o * _silu(z_ref[0, pl.ds(r, tile), :])).astype(y_ref.dtype)
        return carry

    lax.fori_loop(0, n_rows // tile, body, 0)


def _ml_rows(g_refs, bias_ref, h, n_heads, rows_ref):
    li_f = g_refs[0][0, 0] + bias_ref[h]
    lf_f = -_softplus(-(g_refs[1][0, 0] + bias_ref[n_heads + h]))
    li_b = g_refs[2][0, 0] + bias_ref[2 * n_heads + h]
    lf_b = -_softplus(-(g_refs[3][0, 0] + bias_ref[3 * n_heads + h]))
    rows_ref[0] = _seg_cumsum(lf_f, False)
    rows_ref[1] = li_f
    rows_ref[2] = _seg_cumsum(lf_b, True)
    rows_ref[3] = li_b


def _ml_prep_cols(q_ref, k_ref, rows_ref, chains):
    eye = _pair_masks(0)[0]
    lane_i = lax.broadcasted_iota(jnp.int32, (1, LANES), 1)
    st = []
    for d, p in chains:
        rows = pl.ds(pl.multiple_of(p * LANES, LANES), LANES)
        k = k_ref[0, rows, :] * (HEAD_DIM ** -0.5)
        b_row = rows_ref[2 * d, pl.ds(p, 1), :]
        li_row = rows_ref[2 * d + 1, pl.ds(p, 1), :]
        last = CHUNK - 1 if d == 0 else 0
        b_tot = (b_row[:, last:last + 1], b_row[:, CHUNK + last:CHUNK + last + 1])
        end_row = jnp.where(lane_i < CHUNK, b_tot[0], b_tot[1]) - b_row + li_row
        e_max = (jnp.max(end_row[:, :CHUNK], axis=1, keepdims=True),
                 jnp.max(end_row[:, CHUNK:], axis=1, keepdims=True))
        st.append(dict(d=d, rows=rows, k=k, b_row=b_row, li_row=li_row, b_tot=b_tot, e_max=e_max,
                       b_cb=_col_bcast(b_row, eye), end_cb=_col_bcast(end_row, eye),
                       qk=_dot_nt(q_ref[0, rows, :], k)))
    return st


def _ml_prep_intra(st):
    for c in st:
        _, incl, _ = _pair_masks(c["d"])
        dlog = jnp.where(incl, c["b_cb"] - c["b_row"] + c["li_row"], -jnp.inf)
        c["rowmax"] = jnp.max(dlog, axis=1, keepdims=True)
        c["p_hi"], c["p_lo"] = _split_bf16(jnp.exp(dlog - c["rowmax"]) * c["qk"])


def _ml_prep_state(st):
    row_i = lax.broadcasted_iota(jnp.int32, (LANES, 1), 0)
    lane_sq = lax.broadcasted_iota(jnp.int32, (LANES, LANES), 1)
    out = []
    for c in st:
        kw_t = (c["k"] * jnp.exp(c["end_cb"] - jnp.where(row_i < CHUNK, c["e_max"][0], c["e_max"][1]))).T
        kw_t2 = jnp.concatenate([jnp.where(lane_sq < CHUNK, kw_t, 0.0), jnp.where(lane_sq < CHUNK, 0.0, kw_t)],
                                axis=0).astype(BF16)
        out.append((c["b_cb"], c["rowmax"], c["p_hi"], c["p_lo"], kw_t2, *c["b_tot"], *c["e_max"]))
    return tuple(out)


def _ml_steps_state(q_ref, v_ref, cnst, pre, chains):
    ones = jnp.ones((LANES, LANES), BF16)
    st = []
    for d, p, half in chains:
        _, _, p_hi, p_lo, kw_t2 = pre[d][:5]
        sl = slice(half * CHUNK, (half + 1) * CHUNK)
        v_ones = jnp.concatenate([v_ref[0, pl.ds(pl.multiple_of(p * LANES, LANES), LANES), :].astype(BF16), ones],
                                 axis=1)
        rhs = jnp.concatenate([v_ones, jnp.concatenate([jnp.zeros_like(ones), ones], axis=1)], axis=0)
        pv_ps = jnp.dot(jnp.concatenate([p_hi[sl], p_lo[sl]], axis=1), rhs, preferred_element_type=F32)
        d_cn = jnp.dot(kw_t2[half * LANES:(half + 1) * LANES], v_ones, preferred_element_type=F32)
        st.append((pv_ps, d_cn))
    for n, (d, p, half) in enumerate(chains):
        r0 = pl.ds(pl.multiple_of(p * LANES + half * CHUNK, CHUNK), CHUNK)
        cn = cnst[d]
        st[n] = (r0, cn, _dot(q_ref[0, r0, :], cn), *st[n])
    return st


def _ml_steps_update(pre, acc_refs, cnst, mst, chains, st):
    for (d, p, half), (r0, cn, q_cn, pv_ps, d_cn) in zip(chains, st):
        b_cb, rowmax = pre[d][:2]
        b_tot = pre[d][5 + half]
        e_max = pre[d][7 + half]
        sl = slice(half * CHUNK, (half + 1) * CHUNK)
        m_s = mst[d, 0:1, 0:1]
        inter = b_cb[sl] + m_s
        m_i = jnp.maximum(inter, rowmax[sl])
        w_inter = jnp.exp(inter - m_i)
        s_intra = jnp.exp(rowmax[sl] - m_i)
        num = w_inter * q_cn[:, :HEAD_DIM] + s_intra * pv_ps[:, :HEAD_DIM]
        den = w_inter * q_cn[:, HEAD_DIM:] + s_intra * pv_ps[:, HEAD_DIM:]
        acc_refs[d][r0, :] = num / jnp.maximum(jnp.abs(den), jnp.exp(-m_i))
        carry_log = b_tot + m_s
        m_new = jnp.maximum(carry_log, e_max)
        cnst[d] = jnp.exp(carry_log - m_new) * cn + jnp.exp(e_max - m_new) * d_cn
        mst[d] = jnp.broadcast_to(m_new, mst.shape[1:])


def _ml_scan(q_ref, k_ref, v_ref, rows_ref, acc_refs, cnst, mst, n_pairs):
    def chains_of(t):
        return [(0, t), (1, n_pairs - 1 - t), (0, t + 1), (1, n_pairs - 2 - t)]

    def prep(chains):
        st = _ml_prep_cols(q_ref, k_ref, rows_ref, chains)
        _ml_prep_intra(st)
        return _ml_prep_state(st)

    def trip(i, pre, prepare_next):
        t = 2 * i
        st = _ml_prep_cols(q_ref, k_ref, rows_ref, chains_of(t + 2)) if prepare_next else None
        nxt = pre
        for j, tt in enumerate((t, t + 1)):
            pf, pb = tt, n_pairs - 1 - tt
            pre_j = pre[2 * j:2 * j + 2]
            for halves in ([(0, pf, 0), (1, pb, 1)], [(0, pf, 1), (1, pb, 0)]):
                held = _ml_steps_state(q_ref, v_ref, cnst, pre_j, halves)
                if prepare_next and j == 0 and halves[0][2] == 0:
                    _ml_prep_intra(st)
                elif prepare_next and j == 0:
                    nxt = _ml_prep_state(st)
                _ml_steps_update(pre_j, acc_refs, cnst, mst, halves, held)
        return nxt

    n_trips = n_pairs // 2
    pre = lax.fori_loop(0, n_trips - 1, lambda i, pre: trip(i, pre, True), prep(chains_of(0)))
    trip(n_trips - 1, pre, False)


def _mlstm_kernel(bias_ref, ql, kl, vl, ol, zl, qc, kc, vc, oc, zc, gl0, gl1, gl2, gl3, gc0, gc1, gc2, gc3,
                  nw_ref, *rest, n_heads, ctx_out):
    if ctx_out:
        yl_ref, yc_ref = rest[:2]
        rest = rest[2:]
    else:
        yl_ref, yc_ref = rest[0], None
        rest = rest[1:]
    cnst, mst, rows_l, rows_c, acc_lf, acc_lb, acc_cf, acc_cb = rest
    h = pl.program_id(1)
    cnst[...] = jnp.zeros(cnst.shape, F32)
    mst[...] = jnp.zeros(mst.shape, F32)
    _ml_rows((gc0, gc1, gc2, gc3), bias_ref, h, n_heads, rows_c)
    _ml_scan(qc, kc, vc, rows_c, (acc_cf, acc_cb), cnst, mst, rows_c.shape[1])
    _ml_rows((gl0, gl1, gl2, gl3), bias_ref, h, n_heads, rows_l)
    _ml_scan(ql, kl, vl, rows_l, (acc_lf, acc_lb), cnst, mst, rows_l.shape[1])
    _head_out_tiles((acc_lf, acc_lb), 0, nw_ref, zl, yl_ref, acc_lf.shape[0], og_ref=ol)
    if ctx_out:
        _head_out_tiles((acc_cf, acc_cb), 0, nw_ref, zc, yc_ref, acc_cf.shape[0], og_ref=oc)


def _head_block(l, blk):
    return pl.BlockSpec((1, l, HEAD_DIM), lambda b, h, blk=blk: (b, 0, blk + h))


def _gate_block(n_pairs, j, n_heads):
    return pl.BlockSpec((1, 1, n_pairs, LANES), lambda b, h, j=j: (b, j * n_heads + h, 0, 0))


def _mlstm(u_l, u_c, gt_l, gt_c, bias, nw, lay, ctx_out):
    b, l, _ = u_l.shape
    lc = u_c.shape[1]
    nh = lay["mh"]
    blk = lambda off: off // HEAD_DIM
    offs = [lay["MQ"], lay["MQ"] + lay["DM"], lay["MQ"] + 2 * lay["DM"], lay["MO"], lay["MZ"]]
    in_specs = [pl.BlockSpec(memory_space=pltpu.SMEM)]
    in_specs += [_head_block(l, blk(o)) for o in offs]
    in_specs += [_head_block(lc, blk(o)) for o in offs]
    in_specs += [_gate_block(l // LANES, j, nh) for j in range(4)]
    in_specs += [_gate_block(lc // LANES, j, nh) for j in range(4)]
    in_specs += [pl.BlockSpec((1, HEAD_DIM), lambda b_, h: (0, 0))]
    out_specs = [pl.BlockSpec((1, l, HEAD_DIM), lambda b_, h: (b_, 0, h))]
    out_shape = [jax.ShapeDtypeStruct((b, l, nh * HEAD_DIM), BF16)]
    if ctx_out:
        out_specs.append(pl.BlockSpec((1, lc, HEAD_DIM), lambda b_, h: (b_, 0, h)))
        out_shape.append(jax.ShapeDtypeStruct((b, lc, nh * HEAD_DIM), BF16))
    scratch = [pltpu.VMEM((2, HEAD_DIM, 2 * HEAD_DIM), F32), pltpu.VMEM((2, 8, LANES), F32),
               pltpu.VMEM((4, l // LANES, LANES), F32), pltpu.VMEM((4, lc // LANES, LANES), F32),
               pltpu.VMEM((l, HEAD_DIM), F32), pltpu.VMEM((l, HEAD_DIM), F32),
               pltpu.VMEM((lc, HEAD_DIM), F32), pltpu.VMEM((lc, HEAD_DIM), F32)]
    res = pl.pallas_call(
        functools.partial(_mlstm_kernel, n_heads=nh, ctx_out=ctx_out),
        grid=(b, nh),
        in_specs=in_specs, out_specs=out_specs, out_shape=out_shape,
        scratch_shapes=scratch,
        compiler_params=_cparams(("parallel", "parallel"), 48),
        name="mlstm",
    )(bias.reshape(-1), *([u_l] * 5), *([u_c] * 5), *([gt_l] * 4), *([gt_c] * 4), nw)
    return (res[0], res[1]) if ctx_out else (res[0], None)


def _gdn_rows(g_refs, alog_ref, dtb_ref, h, n_heads, rows_ref, base):
    n = g_refs[0].shape[2]
    g_f = -jnp.exp(alog_ref[h]) * _softplus(g_refs[0][0, 0] + dtb_ref[h])
    g_b = -jnp.exp(alog_ref[n_heads + h]) * _softplus(g_refs[1][0, 0] + dtb_ref[n_heads + h])
    rows_ref[0, base:base + n] = _seg_cumsum(g_f, False)
    rows_ref[1, base:base + n] = jax.nn.sigmoid(g_refs[2][0, 0])
    rows_ref[2, base:base + n] = _seg_cumsum(g_b, True)
    rows_ref[3, base:base + n] = jax.nn.sigmoid(g_refs[3][0, 0])


def _masked_taps(w, period):
    taps = w.shape[0]
    pad = taps // 2
    pos = lax.broadcasted_iota(jnp.int32, (period, w.shape[1]), 0)
    out = []
    for j in range(taps):
        off = j - pad
        ok = (pos >= -off) if off < 0 else (pos < period - off)
        out.append(jnp.where(ok, w[j:j + 1], 0.0))
    return out


def _conv_rows_pre(x, wm_ref, first):
    rows = x.shape[0]
    taps = wm_ref.shape[0] // 3
    period = wm_ref.shape[1]
    pad = taps // 2
    y = None
    for j in range(taps):
        off = j - pad
        shifted = x if off == 0 else pltpu.roll(x, (-off) % rows, axis=0)
        term = shifted * jnp.concatenate([wm_ref[first + j]] * (rows // period), axis=0)
        y = term if y is None else y + term
    return y


def _gdn_conv(src_refs, w_refs, dst_refs, base, n_units, unit_rows, period, wm_ref=None):
    taps = w_refs[0].shape[0]
    if wm_ref is not None:
        for idx in range(3):
            for j, wm in enumerate(_masked_taps(w_refs[idx][...], period)):
                wm_ref[idx * taps + j] = wm

    def body(i, carry):
        r = pl.multiple_of(i * unit_rows, unit_rows)
        for idx in range(3):
            x = src_refs[idx][0, pl.ds(r, unit_rows), :]
            if wm_ref is not None:
                t = _silu(_conv_rows_pre(x, wm_ref, idx * taps))
            else:
                t = _silu(_conv_rows(x, w_refs[idx][...], period))
            if idx < 2:
                t = t * lax.rsqrt(jnp.sum(t * t, axis=-1, keepdims=True) + NORM_EPS)
            if idx == 0:
                t = t * (HEAD_DIM ** -0.5)
            dst_refs[idx][pl.ds(base + r, unit_rows), :] = t
        return carry

    lax.fori_loop(0, n_units, body, 0)


N_NEUMANN = CHUNK.bit_length() - 2


def _side_masks(d):
    ii = lax.broadcasted_iota(jnp.int32, (CHUNK, LANES), 0)
    lane = lax.broadcasted_iota(jnp.int32, (CHUNK, LANES), 1)
    jj = lane % CHUNK
    first = lane < CHUNK
    if d == 0:
        return first, jj == ii, jj <= ii, jj < ii
    return first, jj == ii, jj >= ii, jj > ii


def _side(x, first):
    return jnp.where(first, x[:CHUNK], x[CHUNK:])


def _block_diag(x, first):
    zero = jnp.zeros_like(x)
    return jnp.concatenate([jnp.where(first, x, zero), jnp.where(first, zero, x)], axis=0)


def _dot3_side(a, b, first):
    a_hi, a_lo = _split_bf16(a)
    b_hi, b_lo = _split_bf16(b)
    rhs = jnp.concatenate([_block_diag(b_hi, first), _block_diag(b_lo, first), _block_diag(b_hi, first)], axis=0)
    return jnp.dot(jnp.concatenate([a_hi, a_hi, a_lo], axis=1), rhs, preferred_element_type=F32)


def _gdn_gates(rows_ref, d, p):
    eye = _pair_masks(0)[0]
    g_row = rows_ref[2 * d, pl.ds(p, 1), :]
    return dict(d=d, p=p, rows=pl.ds(pl.multiple_of(p * LANES, LANES), LANES), g_row=g_row,
                g_col=_to_col(g_row, eye), beta_col=_to_col(rows_ref[2 * d + 1, pl.ds(p, 1), :], eye))


def _gdn_prep_start(qs, ks, rows_ref, chains):
    st = []
    for d, p in chains:
        c = _gdn_gates(rows_ref, d, p)
        rows, g_row, g_col, beta_col = c["rows"], c["g_row"], c["g_col"], c["beta_col"]
        k = ks[rows, :]
        first, eye_s, incl, strict = _side_masks(d)
        dec = jnp.exp(jnp.where(incl, _side(g_col, first) - g_row, -jnp.inf))
        aqk = _block_diag(_side(_dot_nt(qs[rows, :], k), first) * dec, first).astype(BF16)
        m_low = jnp.where(strict, _side(beta_col, first) * _side(_dot_nt(k, k), first) * dec, 0.0)
        st.append(dict(c, aqk=aqk, pw=m_low, t_inv=jnp.where(eye_s, 1.0, 0.0) - m_low))
    return st


def _gdn_prep_neumann(st, after_square=None, after_product=None):
    first, eye_s, _, _ = _side_masks(0)
    eye_f = jnp.where(eye_s, 1.0, 0.0)
    for c in st:
        c["pw"] = _dot3_side(c["pw"], c["pw"], first)
    if after_square is not None:
        after_square()
    for c in st:
        c["t_inv"] = _dot3_side(c["t_inv"], eye_f + c["pw"], first)
    if after_product is not None:
        after_product()


NQ_ROWS = HEAD_DIM + CHUNK


def _gdn_prep_solve(qs, ks, vs, st):
    row_i = lax.broadcasted_iota(jnp.int32, (LANES, 1), 0)
    first = _side_masks(0)[0]
    for c in st:
        d, rows, g_row, g_col, beta_col = c["d"], c["rows"], c["g_row"], c["g_col"], c["beta_col"]
        k = ks[rows, :]
        e_g = jnp.exp(g_col)
        c["w_ut"] = _dot3(_block_diag(c["t_inv"], first),
                          jnp.concatenate([(beta_col * e_g) * k, beta_col * vs[rows, :]], axis=1)).astype(BF16)
        c["q_dec"] = qs[rows, :] * e_g
        last = CHUNK - 1 if d == 0 else 0
        g_tot_col = jnp.where(row_i < CHUNK, g_row[:, last:last + 1], g_row[:, CHUNK + last:CHUNK + last + 1])
        c["k_end_t"] = (k * jnp.exp(g_tot_col - g_col)).T.astype(BF16)


def _gdn_prep_finish(nq_s, r_s, acc_refs, st):
    row_i = lax.broadcasted_iota(jnp.int32, (LANES, 1), 0)
    for c in st:
        d, w_ut, q_dec = c["d"], c["w_ut"], c["q_dec"]
        zero = jnp.zeros_like(w_ut)
        for half in (0, 1):
            sl = slice(half * CHUNK, (half + 1) * CHUNK)
            lhs = jnp.concatenate([c["k_end_t"], c["aqk"][sl]], axis=0)
            mine = (row_i < CHUNK) if half == 0 else (row_i >= CHUNK)
            prod = jnp.dot(lhs, jnp.where(mine, w_ut, zero), preferred_element_type=F32)
            n_q = jnp.concatenate([prod[:HEAD_DIM, :HEAD_DIM], q_dec[sl] - prod[HEAD_DIM:, :HEAD_DIM]], axis=0)
            chunk = 2 * c["p"] + half
            nq_s[d, pl.ds(pl.multiple_of(chunk * NQ_ROWS, NQ_ROWS), NQ_ROWS), :] = n_q.astype(BF16)
            r_s[d, pl.ds(pl.multiple_of(chunk * HEAD_DIM, HEAD_DIM), HEAD_DIM), :] = prod[:HEAD_DIM, HEAD_DIM:]
            acc_refs[d][pl.ds(pl.multiple_of(chunk * CHUNK, CHUNK), CHUNK), :] = prod[HEAD_DIM:, HEAD_DIM:]


def _gdn_steps_state(nq_s, sst, chains):
    st = []
    for d, p, half in chains:
        chunk = 2 * p + half
        s = sst[d]
        n_q = nq_s[d, pl.ds(pl.multiple_of(chunk * NQ_ROWS, NQ_ROWS), NQ_ROWS), :]
        st.append((chunk, s, jnp.dot(n_q, s.astype(BF16), preferred_element_type=F32)))
    return st


def _gdn_steps_update(rows_ref, r_s, sst, acc_refs, chains, st):
    for (d, p, half), (chunk, s, prod) in zip(chains, st):
        lane = half * CHUNK + (CHUNK - 1 if d == 0 else 0)
        g_tot = rows_ref[2 * d, pl.ds(p, 1), lane:lane + 1]
        r0 = pl.ds(pl.multiple_of(chunk * CHUNK, CHUNK), CHUNK)
        acc_refs[d][r0, :] = acc_refs[d][r0, :] + prod[HEAD_DIM:]
        r_c = r_s[d, pl.ds(pl.multiple_of(chunk * HEAD_DIM, HEAD_DIM), HEAD_DIM), :]
        sst[d] = jnp.exp(g_tot) * s - prod[:HEAD_DIM] + r_c


def _gdn_kernel(alog_ref, dtb_ref, ql, kl, vl, zl, qc, kc, vc, zc, gl0, gl1, gl2, gl3, gc0, gc1, gc2, gc3,
                wq, wk, wv, nw_ref, *rest, n_heads, ctx_out):
    if ctx_out:
        yl_ref, yc_ref = rest[:2]
        rest = rest[2:]
    else:
        yl_ref, yc_ref = rest[0], None
        rest = rest[1:]
    sst, rows, qs, ks, vs, nq_s, r_s, acc_f, acc_b, nm_s, aq_s, wm_s = rest
    h = pl.program_id(1)
    n_l = ql.shape[1]
    n_c = qc.shape[1]
    nl = n_l // LANES
    nc = n_c // LANES
    sst[...] = jnp.zeros(sst.shape, F32)
    _gdn_rows((gl0, gl1, gl2, gl3), alog_ref, dtb_ref, h, n_heads, rows, 0)
    _gdn_rows((gc0, gc1, gc2, gc3), alog_ref, dtb_ref, h, n_heads, rows, nl)
    conv_rows = _pick_tile(n_l, 4 * CHUNK, CHUNK)
    _gdn_conv((ql, kl, vl), (wq, wk, wv), (qs, ks, vs), 0, n_l // conv_rows, conv_rows, CHUNK, wm_ref=wm_s)
    _gdn_conv((qc, kc, vc), (wq, wk, wv), (qs, ks, vs), n_l, 1, n_c, n_c)
    n_steps = nl + nc
    accs = (acc_f, acc_b)

    def fwd_pair(t):
        return jnp.where(t < nc, nl + t, t - nc)

    def bwd_pair(t):
        return n_steps - 1 - t

    n_trips = n_steps // 2
    n_early = (N_NEUMANN + 1) // 2

    def chains_of(trip):
        t = 2 * jnp.minimum(trip, n_trips - 1)
        return [(0, fwd_pair(t)), (1, bwd_pair(t)), (0, fwd_pair(t + 1)), (1, bwd_pair(t + 1))]

    def neumann(st, half=None):
        if half is None:
            _gdn_prep_neumann(st)
            return
        held = []
        _gdn_prep_neumann(st, lambda: held.append(_gdn_steps_state(nq_s, sst, half)),
                          lambda: _gdn_steps_update(rows, r_s, sst, accs, half, held[0]))

    def finish(st):
        _gdn_prep_solve(qs, ks, vs, st)
        _gdn_prep_finish(nq_s, r_s, accs, st)

    def halves_of(trip):
        out = []
        for t in (2 * trip, 2 * trip + 1):
            pf, pb = fwd_pair(t), bwd_pair(t)
            out += [[(0, pf, 0), (1, pb, 1)], [(0, pf, 1), (1, pb, 0)]]
        return out

    st = _gdn_prep_start(qs, ks, rows, chains_of(0))
    for _ in range(N_NEUMANN):
        neumann(st)
    finish(st)
    st = _gdn_prep_start(qs, ks, rows, chains_of(1))
    for _ in range(n_early):
        neumann(st)

    def park(st_part):
        for n, c in enumerate(st_part):
            nm_s[2 * n] = c["pw"]
            nm_s[2 * n + 1] = c["t_inv"]
            aq_s[n] = c["aqk"]

    park(st)

    def trip(i, carry):
        halves = halves_of(i)
        st_late = [dict(_gdn_gates(rows, d, p), pw=nm_s[2 * n], t_inv=nm_s[2 * n + 1], aqk=aq_s[n])
                   for n, (d, p) in enumerate(chains_of(i + 1))]
        st_early = _gdn_prep_start(qs, ks, rows, chains_of(i + 2))
        n_late = N_NEUMANN - n_early
        slot = 0
        for n in range(n_early):
            neumann(st_early, halves[slot] if slot < len(halves) else None)
            slot += 1
            if n < n_late:
                neumann(st_late, halves[slot] if slot < len(halves) else None)
                slot += 1
            if n == n_late - 1:
                _gdn_prep_solve(qs, ks, vs, st_late)
        _gdn_prep_finish(nq_s, r_s, accs, st_late)
        park(st_early)
        return carry

    lax.fori_loop(0, n_trips - 1, trip, 0)
    for half in halves_of(n_trips - 1):
        _gdn_steps_update(rows, r_s, sst, accs, half, _gdn_steps_state(nq_s, sst, half))
    _head_out_tiles((acc_f, acc_b), 0, nw_ref, zl, yl_ref, n_l)
    if ctx_out:
        _head_out_tiles((acc_f, acc_b), n_l, nw_ref, zc, yc_ref, n_c)


def _gdn(u_l, u_c, gt_l, gt_c, conv_w, a_log, dt_bias, nw, lay, ctx_out):
    b, l, _ = u_l.shape
    lc = u_c.shape[1]
    nh = lay["gh"]
    taps = conv_w.shape[0]
    blk = lambda off: off // HEAD_DIM
    offs = [lay["GQ"], lay["GQ"] + lay["DG"], lay["GQ"] + 2 * lay["DG"], lay["GZ"]]
    smem = pl.BlockSpec(memory_space=pltpu.SMEM)
    in_specs = [smem, smem]
    in_specs += [_head_block(l, blk(o)) for o in offs]
    in_specs += [_head_block(lc, blk(o)) for o in offs]
    in_specs += [_gate_block(l // LANES, j, nh) for j in range(4)]
    in_specs += [_gate_block(lc // LANES, j, nh) for j in range(4)]
    in_specs += [pl.BlockSpec((taps, HEAD_DIM), lambda b_, h, j=j: (0, j * nh + h)) for j in range(3)]
    in_specs += [pl.BlockSpec((1, HEAD_DIM), lambda b_, h: (0, 0))]
    out_specs = [pl.BlockSpec((1, l, HEAD_DIM), lambda b_, h: (b_, 0, h))]
    out_shape = [jax.ShapeDtypeStruct((b, l, nh * HEAD_DIM), BF16)]
    if ctx_out:
        out_specs.append(pl.BlockSpec((1, lc, HEAD_DIM), lambda b_, h: (b_, 0, h)))
        out_shape.append(jax.ShapeDtypeStruct((b, lc, nh * HEAD_DIM), BF16))
    lt = l + lc
    seq_f32 = pltpu.VMEM((lt, HEAD_DIM), F32)
    n_chunks = lt // CHUNK
    scratch = [pltpu.VMEM((2, HEAD_DIM, HEAD_DIM), F32), pltpu.VMEM((4, lt // LANES, LANES), F32),
               seq_f32, seq_f32, seq_f32,
               pltpu.VMEM((2, n_chunks * NQ_ROWS, HEAD_DIM), BF16), pltpu.VMEM((2, n_chunks * HEAD_DIM, HEAD_DIM), F32),
               seq_f32, seq_f32, pltpu.VMEM((8, CHUNK, LANES), F32), pltpu.VMEM((4, LANES, LANES), BF16),
               pltpu.VMEM((3 * taps, CHUNK, HEAD_DIM), F32)]
    res = pl.pallas_call(
        functools.partial(_gdn_kernel, n_heads=nh, ctx_out=ctx_out),
        grid=(b, nh),
        in_specs=in_specs, out_specs=out_specs, out_shape=out_shape,
        scratch_shapes=scratch,
        compiler_params=_cparams(("parallel", "parallel"), 56),
        name="gdn",
    )(a_log.reshape(-1), dt_bias.reshape(-1), *([u_l] * 4), *([u_c] * 4), *([gt_l] * 4), *([gt_c] * 4),
      *([conv_w] * 3), nw)
    return (res[0], res[1]) if ctx_out else (res[0], None)


def _dft_tables(l):
    k = jnp.arange(l, dtype=jnp.int32)
    ang = lambda t: ((k[:, None] * t[None, :]) % (2 * l)).astype(F32) * (math.pi / l)
    ang_a = ang(jnp.arange(l // CHUNK, dtype=jnp.int32) * CHUNK)[:, :, None]
    ang_b = ang(jnp.arange(CHUNK, dtype=jnp.int32))[:, None, :]
    cos_t = (jnp.cos(ang_a) * jnp.cos(ang_b) - jnp.sin(ang_a) * jnp.sin(ang_b)).reshape(l, l)
    sin_t = (jnp.sin(ang_a) * jnp.cos(ang_b) + jnp.cos(ang_a) * jnp.sin(ang_b)).reshape(l, l)
    alt = jnp.where(k % 2 == 0, 1.0, -1.0).astype(F32)
    sin_f = sin_t.at[0, :].set(alt)
    return cos_t.astype(BF16), sin_f.astype(BF16), sin_f.T.astype(BF16)


def _filter_kernel(feats_ref, featr_ref, w1_ref, b1_ref, fr_ref, w2_ref, b2_ref, w3c_ref, w3a_ref, dl_ref,
                   h2_ref, h2r_ref, hid_ref, hidr_ref):
    @pl.when((pl.program_id(0) == 0) & (pl.program_id(1) == 0))
    def _():
        for f_ref, h_ref in ((feats_ref, hid_ref), (featr_ref, hidr_ref)):
            hid = jnp.sin(fr_ref[...] * (_dot_hi(f_ref[...], w1_ref[...]) + b1_ref[...]))
            h_ref[...] = jnp.sin(fr_ref[...] * (_dot_hi(hid, w2_ref[...]) + b2_ref[...]))

    def raw(h_ref, f_ref, w3_ref):
        return _dot3(h_ref[...], w3_ref[...]) * jnp.exp(-f_ref[:, 0:1] * dl_ref[...])

    c_f = raw(hid_ref, feats_ref, w3c_ref)
    a_f = raw(hid_ref, feats_ref, w3a_ref)
    den_c = jnp.sum(jnp.abs(c_f), axis=0, keepdims=True) + NORM_EPS
    den_a = jnp.sum(jnp.abs(a_f), axis=0, keepdims=True) + NORM_EPS
    c_f = c_f / den_c
    a_f = a_f / den_a
    row0 = lax.broadcasted_iota(jnp.int32, c_f.shape, 0) == 0
    centre = c_f[0:1] + a_f[0:1]
    h2_ref[0, 0] = jnp.where(row0, 0.0, raw(hidr_ref, featr_ref, w3a_ref) / den_a)
    h2_ref[0, 1] = jnp.where(row0, centre, c_f)
    h2r_ref[0, 0] = jnp.where(row0, 0.0, raw(hidr_ref, featr_ref, w3c_ref) / den_c)
    h2r_ref[0, 1] = jnp.where(row0, centre, a_f)


def _filter_feats(pos, l, n_emb):
    t = pos / max(l - 1, 1)
    ang = 2.0 * math.pi * pos / l
    bands = jnp.linspace(1e-4, FILTER_BANDS - 1, FILTER_BANDS, dtype=F32)
    feats = jnp.concatenate([t[:, None], jnp.cos(ang[:, None] * bands), -jnp.sin(ang[:, None] * bands)], axis=-1)
    return jnp.pad(feats, ((0, 0), (0, LANES - n_emb)))


def _hy_filters(l, w1, b1, freq, w2, b2, w3, dh):
    n_emb, n_hid = w1.shape
    pos = jnp.arange(l, dtype=F32)
    feats = _filter_feats(pos, l, n_emb)
    feats_r = _filter_feats(l - pos, l, n_emb)
    pc = LANES - n_hid
    w1p = jnp.pad(w1, ((0, LANES - n_emb), (0, pc)))
    w2p = jnp.pad(w2, ((0, pc), (0, pc)))
    w3p = jnp.pad(w3, ((0, pc), (0, 0)))
    row = lambda a: jnp.pad(a, (0, pc))[None]
    deltas = jnp.abs(jnp.linspace(MIN_DECAY, MAX_DECAY, dh, dtype=F32))[None]
    nct = dh // LANES
    const = lambda shape: pl.BlockSpec(shape, lambda o, c: (0, 0))
    o_spec = pl.BlockSpec((1, 2, l, LANES), lambda o, c: (o, 0, 0, c))
    return pl.pallas_call(
        _filter_kernel,
        grid=(HYENA_ORDER, nct),
        in_specs=[const((l, LANES)), const((l, LANES)), const((LANES, LANES)), const((1, LANES)),
                  const((1, LANES)), const((LANES, LANES)), const((1, LANES)),
                  pl.BlockSpec((LANES, LANES), lambda o, c: (0, o * 2 * nct + c)),
                  pl.BlockSpec((LANES, LANES), lambda o, c: (0, o * 2 * nct + nct + c)),
                  pl.BlockSpec((1, LANES), lambda o, c: (0, c))],
        out_specs=[o_spec, o_spec],
        out_shape=[jax.ShapeDtypeStruct((HYENA_ORDER, 2, l, dh), F32)] * 2,
        scratch_shapes=[pltpu.VMEM((l, LANES), F32)] * 2,
        compiler_params=_cparams(("arbitrary", "arbitrary"), 56),
        name="hy_filter",
    )(feats, feats_r, w1p, row(b1), row(freq), w2p, row(b2), w3p, w3p, deltas)


def _spectrum_kernel(c_ref, s_ref, h2_ref, h2r_ref, a_ref, nyq_ref, hsin_ref, *, p):
    pos = h2_ref[0, 0]
    neg = h2r_ref[0, 0]
    is0 = lax.broadcasted_iota(jnp.int32, pos.shape, 0) == 0
    h_sum = jnp.where(is0, pos, pos + neg)
    sign = jnp.where(lax.broadcasted_iota(jnp.int32, pos.shape, 0) % 2 == 0, 1.0, -1.0)
    nyq = jnp.sum(h_sum * sign, axis=0, keepdims=True)
    a = jnp.dot(c_ref[...], h_sum.astype(BF16), preferred_element_type=F32)
    s = jnp.dot(s_ref[...], (pos - neg).astype(BF16), preferred_element_type=F32)
    wk = jnp.where(is0, 0.5 / p, 1.0 / p)
    a_ref[0, 0] = a * wk
    nyq_ref[0, 0] = jnp.broadcast_to((nyq - a[0:1]) * (0.5 / p), nyq_ref.shape[2:])
    hsin_ref[0, 0] = jnp.where(is0, 0.0, s) * wk


def _hy_spectrum(tabs, h2, h2r, p):
    cos_t, sin_f, _ = tabs
    n_ord, _, l, dh = h2.shape
    nb = l // p
    n_win = 2 * nb - 1
    h2 = h2.reshape(n_ord, 2 * nb, p, dh)
    h2r = h2r.reshape(n_ord, 2 * nb, p, dh)
    tab = pl.BlockSpec((p, p), lambda o, m: (0, 0))
    o_spec = pl.BlockSpec((1, 1, p, dh), lambda o, m: (o, m, 0, 0))
    return pl.pallas_call(
        functools.partial(_spectrum_kernel, p=p),
        grid=(n_ord, n_win),
        in_specs=[tab, tab,
                  pl.BlockSpec((1, 1, p, dh), lambda o, m: (o, m + 1, 0, 0)),
                  pl.BlockSpec((1, 1, p, dh), lambda o, m: (o, 2 * nb - 1 - m, 0, 0))],
        out_specs=[o_spec, pl.BlockSpec((1, 1, 8, dh), lambda o, m: (o, m, 0, 0)), o_spec],
        out_shape=[jax.ShapeDtypeStruct((n_ord, n_win, p, dh), F32),
                   jax.ShapeDtypeStruct((n_ord, n_win, 8, dh), F32),
                   jax.ShapeDtypeStruct((n_ord, n_win, p, dh), F32)],
        compiler_params=_cparams(("parallel", "parallel"), 48),
        name="hy_spectrum",
    )(cos_t, sin_f, h2, h2r)


HY_ROWS = 8


def _hy_block_kernel(c_ref, sf_ref, si_ref, a_ref, nyq_ref, hsin_ref, sk_ref, ug_ref, wg_ref, *rest,
                     nb, period, first):
    if first:
        uv_ref, wv_ref, of_ref, ob_ref, xc_s, xs_s, yc_s, ys_s, v_s = rest
    else:
        vb_ref, vp_ref, z_ref, o_ref, xc_s, xs_s, yc_s, ys_s = rest
    p = c_ref.shape[0]
    for j in range(nb):
        rows = slice(j * p, (j + 1) * p)
        if first:
            vj = _conv_rows(uv_ref[0, rows, :], wv_ref[...], period)
            v_s[rows, :] = vj
            xj = vj.astype(BF16)
        else:
            xj = vb_ref[0, rows, :]
        xc_s[j] = jnp.dot(c_ref[...], xj, preferred_element_type=F32)
        xs_s[j] = jnp.dot(sf_ref[...], xj, preferred_element_type=F32)

    nyq_fix = []
    for i in range(nb):
        fix = jnp.zeros((1, LANES), F32)
        for j in range(nb):
            fix = fix + xs_s[j, 0:1, :] * nyq_ref[0, i - j + (nb - 1), 0:1, :]
        nyq_fix.append(fix)

    def rows_body(r, carry):
        rs = pl.ds(pl.multiple_of(r * HY_ROWS, HY_ROWS), HY_ROWS)
        xc = [xc_s[j, rs, :] for j in range(nb)]
        xs = [xs_s[j, rs, :] for j in range(nb)]
        yc = [None] * nb
        ys = [None] * nb
        for w in range(2 * nb - 1):
            a = a_ref[0, w, rs, :]
            hsin = hsin_ref[0, w, rs, :]
            for i in range(nb):
                j = i - (w - (nb - 1))
                if 0 <= j < nb:
                    tc = xc[j] * a - xs[j] * hsin
                    ts = xc[j] * hsin + xs[j] * a
                    yc[i] = tc if yc[i] is None else yc[i] + tc
                    ys[i] = ts if ys[i] is None else ys[i] + ts
        row0 = (lax.broadcasted_iota(jnp.int32, (HY_ROWS, LANES), 0) + r * HY_ROWS) == 0
        for i in range(nb):
            yc_s[i, rs, :] = yc[i]
            ys_s[i, rs, :] = ys[i] + jnp.where(row0, nyq_fix[i], 0.0)
        return carry

    lax.fori_loop(0, p // HY_ROWS, rows_body, 0)
    for i in range(nb):
        rows = slice(i * p, (i + 1) * p)
        conv = jnp.dot(c_ref[...], yc_s[i].astype(BF16), preferred_element_type=F32)
        conv = conv + jnp.dot(si_ref[...], ys_s[i].astype(BF16), preferred_element_type=F32)
        xg = _conv_rows(ug_ref[0, rows, :], wg_ref[...], period)
        if first:
            y = xg * (conv + sk_ref[0] * v_s[rows, :])
            of_ref[0, rows, :] = y
            ob_ref[0, rows, :] = y.astype(BF16)
        else:
            y = xg * (conv + sk_ref[0] * vp_ref[0, rows, :])
            o_ref[0, rows, :] = (y * _silu(z_ref[0, rows, :])).astype(BF16)


def _hy_block(tabs, spec, u, conv_w, skip, order, p, dh, period, y_prev=None):
    b, l, _ = u.shape
    nb = l // p
    n_win = 2 * nb - 1
    nct = dh // LANES
    taps = conv_w.shape[0]
    first = y_prev is None
    tab = pl.BlockSpec((p, p), lambda c, i: (0, 0))
    filt = pl.BlockSpec((1, n_win, p, LANES), lambda c, i: (order, 0, 0, c))
    nyq = pl.BlockSpec((1, n_win, 8, LANES), lambda c, i: (order, 0, 0, c))
    u_cols = lambda k: pl.BlockSpec((1, l, LANES), lambda c, i, k=k: (i, 0, k * nct + c))
    w_cols = lambda k: pl.BlockSpec((taps, LANES), lambda c, i, k=k: (0, k * nct + c))
    col = pl.BlockSpec((1, l, LANES), lambda c, i: (i, 0, c))
    in_specs = [tab, tab, tab, filt, nyq, filt, pl.BlockSpec((1, 1, LANES), lambda c, i: (order, 0, c)),
                u_cols(order), w_cols(order)]
    args = [*tabs, *spec, skip[:, None, :], u, conv_w]
    scratch = [pltpu.VMEM((nb, p, LANES), F32)] * 4
    if first:
        in_specs += [u_cols(2), w_cols(2)]
        args += [u, conv_w]
        out_specs = [col, col]
        out_shape = [jax.ShapeDtypeStruct((b, l, dh), F32), jax.ShapeDtypeStruct((b, l, dh), BF16)]
        scratch.append(pltpu.VMEM((l, LANES), F32))
    else:
        in_specs += [col, col, u_cols(3)]
        args += [y_prev[1], y_prev[0], u]
        out_specs = col
        out_shape = jax.ShapeDtypeStruct((b, l, dh), BF16)
    return pl.pallas_call(
        functools.partial(_hy_block_kernel, nb=nb, period=period, first=first),
        grid=(nct, b),
        in_specs=in_specs, out_specs=out_specs, out_shape=out_shape,
        scratch_shapes=scratch,
        compiler_params=_cparams(("parallel", "parallel"), 56),
        name="hy_block",
    )(*args)


def _hyena(u, conv_w, w1, b1, freq, w2, b2, w3, skip, lay, grid_mask):
    assert HYENA_ORDER == 2
    dh = lay["DH"]
    l = u.shape[1]
    p = min(HY_BLOCK, l)
    period = CHUNK if grid_mask else l
    assert p % period == 0
    tabs = _dft_tables(p)
    spec = _hy_spectrum(tabs, *_hy_filters(l, w1, b1, freq, w2, b2, w3, dh), p)
    y1 = _hy_block(tabs, spec, u, conv_w, skip, 0, p, dh, period)
    return _hy_block(tabs, spec, u, conv_w, skip, 1, p, dh, period, y_prev=y1)


def _layout(d):
    dg, dh, dm = 3 * d // 8, d // 4, 3 * d // 8
    lay = {"DG": dg, "DH": dh, "DM": dm, "gh": dg // HEAD_DIM, "mh": dm // HEAD_DIM}
    lay["HZ"] = 3 * dh
    lay["GQ"] = 4 * dh
    lay["GZ"] = lay["GQ"] + 3 * dg
    lay["GAB"] = lay["GZ"] + dg
    lay["MQ"] = lay["GAB"] + LANES
    lay["MO"] = lay["MQ"] + 3 * dm
    lay["MZ"] = lay["MO"] + dm
    lay["MG"] = lay["MZ"] + dm
    lay["NP"] = lay["MG"] + LANES
    return lay


def _pack_w_in(w, lay):
    dg, dh, dm, gh, mh = lay["DG"], lay["DH"], lay["DM"], lay["gh"], lay["mh"]
    sizes = (3 * dg, dg, 4 * gh, 3 * dh, dh, 3 * dm, dm, dm, 4 * mh)
    offs = [0]
    for s in sizes:
        offs.append(offs[-1] + s)
    seg = [w[:, offs[i]:offs[i + 1]] for i in range(len(sizes))]
    g_qkv, g_z, g_ab, h_p, h_z, m_qkv, m_o, m_z, m_g = seg
    padl = lambda a: jnp.pad(a, ((0, 0), (0, LANES - a.shape[1])))
    return jnp.concatenate([h_p, h_z, g_qkv, g_z, padl(g_ab), m_qkv, m_o, m_z, padl(m_g)], axis=1).astype(BF16)


def _gate_rows(u, off, n):
    b, l, _ = u.shape
    return jnp.transpose(u[:, :, off:off + n], (0, 2, 1)).reshape(b, n, l // LANES, LANES)


def kernel(x, c, ctx, c_ctx, norm_w, mod_w, mod_b, w_in, gdn_conv, gdn_a_log, gdn_dt_bias, gdn_norm, hy_conv,
           hy_w1, hy_b1, hy_freq, hy_w2, hy_b2, hy_w3, hy_skip, ml_gate_bias, ml_norm, w_out, final_norm):
    b, l, d = x.shape
    lc = ctx.shape[1]
    depth = norm_w.shape[0]
    lay = _layout(d)
    dg, dh = lay["DG"], lay["DH"]
    assert b < COND_ROWS and l % (2 * LANES) == 0 and lc % (2 * LANES) == 0 and d % 1024 == 0
    assert l % min(HY_BLOCK, l) == 0
    cond = jnp.zeros((COND_ROWS, d), F32).at[:b].set(c).at[b].set(c_ctx)
    all_mods = _adaln(cond, mod_w, mod_b[:, None, :])
    for layer in range(depth):
        last = layer == depth - 1
        mods = all_mods[layer]
        sh, sc, gt = mods[:, :d], mods[:, d:2 * d], mods[:, 2 * d:]
        lat = lambda m: m[:b, None, :]
        cx = lambda m: jnp.broadcast_to(m[b][None, None, :], (b, 1, d))
        wp = _pack_w_in(w_in[layer], lay)
        nw = norm_w[layer][None]
        u_l = _inproj(x, nw, lat(sc), lat(sh), wp)
        u_c = _inproj(ctx, nw, cx(sc), cx(sh), wp)
        g_rows = lambda u, off, n: _gate_rows(u, off, n)
        yg_l, yg_c = _gdn(u_l, u_c, g_rows(u_l, lay["GAB"], 4 * lay["gh"]), g_rows(u_c, lay["GAB"], 4 * lay["gh"]),
                          gdn_conv[layer], gdn_a_log[layer], gdn_dt_bias[layer], gdn_norm[layer][None], lay,
                          not last)
        ym_l, ym_c = _mlstm(u_l, u_c, g_rows(u_l, lay["MG"], 4 * lay["mh"]), g_rows(u_c, lay["MG"], 4 * lay["mh"]),
                            ml_gate_bias[layer], ml_norm[layer][None], lay, not last)
        hy = (hy_conv[layer], hy_w1[layer], hy_b1[layer], hy_freq[layer], hy_w2[layer], hy_b2[layer],
              hy_w3[layer], hy_skip[layer])
        yh_l = _hyena(u_l, *hy, lay, True)
        wo = w_out[layer].astype(BF16)
        wg, wh, wm = wo[:dg], wo[dg:dg + dh], wo[dg + dh:]
        fw = final_norm[None]
        x = _outproj(x, yg_l, yh_l, ym_l, wg, wh, wm, lat(gt), fw, last)
        if not last:
            yh_c = _hyena(u_c, *hy, lay, False)
            ctx = _outproj(ctx, yg_c, yh_c, ym_c, wg, wh, wm, cx(gt), fw, False)
    return x
```

```python
import functools
import math

import jax
import jax.numpy as jnp
from jax import lax
from jax.experimental import pallas as pl
from jax.experimental.pallas import tpu as pltpu

HEAD_DIM = 128
CHUNK = 64
LANES = 128
NORM_EPS = 1e-6
HYENA_ORDER = 2
HY_BLOCK = 512
FILTER_BANDS = 16
DECAY_TARGET = 1e-2
MIN_DECAY = math.log(DECAY_TARGET) / 1.5
MAX_DECAY = math.log(DECAY_TARGET) / 0.3
COND_ROWS = 16

F32 = jnp.float32
BF16 = jnp.bfloat16
HI = lax.Precision.HIGHEST


def _cparams(sem, vmem_mb):
    return pltpu.CompilerParams(dimension_semantics=sem, vmem_limit_bytes=vmem_mb << 20)


def _dot(a, b):
    return jnp.dot(a.astype(BF16), b.astype(BF16), preferred_element_type=F32)


def _dot_hi(a, b):
    return jnp.dot(a, b, precision=HI, preferred_element_type=F32)


def _dot_nt(a, b):
    return lax.dot_general(a.astype(BF16), b.astype(BF16), (((1,), (1,)), ((), ())),
                           preferred_element_type=F32)


def _sigmoid(x):
    return 0.5 * jnp.tanh(0.5 * x) + 0.5


def _silu(x):
    return x * _sigmoid(x)


def _softplus(x):
    return jnp.maximum(x, 0.0) + jnp.log(1.0 + jnp.exp(-jnp.abs(x)))


def _pick_tile(n, cap, unit):
    t = (min(n, cap) // unit) * unit
    while n % t:
        t -= unit
    return t


def _adaln_kernel(c_ref, w_ref, b_ref, o_ref):
    o_ref[0] = _dot_hi(_silu(c_ref[...]), w_ref[0]) + b_ref[0]


def _adaln(cond, w, b):
    depth, d, n = w.shape
    tn = _pick_tile(n, 768, LANES)
    return pl.pallas_call(
        _adaln_kernel,
        grid=(depth, n // tn),
        in_specs=[pl.BlockSpec((COND_ROWS, d), lambda i, j: (0, 0)),
                  pl.BlockSpec((1, d, tn), lambda i, j: (i, 0, j)),
                  pl.BlockSpec((1, 1, tn), lambda i, j: (i, 0, j))],
        out_specs=pl.BlockSpec((1, COND_ROWS, tn), lambda i, j: (i, 0, j)),
        out_shape=jax.ShapeDtypeStruct((depth, COND_ROWS, n), F32),
        compiler_params=_cparams(("parallel", "parallel"), 40),
        name="adaln",
    )(cond, w, b)


INPROJ_NORM_ROWS = 256


def _inproj_kernel(x_ref, nw_ref, sc_ref, sh_ref, w_ref, o_ref, xn_ref):
    first = pl.program_id(2) == 0

    @pl.when(first)
    def _():
        rows = _pick_tile(xn_ref.shape[0], INPROJ_NORM_ROWS, 8)
        for s in range(xn_ref.shape[0] // rows):
            sl = slice(s * rows, (s + 1) * rows)
            x = x_ref[0, sl, :]
            r = lax.rsqrt(jnp.mean(x * x, axis=-1, keepdims=True) + NORM_EPS)
            y = (x * r * nw_ref[...]) * (1.0 + sc_ref[0]) + sh_ref[0]
            y = y.astype(BF16)
            xn_ref[sl, :] = y
            o_ref[0, sl, :] = jnp.dot(y, w_ref[...], preferred_element_type=F32)

    @pl.when(jnp.logical_not(first))
    def _():
        o_ref[0] = jnp.dot(xn_ref[...], w_ref[...], preferred_element_type=F32)


def _inproj(x, nw, sc, sh, wp):
    b, l, d = x.shape
    n = wp.shape[1]
    tm = _pick_tile(l, 1024, 8)
    tn = _pick_tile(n, 1280, LANES)
    return pl.pallas_call(
        _inproj_kernel,
        grid=(b, l // tm, n // tn),
        in_specs=[pl.BlockSpec((1, tm, d), lambda i, m, j: (i, m, 0)),
                  pl.BlockSpec((1, d), lambda i, m, j: (0, 0)),
                  pl.BlockSpec((1, 1, d), lambda i, m, j: (i, 0, 0)),
                  pl.BlockSpec((1, 1, d), lambda i, m, j: (i, 0, 0)),
                  pl.BlockSpec((d, tn), lambda i, m, j: (0, j))],
        out_specs=pl.BlockSpec((1, tm, tn), lambda i, m, j: (i, m, j)),
        out_shape=jax.ShapeDtypeStruct((b, l, n), F32),
        scratch_shapes=[pltpu.VMEM((tm, d), BF16)],
        compiler_params=_cparams(("parallel", "parallel", "arbitrary"), 56),
        name="inproj",
    )(x, nw, sc, sh, wp)


def _outproj_kernel(x_ref, yg_ref, yh_ref, ym_ref, wg_ref, wh_ref, wm_ref, gt_ref, fw_ref, o_ref, *, final):
    acc = jnp.dot(yg_ref[0], wg_ref[...], preferred_element_type=F32)
    acc = acc + jnp.dot(yh_ref[0], wh_ref[...], preferred_element_type=F32)
    acc = acc + jnp.dot(ym_ref[0], wm_ref[...], preferred_element_type=F32)
    xn = x_ref[0] + gt_ref[0] * acc
    if final:
        r = lax.rsqrt(jnp.mean(xn * xn, axis=-1, keepdims=True) + NORM_EPS)
        xn = xn * r * fw_ref[...]
    o_ref[0] = xn


def _outproj(x, yg, yh, ym, wg, wh, wm, gt, fw, final):
    b, l, d = x.shape
    tm = _pick_tile(l, 512, 8)
    row = lambda w: pl.BlockSpec((1, tm, w), lambda i, m: (i, m, 0))
    full = lambda a: pl.BlockSpec(a.shape, lambda i, m: (0, 0))
    return pl.pallas_call(
        functools.partial(_outproj_kernel, final=final),
        grid=(b, l // tm),
        in_specs=[row(d), row(yg.shape[2]), row(yh.shape[2]), row(ym.shape[2]),
                  full(wg), full(wh), full(wm),
                  pl.BlockSpec((1, 1, d), lambda i, m: (i, 0, 0)),
                  pl.BlockSpec((1, d), lambda i, m: (0, 0))],
        out_specs=row(d),
        out_shape=jax.ShapeDtypeStruct((b, l, d), F32),
        compiler_params=_cparams(("parallel", "parallel"), 48),
        name="outproj",
    )(x, yg, yh, ym, wg, wh, wm, gt, fw)


def _conv_rows(x, w, period):
    rows = x.shape[0]
    taps = w.shape[0]
    pad = taps // 2
    pos = lax.broadcasted_iota(jnp.int32, x.shape, 0) % period
    y = None
    for j in range(taps):
        off = j - pad
        if off == 0:
            term = x * w[j:j + 1]
        else:
            shifted = pltpu.roll(x, (-off) % rows, axis=0)
            ok = (pos >= -off) if off < 0 else (pos < period - off)
            term = jnp.where(ok, shifted, 0.0) * w[j:j + 1]
        y = term if y is None else y + term
    return y


def _seg_cumsum(x, reverse):
    lane = lax.broadcasted_iota(jnp.int32, x.shape, 1) % CHUNK
    s = 1
    while s < CHUNK:
        if reverse:
            shifted = pltpu.roll(x, LANES - s, axis=1)
            ok = lane < CHUNK - s
        else:
            shifted = pltpu.roll(x, s, axis=1)
            ok = lane >= s
        x = x + jnp.where(ok, shifted, 0.0)
        s *= 2
    return x


def _pair_masks(d):
    ii = lax.broadcasted_iota(jnp.int32, (LANES, LANES), 0)
    jj = lax.broadcasted_iota(jnp.int32, (LANES, LANES), 1)
    lo = (ii // CHUNK) * CHUNK
    eye = ii == jj
    if d == 0:
        return eye, (jj >= lo) & (jj <= ii), (jj >= lo) & (jj < ii)
    return eye, (jj < lo + CHUNK) & (jj >= ii), (jj < lo + CHUNK) & (jj > ii)


def _to_col(row, eye):
    return jnp.sum(jnp.where(eye, jnp.broadcast_to(row, eye.shape), 0.0), axis=1, keepdims=True)


def _split_bf16(a):
    hi = a.astype(BF16)
    return hi, (a - hi.astype(F32)).astype(BF16)


def _col_bcast(row, eye):
    x = jnp.where(eye, jnp.broadcast_to(row, eye.shape), 0.0)
    hi = x.astype(BF16)
    mid, lo = _split_bf16(x - hi.astype(F32))
    ones = jnp.ones((3 * eye.shape[1], LANES), BF16)
    return jnp.dot(jnp.concatenate([hi, mid, lo], axis=1), ones, preferred_element_type=F32)


def _dot3(a, b):
    a_hi, a_lo = _split_bf16(a)
    b_hi, b_lo = _split_bf16(b)
    return jnp.dot(jnp.concatenate([a_hi, a_hi, a_lo], axis=1), jnp.concatenate([b_hi, b_lo, b_hi], axis=0),
                   preferred_element_type=F32)


def _head_out_tiles(acc_refs, base, nw_ref, z_ref, y_ref, n_rows, og_ref=None):
    tile = _pick_tile(n_rows, 256, 8)

    def body(i, carry):
        r = pl.multiple_of(i * tile, tile)
        o = acc_refs[0][pl.ds(base + r, tile), :]
        for acc_ref in acc_refs[1:]:
            o = o + acc_ref[pl.ds(base + r, tile), :]
        if og_ref is not None:
            o = _sigmoid(og_ref[0, pl.ds(r, tile), :]) * o
        o = o * lax.rsqrt(jnp.mean(o * o, axis=-1, keepdims=True) + NORM_EPS) * nw_ref[...]
        y_ref[0, pl.ds(r, tile), :] = (o * _silu(z_ref[0, pl.ds(r, tile), :])).astype(y_ref.dtype)
        return carry

    lax.fori_loop(0, n_rows // tile, body, 0)


def _ml_rows(g_refs, bias_ref, h, n_heads, rows_ref):
    li_f = g_refs[0][0, 0] + bias_ref[h]
    lf_f = -_softplus(-(g_refs[1][0, 0] + bias_ref[n_heads + h]))
    li_b = g_refs[2][0, 0] + bias_ref[2 * n_heads + h]
    lf_b = -_softplus(-(g_refs[3][0, 0] + bias_ref[3 * n_heads + h]))
    rows_ref[0] = _seg_cumsum(lf_f, False)
    rows_ref[1] = li_f
    rows_ref[2] = _seg_cumsum(lf_b, True)
    rows_ref[3] = li_b


def _ml_prep_cols(q_ref, k_ref, rows_ref, chains):
    eye = _pair_masks(0)[0]
    lane_i = lax.broadcasted_iota(jnp.int32, (1, LANES), 1)
    st = []
    for d, p in chains:
        rows = pl.ds(pl.multiple_of(p * LANES, LANES), LANES)
        k = k_ref[0, rows, :] * (HEAD_DIM ** -0.5)
        b_row = rows_ref[2 * d, pl.ds(p, 1), :]
        li_row = rows_ref[2 * d + 1, pl.ds(p, 1), :]
        last = CHUNK - 1 if d == 0 else 0
        b_tot = (b_row[:, last:last + 1], b_row[:, CHUNK + last:CHUNK + last + 1])
        end_row = jnp.where(lane_i < CHUNK, b_tot[0], b_tot[1]) - b_row + li_row
        e_max = (jnp.max(end_row[:, :CHUNK], axis=1, keepdims=True),
                 jnp.max(end_row[:, CHUNK:], axis=1, keepdims=True))
        st.append(dict(d=d, rows=rows, k=k, b_row=b_row, li_row=li_row, b_tot=b_tot, e_max=e_max,
                       b_cb=_col_bcast(b_row, eye), end_cb=_col_bcast(end_row, eye),
                       qk=_dot_nt(q_ref[0, rows, :], k)))
    return st


def _ml_prep_intra(st):
    for c in st:
        _, incl, _ = _pair_masks(c["d"])
        dlog = jnp.where(incl, c["b_cb"] - c["b_row"] + c["li_row"], -jnp.inf)
        c["rowmax"] = jnp.max(dlog, axis=1, keepdims=True)
        c["p_hi"], c["p_lo"] = _split_bf16(jnp.exp(dlog - c["rowmax"]) * c["qk"])


def _ml_prep_state(st):
    row_i = lax.broadcasted_iota(jnp.int32, (LANES, 1), 0)
    lane_sq = lax.broadcasted_iota(jnp.int32, (LANES, LANES), 1)
    out = []
    for c in st:
        kw_t = (c["k"] * jnp.exp(c["end_cb"] - jnp.where(row_i < CHUNK, c["e_max"][0], c["e_max"][1]))).T
        kw_t2 = jnp.concatenate([jnp.where(lane_sq < CHUNK, kw_t, 0.0), jnp.where(lane_sq < CHUNK, 0.0, kw_t)],
                                axis=0).astype(BF16)
        out.append((c["b_cb"], c["rowmax"], c["p_hi"], c["p_lo"], kw_t2, *c["b_tot"], *c["e_max"]))
    return tuple(out)


def _ml_steps_state(q_ref, v_ref, cnst, pre, chains):
    ones = jnp.ones((LANES, LANES), BF16)
    st = []
    for d, p, half in chains:
        _, _, p_hi, p_lo, kw_t2 = pre[d][:5]
        sl = slice(half * CHUNK, (half + 1) * CHUNK)
        v_ones = jnp.concatenate([v_ref[0, pl.ds(pl.multiple_of(p * LANES, LANES), LANES), :].astype(BF16), ones],
                                 axis=1)
        rhs = jnp.concatenate([v_ones, jnp.concatenate([jnp.zeros_like(ones), ones], axis=1)], axis=0)
        pv_ps = jnp.dot(jnp.concatenate([p_hi[sl], p_lo[sl]], axis=1), rhs, preferred_element_type=F32)
        d_cn = jnp.dot(kw_t2[half * LANES:(half + 1) * LANES], v_ones, preferred_element_type=F32)
        st.append((pv_ps, d_cn))
    for n, (d, p, half) in enumerate(chains):
        r0 = pl.ds(pl.multiple_of(p * LANES + half * CHUNK, CHUNK), CHUNK)
        cn = cnst[d]
        st[n] = (r0, cn, _dot(q_ref[0, r0, :], cn), *st[n])
    return st


def _ml_steps_update(pre, acc_refs, cnst, mst, chains, st):
    for (d, p, half), (r0, cn, q_cn, pv_ps, d_cn) in zip(chains, st):
        b_cb, rowmax = pre[d][:2]
        b_tot = pre[d][5 + half]
        e_max = pre[d][7 + half]
        sl = slice(half * CHUNK, (half + 1) * CHUNK)
        m_s = mst[d, 0:1, 0:1]
        inter = b_cb[sl] + m_s
        m_i = jnp.maximum(inter, rowmax[sl])
        w_inter = jnp.exp(inter - m_i)
        s_intra = jnp.exp(rowmax[sl] - m_i)
        num = w_inter * q_cn[:, :HEAD_DIM] + s_intra * pv_ps[:, :HEAD_DIM]
        den = w_inter * q_cn[:, HEAD_DIM:] + s_intra * pv_ps[:, HEAD_DIM:]
        acc_refs[d][r0, :] = num / jnp.maximum(jnp.abs(den), jnp.exp(-m_i))
        carry_log = b_tot + m_s
        m_new = jnp.maximum(carry_log, e_max)
        cnst[d] = jnp.exp(carry_log - m_new) * cn + jnp.exp(e_max - m_new) * d_cn
        mst[d] = jnp.broadcast_to(m_new, mst.shape[1:])


def _ml_scan(q_ref, k_ref, v_ref, rows_ref, acc_refs, cnst, mst, n_pairs):
    def chains_of(t):
        return [(0, t), (1, n_pairs - 1 - t), (0, t + 1), (1, n_pairs - 2 - t)]

    def prep(chains):
        st = _ml_prep_cols(q_ref, k_ref, rows_ref, chains)
        _ml_prep_intra(st)
        return _ml_prep_state(st)

    def trip(i, pre, prepare_next):
        t = 2 * i
        st = _ml_prep_cols(q_ref, k_ref, rows_ref, chains_of(t + 2)) if prepare_next else None
        nxt = pre
        for j, tt in enumerate((t, t + 1)):
            pf, pb = tt, n_pairs - 1 - tt
            pre_j = pre[2 * j:2 * j + 2]
            for halves in ([(0, pf, 0), (1, pb, 1)], [(0, pf, 1), (1, pb, 0)]):
                held = _ml_steps_state(q_ref, v_ref, cnst, pre_j, halves)
                if prepare_next and j == 0 and halves[0][2] == 0:
                    _ml_prep_intra(st)
                elif prepare_next and j == 0:
                    nxt = _ml_prep_state(st)
                _ml_steps_update(pre_j, acc_refs, cnst, mst, halves, held)
        return nxt

    n_trips = n_pairs // 2
    pre = lax.fori_loop(0, n_trips - 1, lambda i, pre: trip(i, pre, True), prep(chains_of(0)))
    trip(n_trips - 1, pre, False)


def _mlstm_kernel(bias_ref, ql, kl, vl, ol, zl, qc, kc, vc, oc, zc, gl0, gl1, gl2, gl3, gc0, gc1, gc2, gc3,
                  nw_ref, *rest, n_heads, ctx_out):
    if ctx_out:
        yl_ref, yc_ref = rest[:2]
        rest = rest[2:]
    else:
        yl_ref, yc_ref = rest[0], None
        rest = rest[1:]
    cnst, mst, rows_l, rows_c, acc_lf, acc_lb, acc_cf, acc_cb = rest
    h = pl.program_id(1)
    cnst[...] = jnp.zeros(cnst.shape, F32)
    mst[...] = jnp.zeros(mst.shape, F32)
    _ml_rows((gc0, gc1, gc2, gc3), bias_ref, h, n_heads, rows_c)
    _ml_scan(qc, kc, vc, rows_c, (acc_cf, acc_cb), cnst, mst, rows_c.shape[1])
    _ml_rows((gl0, gl1, gl2, gl3), bias_ref, h, n_heads, rows_l)
    _ml_scan(ql, kl, vl, rows_l, (acc_lf, acc_lb), cnst, mst, rows_l.shape[1])
    _head_out_tiles((acc_lf, acc_lb), 0, nw_ref, zl, yl_ref, acc_lf.shape[0], og_ref=ol)
    if ctx_out:
        _head_out_tiles((acc_cf, acc_cb), 0, nw_ref, zc, yc_ref, acc_cf.shape[0], og_ref=oc)


def _head_block(l, blk):
    return pl.BlockSpec((1, l, HEAD_DIM), lambda b, h, blk=blk: (b, 0, blk + h))


def _gate_block(n_pairs, j, n_heads):
    return pl.BlockSpec((1, 1, n_pairs, LANES), lambda b, h, j=j: (b, j * n_heads + h, 0, 0))


def _mlstm(u_l, u_c, gt_l, gt_c, bias, nw, lay, ctx_out):
    b, l, _ = u_l.shape
    lc = u_c.shape[1]
    nh = lay["mh"]
    blk = lambda off: off // HEAD_DIM
    offs = [lay["MQ"], lay["MQ"] + lay["DM"], lay["MQ"] + 2 * lay["DM"], lay["MO"], lay["MZ"]]
    in_specs = [pl.BlockSpec(memory_space=pltpu.SMEM)]
    in_specs += [_head_block(l, blk(o)) for o in offs]
    in_specs += [_head_block(lc, blk(o)) for o in offs]
    in_specs += [_gate_block(l // LANES, j, nh) for j in range(4)]
    in_specs += [_gate_block(lc // LANES, j, nh) for j in range(4)]
    in_specs += [pl.BlockSpec((1, HEAD_DIM), lambda b_, h: (0, 0))]
    out_specs = [pl.BlockSpec((1, l, HEAD_DIM), lambda b_, h: (b_, 0, h))]
    out_shape = [jax.ShapeDtypeStruct((b, l, nh * HEAD_DIM), BF16)]
    if ctx_out:
        out_specs.append(pl.BlockSpec((1, lc, HEAD_DIM), lambda b_, h: (b_, 0, h)))
        out_shape.append(jax.ShapeDtypeStruct((b, lc, nh * HEAD_DIM), BF16))
    scratch = [pltpu.VMEM((2, HEAD_DIM, 2 * HEAD_DIM), F32), pltpu.VMEM((2, 8, LANES), F32),
               pltpu.VMEM((4, l // LANES, LANES), F32), pltpu.VMEM((4, lc // LANES, LANES), F32),
               pltpu.VMEM((l, HEAD_DIM), F32), pltpu.VMEM((l, HEAD_DIM), F32),
               pltpu.VMEM((lc, HEAD_DIM), F32), pltpu.VMEM((lc, HEAD_DIM), F32)]
    res = pl.pallas_call(
        functools.partial(_mlstm_kernel, n_heads=nh, ctx_out=ctx_out),
        grid=(b, nh),
        in_specs=in_specs, out_specs=out_specs, out_shape=out_shape,
        scratch_shapes=scratch,
        compiler_params=_cparams(("parallel", "parallel"), 48),
        name="mlstm",
    )(bias.reshape(-1), *([u_l] * 5), *([u_c] * 5), *([gt_l] * 4), *([gt_c] * 4), nw)
    return (res[0], res[1]) if ctx_out else (res[0], None)


def _gdn_rows(g_refs, alog_ref, dtb_ref, h, n_heads, rows_ref, base):
    n = g_refs[0].shape[2]
    g_f = -jnp.exp(alog_ref[h]) * _softplus(g_refs[0][0, 0] + dtb_ref[h])
    g_b = -jnp.exp(alog_ref[n_heads + h]) * _softplus(g_refs[1][0, 0] + dtb_ref[n_heads + h])
    rows_ref[0, base:base + n] = _seg_cumsum(g_f, False)
    rows_ref[1, base:base + n] = jax.nn.sigmoid(g_refs[2][0, 0])
    rows_ref[2, base:base + n] = _seg_cumsum(g_b, True)
    rows_ref[3, base:base + n] = jax.nn.sigmoid(g_refs[3][0, 0])


def _masked_taps(w, period):
    taps = w.shape[0]
    pad = taps // 2
    pos = lax.broadcasted_iota(jnp.int32, (period, w.shape[1]), 0)
    out = []
    for j in range(taps):
        off = j - pad
        ok = (pos >= -off) if off < 0 else (pos < period - off)
        out.append(jnp.where(ok, w[j:j + 1], 0.0))
    return out


def _conv_rows_pre(x, wm_ref, first):
    rows = x.shape[0]
    taps = wm_ref.shape[0] // 3
    period = wm_ref.shape[1]
    pad = taps // 2
    y = None
    for j in range(taps):
        off = j - pad
        shifted = x if off == 0 else pltpu.roll(x, (-off) % rows, axis=0)
        term = shifted * jnp.concatenate([wm_ref[first + j]] * (rows // period), axis=0)
        y = term if y is None else y + term
    return y


def _gdn_conv(src_refs, w_refs, dst_refs, base, n_units, unit_rows, period, wm_ref=None):
    taps = w_refs[0].shape[0]
    if wm_ref is not None:
        for idx in range(3):
            for j, wm in enumerate(_masked_taps(w_refs[idx][...], period)):
                wm_ref[idx * taps + j] = wm

    def body(i, carry):
        r = pl.multiple_of(i * unit_rows, unit_rows)
        for idx in range(3):
            x = src_refs[idx][0, pl.ds(r, unit_rows), :]
            if wm_ref is not None:
                t = _silu(_conv_rows_pre(x, wm_ref, idx * taps))
            else:
                t = _silu(_conv_rows(x, w_refs[idx][...], period))
            if idx < 2:
                t = t * lax.rsqrt(jnp.sum(t * t, axis=-1, keepdims=True) + NORM_EPS)
            if idx == 0:
                t = t * (HEAD_DIM ** -0.5)
            dst_refs[idx][pl.ds(base + r, unit_rows), :] = t
        return carry

    lax.fori_loop(0, n_units, body, 0)


N_NEUMANN = CHUNK.bit_length() - 2


def _side_masks(d):
    ii = lax.broadcasted_iota(jnp.int32, (CHUNK, LANES), 0)
    lane = lax.broadcasted_iota(jnp.int32, (CHUNK, LANES), 1)
    jj = lane % CHUNK
    first = lane < CHUNK
    if d == 0:
        return first, jj == ii, jj <= ii, jj < ii
    return first, jj == ii, jj >= ii, jj > ii


def _side(x, first):
    return jnp.where(first, x[:CHUNK], x[CHUNK:])


def _block_diag(x, first):
    zero = jnp.zeros_like(x)
    return jnp.concatenate([jnp.where(first, x, zero), jnp.where(first, zero, x)], axis=0)


def _dot3_side(a, b, first):
    a_hi, a_lo = _split_bf16(a)
    b_hi, b_lo = _split_bf16(b)
    rhs = jnp.concatenate([_block_diag(b_hi, first), _block_diag(b_lo, first), _block_diag(b_hi, first)], axis=0)
    return jnp.dot(jnp.concatenate([a_hi, a_hi, a_lo], axis=1), rhs, preferred_element_type=F32)


def _gdn_gates(rows_ref, d, p):
    eye = _pair_masks(0)[0]
    g_row = rows_ref[2 * d, pl.ds(p, 1), :]
    return dict(d=d, p=p, rows=pl.ds(pl.multiple_of(p * LANES, LANES), LANES), g_row=g_row,
                g_col=_to_col(g_row, eye), beta_col=_to_col(rows_ref[2 * d + 1, pl.ds(p, 1), :], eye))


def _gdn_prep_start(qs, ks, rows_ref, chains):
    st = []
    for d, p in chains:
        c = _gdn_gates(rows_ref, d, p)
        rows, g_row, g_col, beta_col = c["rows"], c["g_row"], c["g_col"], c["beta_col"]
        k = ks[rows, :]
        first, eye_s, incl, strict = _side_masks(d)
        dec = jnp.exp(jnp.where(incl, _side(g_col, first) - g_row, -jnp.inf))
        aqk = _block_diag(_side(_dot_nt(qs[rows, :], k), first) * dec, first).astype(BF16)
        m_low = jnp.where(strict, _side(beta_col, first) * _side(_dot_nt(k, k), first) * dec, 0.0)
        st.append(dict(c, aqk=aqk, pw=m_low, t_inv=jnp.where(eye_s, 1.0, 0.0) - m_low))
    return st


def _gdn_prep_neumann(st, after_square=None, after_product=None):
    first, eye_s, _, _ = _side_masks(0)
    eye_f = jnp.where(eye_s, 1.0, 0.0)
    for c in st:
        c["pw"] = _dot3_side(c["pw"], c["pw"], first)
    if after_square is not None:
        after_square()
    for c in st:
        c["t_inv"] = _dot3_side(c["t_inv"], eye_f + c["pw"], first)
    if after_product is not None:
        after_product()


NQ_ROWS = HEAD_DIM + CHUNK


def _gdn_prep_solve(qs, ks, vs, st):
    row_i = lax.broadcasted_iota(jnp.int32, (LANES, 1), 0)
    first = _side_masks(0)[0]
    for c in st:
        d, rows, g_row, g_col, beta_col = c["d"], c["rows"], c["g_row"], c["g_col"], c["beta_col"]
        k = ks[rows, :]
        e_g = jnp.exp(g_col)
        c["w_ut"] = _dot3(_block_diag(c["t_inv"], first),
                          jnp.concatenate([(beta_col * e_g) * k, beta_col * vs[rows, :]], axis=1)).astype(BF16)
        c["q_dec"] = qs[rows, :] * e_g
        last = CHUNK - 1 if d == 0 else 0
        g_tot_col = jnp.where(row_i < CHUNK, g_row[:, last:last + 1], g_row[:, CHUNK + last:CHUNK + last + 1])
        c["k_end_t"] = (k * jnp.exp(g_tot_col - g_col)).T.astype(BF16)


def _gdn_prep_finish(nq_s, r_s, acc_refs, st):
    row_i = lax.broadcasted_iota(jnp.int32, (LANES, 1), 0)
    for c in st:
        d, w_ut, q_dec = c["d"], c["w_ut"], c["q_dec"]
        zero = jnp.zeros_like(w_ut)
        for half in (0, 1):
            sl = slice(half * CHUNK, (half + 1) * CHUNK)
            lhs = jnp.concatenate([c["k_end_t"], c["aqk"][sl]], axis=0)
            mine = (row_i < CHUNK) if half == 0 else (row_i >= CHUNK)
            prod = jnp.dot(lhs, jnp.where(mine, w_ut, zero), preferred_element_type=F32)
            n_q = jnp.concatenate([prod[:HEAD_DIM, :HEAD_DIM], q_dec[sl] - prod[HEAD_DIM:, :HEAD_DIM]], axis=0)
            chunk = 2 * c["p"] + half
            nq_s[d, pl.ds(pl.multiple_of(chunk * NQ_ROWS, NQ_ROWS), NQ_ROWS), :] = n_q.astype(BF16)
            r_s[d, pl.ds(pl.multiple_of(chunk * HEAD_DIM, HEAD_DIM), HEAD_DIM), :] = prod[:HEAD_DIM, HEAD_DIM:]
            acc_refs[d][pl.ds(pl.multiple_of(chunk * CHUNK, CHUNK), CHUNK), :] = prod[HEAD_DIM:, HEAD_DIM:]


def _gdn_steps_state(nq_s, sst, chains):
    st = []
    for d, p, half in chains:
        chunk = 2 * p + half
        s = sst[d]
        n_q = nq_s[d, pl.ds(pl.multiple_of(chunk * NQ_ROWS, NQ_ROWS), NQ_ROWS), :]
        st.append((chunk, s, jnp.dot(n_q, s.astype(BF16), preferred_element_type=F32)))
    return st


def _gdn_steps_update(rows_ref, r_s, sst, acc_refs, chains, st):
    for (d, p, half), (chunk, s, prod) in zip(chains, st):
        lane = half * CHUNK + (CHUNK - 1 if d == 0 else 0)
        g_tot = rows_ref[2 * d, pl.ds(p, 1), lane:lane + 1]
        r0 = pl.ds(pl.multiple_of(chunk * CHUNK, CHUNK), CHUNK)
        acc_refs[d][r0, :] = acc_refs[d][r0, :] + prod[HEAD_DIM:]
        r_c = r_s[d, pl.ds(pl.multiple_of(chunk * HEAD_DIM, HEAD_DIM), HEAD_DIM), :]
        sst[d] = jnp.exp(g_tot) * s - prod[:HEAD_DIM] + r_c


def _gdn_kernel(alog_ref, dtb_ref, ql, kl, vl, zl, qc, kc, vc, zc, gl0, gl1, gl2, gl3, gc0, gc1, gc2, gc3,
                wq, wk, wv, nw_ref, *rest, n_heads, ctx_out):
    if ctx_out:
        yl_ref, yc_ref = rest[:2]
        rest = rest[2:]
    else:
        yl_ref, yc_ref = rest[0], None
        rest = rest[1:]
    sst, rows, qs, ks, vs, nq_s, r_s, acc_f, acc_b, nm_s, aq_s, wm_s = rest
    h = pl.program_id(1)
    n_l = ql.shape[1]
    n_c = qc.shape[1]
    nl = n_l // LANES
    nc = n_c // LANES
    sst[...] = jnp.zeros(sst.shape, F32)
    _gdn_rows((gl0, gl1, gl2, gl3), alog_ref, dtb_ref, h, n_heads, rows, 0)
    _gdn_rows((gc0, gc1, gc2, gc3), alog_ref, dtb_ref, h, n_heads, rows, nl)
    conv_rows = _pick_tile(n_l, 4 * CHUNK, CHUNK)
    _gdn_conv((ql, kl, vl), (wq, wk, wv), (qs, ks, vs), 0, n_l // conv_rows, conv_rows, CHUNK, wm_ref=wm_s)
    _gdn_conv((qc, kc, vc), (wq, wk, wv), (qs, ks, vs), n_l, 1, n_c, n_c)
    n_steps = nl + nc
    accs = (acc_f, acc_b)

    def fwd_pair(t):
        return jnp.where(t < nc, nl + t, t - nc)

    def bwd_pair(t):
        return n_steps - 1 - t

    n_trips = n_steps // 2
    n_early = (N_NEUMANN + 1) // 2

    def chains_of(trip):
        t = 2 * jnp.minimum(trip, n_trips - 1)
        return [(0, fwd_pair(t)), (1, bwd_pair(t)), (0, fwd_pair(t + 1)), (1, bwd_pair(t + 1))]

    def neumann(st, half=None):
        if half is None:
            _gdn_prep_neumann(st)
            return
        held = []
        _gdn_prep_neumann(st, lambda: held.append(_gdn_steps_state(nq_s, sst, half)),
                          lambda: _gdn_steps_update(rows, r_s, sst, accs, half, held[0]))

    def finish(st):
        _gdn_prep_solve(qs, ks, vs, st)
        _gdn_prep_finish(nq_s, r_s, accs, st)

    def halves_of(trip):
        out = []
        for t in (2 * trip, 2 * trip + 1):
            pf, pb = fwd_pair(t), bwd_pair(t)
            out += [[(0, pf, 0), (1, pb, 1)], [(0, pf, 1), (1, pb, 0)]]
        return out

    st = _gdn_prep_start(qs, ks, rows, chains_of(0))
    for _ in range(N_NEUMANN):
        neumann(st)
    finish(st)
    st = _gdn_prep_start(qs, ks, rows, chains_of(1))
    for _ in range(n_early):
        neumann(st)

    def park(st_part):
        for n, c in enumerate(st_part):
            nm_s[2 * n] = c["pw"]
            nm_s[2 * n + 1] = c["t_inv"]
            aq_s[n] = c["aqk"]

    park(st)

    def trip(i, carry):
        halves = halves_of(i)
        st_late = [dict(_gdn_gates(rows, d, p), pw=nm_s[2 * n], t_inv=nm_s[2 * n + 1], aqk=aq_s[n])
                   for n, (d, p) in enumerate(chains_of(i + 1))]
        st_early = _gdn_prep_start(qs, ks, rows, chains_of(i + 2))
        n_late = N_NEUMANN - n_early
        slot = 0
        for n in range(n_early):
            neumann(st_early, halves[slot] if slot < len(halves) else None)
            slot += 1
            if n < n_late:
                neumann(st_late, halves[slot] if slot < len(halves) else None)
                slot += 1
            if n == n_late - 1:
                _gdn_prep_solve(qs, ks, vs, st_late)
        _gdn_prep_finish(nq_s, r_s, accs, st_late)
        park(st_early)
        return carry

    lax.fori_loop(0, n_trips - 1, trip, 0)
    for half in halves_of(n_trips - 1):
        _gdn_steps_update(rows, r_s, sst, accs, half, _gdn_steps_state(nq_s, sst, half))
    _head_out_tiles((acc_f, acc_b), 0, nw_ref, zl, yl_ref, n_l)
    if ctx_out:
        _head_out_tiles((acc_f, acc_b), n_l, nw_ref, zc, yc_ref, n_c)


def _gdn(u_l, u_c, gt_l, gt_c, conv_w, a_log, dt_bias, nw, lay, ctx_out):
    b, l, _ = u_l.shape
    lc = u_c.shape[1]
    nh = lay["gh"]
    taps = conv_w.shape[0]
    blk = lambda off: off // HEAD_DIM
    offs = [lay["GQ"], lay["GQ"] + lay["DG"], lay["GQ"] + 2 * lay["DG"], lay["GZ"]]
    smem = pl.BlockSpec(memory_space=pltpu.SMEM)
    in_specs = [smem, smem]
    in_specs += [_head_block(l, blk(o)) for o in offs]
    in_specs += [_head_block(lc, blk(o)) for o in offs]
    in_specs += [_gate_block(l // LANES, j, nh) for j in range(4)]
    in_specs += [_gate_block(lc // LANES, j, nh) for j in range(4)]
    in_specs += [pl.BlockSpec((taps, HEAD_DIM), lambda b_, h, j=j: (0, j * nh + h)) for j in range(3)]
    in_specs += [pl.BlockSpec((1, HEAD_DIM), lambda b_, h: (0, 0))]
    out_specs = [pl.BlockSpec((1, l, HEAD_DIM), lambda b_, h: (b_, 0, h))]
    out_shape = [jax.ShapeDtypeStruct((b, l, nh * HEAD_DIM), BF16)]
    if ctx_out:
        out_specs.append(pl.BlockSpec((1, lc, HEAD_DIM), lambda b_, h: (b_, 0, h)))
        out_shape.append(jax.ShapeDtypeStruct((b, lc, nh * HEAD_DIM), BF16))
    lt = l + lc
    seq_f32 = pltpu.VMEM((lt, HEAD_DIM), F32)
    n_chunks = lt // CHUNK
    scratch = [pltpu.VMEM((2, HEAD_DIM, HEAD_DIM), F32), pltpu.VMEM((4, lt // LANES, LANES), F32),
               seq_f32, seq_f32, seq_f32,
               pltpu.VMEM((2, n_chunks * NQ_ROWS, HEAD_DIM), BF16), pltpu.VMEM((2, n_chunks * HEAD_DIM, HEAD_DIM), F32),
               seq_f32, seq_f32, pltpu.VMEM((8, CHUNK, LANES), F32), pltpu.VMEM((4, LANES, LANES), BF16),
               pltpu.VMEM((3 * taps, CHUNK, HEAD_DIM), F32)]
    res = pl.pallas_call(
        functools.partial(_gdn_kernel, n_heads=nh, ctx_out=ctx_out),
        grid=(b, nh),
        in_specs=in_specs, out_specs=out_specs, out_shape=out_shape,
        scratch_shapes=scratch,
        compiler_params=_cparams(("parallel", "parallel"), 56),
        name="gdn",
    )(a_log.reshape(-1), dt_bias.reshape(-1), *([u_l] * 4), *([u_c] * 4), *([gt_l] * 4), *([gt_c] * 4),
      *([conv_w] * 3), nw)
    return (res[0], res[1]) if ctx_out else (res[0], None)


def _dft_tables(l):
    k = jnp.arange(l, dtype=jnp.int32)
    ang = lambda t: ((k[:, None] * t[None, :]) % (2 * l)).astype(F32) * (math.pi / l)
    ang_a = ang(jnp.arange(l // CHUNK, dtype=jnp.int32) * CHUNK)[:, :, None]
    ang_b = ang(jnp.arange(CHUNK, dtype=jnp.int32))[:, None, :]
    cos_t = (jnp.cos(ang_a) * jnp.cos(ang_b) - jnp.sin(ang_a) * jnp.sin(ang_b)).reshape(l, l)
    sin_t = (jnp.sin(ang_a) * jnp.cos(ang_b) + jnp.cos(ang_a) * jnp.sin(ang_b)).reshape(l, l)
    alt = jnp.where(k % 2 == 0, 1.0, -1.0).astype(F32)
    sin_f = sin_t.at[0, :].set(alt)
    return cos_t.astype(BF16), sin_f.astype(BF16), sin_f.T.astype(BF16)


def _filter_kernel(feats_ref, featr_ref, w1_ref, b1_ref, fr_ref, w2_ref, b2_ref, w3c_ref, w3a_ref, dl_ref,
                   h2_ref, h2r_ref, hid_ref, hidr_ref):
    @pl.when((pl.program_id(0) == 0) & (pl.program_id(1) == 0))
    def _():
        for f_ref, h_ref in ((feats_ref, hid_ref), (featr_ref, hidr_ref)):
            hid = jnp.sin(fr_ref[...] * (_dot_hi(f_ref[...], w1_ref[...]) + b1_ref[...]))
            h_ref[...] = jnp.sin(fr_ref[...] * (_dot_hi(hid, w2_ref[...]) + b2_ref[...]))

    def raw(h_ref, f_ref, w3_ref):
        return _dot3(h_ref[...], w3_ref[...]) * jnp.exp(-f_ref[:, 0:1] * dl_ref[...])

    c_f = raw(hid_ref, feats_ref, w3c_ref)
    a_f = raw(hid_ref, feats_ref, w3a_ref)
    den_c = jnp.sum(jnp.abs(c_f), axis=0, keepdims=True) + NORM_EPS
    den_a = jnp.sum(jnp.abs(a_f), axis=0, keepdims=True) + NORM_EPS
    c_f = c_f / den_c
    a_f = a_f / den_a
    row0 = lax.broadcasted_iota(jnp.int32, c_f.shape, 0) == 0
    centre = c_f[0:1] + a_f[0:1]
    h2_ref[0, 0] = jnp.where(row0, 0.0, raw(hidr_ref, featr_ref, w3a_ref) / den_a)
    h2_ref[0, 1] = jnp.where(row0, centre, c_f)
    h2r_ref[0, 0] = jnp.where(row0, 0.0, raw(hidr_ref, featr_ref, w3c_ref) / den_c)
    h2r_ref[0, 1] = jnp.where(row0, centre, a_f)


def _filter_feats(pos, l, n_emb):
    t = pos / max(l - 1, 1)
    ang = 2.0 * math.pi * pos / l
    bands = jnp.linspace(1e-4, FILTER_BANDS - 1, FILTER_BANDS, dtype=F32)
    feats = jnp.concatenate([t[:, None], jnp.cos(ang[:, None] * bands), -jnp.sin(ang[:, None] * bands)], axis=-1)
    return jnp.pad(feats, ((0, 0), (0, LANES - n_emb)))


def _hy_filters(l, w1, b1, freq, w2, b2, w3, dh):
    n_emb, n_hid = w1.shape
    pos = jnp.arange(l, dtype=F32)
    feats = _filter_feats(pos, l, n_emb)
    feats_r = _filter_feats(l - pos, l, n_emb)
    pc = LANES - n_hid
    w1p = jnp.pad(w1, ((0, LANES - n_emb), (0, pc)))
    w2p = jnp.pad(w2, ((0, pc), (0, pc)))
    w3p = jnp.pad(w3, ((0, pc), (0, 0)))
    row = lambda a: jnp.pad(a, (0, pc))[None]
    deltas = jnp.abs(jnp.linspace(MIN_DECAY, MAX_DECAY, dh, dtype=F32))[None]
    nct = dh // LANES
    const = lambda shape: pl.BlockSpec(shape, lambda o, c: (0, 0))
    o_spec = pl.BlockSpec((1, 2, l, LANES), lambda o, c: (o, 0, 0, c))
    return pl.pallas_call(
        _filter_kernel,
        grid=(HYENA_ORDER, nct),
        in_specs=[const((l, LANES)), const((l, LANES)), const((LANES, LANES)), const((1, LANES)),
                  const((1, LANES)), const((LANES, LANES)), const((1, LANES)),
                  pl.BlockSpec((LANES, LANES), lambda o, c: (0, o * 2 * nct + c)),
                  pl.BlockSpec((LANES, LANES), lambda o, c: (0, o * 2 * nct + nct + c)),
                  pl.BlockSpec((1, LANES), lambda o, c: (0, c))],
        out_specs=[o_spec, o_spec],
        out_shape=[jax.ShapeDtypeStruct((HYENA_ORDER, 2, l, dh), F32)] * 2,
        scratch_shapes=[pltpu.VMEM((l, LANES), F32)] * 2,
        compiler_params=_cparams(("arbitrary", "arbitrary"), 56),
        name="hy_filter",
    )(feats, feats_r, w1p, row(b1), row(freq), w2p, row(b2), w3p, w3p, deltas)


def _spectrum_kernel(c_ref, s_ref, h2_ref, h2r_ref, a_ref, nyq_ref, hsin_ref, *, p):
    pos = h2_ref[0, 0]
    neg = h2r_ref[0, 0]
    is0 = lax.broadcasted_iota(jnp.int32, pos.shape, 0) == 0
    h_sum = jnp.where(is0, pos, pos + neg)
    sign = jnp.where(lax.broadcasted_iota(jnp.int32, pos.shape, 0) % 2 == 0, 1.0, -1.0)
    nyq = jnp.sum(h_sum * sign, axis=0, keepdims=True)
    a = jnp.dot(c_ref[...], h_sum.astype(BF16), preferred_element_type=F32)
    s = jnp.dot(s_ref[...], (pos - neg).astype(BF16), preferred_element_type=F32)
    wk = jnp.where(is0, 0.5 / p, 1.0 / p)
    a_ref[0, 0] = a * wk
    nyq_ref[0, 0] = jnp.broadcast_to((nyq - a[0:1]) * (0.5 / p), nyq_ref.shape[2:])
    hsin_ref[0, 0] = jnp.where(is0, 0.0, s) * wk


def _hy_spectrum(tabs, h2, h2r, p):
    cos_t, sin_f, _ = tabs
    n_ord, _, l, dh = h2.shape
    nb = l // p
    n_win = 2 * nb - 1
    h2 = h2.reshape(n_ord, 2 * nb, p, dh)
    h2r = h2r.reshape(n_ord, 2 * nb, p, dh)
    tab = pl.BlockSpec((p, p), lambda o, m: (0, 0))
    o_spec = pl.BlockSpec((1, 1, p, dh), lambda o, m: (o, m, 0, 0))
    return pl.pallas_call(
        functools.partial(_spectrum_kernel, p=p),
        grid=(n_ord, n_win),
        in_specs=[tab, tab,
                  pl.BlockSpec((1, 1, p, dh), lambda o, m: (o, m + 1, 0, 0)),
                  pl.BlockSpec((1, 1, p, dh), lambda o, m: (o, 2 * nb - 1 - m, 0, 0))],
        out_specs=[o_spec, pl.BlockSpec((1, 1, 8, dh), lambda o, m: (o, m, 0, 0)), o_spec],
        out_shape=[jax.ShapeDtypeStruct((n_ord, n_win, p, dh), F32),
                   jax.ShapeDtypeStruct((n_ord, n_win, 8, dh), F32),
                   jax.ShapeDtypeStruct((n_ord, n_win, p, dh), F32)],
        compiler_params=_cparams(("parallel", "parallel"), 48),
        name="hy_spectrum",
    )(cos_t, sin_f, h2, h2r)


HY_ROWS = 8


def _hy_block_kernel(c_ref, sf_ref, si_ref, a_ref, nyq_ref, hsin_ref, sk_ref, ug_ref, wg_ref, *rest,
                     nb, period, first):
    if first:
        uv_ref, wv_ref, of_ref, ob_ref, xc_s, xs_s, yc_s, ys_s, v_s = rest
    else:
        vb_ref, vp_ref, z_ref, o_ref, xc_s, xs_s, yc_s, ys_s = rest
    p = c_ref.shape[0]
    for j in range(nb):
        rows = slice(j * p, (j + 1) * p)
        if first:
            vj = _conv_rows(uv_ref[0, rows, :], wv_ref[...], period)
            v_s[rows, :] = vj
            xj = vj.astype(BF16)
        else:
            xj = vb_ref[0, rows, :]
        xc_s[j] = jnp.dot(c_ref[...], xj, preferred_element_type=F32)
        xs_s[j] = jnp.dot(sf_ref[...], xj, preferred_element_type=F32)

    nyq_fix = []
    for i in range(nb):
        fix = jnp.zeros((1, LANES), F32)
        for j in range(nb):
            fix = fix + xs_s[j, 0:1, :] * nyq_ref[0, i - j + (nb - 1), 0:1, :]
        nyq_fix.append(fix)

    def rows_body(r, carry):
        rs = pl.ds(pl.multiple_of(r * HY_ROWS, HY_ROWS), HY_ROWS)
        xc = [xc_s[j, rs, :] for j in range(nb)]
        xs = [xs_s[j, rs, :] for j in range(nb)]
        yc = [None] * nb
        ys = [None] * nb
        for w in range(2 * nb - 1):
            a = a_ref[0, w, rs, :]
            hsin = hsin_ref[0, w, rs, :]
            for i in range(nb):
                j = i - (w - (nb - 1))
                if 0 <= j < nb:
                    tc = xc[j] * a - xs[j] * hsin
                    ts = xc[j] * hsin + xs[j] * a
                    yc[i] = tc if yc[i] is None else yc[i] + tc
                    ys[i] = ts if ys[i] is None else ys[i] + ts
        row0 = (lax.broadcasted_iota(jnp.int32, (HY_ROWS, LANES), 0) + r * HY_ROWS) == 0
        for i in range(nb):
            yc_s[i, rs, :] = yc[i]
            ys_s[i, rs, :] = ys[i] + jnp.where(row0, nyq_fix[i], 0.0)
        return carry

    lax.fori_loop(0, p // HY_ROWS, rows_body, 0)
    for i in range(nb):
        rows = slice(i * p, (i + 1) * p)
        conv = jnp.dot(c_ref[...], yc_s[i].astype(BF16), preferred_element_type=F32)
        conv = conv + jnp.dot(si_ref[...], ys_s[i].astype(BF16), preferred_element_type=F32)
        xg = _conv_rows(ug_ref[0, rows, :], wg_ref[...], period)
        if first:
            y = xg * (conv + sk_ref[0] * v_s[rows, :])
            of_ref[0, rows, :] = y
            ob_ref[0, rows, :] = y.astype(BF16)
        else:
            y = xg * (conv + sk_ref[0] * vp_ref[0, rows, :])
            o_ref[0, rows, :] = (y * _silu(z_ref[0, rows, :])).astype(BF16)


def _hy_block(tabs, spec, u, conv_w, skip, order, p, dh, period, y_prev=None):
    b, l, _ = u.shape
    nb = l // p
    n_win = 2 * nb - 1
    nct = dh // LANES
    taps = conv_w.shape[0]
    first = y_prev is None
    tab = pl.BlockSpec((p, p), lambda c, i: (0, 0))
    filt = pl.BlockSpec((1, n_win, p, LANES), lambda c, i: (order, 0, 0, c))
    nyq = pl.BlockSpec((1, n_win, 8, LANES), lambda c, i: (order, 0, 0, c))
    u_cols = lambda k: pl.BlockSpec((1, l, LANES), lambda c, i, k=k: (i, 0, k * nct + c))
    w_cols = lambda k: pl.BlockSpec((taps, LANES), lambda c, i, k=k: (0, k * nct + c))
    col = pl.BlockSpec((1, l, LANES), lambda c, i: (i, 0, c))
    in_specs = [tab, tab, tab, filt, nyq, filt, pl.BlockSpec((1, 1, LANES), lambda c, i: (order, 0, c)),
                u_cols(order), w_cols(order)]
    args = [*tabs, *spec, skip[:, None, :], u, conv_w]
    scratch = [pltpu.VMEM((nb, p, LANES), F32)] * 4
    if first:
        in_specs += [u_cols(2), w_cols(2)]
        args += [u, conv_w]
        out_specs = [col, col]
        out_shape = [jax.ShapeDtypeStruct((b, l, dh), F32), jax.ShapeDtypeStruct((b, l, dh), BF16)]
        scratch.append(pltpu.VMEM((l, LANES), F32))
    else:
        in_specs += [col, col, u_cols(3)]
        args += [y_prev[1], y_prev[0], u]
        out_specs = col
        out_shape = jax.ShapeDtypeStruct((b, l, dh), BF16)
    return pl.pallas_call(
        functools.partial(_hy_block_kernel, nb=nb, period=period, first=first),
        grid=(nct, b),
        in_specs=in_specs, out_specs=out_specs, out_shape=out_shape,
        scratch_shapes=scratch,
        compiler_params=_cparams(("parallel", "parallel"), 56),
        name="hy_block",
    )(*args)


def _hyena(u, conv_w, w1, b1, freq, w2, b2, w3, skip, lay, grid_mask):
    assert HYENA_ORDER == 2
    dh = lay["DH"]
    l = u.shape[1]
    p = min(HY_BLOCK, l)
    period = CHUNK if grid_mask else l
    assert p % period == 0
    tabs = _dft_tables(p)
    spec = _hy_spectrum(tabs, *_hy_filters(l, w1, b1, freq, w2, b2, w3, dh), p)
    y1 = _hy_block(tabs, spec, u, conv_w, skip, 0, p, dh, period)
    return _hy_block(tabs, spec, u, conv_w, skip, 1, p, dh, period, y_prev=y1)


def _layout(d):
    dg, dh, dm = 3 * d // 8, d // 4, 3 * d // 8
    lay = {"DG": dg, "DH": dh, "DM": dm, "gh": dg // HEAD_DIM, "mh": dm // HEAD_DIM}
    lay["HZ"] = 3 * dh
    lay["GQ"] = 4 * dh
    lay["GZ"] = lay["GQ"] + 3 * dg
    lay["GAB"] = lay["GZ"] + dg
    lay["MQ"] = lay["GAB"] + LANES
    lay["MO"] = lay["MQ"] + 3 * dm
    lay["MZ"] = lay["MO"] + dm
    lay["MG"] = lay["MZ"] + dm
    lay["NP"] = lay["MG"] + LANES
    return lay


def _pack_w_in(w, lay):
    dg, dh, dm, gh, mh = lay["DG"], lay["DH"], lay["DM"], lay["gh"], lay["mh"]
    sizes = (3 * dg, dg, 4 * gh, 3 * dh, dh, 3 * dm, dm, dm, 4 * mh)
    offs = [0]
    for s in sizes:
        offs.append(offs[-1] + s)
    seg = [w[:, offs[i]:offs[i + 1]] for i in range(len(sizes))]
    g_qkv, g_z, g_ab, h_p, h_z, m_qkv, m_o, m_z, m_g = seg
    padl = lambda a: jnp.pad(a, ((0, 0), (0, LANES - a.shape[1])))
    return jnp.concatenate([h_p, h_z, g_qkv, g_z, padl(g_ab), m_qkv, m_o, m_z, padl(m_g)], axis=1).astype(BF16)


def _gate_rows(u, off, n):
    b, l, _ = u.shape
    return jnp.transpose(u[:, :, off:off + n], (0, 2, 1)).reshape(b, n, l // LANES, LANES)


def kernel(x, c, ctx, c_ctx, norm_w, mod_w, mod_b, w_in, gdn_conv, gdn_a_log, gdn_dt_bias, gdn_norm, hy_conv,
           hy_w1, hy_b1, hy_freq, hy_w2, hy_b2, hy_w3, hy_skip, ml_gate_bias, ml_norm, w_out, final_norm):
    b, l, d = x.shape
    lc = ctx.shape[1]
    depth = norm_w.shape[0]
    lay = _layout(d)
    dg, dh = lay["DG"], lay["DH"]
    assert b < COND_ROWS and l % (2 * LANES) == 0 and lc % (2 * LANES) == 0 and d % 1024 == 0
    assert l % min(HY_BLOCK, l) == 0
    cond = jnp.zeros((COND_ROWS, d), F32).at[:b].set(c).at[b].set(c_ctx)
    all_mods = _adaln(cond, mod_w, mod_b[:, None, :])
    for layer in range(depth):
        last = layer == depth - 1
        mods = all_mods[layer]
        sh, sc, gt = mods[:, :d], mods[:, d:2 * d], mods[:, 2 * d:]
        lat = lambda m: m[:b, None, :]
        cx = lambda m: jnp.broadcast_to(m[b][None, None, :], (b, 1, d))
        wp = _pack_w_in(w_in[layer], lay)
        nw = norm_w[layer][None]
        u_l = _inproj(x, nw, lat(sc), lat(sh), wp)
        u_c = _inproj(ctx, nw, cx(sc), cx(sh), wp)
        g_rows = lambda u, off, n: _gate_rows(u, off, n)
        yg_l, yg_c = _gdn(u_l, u_c, g_rows(u_l, lay["GAB"], 4 * lay["gh"]), g_rows(u_c, lay["GAB"], 4 * lay["gh"]),
                          gdn_conv[layer], gdn_a_log[layer], gdn_dt_bias[layer], gdn_norm[layer][None], lay,
                          not last)
        ym_l, ym_c = _mlstm(u_l, u_c, g_rows(u_l, lay["MG"], 4 * lay["mh"]), g_rows(u_c, lay["MG"], 4 * lay["mh"]),
                            ml_gate_bias[layer], ml_norm[layer][None], lay, not last)
        hy = (hy_conv[layer], hy_w1[layer], hy_b1[layer], hy_freq[layer], hy_w2[layer], hy_b2[layer],
              hy_w3[layer], hy_skip[layer])
        yh_l = _hyena(u_l, *hy, lay, True)
        wo = w_out[layer].astype(BF16)
        wg, wh, wm = wo[:dg], wo[dg:dg + dh], wo[dg + dh:]
        fw = final_norm[None]
        x = _outproj(x, yg_l, yh_l, ym_l, wg, wh, wm, lat(gt), fw, last)
        if not last:
            yh_c = _hyena(u_c, *hy, lay, False)
            ctx = _outproj(ctx, yg_c, yh_c, ym_c, wg, wh, wm, cx(gt), fw, False)
    return x
```

```python
import functools
import math

import jax
import jax.numpy as jnp
from jax import lax
from jax.experimental import pallas as pl
from jax.experimental.pallas import tpu as pltpu

HEAD_DIM = 128
CHUNK = 64
LANES = 128
NORM_EPS = 1e-6
HYENA_ORDER = 2
HY_BLOCK = 512
FILTER_BANDS = 16
DECAY_TARGET = 1e-2
MIN_DECAY = math.log(DECAY_TARGET) / 1.5
MAX_DECAY = math.log(DECAY_TARGET) / 0.3
COND_ROWS = 16

F32 = jnp.float32
BF16 = jnp.bfloat16
HI = lax.Precision.HIGHEST


def _cparams(sem, vmem_mb):
    return pltpu.CompilerParams(dimension_semantics=sem, vmem_limit_bytes=vmem_mb << 20)


def _dot(a, b):
    return jnp.dot(a.astype(BF16), b.astype(BF16), preferred_element_type=F32)


def _dot_hi(a, b):
    return jnp.dot(a, b, precision=HI, preferred_element_type=F32)


def _dot_nt(a, b):
    return lax.dot_general(a.astype(BF16), b.astype(BF16), (((1,), (1,)), ((), ())),
                           preferred_element_type=F32)


def _silu(x):
    return x * jax.nn.sigmoid(x)


def _softplus(x):
    return jnp.maximum(x, 0.0) + jnp.log(1.0 + jnp.exp(-jnp.abs(x)))


def _pick_tile(n, cap, unit):
    t = (min(n, cap) // unit) * unit
    while n % t:
        t -= unit
    return t


def _adaln_kernel(c_ref, w_ref, b_ref, o_ref):
    o_ref[0] = _dot_hi(_silu(c_ref[...]), w_ref[0]) + b_ref[0]


def _adaln(cond, w, b):
    depth, d, n = w.shape
    tn = _pick_tile(n, 768, LANES)
    return pl.pallas_call(
        _adaln_kernel,
        grid=(depth, n // tn),
        in_specs=[pl.BlockSpec((COND_ROWS, d), lambda i, j: (0, 0)),
                  pl.BlockSpec((1, d, tn), lambda i, j: (i, 0, j)),
                  pl.BlockSpec((1, 1, tn), lambda i, j: (i, 0, j))],
        out_specs=pl.BlockSpec((1, COND_ROWS, tn), lambda i, j: (i, 0, j)),
        out_shape=jax.ShapeDtypeStruct((depth, COND_ROWS, n), F32),
        compiler_params=_cparams(("parallel", "parallel"), 40),
        name="adaln",
    )(cond, w, b)


INPROJ_NORM_ROWS = 256


def _inproj_kernel(x_ref, nw_ref, sc_ref, sh_ref, w_ref, o_ref, xn_ref):
    first = pl.program_id(2) == 0

    @pl.when(first)
    def _():
        rows = _pick_tile(xn_ref.shape[0], INPROJ_NORM_ROWS, 8)
        for s in range(xn_ref.shape[0] // rows):
            sl = slice(s * rows, (s + 1) * rows)
            x = x_ref[0, sl, :]
            r = lax.rsqrt(jnp.mean(x * x, axis=-1, keepdims=True) + NORM_EPS)
            y = (x * r * nw_ref[...]) * (1.0 + sc_ref[0]) + sh_ref[0]
            y = y.astype(BF16)
            xn_ref[sl, :] = y
            o_ref[0, sl, :] = jnp.dot(y, w_ref[...], preferred_element_type=F32)

    @pl.when(jnp.logical_not(first))
    def _():
        o_ref[0] = jnp.dot(xn_ref[...], w_ref[...], preferred_element_type=F32)


def _inproj(x, nw, sc, sh, wp):
    b, l, d = x.shape
    n = wp.shape[1]
    tm = _pick_tile(l, 1024, 8)
    tn = _pick_tile(n, 1280, LANES)
    return pl.pallas_call(
        _inproj_kernel,
        grid=(b, l // tm, n // tn),
        in_specs=[pl.BlockSpec((1, tm, d), lambda i, m, j: (i, m, 0)),
                  pl.BlockSpec((1, d), lambda i, m, j: (0, 0)),
                  pl.BlockSpec((1, 1, d), lambda i, m, j: (i, 0, 0)),
                  pl.BlockSpec((1, 1, d), lambda i, m, j: (i, 0, 0)),
                  pl.BlockSpec((d, tn), lambda i, m, j: (0, j))],
        out_specs=pl.BlockSpec((1, tm, tn), lambda i, m, j: (i, m, j)),
        out_shape=jax.ShapeDtypeStruct((b, l, n), F32),
        scratch_shapes=[pltpu.VMEM((tm, d), BF16)],
        compiler_params=_cparams(("parallel", "parallel", "arbitrary"), 56),
        name="inproj",
    )(x, nw, sc, sh, wp)


def _outproj_kernel(x_ref, yg_ref, yh_ref, ym_ref, wg_ref, wh_ref, wm_ref, gt_ref, fw_ref, o_ref, *, final):
    acc = jnp.dot(yg_ref[0], wg_ref[...], preferred_element_type=F32)
    acc = acc + jnp.dot(yh_ref[0], wh_ref[...], preferred_element_type=F32)
    acc = acc + jnp.dot(ym_ref[0], wm_ref[...], preferred_element_type=F32)
    xn = x_ref[0] + gt_ref[0] * acc
    if final:
        r = lax.rsqrt(jnp.mean(xn * xn, axis=-1, keepdims=True) + NORM_EPS)
        xn = xn * r * fw_ref[...]
    o_ref[0] = xn


def _outproj(x, yg, yh, ym, wg, wh, wm, gt, fw, final):
    b, l, d = x.shape
    tm = _pick_tile(l, 512, 8)
    row = lambda w: pl.BlockSpec((1, tm, w), lambda i, m: (i, m, 0))
    full = lambda a: pl.BlockSpec(a.shape, lambda i, m: (0, 0))
    return pl.pallas_call(
        functools.partial(_outproj_kernel, final=final),
        grid=(b, l // tm),
        in_specs=[row(d), row(yg.shape[2]), row(yh.shape[2]), row(ym.shape[2]),
                  full(wg), full(wh), full(wm),
                  pl.BlockSpec((1, 1, d), lambda i, m: (i, 0, 0)),
                  pl.BlockSpec((1, d), lambda i, m: (0, 0))],
        out_specs=row(d),
        out_shape=jax.ShapeDtypeStruct((b, l, d), F32),
        compiler_params=_cparams(("parallel", "parallel"), 48),
        name="outproj",
    )(x, yg, yh, ym, wg, wh, wm, gt, fw)


def _conv_rows(x, w, period):
    rows = x.shape[0]
    taps = w.shape[0]
    pad = taps // 2
    pos = lax.broadcasted_iota(jnp.int32, x.shape, 0) % period
    y = None
    for j in range(taps):
        off = j - pad
        if off == 0:
            term = x * w[j:j + 1]
        else:
            shifted = pltpu.roll(x, (-off) % rows, axis=0)
            ok = (pos >= -off) if off < 0 else (pos < period - off)
            term = jnp.where(ok, shifted, 0.0) * w[j:j + 1]
        y = term if y is None else y + term
    return y


def _seg_cumsum(x, reverse):
    lane = lax.broadcasted_iota(jnp.int32, x.shape, 1) % CHUNK
    s = 1
    while s < CHUNK:
        if reverse:
            shifted = pltpu.roll(x, LANES - s, axis=1)
            ok = lane < CHUNK - s
        else:
            shifted = pltpu.roll(x, s, axis=1)
            ok = lane >= s
        x = x + jnp.where(ok, shifted, 0.0)
        s *= 2
    return x


def _pair_masks(d):
    ii = lax.broadcasted_iota(jnp.int32, (LANES, LANES), 0)
    jj = lax.broadcasted_iota(jnp.int32, (LANES, LANES), 1)
    lo = (ii // CHUNK) * CHUNK
    eye = ii == jj
    if d == 0:
        return eye, (jj >= lo) & (jj <= ii), (jj >= lo) & (jj < ii)
    return eye, (jj < lo + CHUNK) & (jj >= ii), (jj < lo + CHUNK) & (jj > ii)


def _to_col(row, eye):
    return jnp.sum(jnp.where(eye, jnp.broadcast_to(row, eye.shape), 0.0), axis=1, keepdims=True)


def _split_bf16(a):
    hi = a.astype(BF16)
    return hi, (a - hi.astype(F32)).astype(BF16)


def _col_bcast(row, eye):
    x = jnp.where(eye, jnp.broadcast_to(row, eye.shape), 0.0)
    hi = x.astype(BF16)
    mid, lo = _split_bf16(x - hi.astype(F32))
    ones = jnp.ones((3 * eye.shape[1], LANES), BF16)
    return jnp.dot(jnp.concatenate([hi, mid, lo], axis=1), ones, preferred_element_type=F32)


def _dot3(a, b):
    a_hi, a_lo = _split_bf16(a)
    b_hi, b_lo = _split_bf16(b)
    return jnp.dot(jnp.concatenate([a_hi, a_hi, a_lo], axis=1), jnp.concatenate([b_hi, b_lo, b_hi], axis=0),
                   preferred_element_type=F32)


def _head_out_tiles(acc_refs, base, nw_ref, z_ref, y_ref, n_rows, og_ref=None):
    tile = _pick_tile(n_rows, 256, 8)

    def body(i, carry):
        r = pl.multiple_of(i * tile, tile)
        o = acc_refs[0][pl.ds(base + r, tile), :]
        for acc_ref in acc_refs[1:]:
            o = o + acc_ref[pl.ds(base + r, tile), :]
        if og_ref is not None:
            o = jax.nn.sigmoid(og_ref[0, pl.ds(r, tile), :]) * o
        o = o * lax.rsqrt(jnp.mean(o * o, axis=-1, keepdims=True) + NORM_EPS) * nw_ref[...]
        y_ref[0, pl.ds(r, tile), :] = (o * _silu(z_ref[0, pl.ds(r, tile), :])).astype(y_ref.dtype)
        return carry

    lax.fori_loop(0, n_rows // tile, body, 0)


def _ml_rows(g_refs, bias_ref, h, n_heads, rows_ref):
    li_f = g_refs[0][0, 0] + bias_ref[h]
    lf_f = -_softplus(-(g_refs[1][0, 0] + bias_ref[n_heads + h]))
    li_b = g_refs[2][0, 0] + bias_ref[2 * n_heads + h]
    lf_b = -_softplus(-(g_refs[3][0, 0] + bias_ref[3 * n_heads + h]))
    rows_ref[0] = _seg_cumsum(lf_f, False)
    rows_ref[1] = li_f
    rows_ref[2] = _seg_cumsum(lf_b, True)
    rows_ref[3] = li_b


def _ml_prep_cols(q_ref, k_ref, rows_ref, chains):
    eye = _pair_masks(0)[0]
    lane_i = lax.broadcasted_iota(jnp.int32, (1, LANES), 1)
    st = []
    for d, p in chains:
        rows = pl.ds(pl.multiple_of(p * LANES, LANES), LANES)
        k = k_ref[0, rows, :] * (HEAD_DIM ** -0.5)
        b_row = rows_ref[2 * d, pl.ds(p, 1), :]
        li_row = rows_ref[2 * d + 1, pl.ds(p, 1), :]
        last = CHUNK - 1 if d == 0 else 0
        b_tot = (b_row[:, last:last + 1], b_row[:, CHUNK + last:CHUNK + last + 1])
        end_row = jnp.where(lane_i < CHUNK, b_tot[0], b_tot[1]) - b_row + li_row
        e_max = (jnp.max(end_row[:, :CHUNK], axis=1, keepdims=True),
                 jnp.max(end_row[:, CHUNK:], axis=1, keepdims=True))
        st.append(dict(d=d, rows=rows, k=k, b_row=b_row, li_row=li_row, b_tot=b_tot, e_max=e_max,
                       b_cb=_col_bcast(b_row, eye), end_cb=_col_bcast(end_row, eye),
                       qk=_dot_nt(q_ref[0, rows, :], k)))
    return st


def _ml_prep_intra(st):
    for c in st:
        _, incl, _ = _pair_masks(c["d"])
        dlog = jnp.where(incl, c["b_cb"] - c["b_row"] + c["li_row"], -jnp.inf)
        c["rowmax"] = jnp.max(dlog, axis=1, keepdims=True)
        c["p_hi"], c["p_lo"] = _split_bf16(jnp.exp(dlog - c["rowmax"]) * c["qk"])


def _ml_prep_state(st):
    row_i = lax.broadcasted_iota(jnp.int32, (LANES, 1), 0)
    lane_sq = lax.broadcasted_iota(jnp.int32, (LANES, LANES), 1)
    out = []
    for c in st:
        kw_t = (c["k"] * jnp.exp(c["end_cb"] - jnp.where(row_i < CHUNK, c["e_max"][0], c["e_max"][1]))).T
        kw_t2 = jnp.concatenate([jnp.where(lane_sq < CHUNK, kw_t, 0.0), jnp.where(lane_sq < CHUNK, 0.0, kw_t)],
                                axis=0).astype(BF16)
        out.append((c["b_cb"], c["rowmax"], c["p_hi"], c["p_lo"], kw_t2, *c["b_tot"], *c["e_max"]))
    return tuple(out)


def _ml_steps_state(q_ref, v_ref, cnst, pre, chains):
    ones = jnp.ones((LANES, LANES), BF16)
    st = []
    for d, p, half in chains:
        _, _, p_hi, p_lo, kw_t2 = pre[d][:5]
        sl = slice(half * CHUNK, (half + 1) * CHUNK)
        v_ones = jnp.concatenate([v_ref[0, pl.ds(pl.multiple_of(p * LANES, LANES), LANES), :].astype(BF16), ones],
                                 axis=1)
        rhs = jnp.concatenate([v_ones, jnp.concatenate([jnp.zeros_like(ones), ones], axis=1)], axis=0)
        pv_ps = jnp.dot(jnp.concatenate([p_hi[sl], p_lo[sl]], axis=1), rhs, preferred_element_type=F32)
        d_cn = jnp.dot(kw_t2[half * LANES:(half + 1) * LANES], v_ones, preferred_element_type=F32)
        st.append((pv_ps, d_cn))
    for n, (d, p, half) in enumerate(chains):
        r0 = pl.ds(pl.multiple_of(p * LANES + half * CHUNK, CHUNK), CHUNK)
        cn = cnst[d]
        st[n] = (r0, cn, _dot(q_ref[0, r0, :], cn), *st[n])
    return st


def _ml_steps_update(pre, acc_refs, cnst, mst, chains, st):
    for (d, p, half), (r0, cn, q_cn, pv_ps, d_cn) in zip(chains, st):
        b_cb, rowmax = pre[d][:2]
        b_tot = pre[d][5 + half]
        e_max = pre[d][7 + half]
        sl = slice(half * CHUNK, (half + 1) * CHUNK)
        m_s = mst[d, 0:1, 0:1]
        inter = b_cb[sl] + m_s
        m_i = jnp.maximum(inter, rowmax[sl])
        w_inter = jnp.exp(inter - m_i)
        s_intra = jnp.exp(rowmax[sl] - m_i)
        num = w_inter * q_cn[:, :HEAD_DIM] + s_intra * pv_ps[:, :HEAD_DIM]
        den = w_inter * q_cn[:, HEAD_DIM:] + s_intra * pv_ps[:, HEAD_DIM:]
        acc_refs[d][r0, :] = num / jnp.maximum(jnp.abs(den), jnp.exp(-m_i))
        carry_log = b_tot + m_s
        m_new = jnp.maximum(carry_log, e_max)
        cnst[d] = jnp.exp(carry_log - m_new) * cn + jnp.exp(e_max - m_new) * d_cn
        mst[d] = jnp.broadcast_to(m_new, mst.shape[1:])


def _ml_scan(q_ref, k_ref, v_ref, rows_ref, acc_refs, cnst, mst, n_pairs):
    def chains_of(t):
        return [(0, t), (1, n_pairs - 1 - t), (0, t + 1), (1, n_pairs - 2 - t)]

    def prep(chains):
        st = _ml_prep_cols(q_ref, k_ref, rows_ref, chains)
        _ml_prep_intra(st)
        return _ml_prep_state(st)

    def trip(i, pre, prepare_next):
        t = 2 * i
        st = _ml_prep_cols(q_ref, k_ref, rows_ref, chains_of(t + 2)) if prepare_next else None
        nxt = pre
        for j, tt in enumerate((t, t + 1)):
            pf, pb = tt, n_pairs - 1 - tt
            pre_j = pre[2 * j:2 * j + 2]
            for halves in ([(0, pf, 0), (1, pb, 1)], [(0, pf, 1), (1, pb, 0)]):
                held = _ml_steps_state(q_ref, v_ref, cnst, pre_j, halves)
                if prepare_next and j == 0 and halves[0][2] == 0:
                    _ml_prep_intra(st)
                elif prepare_next and j == 0:
                    nxt = _ml_prep_state(st)
                _ml_steps_update(pre_j, acc_refs, cnst, mst, halves, held)
        return nxt

    n_trips = n_pairs // 2
    pre = lax.fori_loop(0, n_trips - 1, lambda i, pre: trip(i, pre, True), prep(chains_of(0)))
    trip(n_trips - 1, pre, False)


def _mlstm_kernel(bias_ref, ql, kl, vl, ol, zl, qc, kc, vc, oc, zc, gl0, gl1, gl2, gl3, gc0, gc1, gc2, gc3,
                  nw_ref, *rest, n_heads, ctx_out):
    if ctx_out:
        yl_ref, yc_ref = rest[:2]
        rest = rest[2:]
    else:
        yl_ref, yc_ref = rest[0], None
        rest = rest[1:]
    cnst, mst, rows_l, rows_c, acc_lf, acc_lb, acc_cf, acc_cb = rest
    h = pl.program_id(1)
    cnst[...] = jnp.zeros(cnst.shape, F32)
    mst[...] = jnp.zeros(mst.shape, F32)
    _ml_rows((gc0, gc1, gc2, gc3), bias_ref, h, n_heads, rows_c)
    _ml_scan(qc, kc, vc, rows_c, (acc_cf, acc_cb), cnst, mst, rows_c.shape[1])
    _ml_rows((gl0, gl1, gl2, gl3), bias_ref, h, n_heads, rows_l)
    _ml_scan(ql, kl, vl, rows_l, (acc_lf, acc_lb), cnst, mst, rows_l.shape[1])
    _head_out_tiles((acc_lf, acc_lb), 0, nw_ref, zl, yl_ref, acc_lf.shape[0], og_ref=ol)
    if ctx_out:
        _head_out_tiles((acc_cf, acc_cb), 0, nw_ref, zc, yc_ref, acc_cf.shape[0], og_ref=oc)


def _head_block(l, blk):
    return pl.BlockSpec((1, l, HEAD_DIM), lambda b, h, blk=blk: (b, 0, blk + h))


def _gate_block(n_pairs, j, n_heads):
    return pl.BlockSpec((1, 1, n_pairs, LANES), lambda b, h, j=j: (b, j * n_heads + h, 0, 0))


def _mlstm(u_l, u_c, gt_l, gt_c, bias, nw, lay, ctx_out):
    b, l, _ = u_l.shape
    lc = u_c.shape[1]
    nh = lay["mh"]
    blk = lambda off: off // HEAD_DIM
    offs = [lay["MQ"], lay["MQ"] + lay["DM"], lay["MQ"] + 2 * lay["DM"], lay["MO"], lay["MZ"]]
    in_specs = [pl.BlockSpec(memory_space=pltpu.SMEM)]
    in_specs += [_head_block(l, blk(o)) for o in offs]
    in_specs += [_head_block(lc, blk(o)) for o in offs]
    in_specs += [_gate_block(l // LANES, j, nh) for j in range(4)]
    in_specs += [_gate_block(lc // LANES, j, nh) for j in range(4)]
    in_specs += [pl.BlockSpec((1, HEAD_DIM), lambda b_, h: (0, 0))]
    out_specs = [pl.BlockSpec((1, l, HEAD_DIM), lambda b_, h: (b_, 0, h))]
    out_shape = [jax.ShapeDtypeStruct((b, l, nh * HEAD_DIM), BF16)]
    if ctx_out:
        out_specs.append(pl.BlockSpec((1, lc, HEAD_DIM), lambda b_, h: (b_, 0, h)))
        out_shape.append(jax.ShapeDtypeStruct((b, lc, nh * HEAD_DIM), BF16))
    scratch = [pltpu.VMEM((2, HEAD_DIM, 2 * HEAD_DIM), F32), pltpu.VMEM((2, 8, LANES), F32),
               pltpu.VMEM((4, l // LANES, LANES), F32), pltpu.VMEM((4, lc // LANES, LANES), F32),
               pltpu.VMEM((l, HEAD_DIM), F32), pltpu.VMEM((l, HEAD_DIM), F32),
               pltpu.VMEM((lc, HEAD_DIM), F32), pltpu.VMEM((lc, HEAD_DIM), F32)]
    res = pl.pallas_call(
        functools.partial(_mlstm_kernel, n_heads=nh, ctx_out=ctx_out),
        grid=(b, nh),
        in_specs=in_specs, out_specs=out_specs, out_shape=out_shape,
        scratch_shapes=scratch,
        compiler_params=_cparams(("parallel", "parallel"), 48),
        name="mlstm",
    )(bias.reshape(-1), *([u_l] * 5), *([u_c] * 5), *([gt_l] * 4), *([gt_c] * 4), nw)
    return (res[0], res[1]) if ctx_out else (res[0], None)


def _gdn_rows(g_refs, alog_ref, dtb_ref, h, n_heads, rows_ref, base):
    n = g_refs[0].shape[2]
    g_f = -jnp.exp(alog_ref[h]) * _softplus(g_refs[0][0, 0] + dtb_ref[h])
    g_b = -jnp.exp(alog_ref[n_heads + h]) * _softplus(g_refs[1][0, 0] + dtb_ref[n_heads + h])
    rows_ref[0, base:base + n] = _seg_cumsum(g_f, False)
    rows_ref[1, base:base + n] = jax.nn.sigmoid(g_refs[2][0, 0])
    rows_ref[2, base:base + n] = _seg_cumsum(g_b, True)
    rows_ref[3, base:base + n] = jax.nn.sigmoid(g_refs[3][0, 0])


def _masked_taps(w, period):
    taps = w.shape[0]
    pad = taps // 2
    pos = lax.broadcasted_iota(jnp.int32, (period, w.shape[1]), 0)
    out = []
    for j in range(taps):
        off = j - pad
        ok = (pos >= -off) if off < 0 else (pos < period - off)
        out.append(jnp.where(ok, w[j:j + 1], 0.0))
    return out


def _conv_rows_pre(x, wm_ref, first):
    rows = x.shape[0]
    taps = wm_ref.shape[0] // 3
    period = wm_ref.shape[1]
    pad = taps // 2
    y = None
    for j in range(taps):
        off = j - pad
        shifted = x if off == 0 else pltpu.roll(x, (-off) % rows, axis=0)
        term = shifted * jnp.concatenate([wm_ref[first + j]] * (rows // period), axis=0)
        y = term if y is None else y + term
    return y


def _gdn_conv(src_refs, w_refs, dst_refs, base, n_units, unit_rows, period, wm_ref=None):
    taps = w_refs[0].shape[0]
    if wm_ref is not None:
        for idx in range(3):
            for j, wm in enumerate(_masked_taps(w_refs[idx][...], period)):
                wm_ref[idx * taps + j] = wm

    def body(i, carry):
        r = pl.multiple_of(i * unit_rows, unit_rows)
        for idx in range(3):
            x = src_refs[idx][0, pl.ds(r, unit_rows), :]
            if wm_ref is not None:
                t = _silu(_conv_rows_pre(x, wm_ref, idx * taps))
            else:
                t = _silu(_conv_rows(x, w_refs[idx][...], period))
            if idx < 2:
                t = t * lax.rsqrt(jnp.sum(t * t, axis=-1, keepdims=True) + NORM_EPS)
            if idx == 0:
                t = t * (HEAD_DIM ** -0.5)
            dst_refs[idx][pl.ds(base + r, unit_rows), :] = t
        return carry

    lax.fori_loop(0, n_units, body, 0)


N_NEUMANN = CHUNK.bit_length() - 2


def _side_masks(d):
    ii = lax.broadcasted_iota(jnp.int32, (CHUNK, LANES), 0)
    lane = lax.broadcasted_iota(jnp.int32, (CHUNK, LANES), 1)
    jj = lane % CHUNK
    first = lane < CHUNK
    if d == 0:
        return first, jj == ii, jj <= ii, jj < ii
    return first, jj == ii, jj >= ii, jj > ii


def _side(x, first):
    return jnp.where(first, x[:CHUNK], x[CHUNK:])


def _block_diag(x, first):
    zero = jnp.zeros_like(x)
    return jnp.concatenate([jnp.where(first, x, zero), jnp.where(first, zero, x)], axis=0)


def _dot3_side(a, b, first):
    a_hi, a_lo = _split_bf16(a)
    b_hi, b_lo = _split_bf16(b)
    rhs = jnp.concatenate([_block_diag(b_hi, first), _block_diag(b_lo, first), _block_diag(b_hi, first)], axis=0)
    return jnp.dot(jnp.concatenate([a_hi, a_hi, a_lo], axis=1), rhs, preferred_element_type=F32)


def _gdn_gates(rows_ref, d, p):
    eye = _pair_masks(0)[0]
    g_row = rows_ref[2 * d, pl.ds(p, 1), :]
    return dict(d=d, p=p, rows=pl.ds(pl.multiple_of(p * LANES, LANES), LANES), g_row=g_row,
                g_col=_to_col(g_row, eye), beta_col=_to_col(rows_ref[2 * d + 1, pl.ds(p, 1), :], eye))


def _gdn_prep_start(qs, ks, rows_ref, chains):
    st = []
    for d, p in chains:
        c = _gdn_gates(rows_ref, d, p)
        rows, g_row, g_col, beta_col = c["rows"], c["g_row"], c["g_col"], c["beta_col"]
        k = ks[rows, :]
        first, eye_s, incl, strict = _side_masks(d)
        dec = jnp.exp(jnp.where(incl, _side(g_col, first) - g_row, -jnp.inf))
        aqk = _block_diag(_side(_dot_nt(qs[rows, :], k), first) * dec, first).astype(BF16)
        m_low = jnp.where(strict, _side(beta_col, first) * _side(_dot_nt(k, k), first) * dec, 0.0)
        st.append(dict(c, aqk=aqk, pw=m_low, t_inv=jnp.where(eye_s, 1.0, 0.0) - m_low))
    return st


def _gdn_prep_neumann(st, after_square=None, after_product=None):
    first, eye_s, _, _ = _side_masks(0)
    eye_f = jnp.where(eye_s, 1.0, 0.0)
    for c in st:
        c["pw"] = _dot3_side(c["pw"], c["pw"], first)
    if after_square is not None:
        after_square()
    for c in st:
        c["t_inv"] = _dot3_side(c["t_inv"], eye_f + c["pw"], first)
    if after_product is not None:
        after_product()


NQ_ROWS = HEAD_DIM + CHUNK


def _gdn_prep_solve(qs, ks, vs, st):
    row_i = lax.broadcasted_iota(jnp.int32, (LANES, 1), 0)
    first = _side_masks(0)[0]
    for c in st:
        d, rows, g_row, g_col, beta_col = c["d"], c["rows"], c["g_row"], c["g_col"], c["beta_col"]
        k = ks[rows, :]
        e_g = jnp.exp(g_col)
        c["w_ut"] = _dot3(_block_diag(c["t_inv"], first),
                          jnp.concatenate([(beta_col * e_g) * k, beta_col * vs[rows, :]], axis=1)).astype(BF16)
        c["q_dec"] = qs[rows, :] * e_g
        last = CHUNK - 1 if d == 0 else 0
        g_tot_col = jnp.where(row_i < CHUNK, g_row[:, last:last + 1], g_row[:, CHUNK + last:CHUNK + last + 1])
        c["k_end_t"] = (k * jnp.exp(g_tot_col - g_col)).T.astype(BF16)


def _gdn_prep_finish(nq_s, r_s, acc_refs, st):
    row_i = lax.broadcasted_iota(jnp.int32, (LANES, 1), 0)
    for c in st:
        d, w_ut, q_dec = c["d"], c["w_ut"], c["q_dec"]
        zero = jnp.zeros_like(w_ut)
        for half in (0, 1):
            sl = slice(half * CHUNK, (half + 1) * CHUNK)
            lhs = jnp.concatenate([c["k_end_t"], c["aqk"][sl]], axis=0)
            mine = (row_i < CHUNK) if half == 0 else (row_i >= CHUNK)
            prod = jnp.dot(lhs, jnp.where(mine, w_ut, zero), preferred_element_type=F32)
            n_q = jnp.concatenate([prod[:HEAD_DIM, :HEAD_DIM], q_dec[sl] - prod[HEAD_DIM:, :HEAD_DIM]], axis=0)
            chunk = 2 * c["p"] + half
            nq_s[d, pl.ds(pl.multiple_of(chunk * NQ_ROWS, NQ_ROWS), NQ_ROWS), :] = n_q.astype(BF16)
            r_s[d, pl.ds(pl.multiple_of(chunk * HEAD_DIM, HEAD_DIM), HEAD_DIM), :] = prod[:HEAD_DIM, HEAD_DIM:]
            acc_refs[d][pl.ds(pl.multiple_of(chunk * CHUNK, CHUNK), CHUNK), :] = prod[HEAD_DIM:, HEAD_DIM:]


def _gdn_steps_state(nq_s, sst, chains):
    st = []
    for d, p, half in chains:
        chunk = 2 * p + half
        s = sst[d]
        n_q = nq_s[d, pl.ds(pl.multiple_of(chunk * NQ_ROWS, NQ_ROWS), NQ_ROWS), :]
        st.append((chunk, s, jnp.dot(n_q, s.astype(BF16), preferred_element_type=F32)))
    return st


def _gdn_steps_update(rows_ref, r_s, sst, acc_refs, chains, st):
    for (d, p, half), (chunk, s, prod) in zip(chains, st):
        lane = half * CHUNK + (CHUNK - 1 if d == 0 else 0)
        g_tot = rows_ref[2 * d, pl.ds(p, 1), lane:lane + 1]
        r0 = pl.ds(pl.multiple_of(chunk * CHUNK, CHUNK), CHUNK)
        acc_refs[d][r0, :] = acc_refs[d][r0, :] + prod[HEAD_DIM:]
        r_c = r_s[d, pl.ds(pl.multiple_of(chunk * HEAD_DIM, HEAD_DIM), HEAD_DIM), :]
        sst[d] = jnp.exp(g_tot) * s - prod[:HEAD_DIM] + r_c


def _gdn_kernel(alog_ref, dtb_ref, ql, kl, vl, zl, qc, kc, vc, zc, gl0, gl1, gl2, gl3, gc0, gc1, gc2, gc3,
                wq, wk, wv, nw_ref, *rest, n_heads, ctx_out):
    if ctx_out:
        yl_ref, yc_ref = rest[:2]
        rest = rest[2:]
    else:
        yl_ref, yc_ref = rest[0], None
        rest = rest[1:]
    sst, rows, qs, ks, vs, nq_s, r_s, acc_f, acc_b, nm_s, aq_s, wm_s = rest
    h = pl.program_id(1)
    n_l = ql.shape[1]
    n_c = qc.shape[1]
    nl = n_l // LANES
    nc = n_c // LANES
    sst[...] = jnp.zeros(sst.shape, F32)
    _gdn_rows((gl0, gl1, gl2, gl3), alog_ref, dtb_ref, h, n_heads, rows, 0)
    _gdn_rows((gc0, gc1, gc2, gc3), alog_ref, dtb_ref, h, n_heads, rows, nl)
    conv_rows = _pick_tile(n_l, 2 * CHUNK, CHUNK)
    _gdn_conv((ql, kl, vl), (wq, wk, wv), (qs, ks, vs), 0, n_l // conv_rows, conv_rows, CHUNK, wm_ref=wm_s)
    _gdn_conv((qc, kc, vc), (wq, wk, wv), (qs, ks, vs), n_l, 1, n_c, n_c)
    n_steps = nl + nc
    accs = (acc_f, acc_b)

    def fwd_pair(t):
        return jnp.where(t < nc, nl + t, t - nc)

    def bwd_pair(t):
        return n_steps - 1 - t

    n_trips = n_steps // 2
    n_early = (N_NEUMANN + 1) // 2

    def chains_of(trip):
        t = 2 * jnp.minimum(trip, n_trips - 1)
        return [(0, fwd_pair(t)), (1, bwd_pair(t)), (0, fwd_pair(t + 1)), (1, bwd_pair(t + 1))]

    def neumann(st, half=None):
        if half is None:
            _gdn_prep_neumann(st)
            return
        held = []
        _gdn_prep_neumann(st, lambda: held.append(_gdn_steps_state(nq_s, sst, half)),
                          lambda: _gdn_steps_update(rows, r_s, sst, accs, half, held[0]))

    def finish(st):
        _gdn_prep_solve(qs, ks, vs, st)
        _gdn_prep_finish(nq_s, r_s, accs, st)

    def halves_of(trip):
        out = []
        for t in (2 * trip, 2 * trip + 1):
            pf, pb = fwd_pair(t), bwd_pair(t)
            out += [[(0, pf, 0), (1, pb, 1)], [(0, pf, 1), (1, pb, 0)]]
        return out

    st = _gdn_prep_start(qs, ks, rows, chains_of(0))
    for _ in range(N_NEUMANN):
        neumann(st)
    finish(st)
    st = _gdn_prep_start(qs, ks, rows, chains_of(1))
    for _ in range(n_early):
        neumann(st)

    def park(st_part):
        for n, c in enumerate(st_part):
            nm_s[2 * n] = c["pw"]
            nm_s[2 * n + 1] = c["t_inv"]
            aq_s[n] = c["aqk"]

    park(st)

    def trip(i, carry):
        halves = halves_of(i)
        st_late = [dict(_gdn_gates(rows, d, p), pw=nm_s[2 * n], t_inv=nm_s[2 * n + 1], aqk=aq_s[n])
                   for n, (d, p) in enumerate(chains_of(i + 1))]
        st_early = _gdn_prep_start(qs, ks, rows, chains_of(i + 2))
        n_late = N_NEUMANN - n_early
        slot = 0
        for n in range(n_early):
            neumann(st_early, halves[slot] if slot < len(halves) else None)
            slot += 1
            if n < n_late:
                neumann(st_late, halves[slot] if slot < len(halves) else None)
                slot += 1
            if n == n_late - 1:
                _gdn_prep_solve(qs, ks, vs, st_late)
        _gdn_prep_finish(nq_s, r_s, accs, st_late)
        park(st_early)
        return carry

    lax.fori_loop(0, n_trips - 1, trip, 0)
    for half in halves_of(n_trips - 1):
        _gdn_steps_update(rows, r_s, sst, accs, half, _gdn_steps_state(nq_s, sst, half))
    _head_out_tiles((acc_f, acc_b), 0, nw_ref, zl, yl_ref, n_l)
    if ctx_out:
        _head_out_tiles((acc_f, acc_b), n_l, nw_ref, zc, yc_ref, n_c)


def _gdn(u_l, u_c, gt_l, gt_c, conv_w, a_log, dt_bias, nw, lay, ctx_out):
    b, l, _ = u_l.shape
    lc = u_c.shape[1]
    nh = lay["gh"]
    taps = conv_w.shape[0]
    blk = lambda off: off // HEAD_DIM
    offs = [lay["GQ"], lay["GQ"] + lay["DG"], lay["GQ"] + 2 * lay["DG"], lay["GZ"]]
    smem = pl.BlockSpec(memory_space=pltpu.SMEM)
    in_specs = [smem, smem]
    in_specs += [_head_block(l, blk(o)) for o in offs]
    in_specs += [_head_block(lc, blk(o)) for o in offs]
    in_specs += [_gate_block(l // LANES, j, nh) for j in range(4)]
    in_specs += [_gate_block(lc // LANES, j, nh) for j in range(4)]
    in_specs += [pl.BlockSpec((taps, HEAD_DIM), lambda b_, h, j=j: (0, j * nh + h)) for j in range(3)]
    in_specs += [pl.BlockSpec((1, HEAD_DIM), lambda b_, h: (0, 0))]
    out_specs = [pl.BlockSpec((1, l, HEAD_DIM), lambda b_, h: (b_, 0, h))]
    out_shape = [jax.ShapeDtypeStruct((b, l, nh * HEAD_DIM), BF16)]
    if ctx_out:
        out_specs.append(pl.BlockSpec((1, lc, HEAD_DIM), lambda b_, h: (b_, 0, h)))
        out_shape.append(jax.ShapeDtypeStruct((b, lc, nh * HEAD_DIM), BF16))
    lt = l + lc
    seq_f32 = pltpu.VMEM((lt, HEAD_DIM), F32)
    n_chunks = lt // CHUNK
    scratch = [pltpu.VMEM((2, HEAD_DIM, HEAD_DIM), F32), pltpu.VMEM((4, lt // LANES, LANES), F32),
               seq_f32, seq_f32, seq_f32,
               pltpu.VMEM((2, n_chunks * NQ_ROWS, HEAD_DIM), BF16), pltpu.VMEM((2, n_chunks * HEAD_DIM, HEAD_DIM), F32),
               seq_f32, seq_f32, pltpu.VMEM((8, CHUNK, LANES), F32), pltpu.VMEM((4, LANES, LANES), BF16),
               pltpu.VMEM((3 * taps, CHUNK, HEAD_DIM), F32)]
    res = pl.pallas_call(
        functools.partial(_gdn_kernel, n_heads=nh, ctx_out=ctx_out),
        grid=(b, nh),
        in_specs=in_specs, out_specs=out_specs, out_shape=out_shape,
        scratch_shapes=scratch,
        compiler_params=_cparams(("parallel", "parallel"), 56),
        name="gdn",
    )(a_log.reshape(-1), dt_bias.reshape(-1), *([u_l] * 4), *([u_c] * 4), *([gt_l] * 4), *([gt_c] * 4),
      *([conv_w] * 3), nw)
    return (res[0], res[1]) if ctx_out else (res[0], None)


def _dft_tables(l):
    k = jnp.arange(l, dtype=jnp.int32)
    ang = lambda t: ((k[:, None] * t[None, :]) % (2 * l)).astype(F32) * (math.pi / l)
    ang_a = ang(jnp.arange(l // CHUNK, dtype=jnp.int32) * CHUNK)[:, :, None]
    ang_b = ang(jnp.arange(CHUNK, dtype=jnp.int32))[:, None, :]
    cos_t = (jnp.cos(ang_a) * jnp.cos(ang_b) - jnp.sin(ang_a) * jnp.sin(ang_b)).reshape(l, l)
    sin_t = (jnp.sin(ang_a) * jnp.cos(ang_b) + jnp.cos(ang_a) * jnp.sin(ang_b)).reshape(l, l)
    alt = jnp.where(k % 2 == 0, 1.0, -1.0).astype(F32)
    sin_f = sin_t.at[0, :].set(alt)
    return cos_t.astype(BF16), sin_f.astype(BF16), sin_f.T.astype(BF16)


def _filter_kernel(feats_ref, featr_ref, w1_ref, b1_ref, fr_ref, w2_ref, b2_ref, w3c_ref, w3a_ref, dl_ref,
                   h2_ref, h2r_ref, hid_ref, hidr_ref):
    @pl.when((pl.program_id(0) == 0) & (pl.program_id(1) == 0))
    def _():
        for f_ref, h_ref in ((feats_ref, hid_ref), (featr_ref, hidr_ref)):
            hid = jnp.sin(fr_ref[...] * (_dot_hi(f_ref[...], w1_ref[...]) + b1_ref[...]))
            h_ref[...] = jnp.sin(fr_ref[...] * (_dot_hi(hid, w2_ref[...]) + b2_ref[...]))

    w3 = jnp.concatenate([w3c_ref[...], w3a_ref[...]], axis=1)

    def raw(h_ref, f_ref):
        win = jnp.exp(-f_ref[:, 0:1] * dl_ref[...])
        both = _dot3(h_ref[...], w3)
        return both[:, :LANES] * win, both[:, LANES:] * win

    c_f, a_f = raw(hid_ref, feats_ref)
    c_r, a_r = raw(hidr_ref, featr_ref)
    den_c = jnp.sum(jnp.abs(c_f), axis=0, keepdims=True) + NORM_EPS
    den_a = jnp.sum(jnp.abs(a_f), axis=0, keepdims=True) + NORM_EPS
    c_f = c_f / den_c
    a_f = a_f / den_a
    row0 = lax.broadcasted_iota(jnp.int32, c_f.shape, 0) == 0
    centre = c_f[0:1] + a_f[0:1]
    h2_ref[0, 0] = jnp.where(row0, 0.0, a_r / den_a)
    h2_ref[0, 1] = jnp.where(row0, centre, c_f)
    h2r_ref[0, 0] = jnp.where(row0, 0.0, c_r / den_c)
    h2r_ref[0, 1] = jnp.where(row0, centre, a_f)


def _filter_feats(pos, l, n_emb):
    t = pos / max(l - 1, 1)
    ang = 2.0 * math.pi * pos / l
    bands = jnp.linspace(1e-4, FILTER_BANDS - 1, FILTER_BANDS, dtype=F32)
    feats = jnp.concatenate([t[:, None], jnp.cos(ang[:, None] * bands), -jnp.sin(ang[:, None] * bands)], axis=-1)
    return jnp.pad(feats, ((0, 0), (0, LANES - n_emb)))


def _hy_filters(l, w1, b1, freq, w2, b2, w3, dh):
    n_emb, n_hid = w1.shape
    pos = jnp.arange(l, dtype=F32)
    feats = _filter_feats(pos, l, n_emb)
    feats_r = _filter_feats(l - pos, l, n_emb)
    pc = LANES - n_hid
    w1p = jnp.pad(w1, ((0, LANES - n_emb), (0, pc)))
    w2p = jnp.pad(w2, ((0, pc), (0, pc)))
    w3p = jnp.pad(w3, ((0, pc), (0, 0)))
    row = lambda a: jnp.pad(a, (0, pc))[None]
    deltas = jnp.abs(jnp.linspace(MIN_DECAY, MAX_DECAY, dh, dtype=F32))[None]
    nct = dh // LANES
    const = lambda shape: pl.BlockSpec(shape, lambda o, c: (0, 0))
    o_spec = pl.BlockSpec((1, 2, l, LANES), lambda o, c: (o, 0, 0, c))
    return pl.pallas_call(
        _filter_kernel,
        grid=(HYENA_ORDER, nct),
        in_specs=[const((l, LANES)), const((l, LANES)), const((LANES, LANES)), const((1, LANES)),
                  const((1, LANES)), const((LANES, LANES)), const((1, LANES)),
                  pl.BlockSpec((LANES, LANES), lambda o, c: (0, o * 2 * nct + c)),
                  pl.BlockSpec((LANES, LANES), lambda o, c: (0, o * 2 * nct + nct + c)),
                  pl.BlockSpec((1, LANES), lambda o, c: (0, c))],
        out_specs=[o_spec, o_spec],
        out_shape=[jax.ShapeDtypeStruct((HYENA_ORDER, 2, l, dh), F32)] * 2,
        scratch_shapes=[pltpu.VMEM((l, LANES), F32)] * 2,
        compiler_params=_cparams(("arbitrary", "arbitrary"), 56),
        name="hy_filter",
    )(feats, feats_r, w1p, row(b1), row(freq), w2p, row(b2), w3p, w3p, deltas)


def _spectrum_kernel(c_ref, s_ref, h2_ref, h2r_ref, a_ref, nyq_ref, hsin_ref, *, p):
    pos = h2_ref[0, 0]
    neg = h2r_ref[0, 0]
    is0 = lax.broadcasted_iota(jnp.int32, pos.shape, 0) == 0
    h_sum = jnp.where(is0, pos, pos + neg)
    sign = jnp.where(lax.broadcasted_iota(jnp.int32, pos.shape, 0) % 2 == 0, 1.0, -1.0)
    nyq = jnp.sum(h_sum * sign, axis=0, keepdims=True)
    a = jnp.dot(c_ref[...], h_sum.astype(BF16), preferred_element_type=F32)
    s = jnp.dot(s_ref[...], (pos - neg).astype(BF16), preferred_element_type=F32)
    wk = jnp.where(is0, 0.5 / p, 1.0 / p)
    a_ref[0, 0] = a * wk
    nyq_ref[0, 0] = jnp.broadcast_to((nyq - a[0:1]) * (0.5 / p), nyq_ref.shape[2:])
    hsin_ref[0, 0] = jnp.where(is0, 0.0, s) * wk


def _hy_spectrum(tabs, h2, h2r, p):
    cos_t, sin_f, _ = tabs
    n_ord, _, l, dh = h2.shape
    nb = l // p
    n_win = 2 * nb - 1
    h2 = h2.reshape(n_ord, 2 * nb, p, dh)
    h2r = h2r.reshape(n_ord, 2 * nb, p, dh)
    tab = pl.BlockSpec((p, p), lambda o, m: (0, 0))
    o_spec = pl.BlockSpec((1, 1, p, dh), lambda o, m: (o, m, 0, 0))
    return pl.pallas_call(
        functools.partial(_spectrum_kernel, p=p),
        grid=(n_ord, n_win),
        in_specs=[tab, tab,
                  pl.BlockSpec((1, 1, p, dh), lambda o, m: (o, m + 1, 0, 0)),
                  pl.BlockSpec((1, 1, p, dh), lambda o, m: (o, 2 * nb - 1 - m, 0, 0))],
        out_specs=[o_spec, pl.BlockSpec((1, 1, 8, dh), lambda o, m: (o, m, 0, 0)), o_spec],
        out_shape=[jax.ShapeDtypeStruct((n_ord, n_win, p, dh), F32),
                   jax.ShapeDtypeStruct((n_ord, n_win, 8, dh), F32),
                   jax.ShapeDtypeStruct((n_ord, n_win, p, dh), F32)],
        compiler_params=_cparams(("parallel", "parallel"), 48),
        name="hy_spectrum",
    )(cos_t, sin_f, h2, h2r)


HY_ROWS = 8


def _hy_block_kernel(c_ref, sf_ref, si_ref, a_ref, nyq_ref, hsin_ref, sk_ref, ug_ref, wg_ref, *rest,
                     nb, period, first):
    if first:
        uv_ref, wv_ref, of_ref, ob_ref, xc_s, xs_s, yc_s, ys_s, v_s = rest
    else:
        vb_ref, vp_ref, z_ref, o_ref, xc_s, xs_s, yc_s, ys_s = rest
    p = c_ref.shape[0]
    for j in range(nb):
        rows = slice(j * p, (j + 1) * p)
        if first:
            vj = _conv_rows(uv_ref[0, rows, :], wv_ref[...], period)
            v_s[rows, :] = vj
            xj = vj.astype(BF16)
        else:
            xj = vb_ref[0, rows, :]
        xc_s[j] = jnp.dot(c_ref[...], xj, preferred_element_type=F32)
        xs_s[j] = jnp.dot(sf_ref[...], xj, preferred_element_type=F32)

    nyq_fix = []
    for i in range(nb):
        fix = jnp.zeros((1, LANES), F32)
        for j in range(nb):
            fix = fix + xs_s[j, 0:1, :] * nyq_ref[0, i - j + (nb - 1), 0:1, :]
        nyq_fix.append(fix)

    def rows_body(r, carry):
        rs = pl.ds(pl.multiple_of(r * HY_ROWS, HY_ROWS), HY_ROWS)
        xc = [xc_s[j, rs, :] for j in range(nb)]
        xs = [xs_s[j, rs, :] for j in range(nb)]
        yc = [None] * nb
        ys = [None] * nb
        for w in range(2 * nb - 1):
            a = a_ref[0, w, rs, :]
            hsin = hsin_ref[0, w, rs, :]
            for i in range(nb):
                j = i - (w - (nb - 1))
                if 0 <= j < nb:
                    tc = xc[j] * a - xs[j] * hsin
                    ts = xc[j] * hsin + xs[j] * a
                    yc[i] = tc if yc[i] is None else yc[i] + tc
                    ys[i] = ts if ys[i] is None else ys[i] + ts
        row0 = (lax.broadcasted_iota(jnp.int32, (HY_ROWS, LANES), 0) + r * HY_ROWS) == 0
        for i in range(nb):
            yc_s[i, rs, :] = yc[i]
            ys_s[i, rs, :] = ys[i] + jnp.where(row0, nyq_fix[i], 0.0)
        return carry

    lax.fori_loop(0, p // HY_ROWS, rows_body, 0)
    for i in range(nb):
        rows = slice(i * p, (i + 1) * p)
        conv = jnp.dot(c_ref[...], yc_s[i].astype(BF16), preferred_element_type=F32)
        conv = conv + jnp.dot(si_ref[...], ys_s[i].astype(BF16), preferred_element_type=F32)
        xg = _conv_rows(ug_ref[0, rows, :], wg_ref[...], period)
        if first:
            y = xg * (conv + sk_ref[0] * v_s[rows, :])
            of_ref[0, rows, :] = y
            ob_ref[0, rows, :] = y.astype(BF16)
        else:
            y = xg * (conv + sk_ref[0] * vp_ref[0, rows, :])
            o_ref[0, rows, :] = (y * _silu(z_ref[0, rows, :])).astype(BF16)


def _hy_block(tabs, spec, u, conv_w, skip, order, p, dh, period, y_prev=None):
    b, l, _ = u.shape
    nb = l // p
    n_win = 2 * nb - 1
    nct = dh // LANES
    taps = conv_w.shape[0]
    first = y_prev is None
    tab = pl.BlockSpec((p, p), lambda c, i: (0, 0))
    filt = pl.BlockSpec((1, n_win, p, LANES), lambda c, i: (order, 0, 0, c))
    nyq = pl.BlockSpec((1, n_win, 8, LANES), lambda c, i: (order, 0, 0, c))
    u_cols = lambda k: pl.BlockSpec((1, l, LANES), lambda c, i, k=k: (i, 0, k * nct + c))
    w_cols = lambda k: pl.BlockSpec((taps, LANES), lambda c, i, k=k: (0, k * nct + c))
    col = pl.BlockSpec((1, l, LANES), lambda c, i: (i, 0, c))
    in_specs = [tab, tab, tab, filt, nyq, filt, pl.BlockSpec((1, 1, LANES), lambda c, i: (order, 0, c)),
                u_cols(order), w_cols(order)]
    args = [*tabs, *spec, skip[:, None, :], u, conv_w]
    scratch = [pltpu.VMEM((nb, p, LANES), F32)] * 4
    if first:
        in_specs += [u_cols(2), w_cols(2)]
        args += [u, conv_w]
        out_specs = [col, col]
        out_shape = [jax.ShapeDtypeStruct((b, l, dh), F32), jax.ShapeDtypeStruct((b, l, dh), BF16)]
        scratch.append(pltpu.VMEM((l, LANES), F32))
    else:
        in_specs += [col, col, u_cols(3)]
        args += [y_prev[1], y_prev[0], u]
        out_specs = col
        out_shape = jax.ShapeDtypeStruct((b, l, dh), BF16)
    return pl.pallas_call(
        functools.partial(_hy_block_kernel, nb=nb, period=period, first=first),
        grid=(nct, b),
        in_specs=in_specs, out_specs=out_specs, out_shape=out_shape,
        scratch_shapes=scratch,
        compiler_params=_cparams(("parallel", "parallel"), 56),
        name="hy_block",
    )(*args)


def _hyena(u, conv_w, w1, b1, freq, w2, b2, w3, skip, lay, grid_mask):
    assert HYENA_ORDER == 2
    dh = lay["DH"]
    l = u.shape[1]
    p = min(HY_BLOCK, l)
    period = CHUNK if grid_mask else l
    assert p % period == 0
    tabs = _dft_tables(p)
    spec = _hy_spectrum(tabs, *_hy_filters(l, w1, b1, freq, w2, b2, w3, dh), p)
    y1 = _hy_block(tabs, spec, u, conv_w, skip, 0, p, dh, period)
    return _hy_block(tabs, spec, u, conv_w, skip, 1, p, dh, period, y_prev=y1)


def _layout(d):
    dg, dh, dm = 3 * d // 8, d // 4, 3 * d // 8
    lay = {"DG": dg, "DH": dh, "DM": dm, "gh": dg // HEAD_DIM, "mh": dm // HEAD_DIM}
    lay["HZ"] = 3 * dh
    lay["GQ"] = 4 * dh
    lay["GZ"] = lay["GQ"] + 3 * dg
    lay["GAB"] = lay["GZ"] + dg
    lay["MQ"] = lay["GAB"] + LANES
    lay["MO"] = lay["MQ"] + 3 * dm
    lay["MZ"] = lay["MO"] + dm
    lay["MG"] = lay["MZ"] + dm
    lay["NP"] = lay["MG"] + LANES
    return lay


def _pack_w_in(w, lay):
    dg, dh, dm, gh, mh = lay["DG"], lay["DH"], lay["DM"], lay["gh"], lay["mh"]
    sizes = (3 * dg, dg, 4 * gh, 3 * dh, dh, 3 * dm, dm, dm, 4 * mh)
    offs = [0]
    for s in sizes:
        offs.append(offs[-1] + s)
    seg = [w[:, offs[i]:offs[i + 1]] for i in range(len(sizes))]
    g_qkv, g_z, g_ab, h_p, h_z, m_qkv, m_o, m_z, m_g = seg
    padl = lambda a: jnp.pad(a, ((0, 0), (0, LANES - a.shape[1])))
    return jnp.concatenate([h_p, h_z, g_qkv, g_z, padl(g_ab), m_qkv, m_o, m_z, padl(m_g)], axis=1).astype(BF16)


def _gate_rows(u, off, n):
    b, l, _ = u.shape
    return jnp.transpose(u[:, :, off:off + n], (0, 2, 1)).reshape(b, n, l // LANES, LANES)


def kernel(x, c, ctx, c_ctx, norm_w, mod_w, mod_b, w_in, gdn_conv, gdn_a_log, gdn_dt_bias, gdn_norm, hy_conv,
           hy_w1, hy_b1, hy_freq, hy_w2, hy_b2, hy_w3, hy_skip, ml_gate_bias, ml_norm, w_out, final_norm):
    b, l, d = x.shape
    lc = ctx.shape[1]
    depth = norm_w.shape[0]
    lay = _layout(d)
    dg, dh = lay["DG"], lay["DH"]
    assert b < COND_ROWS and l % (2 * LANES) == 0 and lc % (2 * LANES) == 0 and d % 1024 == 0
    assert l % min(HY_BLOCK, l) == 0
    cond = jnp.zeros((COND_ROWS, d), F32).at[:b].set(c).at[b].set(c_ctx)
    all_mods = _adaln(cond, mod_w, mod_b[:, None, :])
    for layer in range(depth):
        last = layer == depth - 1
        mods = all_mods[layer]
        sh, sc, gt = mods[:, :d], mods[:, d:2 * d], mods[:, 2 * d:]
        lat = lambda m: m[:b, None, :]
        cx = lambda m: jnp.broadcast_to(m[b][None, None, :], (b, 1, d))
        wp = _pack_w_in(w_in[layer], lay)
        nw = norm_w[layer][None]
        u_l = _inproj(x, nw, lat(sc), lat(sh), wp)
        u_c = _inproj(ctx, nw, cx(sc), cx(sh), wp)
        g_rows = lambda u, off, n: _gate_rows(u, off, n)
        yg_l, yg_c = _gdn(u_l, u_c, g_rows(u_l, lay["GAB"], 4 * lay["gh"]), g_rows(u_c, lay["GAB"], 4 * lay["gh"]),
                          gdn_conv[layer], gdn_a_log[layer], gdn_dt_bias[layer], gdn_norm[layer][None], lay,
                          not last)
        ym_l, ym_c = _mlstm(u_l, u_c, g_rows(u_l, lay["MG"], 4 * lay["mh"]), g_rows(u_c, lay["MG"], 4 * lay["mh"]),
                            ml_gate_bias[layer], ml_norm[layer][None], lay, not last)
        hy = (hy_conv[layer], hy_w1[layer], hy_b1[layer], hy_freq[layer], hy_w2[layer], hy_b2[layer],
              hy_w3[layer], hy_skip[layer])
        yh_l = _hyena(u_l, *hy, lay, True)
        wo = w_out[layer].astype(BF16)
        wg, wh, wm = wo[:dg], wo[dg:dg + dh], wo[dg + dh:]
        fw = final_norm[None]
        x = _outproj(x, yg_l, yh_l, ym_l, wg, wh, wm, lat(gt), fw, last)
        if not last:
            yh_c = _hyena(u_c, *hy, lay, False)
            ctx = _outproj(ctx, yg_c, yh_c, ym_c, wg, wh, wm, cx(gt), fw, False)
    return x
```

```python
import functools
import math

import jax
import jax.numpy as jnp
from jax import lax
from jax.experimental import pallas as pl
from jax.experimental.pallas import tpu as pltpu

HEAD_DIM = 128
CHUNK = 64
LANES = 128
NORM_EPS = 1e-6
HYENA_ORDER = 2
HY_BLOCK = 512
FILTER_BANDS = 16
DECAY_TARGET = 1e-2
MIN_DECAY = math.log(DECAY_TARGET) / 1.5
MAX_DECAY = math.log(DECAY_TARGET) / 0.3
COND_ROWS = 16

F32 = jnp.float32
BF16 = jnp.bfloat16
HI = lax.Precision.HIGHEST


def _cparams(sem, vmem_mb):
    return pltpu.CompilerParams(dimension_semantics=sem, vmem_limit_bytes=vmem_mb << 20)


def _dot(a, b):
    return jnp.dot(a.astype(BF16), b.astype(BF16), preferred_element_type=F32)


def _dot_hi(a, b):
    return jnp.dot(a, b, precision=HI, preferred_element_type=F32)


def _dot_nt(a, b):
    return lax.dot_general(a.astype(BF16), b.astype(BF16), (((1,), (1,)), ((), ())),
                           preferred_element_type=F32)


def _silu(x):
    return x * jax.nn.sigmoid(x)


def _softplus(x):
    return jnp.maximum(x, 0.0) + jnp.log(1.0 + jnp.exp(-jnp.abs(x)))


def _pick_tile(n, cap, unit):
    t = (min(n, cap) // unit) * unit
    while n % t:
        t -= unit
    return t


def _adaln_kernel(c_ref, w_ref, b_ref, o_ref):
    o_ref[0] = _dot_hi(_silu(c_ref[...]), w_ref[0]) + b_ref[0]


def _adaln(cond, w, b):
    depth, d, n = w.shape
    tn = _pick_tile(n, 768, LANES)
    return pl.pallas_call(
        _adaln_kernel,
        grid=(depth, n // tn),
        in_specs=[pl.BlockSpec((COND_ROWS, d), lambda i, j: (0, 0)),
                  pl.BlockSpec((1, d, tn), lambda i, j: (i, 0, j)),
                  pl.BlockSpec((1, 1, tn), lambda i, j: (i, 0, j))],
        out_specs=pl.BlockSpec((1, COND_ROWS, tn), lambda i, j: (i, 0, j)),
        out_shape=jax.ShapeDtypeStruct((depth, COND_ROWS, n), F32),
        compiler_params=_cparams(("parallel", "parallel"), 40),
        name="adaln",
    )(cond, w, b)


INPROJ_NORM_ROWS = 256


def _inproj_kernel(x_ref, nw_ref, sc_ref, sh_ref, w_ref, o_ref, xn_ref):
    first = pl.program_id(2) == 0

    @pl.when(first)
    def _():
        rows = _pick_tile(xn_ref.shape[0], INPROJ_NORM_ROWS, 8)
        for s in range(xn_ref.shape[0] // rows):
            sl = slice(s * rows, (s + 1) * rows)
            x = x_ref[0, sl, :]
            r = lax.rsqrt(jnp.mean(x * x, axis=-1, keepdims=True) + NORM_EPS)
            y = (x * r * nw_ref[...]) * (1.0 + sc_ref[0]) + sh_ref[0]
            y = y.astype(BF16)
            xn_ref[sl, :] = y
            o_ref[0, sl, :] = jnp.dot(y, w_ref[...], preferred_element_type=F32)

    @pl.when(jnp.logical_not(first))
    def _():
        o_ref[0] = jnp.dot(xn_ref[...], w_ref[...], preferred_element_type=F32)


def _inproj(x, nw, sc, sh, wp):
    b, l, d = x.shape
    n = wp.shape[1]
    tm = _pick_tile(l, 1024, 8)
    tn = _pick_tile(n, 1280, LANES)
    return pl.pallas_call(
        _inproj_kernel,
        grid=(b, l // tm, n // tn),
        in_specs=[pl.BlockSpec((1, tm, d), lambda i, m, j: (i, m, 0)),
                  pl.BlockSpec((1, d), lambda i, m, j: (0, 0)),
                  pl.BlockSpec((1, 1, d), lambda i, m, j: (i, 0, 0)),
                  pl.BlockSpec((1, 1, d), lambda i, m, j: (i, 0, 0)),
                  pl.BlockSpec((d, tn), lambda i, m, j: (0, j))],
        out_specs=pl.BlockSpec((1, tm, tn), lambda i, m, j: (i, m, j)),
        out_shape=jax.ShapeDtypeStruct((b, l, n), F32),
        scratch_shapes=[pltpu.VMEM((tm, d), BF16)],
        compiler_params=pltpu.CompilerParams(dimension_semantics=("parallel", "parallel", "arbitrary"),
                                             vmem_limit_bytes=56 << 20,
                                             allow_input_fusion=[False, False, False, False, True]),
        name="inproj",
    )(x, nw, sc, sh, wp)


def _outproj_kernel(x_ref, yg_ref, yh_ref, ym_ref, wg_ref, wh_ref, wm_ref, gt_ref, fw_ref, o_ref, *, final):
    acc = jnp.dot(yg_ref[0], wg_ref[...], preferred_element_type=F32)
    acc = acc + jnp.dot(yh_ref[0], wh_ref[...], preferred_element_type=F32)
    acc = acc + jnp.dot(ym_ref[0], wm_ref[...], preferred_element_type=F32)
    xn = x_ref[0] + gt_ref[0] * acc
    if final:
        r = lax.rsqrt(jnp.mean(xn * xn, axis=-1, keepdims=True) + NORM_EPS)
        xn = xn * r * fw_ref[...]
    o_ref[0] = xn


def _outproj(x, yg, yh, ym, wg, wh, wm, gt, fw, final):
    b, l, d = x.shape
    tm = _pick_tile(l, 512, 8)
    row = lambda w: pl.BlockSpec((1, tm, w), lambda i, m: (i, m, 0))
    full = lambda a: pl.BlockSpec(a.shape, lambda i, m: (0, 0))
    return pl.pallas_call(
        functools.partial(_outproj_kernel, final=final),
        grid=(b, l // tm),
        in_specs=[row(d), row(yg.shape[2]), row(yh.shape[2]), row(ym.shape[2]),
                  full(wg), full(wh), full(wm),
                  pl.BlockSpec((1, 1, d), lambda i, m: (i, 0, 0)),
                  pl.BlockSpec((1, d), lambda i, m: (0, 0))],
        out_specs=row(d),
        out_shape=jax.ShapeDtypeStruct((b, l, d), F32),
        compiler_params=_cparams(("parallel", "parallel"), 48),
        name="outproj",
    )(x, yg, yh, ym, wg, wh, wm, gt, fw)


def _conv_rows(x, w, period):
    rows = x.shape[0]
    taps = w.shape[0]
    pad = taps // 2
    pos = lax.broadcasted_iota(jnp.int32, x.shape, 0) % period
    y = None
    for j in range(taps):
        off = j - pad
        if off == 0:
            term = x * w[j:j + 1]
        else:
            shifted = pltpu.roll(x, (-off) % rows, axis=0)
            ok = (pos >= -off) if off < 0 else (pos < period - off)
            term = jnp.where(ok, shifted, 0.0) * w[j:j + 1]
        y = term if y is None else y + term
    return y


def _seg_cumsum(x, reverse):
    lane = lax.broadcasted_iota(jnp.int32, x.shape, 1) % CHUNK
    s = 1
    while s < CHUNK:
        if reverse:
            shifted = pltpu.roll(x, LANES - s, axis=1)
            ok = lane < CHUNK - s
        else:
            shifted = pltpu.roll(x, s, axis=1)
            ok = lane >= s
        x = x + jnp.where(ok, shifted, 0.0)
        s *= 2
    return x


def _pair_masks(d):
    ii = lax.broadcasted_iota(jnp.int32, (LANES, LANES), 0)
    jj = lax.broadcasted_iota(jnp.int32, (LANES, LANES), 1)
    lo = (ii // CHUNK) * CHUNK
    eye = ii == jj
    if d == 0:
        return eye, (jj >= lo) & (jj <= ii), (jj >= lo) & (jj < ii)
    return eye, (jj < lo + CHUNK) & (jj >= ii), (jj < lo + CHUNK) & (jj > ii)


def _to_col(row, eye):
    return jnp.sum(jnp.where(eye, jnp.broadcast_to(row, eye.shape), 0.0), axis=1, keepdims=True)


def _split_bf16(a):
    hi = a.astype(BF16)
    return hi, (a - hi.astype(F32)).astype(BF16)


def _col_bcast(row, eye):
    x = jnp.where(eye, jnp.broadcast_to(row, eye.shape), 0.0)
    hi = x.astype(BF16)
    mid, lo = _split_bf16(x - hi.astype(F32))
    ones = jnp.ones((3 * eye.shape[1], LANES), BF16)
    return jnp.dot(jnp.concatenate([hi, mid, lo], axis=1), ones, preferred_element_type=F32)


def _dot3(a, b):
    a_hi, a_lo = _split_bf16(a)
    b_hi, b_lo = _split_bf16(b)
    return jnp.dot(jnp.concatenate([a_hi, a_hi, a_lo], axis=1), jnp.concatenate([b_hi, b_lo, b_hi], axis=0),
                   preferred_element_type=F32)


def _head_out_tiles(acc_refs, base, nw_ref, z_ref, y_ref, n_rows, og_ref=None):
    tile = _pick_tile(n_rows, 256, 8)

    def body(i, carry):
        r = pl.multiple_of(i * tile, tile)
        o = acc_refs[0][pl.ds(base + r, tile), :]
        for acc_ref in acc_refs[1:]:
            o = o + acc_ref[pl.ds(base + r, tile), :]
        if og_ref is not None:
            o = jax.nn.sigmoid(og_ref[0, pl.ds(r, tile), :]) * o
        o = o * lax.rsqrt(jnp.mean(o * o, axis=-1, keepdims=True) + NORM_EPS) * nw_ref[...]
        y_ref[0, pl.ds(r, tile), :] = (o * _silu(z_ref[0, pl.ds(r, tile), :])).astype(y_ref.dtype)
        return carry

    lax.fori_loop(0, n_rows // tile, body, 0)


def _ml_rows(g_refs, bias_ref, h, n_heads, rows_ref):
    li_f = g_refs[0][0, 0] + bias_ref[h]
    lf_f = -_softplus(-(g_refs[1][0, 0] + bias_ref[n_heads + h]))
    li_b = g_refs[2][0, 0] + bias_ref[2 * n_heads + h]
    lf_b = -_softplus(-(g_refs[3][0, 0] + bias_ref[3 * n_heads + h]))
    rows_ref[0] = _seg_cumsum(lf_f, False)
    rows_ref[1] = li_f
    rows_ref[2] = _seg_cumsum(lf_b, True)
    rows_ref[3] = li_b


def _ml_prep_cols(q_ref, k_ref, rows_ref, chains):
    eye = _pair_masks(0)[0]
    lane_i = lax.broadcasted_iota(jnp.int32, (1, LANES), 1)
    st = []
    for d, p in chains:
        rows = pl.ds(pl.multiple_of(p * LANES, LANES), LANES)
        k = k_ref[0, rows, :] * (HEAD_DIM ** -0.5)
        b_row = rows_ref[2 * d, pl.ds(p, 1), :]
        li_row = rows_ref[2 * d + 1, pl.ds(p, 1), :]
        last = CHUNK - 1 if d == 0 else 0
        b_tot = (b_row[:, last:last + 1], b_row[:, CHUNK + last:CHUNK + last + 1])
        end_row = jnp.where(lane_i < CHUNK, b_tot[0], b_tot[1]) - b_row + li_row
        e_max = (jnp.max(end_row[:, :CHUNK], axis=1, keepdims=True),
                 jnp.max(end_row[:, CHUNK:], axis=1, keepdims=True))
        st.append(dict(d=d, rows=rows, k=k, b_row=b_row, li_row=li_row, b_tot=b_tot, e_max=e_max,
                       b_cb=_col_bcast(b_row, eye), end_cb=_col_bcast(end_row, eye),
                       qk=_dot_nt(q_ref[0, rows, :], k)))
    return st


def _ml_prep_intra(st):
    for c in st:
        _, incl, _ = _pair_masks(c["d"])
        dlog = jnp.where(incl, c["b_cb"] - c["b_row"] + c["li_row"], -jnp.inf)
        c["rowmax"] = jnp.max(dlog, axis=1, keepdims=True)
        c["p_hi"], c["p_lo"] = _split_bf16(jnp.exp(dlog - c["rowmax"]) * c["qk"])


def _ml_prep_state(st):
    row_i = lax.broadcasted_iota(jnp.int32, (LANES, 1), 0)
    lane_sq = lax.broadcasted_iota(jnp.int32, (LANES, LANES), 1)
    out = []
    for c in st:
        kw_t = (c["k"] * jnp.exp(c["end_cb"] - jnp.where(row_i < CHUNK, c["e_max"][0], c["e_max"][1]))).T
        kw_t2 = jnp.concatenate([jnp.where(lane_sq < CHUNK, kw_t, 0.0), jnp.where(lane_sq < CHUNK, 0.0, kw_t)],
                                axis=0).astype(BF16)
        out.append((c["b_cb"], c["rowmax"], c["p_hi"], c["p_lo"], kw_t2, *c["b_tot"], *c["e_max"]))
    return tuple(out)


def _ml_steps_state(q_ref, v_ref, cnst, pre, chains):
    ones = jnp.ones((LANES, LANES), BF16)
    st = []
    for d, p, half in chains:
        _, _, p_hi, p_lo, kw_t2 = pre[d][:5]
        sl = slice(half * CHUNK, (half + 1) * CHUNK)
        v_ones = jnp.concatenate([v_ref[0, pl.ds(pl.multiple_of(p * LANES, LANES), LANES), :].astype(BF16), ones],
                                 axis=1)
        rhs = jnp.concatenate([v_ones, jnp.concatenate([jnp.zeros_like(ones), ones], axis=1)], axis=0)
        pv_ps = jnp.dot(jnp.concatenate([p_hi[sl], p_lo[sl]], axis=1), rhs, preferred_element_type=F32)
        d_cn = jnp.dot(kw_t2[half * LANES:(half + 1) * LANES], v_ones, preferred_element_type=F32)
        st.append((pv_ps, d_cn))
    for n, (d, p, half) in enumerate(chains):
        r0 = pl.ds(pl.multiple_of(p * LANES + half * CHUNK, CHUNK), CHUNK)
        cn = cnst[d]
        st[n] = (r0, cn, _dot(q_ref[0, r0, :], cn), *st[n])
    return st


def _ml_steps_update(pre, acc_refs, cnst, mst, chains, st):
    for (d, p, half), (r0, cn, q_cn, pv_ps, d_cn) in zip(chains, st):
        b_cb, rowmax = pre[d][:2]
        b_tot = pre[d][5 + half]
        e_max = pre[d][7 + half]
        sl = slice(half * CHUNK, (half + 1) * CHUNK)
        m_s = mst[d, 0:1, 0:1]
        inter = b_cb[sl] + m_s
        m_i = jnp.maximum(inter, rowmax[sl])
        w_inter = jnp.exp(inter - m_i)
        s_intra = jnp.exp(rowmax[sl] - m_i)
        num = w_inter * q_cn[:, :HEAD_DIM] + s_intra * pv_ps[:, :HEAD_DIM]
        den = w_inter * q_cn[:, HEAD_DIM:] + s_intra * pv_ps[:, HEAD_DIM:]
        acc_refs[d][r0, :] = num / jnp.maximum(jnp.abs(den), jnp.exp(-m_i))
        carry_log = b_tot + m_s
        m_new = jnp.maximum(carry_log, e_max)
        cnst[d] = jnp.exp(carry_log - m_new) * cn + jnp.exp(e_max - m_new) * d_cn
        mst[d] = jnp.broadcast_to(m_new, mst.shape[1:])


def _ml_scan(q_ref, k_ref, v_ref, rows_ref, acc_refs, cnst, mst, n_pairs):
    def chains_of(t):
        return [(0, t), (1, n_pairs - 1 - t), (0, t + 1), (1, n_pairs - 2 - t)]

    def prep(chains):
        st = _ml_prep_cols(q_ref, k_ref, rows_ref, chains)
        _ml_prep_intra(st)
        return _ml_prep_state(st)

    def trip(i, pre, prepare_next):
        t = 2 * i
        st = _ml_prep_cols(q_ref, k_ref, rows_ref, chains_of(t + 2)) if prepare_next else None
        nxt = pre
        for j, tt in enumerate((t, t + 1)):
            pf, pb = tt, n_pairs - 1 - tt
            pre_j = pre[2 * j:2 * j + 2]
            for halves in ([(0, pf, 0), (1, pb, 1)], [(0, pf, 1), (1, pb, 0)]):
                held = _ml_steps_state(q_ref, v_ref, cnst, pre_j, halves)
                if prepare_next and j == 0 and halves[0][2] == 0:
                    _ml_prep_intra(st)
                elif prepare_next and j == 0:
                    nxt = _ml_prep_state(st)
                _ml_steps_update(pre_j, acc_refs, cnst, mst, halves, held)
        return nxt

    n_trips = n_pairs // 2
    pre = lax.fori_loop(0, n_trips - 1, lambda i, pre: trip(i, pre, True), prep(chains_of(0)))
    trip(n_trips - 1, pre, False)


def _mlstm_kernel(bias_ref, ql, kl, vl, ol, zl, qc, kc, vc, oc, zc, gl0, gl1, gl2, gl3, gc0, gc1, gc2, gc3,
                  nw_ref, *rest, n_heads, ctx_out):
    if ctx_out:
        yl_ref, yc_ref = rest[:2]
        rest = rest[2:]
    else:
        yl_ref, yc_ref = rest[0], None
        rest = rest[1:]
    cnst, mst, rows_l, rows_c, acc_lf, acc_lb, acc_cf, acc_cb = rest
    h = pl.program_id(1)
    cnst[...] = jnp.zeros(cnst.shape, F32)
    mst[...] = jnp.zeros(mst.shape, F32)
    _ml_rows((gc0, gc1, gc2, gc3), bias_ref, h, n_heads, rows_c)
    _ml_scan(qc, kc, vc, rows_c, (acc_cf, acc_cb), cnst, mst, rows_c.shape[1])
    _ml_rows((gl0, gl1, gl2, gl3), bias_ref, h, n_heads, rows_l)
    _ml_scan(ql, kl, vl, rows_l, (acc_lf, acc_lb), cnst, mst, rows_l.shape[1])
    _head_out_tiles((acc_lf, acc_lb), 0, nw_ref, zl, yl_ref, acc_lf.shape[0], og_ref=ol)
    if ctx_out:
        _head_out_tiles((acc_cf, acc_cb), 0, nw_ref, zc, yc_ref, acc_cf.shape[0], og_ref=oc)


def _head_block(l, blk):
    return pl.BlockSpec((1, l, HEAD_DIM), lambda b, h, blk=blk: (b, 0, blk + h))


def _gate_block(n_pairs, j, n_heads):
    return pl.BlockSpec((1, 1, n_pairs, LANES), lambda b, h, j=j: (b, j * n_heads + h, 0, 0))


def _mlstm(u_l, u_c, gt_l, gt_c, bias, nw, lay, ctx_out):
    b, l, _ = u_l.shape
    lc = u_c.shape[1]
    nh = lay["mh"]
    blk = lambda off: off // HEAD_DIM
    offs = [lay["MQ"], lay["MQ"] + lay["DM"], lay["MQ"] + 2 * lay["DM"], lay["MO"], lay["MZ"]]
    in_specs = [pl.BlockSpec(memory_space=pltpu.SMEM)]
    in_specs += [_head_block(l, blk(o)) for o in offs]
    in_specs += [_head_block(lc, blk(o)) for o in offs]
    in_specs += [_gate_block(l // LANES, j, nh) for j in range(4)]
    in_specs += [_gate_block(lc // LANES, j, nh) for j in range(4)]
    in_specs += [pl.BlockSpec((1, HEAD_DIM), lambda b_, h: (0, 0))]
    out_specs = [pl.BlockSpec((1, l, HEAD_DIM), lambda b_, h: (b_, 0, h))]
    out_shape = [jax.ShapeDtypeStruct((b, l, nh * HEAD_DIM), BF16)]
    if ctx_out:
        out_specs.append(pl.BlockSpec((1, lc, HEAD_DIM), lambda b_, h: (b_, 0, h)))
        out_shape.append(jax.ShapeDtypeStruct((b, lc, nh * HEAD_DIM), BF16))
    scratch = [pltpu.VMEM((2, HEAD_DIM, 2 * HEAD_DIM), F32), pltpu.VMEM((2, 8, LANES), F32),
               pltpu.VMEM((4, l // LANES, LANES), F32), pltpu.VMEM((4, lc // LANES, LANES), F32),
               pltpu.VMEM((l, HEAD_DIM), F32), pltpu.VMEM((l, HEAD_DIM), F32),
               pltpu.VMEM((lc, HEAD_DIM), F32), pltpu.VMEM((lc, HEAD_DIM), F32)]
    res = pl.pallas_call(
        functools.partial(_mlstm_kernel, n_heads=nh, ctx_out=ctx_out),
        grid=(b, nh),
        in_specs=in_specs, out_specs=out_specs, out_shape=out_shape,
        scratch_shapes=scratch,
        compiler_params=_cparams(("parallel", "parallel"), 48),
        name="mlstm",
    )(bias.reshape(-1), *([u_l] * 5), *([u_c] * 5), *([gt_l] * 4), *([gt_c] * 4), nw)
    return (res[0], res[1]) if ctx_out else (res[0], None)


def _gdn_rows(g_refs, alog_ref, dtb_ref, h, n_heads, rows_ref, base):
    n = g_refs[0].shape[2]
    g_f = -jnp.exp(alog_ref[h]) * _softplus(g_refs[0][0, 0] + dtb_ref[h])
    g_b = -jnp.exp(alog_ref[n_heads + h]) * _softplus(g_refs[1][0, 0] + dtb_ref[n_heads + h])
    rows_ref[0, base:base + n] = _seg_cumsum(g_f, False)
    rows_ref[1, base:base + n] = jax.nn.sigmoid(g_refs[2][0, 0])
    rows_ref[2, base:base + n] = _seg_cumsum(g_b, True)
    rows_ref[3, base:base + n] = jax.nn.sigmoid(g_refs[3][0, 0])


def _masked_taps(w, period):
    taps = w.shape[0]
    pad = taps // 2
    pos = lax.broadcasted_iota(jnp.int32, (period, w.shape[1]), 0)
    out = []
    for j in range(taps):
        off = j - pad
        ok = (pos >= -off) if off < 0 else (pos < period - off)
        out.append(jnp.where(ok, w[j:j + 1], 0.0))
    return out


def _conv_rows_pre(x, wm_ref, first):
    rows = x.shape[0]
    taps = wm_ref.shape[0] // 3
    period = wm_ref.shape[1]
    pad = taps // 2
    y = None
    for j in range(taps):
        off = j - pad
        shifted = x if off == 0 else pltpu.roll(x, (-off) % rows, axis=0)
        term = shifted * jnp.concatenate([wm_ref[first + j]] * (rows // period), axis=0)
        y = term if y is None else y + term
    return y


def _gdn_conv(src_refs, w_refs, dst_refs, base, n_units, unit_rows, period, wm_ref=None):
    taps = w_refs[0].shape[0]
    if wm_ref is not None:
        for idx in range(3):
            for j, wm in enumerate(_masked_taps(w_refs[idx][...], period)):
                wm_ref[idx * taps + j] = wm

    def body(i, carry):
        r = pl.multiple_of(i * unit_rows, unit_rows)
        for idx in range(3):
            x = src_refs[idx][0, pl.ds(r, unit_rows), :]
            if wm_ref is not None:
                t = _silu(_conv_rows_pre(x, wm_ref, idx * taps))
            else:
                t = _silu(_conv_rows(x, w_refs[idx][...], period))
            if idx < 2:
                t = t * lax.rsqrt(jnp.sum(t * t, axis=-1, keepdims=True) + NORM_EPS)
            if idx == 0:
                t = t * (HEAD_DIM ** -0.5)
            dst_refs[idx][pl.ds(base + r, unit_rows), :] = t
        return carry

    lax.fori_loop(0, n_units, body, 0)


N_NEUMANN = CHUNK.bit_length() - 2


def _side_masks(d):
    ii = lax.broadcasted_iota(jnp.int32, (CHUNK, LANES), 0)
    lane = lax.broadcasted_iota(jnp.int32, (CHUNK, LANES), 1)
    jj = lane % CHUNK
    first = lane < CHUNK
    if d == 0:
        return first, jj == ii, jj <= ii, jj < ii
    return first, jj == ii, jj >= ii, jj > ii


def _side(x, first):
    return jnp.where(first, x[:CHUNK], x[CHUNK:])


def _block_diag(x, first):
    zero = jnp.zeros_like(x)
    return jnp.concatenate([jnp.where(first, x, zero), jnp.where(first, zero, x)], axis=0)


def _dot3_side(a, b, first):
    a_hi, a_lo = _split_bf16(a)
    b_hi, b_lo = _split_bf16(b)
    rhs = jnp.concatenate([_block_diag(b_hi, first), _block_diag(b_lo, first), _block_diag(b_hi, first)], axis=0)
    return jnp.dot(jnp.concatenate([a_hi, a_hi, a_lo], axis=1), rhs, preferred_element_type=F32)


def _gdn_gates(rows_ref, d, p):
    eye = _pair_masks(0)[0]
    g_row = rows_ref[2 * d, pl.ds(p, 1), :]
    return dict(d=d, p=p, rows=pl.ds(pl.multiple_of(p * LANES, LANES), LANES), g_row=g_row,
                g_col=_to_col(g_row, eye), beta_col=_to_col(rows_ref[2 * d + 1, pl.ds(p, 1), :], eye))


def _gdn_prep_start(qs, ks, rows_ref, chains):
    st = []
    for d, p in chains:
        c = _gdn_gates(rows_ref, d, p)
        rows, g_row, g_col, beta_col = c["rows"], c["g_row"], c["g_col"], c["beta_col"]
        k = ks[rows, :]
        first, eye_s, incl, strict = _side_masks(d)
        dec = jnp.exp(jnp.where(incl, _side(g_col, first) - g_row, -jnp.inf))
        aqk = _block_diag(_side(_dot_nt(qs[rows, :], k), first) * dec, first).astype(BF16)
        m_low = jnp.where(strict, _side(beta_col, first) * _side(_dot_nt(k, k), first) * dec, 0.0)
        st.append(dict(c, aqk=aqk, pw=m_low, t_inv=jnp.where(eye_s, 1.0, 0.0) - m_low))
    return st


def _gdn_prep_neumann(st, after_square=None, after_product=None):
    first, eye_s, _, _ = _side_masks(0)
    eye_f = jnp.where(eye_s, 1.0, 0.0)
    for c in st:
        c["pw"] = _dot3_side(c["pw"], c["pw"], first)
    if after_square is not None:
        after_square()
    for c in st:
        c["t_inv"] = _dot3_side(c["t_inv"], eye_f + c["pw"], first)
    if after_product is not None:
        after_product()


NQ_ROWS = HEAD_DIM + CHUNK


def _gdn_prep_solve(qs, ks, vs, st):
    row_i = lax.broadcasted_iota(jnp.int32, (LANES, 1), 0)
    first = _side_masks(0)[0]
    for c in st:
        d, rows, g_row, g_col, beta_col = c["d"], c["rows"], c["g_row"], c["g_col"], c["beta_col"]
        k = ks[rows, :]
        e_g = jnp.exp(g_col)
        c["w_ut"] = _dot3(_block_diag(c["t_inv"], first),
                          jnp.concatenate([(beta_col * e_g) * k, beta_col * vs[rows, :]], axis=1)).astype(BF16)
        c["q_dec"] = qs[rows, :] * e_g
        last = CHUNK - 1 if d == 0 else 0
        g_tot_col = jnp.where(row_i < CHUNK, g_row[:, last:last + 1], g_row[:, CHUNK + last:CHUNK + last + 1])
        c["k_end_t"] = (k * jnp.exp(g_tot_col - g_col)).T.astype(BF16)


def _gdn_prep_finish(nq_s, r_s, acc_refs, st):
    row_i = lax.broadcasted_iota(jnp.int32, (LANES, 1), 0)
    for c in st:
        d, w_ut, q_dec = c["d"], c["w_ut"], c["q_dec"]
        zero = jnp.zeros_like(w_ut)
        for half in (0, 1):
            sl = slice(half * CHUNK, (half + 1) * CHUNK)
            lhs = jnp.concatenate([c["k_end_t"], c["aqk"][sl]], axis=0)
            mine = (row_i < CHUNK) if half == 0 else (row_i >= CHUNK)
            prod = jnp.dot(lhs, jnp.where(mine, w_ut, zero), preferred_element_type=F32)
            n_q = jnp.concatenate([prod[:HEAD_DIM, :HEAD_DIM], q_dec[sl] - prod[HEAD_DIM:, :HEAD_DIM]], axis=0)
            chunk = 2 * c["p"] + half
            nq_s[d, pl.ds(pl.multiple_of(chunk * NQ_ROWS, NQ_ROWS), NQ_ROWS), :] = n_q.astype(BF16)
            r_s[d, pl.ds(pl.multiple_of(chunk * HEAD_DIM, HEAD_DIM), HEAD_DIM), :] = prod[:HEAD_DIM, HEAD_DIM:]
            acc_refs[d][pl.ds(pl.multiple_of(chunk * CHUNK, CHUNK), CHUNK), :] = prod[HEAD_DIM:, HEAD_DIM:]


def _gdn_steps_state(nq_s, sst, chains):
    st = []
    for d, p, half in chains:
        chunk = 2 * p + half
        s = sst[d]
        n_q = nq_s[d, pl.ds(pl.multiple_of(chunk * NQ_ROWS, NQ_ROWS), NQ_ROWS), :]
        st.append((chunk, s, jnp.dot(n_q, s.astype(BF16), preferred_element_type=F32)))
    return st


def _gdn_steps_update(rows_ref, r_s, sst, acc_refs, chains, st):
    for (d, p, half), (chunk, s, prod) in zip(chains, st):
        lane = half * CHUNK + (CHUNK - 1 if d == 0 else 0)
        g_tot = rows_ref[2 * d, pl.ds(p, 1), lane:lane + 1]
        r0 = pl.ds(pl.multiple_of(chunk * CHUNK, CHUNK), CHUNK)
        acc_refs[d][r0, :] = acc_refs[d][r0, :] + prod[HEAD_DIM:]
        r_c = r_s[d, pl.ds(pl.multiple_of(chunk * HEAD_DIM, HEAD_DIM), HEAD_DIM), :]
        sst[d] = jnp.exp(g_tot) * s - prod[:HEAD_DIM] + r_c


def _gdn_kernel(alog_ref, dtb_ref, ql, kl, vl, zl, qc, kc, vc, zc, gl0, gl1, gl2, gl3, gc0, gc1, gc2, gc3,
                wq, wk, wv, nw_ref, *rest, n_heads, ctx_out):
    if ctx_out:
        yl_ref, yc_ref = rest[:2]
        rest = rest[2:]
    else:
        yl_ref, yc_ref = rest[0], None
        rest = rest[1:]
    sst, rows, qs, ks, vs, nq_s, r_s, acc_f, acc_b, nm_s, aq_s, wm_s = rest
    h = pl.program_id(1)
    n_l = ql.shape[1]
    n_c = qc.shape[1]
    nl = n_l // LANES
    nc = n_c // LANES
    sst[...] = jnp.zeros(sst.shape, F32)
    _gdn_rows((gl0, gl1, gl2, gl3), alog_ref, dtb_ref, h, n_heads, rows, 0)
    _gdn_rows((gc0, gc1, gc2, gc3), alog_ref, dtb_ref, h, n_heads, rows, nl)
    conv_rows = _pick_tile(n_l, 4 * CHUNK, CHUNK)
    _gdn_conv((ql, kl, vl), (wq, wk, wv), (qs, ks, vs), 0, n_l // conv_rows, conv_rows, CHUNK, wm_ref=wm_s)
    _gdn_conv((qc, kc, vc), (wq, wk, wv), (qs, ks, vs), n_l, 1, n_c, n_c)
    n_steps = nl + nc
    accs = (acc_f, acc_b)

    def fwd_pair(t):
        return jnp.where(t < nc, nl + t, t - nc)

    def bwd_pair(t):
        return n_steps - 1 - t

    n_trips = n_steps // 2
    n_early = (N_NEUMANN + 1) // 2

    def chains_of(trip):
        t = 2 * jnp.minimum(trip, n_trips - 1)
        return [(0, fwd_pair(t)), (1, bwd_pair(t)), (0, fwd_pair(t + 1)), (1, bwd_pair(t + 1))]

    def neumann(st, half=None):
        if half is None:
            _gdn_prep_neumann(st)
            return
        held = []
        _gdn_prep_neumann(st, lambda: held.append(_gdn_steps_state(nq_s, sst, half)),
                          lambda: _gdn_steps_update(rows, r_s, sst, accs, half, held[0]))

    def finish(st):
        _gdn_prep_solve(qs, ks, vs, st)
        _gdn_prep_finish(nq_s, r_s, accs, st)

    def halves_of(trip):
        out = []
        for t in (2 * trip, 2 * trip + 1):
            pf, pb = fwd_pair(t), bwd_pair(t)
            out += [[(0, pf, 0), (1, pb, 1)], [(0, pf, 1), (1, pb, 0)]]
        return out

    st = _gdn_prep_start(qs, ks, rows, chains_of(0))
    for _ in range(N_NEUMANN):
        neumann(st)
    finish(st)
    st = _gdn_prep_start(qs, ks, rows, chains_of(1))
    for _ in range(n_early):
        neumann(st)

    def park(st_part):
        for n, c in enumerate(st_part):
            nm_s[2 * n] = c["pw"]
            nm_s[2 * n + 1] = c["t_inv"]
            aq_s[n] = c["aqk"]

    park(st)

    def trip(i, carry):
        halves = halves_of(i)
        st_late = [dict(_gdn_gates(rows, d, p), pw=nm_s[2 * n], t_inv=nm_s[2 * n + 1], aqk=aq_s[n])
                   for n, (d, p) in enumerate(chains_of(i + 1))]
        st_early = _gdn_prep_start(qs, ks, rows, chains_of(i + 2))
        n_late = N_NEUMANN - n_early
        slot = 0
        for n in range(n_early):
            neumann(st_early, halves[slot] if slot < len(halves) else None)
            slot += 1
            if n < n_late:
                neumann(st_late, halves[slot] if slot < len(halves) else None)
                slot += 1
            if n == n_late - 1:
                _gdn_prep_solve(qs, ks, vs, st_late)
        _gdn_prep_finish(nq_s, r_s, accs, st_late)
        park(st_early)
        return carry

    lax.fori_loop(0, n_trips - 1, trip, 0)
    for half in halves_of(n_trips - 1):
        _gdn_steps_update(rows, r_s, sst, accs, half, _gdn_steps_state(nq_s, sst, half))
    _head_out_tiles((acc_f, acc_b), 0, nw_ref, zl, yl_ref, n_l)
    if ctx_out:
        _head_out_tiles((acc_f, acc_b), n_l, nw_ref, zc, yc_ref, n_c)


def _gdn(u_l, u_c, gt_l, gt_c, conv_w, a_log, dt_bias, nw, lay, ctx_out):
    b, l, _ = u_l.shape
    lc = u_c.shape[1]
    nh = lay["gh"]
    taps = conv_w.shape[0]
    blk = lambda off: off // HEAD_DIM
    offs = [lay["GQ"], lay["GQ"] + lay["DG"], lay["GQ"] + 2 * lay["DG"], lay["GZ"]]
    smem = pl.BlockSpec(memory_space=pltpu.SMEM)
    in_specs = [smem, smem]
    in_specs += [_head_block(l, blk(o)) for o in offs]
    in_specs += [_head_block(lc, blk(o)) for o in offs]
    in_specs += [_gate_block(l // LANES, j, nh) for j in range(4)]
    in_specs += [_gate_block(lc // LANES, j, nh) for j in range(4)]
    in_specs += [pl.BlockSpec((taps, HEAD_DIM), lambda b_, h, j=j: (0, j * nh + h)) for j in range(3)]
    in_specs += [pl.BlockSpec((1, HEAD_DIM), lambda b_, h: (0, 0))]
    out_specs = [pl.BlockSpec((1, l, HEAD_DIM), lambda b_, h: (b_, 0, h))]
    out_shape = [jax.ShapeDtypeStruct((b, l, nh * HEAD_DIM), BF16)]
    if ctx_out:
        out_specs.append(pl.BlockSpec((1, lc, HEAD_DIM), lambda b_, h: (b_, 0, h)))
        out_shape.append(jax.ShapeDtypeStruct((b, lc, nh * HEAD_DIM), BF16))
    lt = l + lc
    seq_f32 = pltpu.VMEM((lt, HEAD_DIM), F32)
    n_chunks = lt // CHUNK
    scratch = [pltpu.VMEM((2, HEAD_DIM, HEAD_DIM), F32), pltpu.VMEM((4, lt // LANES, LANES), F32),
               seq_f32, seq_f32, seq_f32,
               pltpu.VMEM((2, n_chunks * NQ_ROWS, HEAD_DIM), BF16), pltpu.VMEM((2, n_chunks * HEAD_DIM, HEAD_DIM), F32),
               seq_f32, seq_f32, pltpu.VMEM((8, CHUNK, LANES), F32), pltpu.VMEM((4, LANES, LANES), BF16),
               pltpu.VMEM((3 * taps, CHUNK, HEAD_DIM), F32)]
    res = pl.pallas_call(
        functools.partial(_gdn_kernel, n_heads=nh, ctx_out=ctx_out),
        grid=(b, nh),
        in_specs=in_specs, out_specs=out_specs, out_shape=out_shape,
        scratch_shapes=scratch,
        compiler_params=_cparams(("parallel", "parallel"), 56),
        name="gdn",
    )(a_log.reshape(-1), dt_bias.reshape(-1), *([u_l] * 4), *([u_c] * 4), *([gt_l] * 4), *([gt_c] * 4),
      *([conv_w] * 3), nw)
    return (res[0], res[1]) if ctx_out else (res[0], None)


def _dft_tables(l):
    k = jnp.arange(l, dtype=jnp.int32)
    ang = lambda t: ((k[:, None] * t[None, :]) % (2 * l)).astype(F32) * (math.pi / l)
    ang_a = ang(jnp.arange(l // CHUNK, dtype=jnp.int32) * CHUNK)[:, :, None]
    ang_b = ang(jnp.arange(CHUNK, dtype=jnp.int32))[:, None, :]
    cos_t = (jnp.cos(ang_a) * jnp.cos(ang_b) - jnp.sin(ang_a) * jnp.sin(ang_b)).reshape(l, l)
    sin_t = (jnp.sin(ang_a) * jnp.cos(ang_b) + jnp.cos(ang_a) * jnp.sin(ang_b)).reshape(l, l)
    alt = jnp.where(k % 2 == 0, 1.0, -1.0).astype(F32)
    sin_f = sin_t.at[0, :].set(alt)
    return cos_t.astype(BF16), sin_f.astype(BF16), sin_f.T.astype(BF16)


def _filter_kernel(feats_ref, featr_ref, w1_ref, b1_ref, fr_ref, w2_ref, b2_ref, w3c_ref, w3a_ref, dl_ref,
                   h2_ref, h2r_ref, hid_ref, hidr_ref):
    @pl.when((pl.program_id(0) == 0) & (pl.program_id(1) == 0))
    def _():
        for f_ref, h_ref in ((feats_ref, hid_ref), (featr_ref, hidr_ref)):
            hid = jnp.sin(fr_ref[...] * (_dot_hi(f_ref[...], w1_ref[...]) + b1_ref[...]))
            h_ref[...] = jnp.sin(fr_ref[...] * (_dot_hi(hid, w2_ref[...]) + b2_ref[...]))

    def raw(h_ref, f_ref, w3_ref):
        return _dot3(h_ref[...], w3_ref[...]) * jnp.exp(-f_ref[:, 0:1] * dl_ref[...])

    c_f = raw(hid_ref, feats_ref, w3c_ref)
    a_f = raw(hid_ref, feats_ref, w3a_ref)
    den_c = jnp.sum(jnp.abs(c_f), axis=0, keepdims=True) + NORM_EPS
    den_a = jnp.sum(jnp.abs(a_f), axis=0, keepdims=True) + NORM_EPS
    c_f = c_f / den_c
    a_f = a_f / den_a
    row0 = lax.broadcasted_iota(jnp.int32, c_f.shape, 0) == 0
    centre = c_f[0:1] + a_f[0:1]
    h2_ref[0, 0] = jnp.where(row0, 0.0, raw(hidr_ref, featr_ref, w3a_ref) / den_a)
    h2_ref[0, 1] = jnp.where(row0, centre, c_f)
    h2r_ref[0, 0] = jnp.where(row0, 0.0, raw(hidr_ref, featr_ref, w3c_ref) / den_c)
    h2r_ref[0, 1] = jnp.where(row0, centre, a_f)


def _filter_feats(pos, l, n_emb):
    t = pos / max(l - 1, 1)
    ang = 2.0 * math.pi * pos / l
    bands = jnp.linspace(1e-4, FILTER_BANDS - 1, FILTER_BANDS, dtype=F32)
    feats = jnp.concatenate([t[:, None], jnp.cos(ang[:, None] * bands), -jnp.sin(ang[:, None] * bands)], axis=-1)
    return jnp.pad(feats, ((0, 0), (0, LANES - n_emb)))


def _hy_filters(l, w1, b1, freq, w2, b2, w3, dh):
    n_emb, n_hid = w1.shape
    pos = jnp.arange(l, dtype=F32)
    feats = _filter_feats(pos, l, n_emb)
    feats_r = _filter_feats(l - pos, l, n_emb)
    pc = LANES - n_hid
    w1p = jnp.pad(w1, ((0, LANES - n_emb), (0, pc)))
    w2p = jnp.pad(w2, ((0, pc), (0, pc)))
    w3p = jnp.pad(w3, ((0, pc), (0, 0)))
    row = lambda a: jnp.pad(a, (0, pc))[None]
    deltas = jnp.abs(jnp.linspace(MIN_DECAY, MAX_DECAY, dh, dtype=F32))[None]
    nct = dh // LANES
    const = lambda shape: pl.BlockSpec(shape, lambda o, c: (0, 0))
    o_spec = pl.BlockSpec((1, 2, l, LANES), lambda o, c: (o, 0, 0, c))
    return pl.pallas_call(
        _filter_kernel,
        grid=(HYENA_ORDER, nct),
        in_specs=[const((l, LANES)), const((l, LANES)), const((LANES, LANES)), const((1, LANES)),
                  const((1, LANES)), const((LANES, LANES)), const((1, LANES)),
                  pl.BlockSpec((LANES, LANES), lambda o, c: (0, o * 2 * nct + c)),
                  pl.BlockSpec((LANES, LANES), lambda o, c: (0, o * 2 * nct + nct + c)),
                  pl.BlockSpec((1, LANES), lambda o, c: (0, c))],
        out_specs=[o_spec, o_spec],
        out_shape=[jax.ShapeDtypeStruct((HYENA_ORDER, 2, l, dh), F32)] * 2,
        scratch_shapes=[pltpu.VMEM((l, LANES), F32)] * 2,
        compiler_params=_cparams(("arbitrary", "arbitrary"), 56),
        name="hy_filter",
    )(feats, feats_r, w1p, row(b1), row(freq), w2p, row(b2), w3p, w3p, deltas)


def _spectrum_kernel(c_ref, s_ref, h2_ref, h2r_ref, a_ref, nyq_ref, hsin_ref, *, p):
    pos = h2_ref[0, 0]
    neg = h2r_ref[0, 0]
    is0 = lax.broadcasted_iota(jnp.int32, pos.shape, 0) == 0
    h_sum = jnp.where(is0, pos, pos + neg)
    sign = jnp.where(lax.broadcasted_iota(jnp.int32, pos.shape, 0) % 2 == 0, 1.0, -1.0)
    nyq = jnp.sum(h_sum * sign, axis=0, keepdims=True)
    a = jnp.dot(c_ref[...], h_sum.astype(BF16), preferred_element_type=F32)
    s = jnp.dot(s_ref[...], (pos - neg).astype(BF16), preferred_element_type=F32)
    wk = jnp.where(is0, 0.5 / p, 1.0 / p)
    a_ref[0, 0] = a * wk
    nyq_ref[0, 0] = jnp.broadcast_to((nyq - a[0:1]) * (0.5 / p), nyq_ref.shape[2:])
    hsin_ref[0, 0] = jnp.where(is0, 0.0, s) * wk


def _hy_spectrum(tabs, h2, h2r, p):
    cos_t, sin_f, _ = tabs
    n_ord, _, l, dh = h2.shape
    nb = l // p
    n_win = 2 * nb - 1
    h2 = h2.reshape(n_ord, 2 * nb, p, dh)
    h2r = h2r.reshape(n_ord, 2 * nb, p, dh)
    tab = pl.BlockSpec((p, p), lambda o, m: (0, 0))
    o_spec = pl.BlockSpec((1, 1, p, dh), lambda o, m: (o, m, 0, 0))
    return pl.pallas_call(
        functools.partial(_spectrum_kernel, p=p),
        grid=(n_ord, n_win),
        in_specs=[tab, tab,
                  pl.BlockSpec((1, 1, p, dh), lambda o, m: (o, m + 1, 0, 0)),
                  pl.BlockSpec((1, 1, p, dh), lambda o, m: (o, 2 * nb - 1 - m, 0, 0))],
        out_specs=[o_spec, pl.BlockSpec((1, 1, 8, dh), lambda o, m: (o, m, 0, 0)), o_spec],
        out_shape=[jax.ShapeDtypeStruct((n_ord, n_win, p, dh), F32),
                   jax.ShapeDtypeStruct((n_ord, n_win, 8, dh), F32),
                   jax.ShapeDtypeStruct((n_ord, n_win, p, dh), F32)],
        compiler_params=_cparams(("parallel", "parallel"), 48),
        name="hy_spectrum",
    )(cos_t, sin_f, h2, h2r)


HY_ROWS = 8


def _hy_block_kernel(c_ref, sf_ref, si_ref, a_ref, nyq_ref, hsin_ref, sk_ref, ug_ref, wg_ref, *rest,
                     nb, period, first):
    if first:
        uv_ref, wv_ref, of_ref, ob_ref, xc_s, xs_s, yc_s, ys_s, v_s = rest
    else:
        vb_ref, vp_ref, z_ref, o_ref, xc_s, xs_s, yc_s, ys_s = rest
    p = c_ref.shape[0]
    for j in range(nb):
        rows = slice(j * p, (j + 1) * p)
        if first:
            vj = _conv_rows(uv_ref[0, rows, :], wv_ref[...], period)
            v_s[rows, :] = vj
            xj = vj.astype(BF16)
        else:
            xj = vb_ref[0, rows, :]
        xc_s[j] = jnp.dot(c_ref[...], xj, preferred_element_type=F32)
        xs_s[j] = jnp.dot(sf_ref[...], xj, preferred_element_type=F32)

    nyq_fix = []
    for i in range(nb):
        fix = jnp.zeros((1, LANES), F32)
        for j in range(nb):
            fix = fix + xs_s[j, 0:1, :] * nyq_ref[0, i - j + (nb - 1), 0:1, :]
        nyq_fix.append(fix)

    def rows_body(r, carry):
        rs = pl.ds(pl.multiple_of(r * HY_ROWS, HY_ROWS), HY_ROWS)
        xc = [xc_s[j, rs, :] for j in range(nb)]
        xs = [xs_s[j, rs, :] for j in range(nb)]
        yc = [None] * nb
        ys = [None] * nb
        for w in range(2 * nb - 1):
            a = a_ref[0, w, rs, :]
            hsin = hsin_ref[0, w, rs, :]
            for i in range(nb):
                j = i - (w - (nb - 1))
                if 0 <= j < nb:
                    tc = xc[j] * a - xs[j] * hsin
                    ts = xc[j] * hsin + xs[j] * a
                    yc[i] = tc if yc[i] is None else yc[i] + tc
                    ys[i] = ts if ys[i] is None else ys[i] + ts
        row0 = (lax.broadcasted_iota(jnp.int32, (HY_ROWS, LANES), 0) + r * HY_ROWS) == 0
        for i in range(nb):
            yc_s[i, rs, :] = yc[i]
            ys_s[i, rs, :] = ys[i] + jnp.where(row0, nyq_fix[i], 0.0)
        return carry

    lax.fori_loop(0, p // HY_ROWS, rows_body, 0)
    for i in range(nb):
        rows = slice(i * p, (i + 1) * p)
        conv = jnp.dot(c_ref[...], yc_s[i].astype(BF16), preferred_element_type=F32)
        conv = conv + jnp.dot(si_ref[...], ys_s[i].astype(BF16), preferred_element_type=F32)
        xg = _conv_rows(ug_ref[0, rows, :], wg_ref[...], period)
        if first:
            y = xg * (conv + sk_ref[0] * v_s[rows, :])
            of_ref[0, rows, :] = y
            ob_ref[0, rows, :] = y.astype(BF16)
        else:
            y = xg * (conv + sk_ref[0] * vp_ref[0, rows, :])
            o_ref[0, rows, :] = (y * _silu(z_ref[0, rows, :])).astype(BF16)


def _hy_block(tabs, spec, u, conv_w, skip, order, p, dh, period, y_prev=None):
    b, l, _ = u.shape
    nb = l // p
    n_win = 2 * nb - 1
    nct = dh // LANES
    taps = conv_w.shape[0]
    first = y_prev is None
    tab = pl.BlockSpec((p, p), lambda c, i: (0, 0))
    filt = pl.BlockSpec((1, n_win, p, LANES), lambda c, i: (order, 0, 0, c))
    nyq = pl.BlockSpec((1, n_win, 8, LANES), lambda c, i: (order, 0, 0, c))
    u_cols = lambda k: pl.BlockSpec((1, l, LANES), lambda c, i, k=k: (i, 0, k * nct + c))
    w_cols = lambda k: pl.BlockSpec((taps, LANES), lambda c, i, k=k: (0, k * nct + c))
    col = pl.BlockSpec((1, l, LANES), lambda c, i: (i, 0, c))
    in_specs = [tab, tab, tab, filt, nyq, filt, pl.BlockSpec((1, 1, LANES), lambda c, i: (order, 0, c)),
                u_cols(order), w_cols(order)]
    args = [*tabs, *spec, skip[:, None, :], u, conv_w]
    scratch = [pltpu.VMEM((nb, p, LANES), F32)] * 4
    if first:
        in_specs += [u_cols(2), w_cols(2)]
        args += [u, conv_w]
        out_specs = [col, col]
        out_shape = [jax.ShapeDtypeStruct((b, l, dh), F32), jax.ShapeDtypeStruct((b, l, dh), BF16)]
        scratch.append(pltpu.VMEM((l, LANES), F32))
    else:
        in_specs += [col, col, u_cols(3)]
        args += [y_prev[1], y_prev[0], u]
        out_specs = col
        out_shape = jax.ShapeDtypeStruct((b, l, dh), BF16)
    return pl.pallas_call(
        functools.partial(_hy_block_kernel, nb=nb, period=period, first=first),
        grid=(nct, b),
        in_specs=in_specs, out_specs=out_specs, out_shape=out_shape,
        scratch_shapes=scratch,
        compiler_params=_cparams(("parallel", "parallel"), 56),
        name="hy_block",
    )(*args)


def _hyena(u, conv_w, w1, b1, freq, w2, b2, w3, skip, lay, grid_mask):
    assert HYENA_ORDER == 2
    dh = lay["DH"]
    l = u.shape[1]
    p = min(HY_BLOCK, l)
    period = CHUNK if grid_mask else l
    assert p % period == 0
    tabs = _dft_tables(p)
    spec = _hy_spectrum(tabs, *_hy_filters(l, w1, b1, freq, w2, b2, w3, dh), p)
    y1 = _hy_block(tabs, spec, u, conv_w, skip, 0, p, dh, period)
    return _hy_block(tabs, spec, u, conv_w, skip, 1, p, dh, period, y_prev=y1)


def _layout(d):
    dg, dh, dm = 3 * d // 8, d // 4, 3 * d // 8
    lay = {"DG": dg, "DH": dh, "DM": dm, "gh": dg // HEAD_DIM, "mh": dm // HEAD_DIM}
    lay["HZ"] = 3 * dh
    lay["GQ"] = 4 * dh
    lay["GZ"] = lay["GQ"] + 3 * dg
    lay["GAB"] = lay["GZ"] + dg
    lay["MQ"] = lay["GAB"] + LANES
    lay["MO"] = lay["MQ"] + 3 * dm
    lay["MZ"] = lay["MO"] + dm
    lay["MG"] = lay["MZ"] + dm
    lay["NP"] = lay["MG"] + LANES
    return lay


def _pack_w_in(w, lay):
    dg, dh, dm, gh, mh = lay["DG"], lay["DH"], lay["DM"], lay["gh"], lay["mh"]
    sizes = (3 * dg, dg, 4 * gh, 3 * dh, dh, 3 * dm, dm, dm, 4 * mh)
    offs = [0]
    for s in sizes:
        offs.append(offs[-1] + s)
    seg = [w[:, offs[i]:offs[i + 1]] for i in range(len(sizes))]
    g_qkv, g_z, g_ab, h_p, h_z, m_qkv, m_o, m_z, m_g = seg
    padl = lambda a: jnp.pad(a, ((0, 0), (0, LANES - a.shape[1])))
    return jnp.concatenate([h_p, h_z, g_qkv, g_z, padl(g_ab), m_qkv, m_o, m_z, padl(m_g)], axis=1).astype(BF16)


def _gate_rows(u, off, n):
    b, l, _ = u.shape
    return jnp.transpose(u[:, :, off:off + n], (0, 2, 1)).reshape(b, n, l // LANES, LANES)


def kernel(x, c, ctx, c_ctx, norm_w, mod_w, mod_b, w_in, gdn_conv, gdn_a_log, gdn_dt_bias, gdn_norm, hy_conv,
           hy_w1, hy_b1, hy_freq, hy_w2, hy_b2, hy_w3, hy_skip, ml_gate_bias, ml_norm, w_out, final_norm):
    b, l, d = x.shape
    lc = ctx.shape[1]
    depth = norm_w.shape[0]
    lay = _layout(d)
    dg, dh = lay["DG"], lay["DH"]
    assert b < COND_ROWS and l % (2 * LANES) == 0 and lc % (2 * LANES) == 0 and d % 1024 == 0
    assert l % min(HY_BLOCK, l) == 0
    cond = jnp.zeros((COND_ROWS, d), F32).at[:b].set(c).at[b].set(c_ctx)
    all_mods = _adaln(cond, mod_w, mod_b[:, None, :])
    for layer in range(depth):
        last = layer == depth - 1
        mods = all_mods[layer]
        sh, sc, gt = mods[:, :d], mods[:, d:2 * d], mods[:, 2 * d:]
        lat = lambda m: m[:b, None, :]
        cx = lambda m: jnp.broadcast_to(m[b][None, None, :], (b, 1, d))
        wp = _pack_w_in(w_in[layer], lay)
        nw = norm_w[layer][None]
        u_l = _inproj(x, nw, lat(sc), lat(sh), wp)
        u_c = _inproj(ctx, nw, cx(sc), cx(sh), wp)
        g_rows = lambda u, off, n: _gate_rows(u, off, n)
        yg_l, yg_c = _gdn(u_l, u_c, g_rows(u_l, lay["GAB"], 4 * lay["gh"]), g_rows(u_c, lay["GAB"], 4 * lay["gh"]),
                          gdn_conv[layer], gdn_a_log[layer], gdn_dt_bias[layer], gdn_norm[layer][None], lay,
                          not last)
        ym_l, ym_c = _mlstm(u_l, u_c, g_rows(u_l, lay["MG"], 4 * lay["mh"]), g_rows(u_c, lay["MG"], 4 * lay["mh"]),
                            ml_gate_bias[layer], ml_norm[layer][None], lay, not last)
        hy = (hy_conv[layer], hy_w1[layer], hy_b1[layer], hy_freq[layer], hy_w2[layer], hy_b2[layer],
              hy_w3[layer], hy_skip[layer])
        yh_l = _hyena(u_l, *hy, lay, True)
        wo = w_out[layer].astype(BF16)
        wg, wh, wm = wo[:dg], wo[dg:dg + dh], wo[dg + dh:]
        fw = final_norm[None]
        x = _outproj(x, yg_l, yh_l, ym_l, wg, wh, wm, lat(gt), fw, last)
        if not last:
            yh_c = _hyena(u_c, *hy, lay, False)
            ctx = _outproj(ctx, yg_c, yh_c, ym_c, wg, wh, wm, cx(gt), fw, False)
    return x
```
